```python
import math
import jax
import jax.numpy as jnp
from jax import lax
import numpy as np

D_MODEL = 1024
BATCH = 4
SEQ = 4096
DEPTH = 2

CHUNK = 64
QBLK = 128
HEAD_DIM = 64
NORM_EPS = 1e-6

A_HEADS = 4
A_QK_WIDTH = A_HEADS * 2 * HEAD_DIM
A_VDIM = 2 * HEAD_DIM
A_WIDTH = A_HEADS * A_VDIM

B_HEADS = 8
B_WIDTH = B_HEADS * HEAD_DIM

C_HEADS = 8
C_KV_HEADS = 2
C_GROUP = C_HEADS // C_KV_HEADS
C_WIDTH = C_HEADS * HEAD_DIM
C_KV_WIDTH = C_KV_HEADS * HEAD_DIM
WINDOW = 128
WIN_CHUNKS = WINDOW // CHUNK

D_HEADS = 8
D_WIDTH = D_HEADS * HEAD_DIM
D_LEFT_CHUNKS = 8
D_BAND = (D_LEFT_CHUNKS + 1) * CHUNK
REL_MAX = 256
REL_SIZE = REL_MAX + CHUNK

EVEN_SPLITS = (A_QK_WIDTH, A_QK_WIDTH, A_WIDTH, A_WIDTH, B_WIDTH, B_WIDTH, B_WIDTH, B_WIDTH, B_HEADS)
ODD_SPLITS = (C_WIDTH, C_KV_WIDTH, C_KV_WIDTH, C_WIDTH, D_WIDTH, D_WIDTH, D_WIDTH, D_WIDTH)
P_EVEN = sum(EVEN_SPLITS)
P_ODD = sum(ODD_SPLITS)
MIX_EVEN = A_WIDTH + B_WIDTH
MIX_ODD = C_WIDTH + D_WIDTH

kernel_name = "hybrid_chunk_causal_attn_trunk"


def rms_norm(x, g):
    xf = x.astype(jnp.float32)
    y = xf * lax.rsqrt(jnp.mean(xf * xf, axis=-1, keepdims=True) + NORM_EPS)
    return (y * g.astype(jnp.float32)).astype(x.dtype)


def split_cols(z, sizes):
    idx = np.cumsum(np.array(sizes))[:-1].tolist()
    return jnp.split(z, idx, axis=-1)


def alibi_slopes(n_heads):
    return 2.0 ** (-8.0 * jnp.arange(1, n_heads + 1, dtype=jnp.float32) / n_heads)


def sweep_query_blocks(fn, n_blocks):
    out = lax.map(fn, jnp.arange(n_blocks))
    out = jnp.moveaxis(out, 0, 1)
    return out.reshape((out.shape[0], -1) + out.shape[3:])


def diff_attention(q, k, v, lam, lam_init, subln_g):
    bsz, seq = q.shape[:2]
    scale = HEAD_DIM ** -0.5
    slopes = alibi_slopes(A_HEADS)
    tk = jnp.arange(seq)

    def block(b):
        start = b * QBLK
        tq = start + jnp.arange(QBLK)
        qb = lax.dynamic_slice_in_dim(q, start, QBLK, axis=1)
        s = jnp.einsum("bqhcd,bkhcd->bhcqk", qb, k).astype(jnp.float32) * scale
        dist = jnp.abs(tq[:, None] - tk[None, :]).astype(jnp.float32)
        s = s - (slopes[:, None, None] * dist)[None, :, None]
        allowed = (tk[None, :] // CHUNK) <= (tq[:, None] // CHUNK)
        s = jnp.where(allowed, s, -jnp.inf)
        p = jax.nn.softmax(s, axis=-1)
        w = p[:, :, 0] - lam * p[:, :, 1]
        return jnp.einsum("bhqk,bkhe->bqhe", w.astype(v.dtype), v)

    o = sweep_query_blocks(block, seq // QBLK)
    o = rms_norm(o, subln_g) * (1.0 - lam_init)
    return o.reshape(bsz, seq, A_WIDTH)


def forgetting_attention(q, k, v, log_f):
    bsz, seq = q.shape[:2]
    scale = HEAD_DIM ** -0.5
    cum = jnp.transpose(jnp.cumsum(log_f, axis=1), (0, 2, 1))
    tk = jnp.arange(seq)

    def block(b):
        start = b * QBLK
        tq = start + jnp.arange(QBLK)
        qb = lax.dynamic_slice_in_dim(q, start, QBLK, axis=1)
        cq = lax.dynamic_slice_in_dim(cum, start, QBLK, axis=2)
        s = jnp.einsum("bqhd,bkhd->bhqk", qb, k).astype(jnp.float32) * scale
        s = s + cq[..., :, None] - cum[:, :, None, :]
        s = jnp.where(tk[None, :] <= tq[:, None], s, -jnp.inf)
        p = jax.nn.softmax(s, axis=-1)
        return jnp.einsum("bhqk,bkhd->bqhd", p.astype(v.dtype), v)

    o = sweep_query_blocks(block, seq // QBLK)
    return o.reshape(bsz, seq, B_WIDTH)


def sliding_window_sink_attention(q, k, v, sinks):
    bsz, seq = q.shape[:2]
    nb = seq // QBLK
    scale = HEAD_DIM ** -0.5

    def band(t):
        tp = jnp.pad(t, ((0, 0), (QBLK, 0), (0, 0), (0, 0)))
        tb = tp.reshape(bsz, nb + 1, QBLK, C_KV_HEADS, HEAD_DIM)
        return jnp.concatenate([tb[:, :-1], tb[:, 1:]], axis=2)

    kb, vb = band(k), band(v)
    qb = q.reshape(bsz, nb, QBLK, C_KV_HEADS, C_GROUP, HEAD_DIM)
    s = jnp.einsum("bnqkgd,bnskd->bnkgqs", qb, kb).astype(jnp.float32) * scale
    iq = jnp.arange(QBLK)
    ik = jnp.arange(2 * QBLK) - QBLK
    dist = jnp.abs(iq[:, None] - ik[None, :]).astype(jnp.float32)
    slopes = alibi_slopes(C_HEADS).reshape(C_KV_HEADS, C_GROUP, 1, 1)
    s = s - (slopes * dist)[None, None]
    chunk_diff = (iq[:, None] // CHUNK + QBLK // CHUNK) - ((ik[None, :] + QBLK) // CHUNK)
    in_window = (chunk_diff >= 0) & (chunk_diff <= WIN_CHUNKS)
    valid = (jnp.arange(nb)[:, None] * QBLK + ik[None, :]) >= 0
    mask = in_window[None] & valid[:, None, :]
    s = jnp.where(mask[None, :, None, None], s, -jnp.inf)
    sink = jnp.broadcast_to(sinks.astype(jnp.float32).reshape(1, 1, C_KV_HEADS, C_GROUP, 1, 1),
                            s.shape[:-1] + (1,))
    p = jax.nn.softmax(jnp.concatenate([s, sink], axis=-1), axis=-1)[..., :-1]
    o = jnp.einsum("bnkgqs,bnskd->bnqkgd", p.astype(v.dtype), vb)
    return o.reshape(bsz, seq, C_WIDTH)


def chunk_relpos_attention(q, k, v, rel_table):
    bsz, seq = q.shape[:2]
    nc = seq // CHUNK
    scale = HEAD_DIM ** -0.5
    idx = jnp.arange(nc)[:, None] + jnp.arange(D_LEFT_CHUNKS + 1)[None, :]

    def band(t):
        tp = jnp.pad(t, ((0, 0), (D_LEFT_CHUNKS * CHUNK, 0), (0, 0), (0, 0)))
        tp = tp.reshape(bsz, nc + D_LEFT_CHUNKS, CHUNK, D_HEADS, HEAD_DIM)
        return tp[:, idx].reshape(bsz, nc, D_BAND, D_HEADS, HEAD_DIM)

    kb, vb = band(k), band(v)
    qc = q.reshape(bsz, nc, CHUNK, D_HEADS, HEAD_DIM)
    s = jnp.einsum("bnqhd,bnkhd->bnhqk", qc, kb).astype(jnp.float32) * scale
    iq = jnp.arange(CHUNK)
    ik = jnp.arange(D_BAND) - D_LEFT_CHUNKS * CHUNK
    rel = iq[:, None] - ik[None, :]
    ridx = jnp.clip(rel, -(CHUNK - 1), REL_MAX) + (CHUNK - 1)
    bias = rel_table[:, ridx].astype(jnp.float32)
    s = s + bias[None, None]
    valid = (jnp.arange(nc)[:, None] * CHUNK + ik[None, :]) >= 0
    s = jnp.where(valid[None, :, None, None, :], s, -jnp.inf)
    p = jax.nn.softmax(s, axis=-1)
    o = jnp.einsum("bnhqk,bnkhd->bnqhd", p.astype(v.dtype), vb)
    return o.reshape(bsz, seq, D_WIDTH)


def even_layer(x, ln_g, w_in, w_out, a_qn_g, a_kn_g, a_lq1, a_lk1, a_lq2, a_lk2, a_subln_g,
               b_qn_g, b_kn_g, b_f_bias, layer_idx):
    bsz, seq, _ = x.shape
    z = rms_norm(x, ln_g) @ w_in
    aq, ak, av, ag, bq, bk, bv, bg, bf = split_cols(z, EVEN_SPLITS)
    aq = rms_norm(aq.reshape(bsz, seq, A_HEADS, 2, HEAD_DIM), a_qn_g)
    ak = rms_norm(ak.reshape(bsz, seq, A_HEADS, 2, HEAD_DIM), a_kn_g)
    av = av.reshape(bsz, seq, A_HEADS, A_VDIM)
    lam_init = 0.8 - 0.6 * math.exp(-0.3 * layer_idx)
    f32 = jnp.float32
    lam = (jnp.exp(jnp.sum(a_lq1.astype(f32) * a_lk1.astype(f32)))
           - jnp.exp(jnp.sum(a_lq2.astype(f32) * a_lk2.astype(f32))) + lam_init)
    a_out = diff_attention(aq, ak, av, lam, lam_init, a_subln_g)
    bq = rms_norm(bq.reshape(bsz, seq, B_HEADS, HEAD_DIM), b_qn_g)
    bk = rms_norm(bk.reshape(bsz, seq, B_HEADS, HEAD_DIM), b_kn_g)
    bv = bv.reshape(bsz, seq, B_HEADS, HEAD_DIM)
    log_f = jax.nn.log_sigmoid(bf.astype(f32) + b_f_bias.astype(f32))
    b_out = forgetting_attention(bq, bk, bv, log_f)
    mixed = jnp.concatenate([a_out * jax.nn.silu(ag), b_out * jax.nn.silu(bg)], axis=-1)
    return x + mixed @ w_out


def odd_layer(x, ln_g, w_in, w_out, c_qn_g, c_kn_g, c_sinks, d_qn_g, d_kn_g, d_rel_bias):
    bsz, seq, _ = x.shape
    z = rms_norm(x, ln_g) @ w_in
    cq, ck, cv, cg, dq, dk, dv, dg = split_cols(z, ODD_SPLITS)
    cq = rms_norm(cq.reshape(bsz, seq, C_HEADS, HEAD_DIM), c_qn_g)
    ck = rms_norm(ck.reshape(bsz, seq, C_KV_HEADS, HEAD_DIM), c_kn_g)
    cv = cv.reshape(bsz, seq, C_KV_HEADS, HEAD_DIM)
    c_out = sliding_window_sink_attention(cq, ck, cv, c_sinks)
    dq = rms_norm(dq.reshape(bsz, seq, D_HEADS, HEAD_DIM), d_qn_g)
    dk = rms_norm(dk.reshape(bsz, seq, D_HEADS, HEAD_DIM), d_kn_g)
    dv = dv.reshape(bsz, seq, D_HEADS, HEAD_DIM)
    d_out = chunk_relpos_attention(dq, dk, dv, d_rel_bias)
    mixed = jnp.concatenate([c_out * jax.nn.silu(cg), d_out * jax.nn.silu(dg)], axis=-1)
    return x + mixed @ w_out


def setup_inputs(seed: int = 0) -> dict:
    key = jax.random.key(seed)
    ne = (DEPTH + 1) // 2
    no = DEPTH // 2
    ks = jax.random.split(key, 23)

    def nrm(k, shape, s):
        return s * jax.random.normal(k, shape, jnp.float32)

    def gain(k, shape):
        return 1.0 + 0.1 * jax.random.normal(k, shape, jnp.float32)

    return {
        "x": jax.random.normal(ks[0], (BATCH, SEQ, D_MODEL), jnp.float32),
        "even_ln_g": gain(ks[1], (ne, D_MODEL)),
        "even_w_in": nrm(ks[2], (ne, D_MODEL, P_EVEN), D_MODEL ** -0.5),
        "even_w_out": nrm(ks[3], (ne, MIX_EVEN, D_MODEL), MIX_EVEN ** -0.5),
        "a_q_norm_g": gain(ks[4], (ne, HEAD_DIM)),
        "a_k_norm_g": gain(ks[5], (ne, HEAD_DIM)),
        "a_lambda_q1": nrm(ks[6], (ne, HEAD_DIM), 0.1),
        "a_lambda_k1": nrm(ks[7], (ne, HEAD_DIM), 0.1),
        "a_lambda_q2": nrm(ks[8], (ne, HEAD_DIM), 0.1),
        "a_lambda_k2": nrm(ks[9], (ne, HEAD_DIM), 0.1),
        "a_subln_g": gain(ks[10], (ne, A_VDIM)),
        "b_q_norm_g": gain(ks[11], (ne, HEAD_DIM)),
        "b_k_norm_g": gain(ks[12], (ne, HEAD_DIM)),
        "b_forget_bias": 3.0 + nrm(ks[13], (ne, B_HEADS), 0.5),
        "odd_ln_g": gain(ks[14], (no, D_MODEL)),
        "odd_w_in": nrm(ks[15], (no, D_MODEL, P_ODD), D_MODEL ** -0.5),
        "odd_w_out": nrm(ks[16], (no, MIX_ODD, D_MODEL), MIX_ODD ** -0.5),
        "c_q_norm_g": gain(ks[17], (no, HEAD_DIM)),
        "c_k_norm_g": gain(ks[18], (no, HEAD_DIM)),
        "c_sinks": nrm(ks[19], (no, C_HEADS), 0.5),
        "d_q_norm_g": gain(ks[20], (no, HEAD_DIM)),
        "d_k_norm_g": gain(ks[21], (no, HEAD_DIM)),
        "d_rel_bias": nrm(ks[22], (no, D_HEADS, REL_SIZE), 0.5),
    }


def reference(x, even_ln_g, even_w_in, even_w_out, a_q_norm_g, a_k_norm_g, a_lambda_q1,
              a_lambda_k1, a_lambda_q2, a_lambda_k2, a_subln_g, b_q_norm_g, b_k_norm_g,
              b_forget_bias, odd_ln_g, odd_w_in, odd_w_out, c_q_norm_g, c_k_norm_g, c_sinks,
              d_q_norm_g, d_k_norm_g, d_rel_bias):
    for i in range(DEPTH):
        j = i // 2
        if i % 2 == 0:
            x = even_layer(x, even_ln_g[j], even_w_in[j], even_w_out[j], a_q_norm_g[j],
                           a_k_norm_g[j], a_lambda_q1[j], a_lambda_k1[j], a_lambda_q2[j],
                           a_lambda_k2[j], a_subln_g[j], b_q_norm_g[j], b_k_norm_g[j],
                           b_forget_bias[j], i)
        else:
            x = odd_layer(x, odd_ln_g[j], odd_w_in[j], odd_w_out[j], c_q_norm_g[j],
                          c_k_norm_g[j], c_sinks[j], d_q_norm_g[j], d_k_norm_g[j],
                          d_rel_bias[j])
    return x
```

```python
import functools
import math

import numpy as np
import jax
import jax.numpy as jnp
from jax import lax
from jax.experimental import pallas as pl
from jax.experimental.pallas import tpu as pltpu

D_MODEL = 1024
CHUNK = 64
HEAD_DIM = 64
NORM_EPS = 1e-6

A_HEADS = 4
A_STREAMS = 2 * A_HEADS
A_VDIM = 2 * HEAD_DIM
B_HEADS = 8
C_HEADS = 8
C_KV_HEADS = 2
C_GROUP = C_HEADS // C_KV_HEADS
WIN_CHUNKS = 2
D_HEADS = 8
D_LEFT_CHUNKS = 8
REL_MAX = 256

P_EVEN = 8 * 512 + B_HEADS
P_ODD = 512 + 128 + 128 + 512 + 4 * 512

LOG2E = 1.4426950408889634
QK_SCALE = HEAD_DIM ** -0.5 * LOG2E
NEG = -1e30

LANES = 128
KPAD = 128
BF16_ROWS = 16

PROJ_TOKENS = 512
ATT_TQ = 256
ATT_TK = 256
BAND_TQ = 128
VMEM_LIMIT = 56 * 1024 * 1024

f32 = jnp.float32
bf16 = jnp.bfloat16


def _split3(v):
    hi = v.astype(bf16).astype(f32)
    r = v - hi
    mid = r.astype(bf16).astype(f32)
    lo = (r - mid).astype(bf16).astype(f32)
    return hi, mid, lo


def _silu(z):
    return z * (1.0 / (1.0 + jnp.exp(-z)))


def _rms_rows(x_ref, g_ref):
    x = x_ref[0]
    ms = jnp.mean(x * x, axis=-1, keepdims=True)
    return (x * lax.rsqrt(ms + NORM_EPS) * g_ref[...]).astype(bf16)


def _proj_t(wt_ref, r0, r1, xn):
    return lax.dot_general(wt_ref[r0:r1, :], xn, (((1,), (1,)), ((), ())),
                           preferred_element_type=f32)


def _head_norm(z_t, gain_col, mult):
    n = z_t.shape[0] // HEAD_DIM
    z3 = z_t.reshape(n, HEAD_DIM, z_t.shape[1])
    ms = jnp.mean(z3 * z3, axis=1, keepdims=True)
    return z3 * lax.rsqrt(ms + NORM_EPS) * (gain_col[...] * mult)[None]


def _store_heads(o_ref, z_t):
    o_ref[0] = z_t.reshape(o_ref.shape[1], o_ref.shape[2], z_t.shape[1]).astype(bf16)


def _store_keys(k_ref, kn, aug_fn):
    n, _, t = kn.shape
    zeros = jnp.zeros((KPAD - HEAD_DIM - BF16_ROWS, t), f32)
    for s in range(n):
        blk = jnp.concatenate([kn[s], aug_fn(s), zeros], axis=0)
        k_ref[0, s] = blk.T.astype(bf16)


def _proj_even_kernel(x_ref, lng_ref, wt_ref, aqg_ref, akg_ref, bqg_ref, bkg_ref, bfb_ref,
                      aq_ref, ak_ref, av_ref, asg_ref, bq_ref, bk_ref, bv_ref, bsg_ref,
                      cum_ref):
    t = pl.program_id(1)
    tt = x_ref.shape[1]
    xn = _rms_rows(x_ref, lng_ref)

    aq_ref[0] = _head_norm(_proj_t(wt_ref, 0, 512, xn), aqg_ref, QK_SCALE).astype(bf16)

    row = lax.broadcasted_iota(jnp.int32, (BF16_ROWS, tt), 0)
    pos = t * tt + lax.broadcasted_iota(jnp.int32, (BF16_ROWS, tt), 1)
    pos_a = lax.shift_right_logical(pos, int(math.log2(CHUNK))).astype(f32)
    pos_b = lax.bitwise_and(pos, CHUNK - 1).astype(f32)
    aug_a = jnp.where(row < 3, pos_a, jnp.where(row < 6, pos_b, 0.0))
    akn = _head_norm(_proj_t(wt_ref, 512, 1024, xn), akg_ref, 1.0)
    _store_keys(ak_ref, akn, lambda s: aug_a)

    _store_heads(av_ref, _proj_t(wt_ref, 1024, 1536, xn))
    _store_heads(asg_ref, _silu(_proj_t(wt_ref, 1536, 2048, xn)))

    bq_ref[0] = _head_norm(_proj_t(wt_ref, 2048, 2560, xn), bqg_ref, QK_SCALE).astype(bf16)

    z = _proj_t(wt_ref, 4096, 4096 + BF16_ROWS, xn)[:B_HEADS] + bfb_ref[...]
    log_f = jnp.minimum(z, 0.0) - jnp.log(1.0 + jnp.exp(-jnp.abs(z)))
    tri = jnp.where(lax.broadcasted_iota(jnp.int32, (tt, tt), 0)
                    <= lax.broadcasted_iota(jnp.int32, (tt, tt), 1), 1.0, 0.0).astype(bf16)
    pieces = jnp.concatenate(_split3(log_f) + (jnp.zeros_like(log_f),), axis=0).astype(bf16)
    part = jnp.dot(pieces, tri, preferred_element_type=f32)
    local = part[:B_HEADS] + part[B_HEADS:2 * B_HEADS] + part[2 * B_HEADS:3 * B_HEADS]

    @pl.when(t == 0)
    def _():
        cum_ref[...] = jnp.zeros_like(cum_ref)

    cum = cum_ref[...] + local
    cum_ref[...] = cum[:, tt - 1:tt]
    g_hi, g_mid, g_lo = _split3(-LOG2E * cum)

    def aug_b(s):
        pick = lambda a: jnp.broadcast_to(a[s:s + 1], (BF16_ROWS, tt))
        return jnp.where(row == 0, pick(g_hi),
                         jnp.where(row == 1, pick(g_mid),
                                   jnp.where(row == 2, pick(g_lo), 0.0)))

    bkn = _head_norm(_proj_t(wt_ref, 2560, 3072, xn), bkg_ref, 1.0)
    _store_keys(bk_ref, bkn, aug_b)

    _store_heads(bv_ref, _proj_t(wt_ref, 3072, 3584, xn))
    _store_heads(bsg_ref, _silu(_proj_t(wt_ref, 3584, 4096, xn)))


def _proj_even(x, ln_g, wt, aqg, akg, bqg, bkg, bfb):
    bsz, seq, _ = x.shape
    tt = PROJ_TOKENS
    col = lambda n: pl.BlockSpec((n, 1), lambda b, t: (0, 0))
    fm = lambda n, d: pl.BlockSpec((1, n, d, tt), lambda b, t: (b, 0, 0, t))
    km = lambda n: pl.BlockSpec((1, n, tt, KPAD), lambda b, t: (b, 0, t, 0))
    fm_shape = lambda n, d: jax.ShapeDtypeStruct((bsz, n, d, seq), bf16)
    km_shape = lambda n: jax.ShapeDtypeStruct((bsz, n, seq, KPAD), bf16)
    return pl.pallas_call(
        _proj_even_kernel,
        grid=(bsz, seq // tt),
        in_specs=[
            pl.BlockSpec((1, tt, D_MODEL), lambda b, t: (b, t, 0)),
            pl.BlockSpec((1, D_MODEL), lambda b, t: (0, 0)),
            pl.BlockSpec(wt.shape, lambda b, t: (0, 0)),
            col(HEAD_DIM), col(HEAD_DIM), col(HEAD_DIM), col(HEAD_DIM), col(B_HEADS),
        ],
        out_specs=[fm(A_STREAMS, HEAD_DIM), km(A_STREAMS), fm(A_HEADS, A_VDIM), fm(A_HEADS, A_VDIM),
                   fm(B_HEADS, HEAD_DIM), km(B_HEADS), fm(B_HEADS, HEAD_DIM), fm(B_HEADS, HEAD_DIM)],
        out_shape=[fm_shape(A_STREAMS, HEAD_DIM), km_shape(A_STREAMS), fm_shape(A_HEADS, A_VDIM),
                   fm_shape(A_HEADS, A_VDIM), fm_shape(B_HEADS, HEAD_DIM), km_shape(B_HEADS),
                   fm_shape(B_HEADS, HEAD_DIM), fm_shape(B_HEADS, HEAD_DIM)],
        scratch_shapes=[pltpu.VMEM((B_HEADS, 1), f32)],
        compiler_params=pltpu.CompilerParams(
            dimension_semantics=("arbitrary", "arbitrary"), vmem_limit_bytes=VMEM_LIMIT),
        name="proj_even",
    )(x, ln_g, wt, aqg, akg, bqg, bkg, bfb)


def _proj_odd_kernel(x_ref, lng_ref, wt_ref, cqg_ref, ckg_ref, dqg_ref, dkg_ref,
                     cq_ref, ck_ref, cv_ref, csg_ref, dq_ref, dk_ref, dv_ref, dsg_ref):
    tt = x_ref.shape[1]
    xn = _rms_rows(x_ref, lng_ref)
    no_aug = lambda s: jnp.zeros((BF16_ROWS, tt), f32)

    cq_ref[0] = _head_norm(_proj_t(wt_ref, 0, 512, xn), cqg_ref, QK_SCALE).astype(bf16)
    _store_keys(ck_ref, _head_norm(_proj_t(wt_ref, 512, 640, xn), ckg_ref, 1.0), no_aug)
    _store_heads(cv_ref, _proj_t(wt_ref, 640, 768, xn))
    _store_heads(csg_ref, _silu(_proj_t(wt_ref, 768, 1280, xn)))
    dq_ref[0] = _head_norm(_proj_t(wt_ref, 1280, 1792, xn), dqg_ref, QK_SCALE).astype(bf16)
    _store_keys(dk_ref, _head_norm(_proj_t(wt_ref, 1792, 2304, xn), dkg_ref, 1.0), no_aug)
    _store_heads(dv_ref, _proj_t(wt_ref, 2304, 2816, xn))
    _store_heads(dsg_ref, _silu(_proj_t(wt_ref, 2816, 3328, xn)))


def _proj_odd(x, ln_g, wt, cqg, ckg, dqg, dkg):
    bsz, seq, _ = x.shape
    tt = PROJ_TOKENS
    col = lambda n: pl.BlockSpec((n, 1), lambda b, t: (0, 0))
    fm = lambda n: pl.BlockSpec((1, n, HEAD_DIM, tt), lambda b, t: (b, 0, 0, t))
    km = lambda n: pl.BlockSpec((1, n, tt, KPAD), lambda b, t: (b, 0, t, 0))
    fm_shape = lambda n: jax.ShapeDtypeStruct((bsz, n, HEAD_DIM, seq), bf16)
    km_shape = lambda n: jax.ShapeDtypeStruct((bsz, n, seq, KPAD), bf16)
    return pl.pallas_call(
        _proj_odd_kernel,
        grid=(bsz, seq // tt),
        in_specs=[
            pl.BlockSpec((1, tt, D_MODEL), lambda b, t: (b, t, 0)),
            pl.BlockSpec((1, D_MODEL), lambda b, t: (0, 0)),
            pl.BlockSpec(wt.shape, lambda b, t: (0, 0)),
            col(HEAD_DIM), col(HEAD_DIM), col(HEAD_DIM), col(HEAD_DIM),
        ],
        out_specs=[fm(C_HEADS), km(C_KV_HEADS), fm(C_KV_HEADS), fm(C_HEADS),
                   fm(D_HEADS), km(D_HEADS), fm(D_HEADS), fm(D_HEADS)],
        out_shape=[fm_shape(C_HEADS), km_shape(C_KV_HEADS), fm_shape(C_KV_HEADS), fm_shape(C_HEADS),
                   fm_shape(D_HEADS), km_shape(D_HEADS), fm_shape(D_HEADS), fm_shape(D_HEADS)],
        compiler_params=pltpu.CompilerParams(
            dimension_semantics=("arbitrary", "arbitrary"), vmem_limit_bytes=VMEM_LIMIT),
        name="proj_odd",
    )(x, ln_g, wt, cqg, ckg, dqg, dkg)


def _online_step(carry, s, v):
    m, l, acc = carry
    m_new = jnp.maximum(m, jnp.max(s, axis=0, keepdims=True))
    p = jnp.exp2(s - m_new)
    alpha = jnp.exp2(m - m_new)
    l = alpha * l + jnp.sum(p, axis=0, keepdims=True)
    acc = alpha * acc + jnp.dot(v, p.astype(bf16), preferred_element_type=f32)
    return m_new, l, acc


def _causal_sweep(k_at, v_at, q_aug, diag_bias, i, vdim):
    tq = q_aug.shape[1]

    def step(j, carry):
        s = jnp.dot(k_at(j), q_aug, preferred_element_type=f32)
        return _online_step(carry, s, v_at(j))

    init = (jnp.full((1, tq), NEG, f32), jnp.zeros((1, tq), f32), jnp.zeros((vdim, tq), f32))
    carry = lax.fori_loop(0, i, step, init)
    s = jnp.dot(k_at(i), q_aug, preferred_element_type=f32) + diag_bias
    _, l, acc = _online_step(carry, s, v_at(i))
    return acc, l


def _augment_q(q, aug_col):
    tq = q.shape[1]
    aug = jnp.broadcast_to(aug_col, (KPAD - HEAD_DIM, tq)).astype(bf16)
    return jnp.concatenate([q, aug], axis=0)


def _attn_a_kernel(q_ref, k_ref, v_ref, sg_ref, dtab_ref, qaug_ref, subg_ref, lamv_ref, o_ref,
                   *, lam_init):
    i = pl.program_id(2)
    lv = lamv_ref[...]
    lam = (jnp.exp(jnp.sum(lv[0:1] * lv[1:2], axis=1, keepdims=True))
           - jnp.exp(jnp.sum(lv[2:3] * lv[3:4], axis=1, keepdims=True)) + lam_init)
    v_at = lambda j: v_ref[0, 0, :, pl.ds(pl.multiple_of(j * ATT_TK, ATT_TK), ATT_TK)]
    outs = []
    for c in range(2):
        k_at = lambda j, c=c: k_ref[0, c, pl.ds(pl.multiple_of(j * ATT_TK, ATT_TK), ATT_TK), :]
        q_aug = _augment_q(q_ref[0, c], qaug_ref[0])
        acc, l = _causal_sweep(k_at, v_at, q_aug, dtab_ref[0], i, A_VDIM)
        outs.append(acc * (1.0 / l))
    o = outs[0] - lam * outs[1]
    ms = jnp.mean(o * o, axis=0, keepdims=True)
    y = o * lax.rsqrt(ms + NORM_EPS) * (subg_ref[...] * (1.0 - lam_init))
    o_ref[0, 0] = (y * sg_ref[0, 0].astype(f32)).astype(bf16)


def _attn_a(aq, ak, av, asg, dtab, qaug, subg, lamv, lam_init):
    bsz, _, _, seq = aq.shape
    nq = seq // ATT_TQ
    return pl.pallas_call(
        functools.partial(_attn_a_kernel, lam_init=lam_init),
        grid=(bsz, A_HEADS, nq),
        in_specs=[
            pl.BlockSpec((1, 2, HEAD_DIM, ATT_TQ), lambda b, h, i: (b, h, 0, i)),
            pl.BlockSpec((1, 2, seq, KPAD), lambda b, h, i: (b, h, 0, 0)),
            pl.BlockSpec((1, 1, A_VDIM, seq), lambda b, h, i: (b, h, 0, 0)),
            pl.BlockSpec((1, 1, A_VDIM, ATT_TQ), lambda b, h, i: (b, h, 0, i)),
            pl.BlockSpec((1, ATT_TK, ATT_TQ), lambda b, h, i: (h, 0, 0)),
            pl.BlockSpec((1, KPAD - HEAD_DIM, 1), lambda b, h, i: (h, 0, 0)),
            pl.BlockSpec((A_VDIM, 1), lambda b, h, i: (0, 0)),
            pl.BlockSpec((4, HEAD_DIM), lambda b, h, i: (0, 0)),
        ],
        out_specs=pl.BlockSpec((1, 1, A_VDIM, ATT_TQ), lambda b, h, i: (b, h, 0, i)),
        out_shape=jax.ShapeDtypeStruct((bsz, A_HEADS, A_VDIM, seq), bf16),
        compiler_params=pltpu.CompilerParams(
            dimension_semantics=("arbitrary", "arbitrary", "arbitrary"),
            vmem_limit_bytes=VMEM_LIMIT),
        name="attn_a",
    )(aq, ak, av, asg, dtab, qaug, subg, lamv)


def _attn_b_kernel(q_ref, k_ref, v_ref, sg_ref, qaug_ref, o_ref):
    i = pl.program_id(2)
    k_at = lambda j: k_ref[0, 0, pl.ds(pl.multiple_of(j * ATT_TK, ATT_TK), ATT_TK), :]
    v_at = lambda j: v_ref[0, 0, :, pl.ds(pl.multiple_of(j * ATT_TK, ATT_TK), ATT_TK)]
    q_aug = _augment_q(q_ref[0, 0], qaug_ref[...])
    causal = jnp.where(lax.broadcasted_iota(jnp.int32, (ATT_TK, ATT_TQ), 0)
                       <= lax.broadcasted_iota(jnp.int32, (ATT_TK, ATT_TQ), 1), 0.0, NEG)
    acc, l = _causal_sweep(k_at, v_at, q_aug, causal, i, HEAD_DIM)
    o_ref[0, 0] = (acc * (1.0 / l) * sg_ref[0, 0].astype(f32)).astype(bf16)


def _attn_b(bq, bk, bv, bsg, qaug):
    bsz, _, _, seq = bq.shape
    nq = seq // ATT_TQ
    return pl.pallas_call(
        _attn_b_kernel,
        grid=(bsz, B_HEADS, nq),
        in_specs=[
            pl.BlockSpec((1, 1, HEAD_DIM, ATT_TQ), lambda b, h, i: (b, h, 0, i)),
            pl.BlockSpec((1, 1, seq, KPAD), lambda b, h, i: (b, h, 0, 0)),
            pl.BlockSpec((1, 1, HEAD_DIM, seq), lambda b, h, i: (b, h, 0, 0)),
            pl.BlockSpec((1, 1, HEAD_DIM, ATT_TQ), lambda b, h, i: (b, h, 0, i)),
            pl.BlockSpec((KPAD - HEAD_DIM, 1), lambda b, h, i: (0, 0)),
        ],
        out_specs=pl.BlockSpec((1, 1, HEAD_DIM, ATT_TQ), lambda b, h, i: (b, h, 0, i)),
        out_shape=jax.ShapeDtypeStruct((bsz, B_HEADS, HEAD_DIM, seq), bf16),
        compiler_params=pltpu.CompilerParams(
            dimension_semantics=("arbitrary", "arbitrary", "arbitrary"),
            vmem_limit_bytes=VMEM_LIMIT),
        name="attn_b",
    )(bq, bk, bv, bsg, qaug)


def _band_kernel(q_ref, k_ref, v_ref, sg_ref, tab_ref, sink_ref, o_ref, m_s, l_s, acc_s,
                 *, group, back):
    i = pl.program_id(2)
    t = BAND_TQ
    q = jnp.concatenate([q_ref[0, g] for g in range(group)], axis=1)
    q = jnp.concatenate([q, jnp.zeros((KPAD - HEAD_DIM, group * t), bf16)], axis=0)
    m_s[...] = sink_ref[0]
    l_s[...] = jnp.ones_like(l_s)
    acc_s[...] = jnp.zeros_like(acc_s)
    for r in range(back + 1):
        kb = i - back + r

        @pl.when(kb >= 0)
        def _(r=r, kb=kb):
            start = pl.multiple_of(kb * t, t)
            s = jnp.dot(k_ref[0, 0, pl.ds(start, t), :], q, preferred_element_type=f32)
            s = s + tab_ref[0, r * t:(r + 1) * t, :]
            m, l, acc = _online_step((m_s[...], l_s[...], acc_s[...]), s,
                                     v_ref[0, 0, :, pl.ds(start, t)])
            m_s[...] = m
            l_s[...] = l
            acc_s[...] = acc

    o = acc_s[...] * (1.0 / l_s[...])
    for g in range(group):
        o_ref[0, g] = (o[:, g * t:(g + 1) * t] * sg_ref[0, g].astype(f32)).astype(bf16)


def _band_attn(q, k, v, sg, tab, sink, group, back, name):
    bsz, nheads, _, seq = q.shape
    nkv = nheads // group
    t = BAND_TQ
    return pl.pallas_call(
        functools.partial(_band_kernel, group=group, back=back),
        grid=(bsz, nkv, seq // t),
        in_specs=[
            pl.BlockSpec((1, group, HEAD_DIM, t), lambda b, h, i: (b, h, 0, i)),
            pl.BlockSpec((1, 1, seq, KPAD), lambda b, h, i: (b, h, 0, 0)),
            pl.BlockSpec((1, 1, HEAD_DIM, seq), lambda b, h, i: (b, h, 0, 0)),
            pl.BlockSpec((1, group, HEAD_DIM, t), lambda b, h, i: (b, h, 0, i)),
            pl.BlockSpec((1, (back + 1) * t, group * t), lambda b, h, i: (h, 0, 0)),
            pl.BlockSpec((1, 1, group * t), lambda b, h, i: (h, 0, 0)),
        ],
        out_specs=pl.BlockSpec((1, group, HEAD_DIM, t), lambda b, h, i: (b, h, 0, i)),
        out_shape=jax.ShapeDtypeStruct((bsz, nheads, HEAD_DIM, seq), bf16),
        scratch_shapes=[pltpu.VMEM((1, group * t), f32), pltpu.VMEM((1, group * t), f32),
                        pltpu.VMEM((HEAD_DIM, group * t), f32)],
        compiler_params=pltpu.CompilerParams(
            dimension_semantics=("arbitrary", "arbitrary", "arbitrary"),
            vmem_limit_bytes=VMEM_LIMIT),
        name=name,
    )(q, k, v, sg, tab, sink)


def _out_proj_kernel(m1_ref, m2_ref, wt_ref, x_ref, o_ref):
    half = m1_ref.shape[1]
    y_t = (jnp.dot(wt_ref[:, :half], m1_ref[0], preferred_element_type=f32)
           + jnp.dot(wt_ref[:, half:], m2_ref[0], preferred_element_type=f32))
    o_ref[0] = x_ref[0] + y_t.T


def _out_proj(m1, m2, wt, x):
    bsz, seq, _ = x.shape
    tt = PROJ_TOKENS
    half = m1.shape[1]
    return pl.pallas_call(
        _out_proj_kernel,
        grid=(bsz, seq // tt),
        in_specs=[
            pl.BlockSpec((1, half, tt), lambda b, t: (b, 0, t)),
            pl.BlockSpec((1, half, tt), lambda b, t: (b, 0, t)),
            pl.BlockSpec(wt.shape, lambda b, t: (0, 0)),
            pl.BlockSpec((1, tt, D_MODEL), lambda b, t: (b, t, 0)),
        ],
        out_specs=pl.BlockSpec((1, tt, D_MODEL), lambda b, t: (b, t, 0)),
        out_shape=jax.ShapeDtypeStruct(x.shape, f32),
        compiler_params=pltpu.CompilerParams(
            dimension_semantics=("arbitrary", "arbitrary"), vmem_limit_bytes=VMEM_LIMIT),
        name="out_proj",
    )(m1, m2, wt, x)


def _alibi_slopes(n):
    return 2.0 ** (-8.0 * np.arange(1, n + 1, dtype=np.float64) / n)


def _np_split3(v):
    v = np.asarray(v, np.float32)
    to_bf = lambda a: a.astype(bf16).astype(np.float32)
    hi = to_bf(v)
    mid = to_bf(v - hi)
    lo = to_bf(v - hi - mid)
    return hi, mid, lo


def _a_tables():
    rate = (_alibi_slopes(A_HEADS) * LOG2E).astype(np.float32)
    qaug = np.zeros((A_HEADS, KPAD - HEAD_DIM, 1), np.float32)
    for idx, piece in enumerate(_np_split3(rate * CHUNK) + _np_split3(rate)):
        qaug[:, idx, 0] = piece
    kk = np.arange(ATT_TK)[:, None]
    qq = np.arange(ATT_TQ)[None, :]
    future = np.maximum(kk - qq, 0).astype(np.float32)
    corr = -2.0 * rate[:, None, None] * future[None]
    allowed = (kk // CHUNK) <= (qq // CHUNK)
    dtab = np.where(allowed[None], corr, NEG).astype(np.float32)
    return jnp.asarray(qaug), jnp.asarray(dtab)


def _band_frames(back):
    t = BAND_TQ
    k_pos = np.arange((back + 1) * t)[:, None]
    q_pos = back * t + np.arange(t)[None, :]
    chunk_diff = q_pos // CHUNK - k_pos // CHUNK
    return q_pos - k_pos, chunk_diff


def _c_tables(sinks):
    back = 1
    rel, chunk_diff = _band_frames(back)
    allowed = (chunk_diff >= 0) & (chunk_diff <= WIN_CHUNKS)
    slopes = _alibi_slopes(C_HEADS)
    per_head = np.where(allowed[None], -slopes[:, None, None] * np.abs(rel)[None] * LOG2E, NEG)
    tab = per_head.reshape(C_KV_HEADS, C_GROUP, *rel.shape).transpose(0, 2, 1, 3)
    tab = tab.reshape(C_KV_HEADS, rel.shape[0], C_GROUP * BAND_TQ).astype(np.float32)
    sink = jnp.repeat(sinks.astype(f32) * LOG2E, BAND_TQ).reshape(C_KV_HEADS, 1, C_GROUP * BAND_TQ)
    return jnp.asarray(tab), sink, back


def _d_tables(rel_table):
    back = D_LEFT_CHUNKS * CHUNK // BAND_TQ
    rel, chunk_diff = _band_frames(back)
    allowed = (chunk_diff >= 0) & (chunk_diff <= D_LEFT_CHUNKS)
    ridx = np.clip(rel, -(CHUNK - 1), REL_MAX) + (CHUNK - 1)
    bias = rel_table.astype(f32)[:, ridx] * LOG2E
    tab = jnp.where(jnp.asarray(allowed)[None], bias, NEG)
    sink = jnp.full((D_HEADS, 1, BAND_TQ), NEG, f32)
    return tab, sink, back


def _pad_rows(w_t, rows):
    return jnp.pad(w_t, ((0, rows - w_t.shape[0]), (0, 0)))


def _even_layer(x, ln_g, w_in, w_out, a_qn_g, a_kn_g, a_lq1, a_lk1, a_lq2, a_lk2, a_subln_g,
                b_qn_g, b_kn_g, b_f_bias, layer_idx):
    bsz, seq, _ = x.shape
    colv = lambda v: v.astype(f32).reshape(-1, 1)
    wt = _pad_rows(w_in.T.astype(bf16), 4096 + BF16_ROWS)
    aq, ak, av, asg, bq, bk, bv, bsg = _proj_even(
        x, ln_g.astype(f32).reshape(1, -1), wt, colv(a_qn_g), colv(a_kn_g), colv(b_qn_g),
        colv(b_kn_g), colv(b_f_bias))
    lam_init = 0.8 - 0.6 * math.exp(-0.3 * layer_idx)
    qaug_a, dtab = _a_tables()
    lamv = jnp.stack([a_lq1, a_lk1, a_lq2, a_lk2]).astype(f32)
    mix_a = _attn_a(aq, ak, av, asg, dtab, qaug_a, colv(a_subln_g), lamv, lam_init)
    qaug_b = np.zeros((KPAD - HEAD_DIM, 1), np.float32)
    qaug_b[:3] = 1.0
    mix_b = _attn_b(bq, bk, bv, bsg, jnp.asarray(qaug_b))
    return _out_proj(mix_a.reshape(bsz, -1, seq), mix_b.reshape(bsz, -1, seq),
                     w_out.T.astype(bf16), x)


def _odd_layer(x, ln_g, w_in, w_out, c_qn_g, c_kn_g, c_sinks, d_qn_g, d_kn_g, d_rel_bias):
    bsz, seq, _ = x.shape
    colv = lambda v: v.astype(f32).reshape(-1, 1)
    cq, ck, cv, csg, dq, dk, dv, dsg = _proj_odd(
        x, ln_g.astype(f32).reshape(1, -1), w_in.T.astype(bf16), colv(c_qn_g), colv(c_kn_g),
        colv(d_qn_g), colv(d_kn_g))
    tab_c, sink_c, back_c = _c_tables(c_sinks)
    mix_c = _band_attn(cq, ck, cv, csg, tab_c, sink_c, C_GROUP, back_c, "attn_c")
    tab_d, sink_d, back_d = _d_tables(d_rel_bias)
    mix_d = _band_attn(dq, dk, dv, dsg, tab_d, sink_d, 1, back_d, "attn_d")
    return _out_proj(mix_c.reshape(bsz, -1, seq), mix_d.reshape(bsz, -1, seq),
                     w_out.T.astype(bf16), x)


def kernel(x, even_ln_g, even_w_in, even_w_out, a_q_norm_g, a_k_norm_g, a_lambda_q1, a_lambda_k1, a_lambda_q2, a_lambda_k2, a_subln_g, b_q_norm_g, b_k_norm_g, b_forget_bias, odd_ln_g, odd_w_in, odd_w_out, c_q_norm_g, c_k_norm_g, c_sinks, d_q_norm_g, d_k_norm_g, d_rel_bias):
    depth = even_ln_g.shape[0] + odd_ln_g.shape[0]
    for i in range(depth):
        j = i // 2
        if i % 2 == 0:
            x = _even_layer(x, even_ln_g[j], even_w_in[j], even_w_out[j], a_q_norm_g[j],
                            a_k_norm_g[j], a_lambda_q1[j], a_lambda_k1[j], a_lambda_q2[j],
                            a_lambda_k2[j], a_subln_g[j], b_q_norm_g[j], b_k_norm_g[j],
                            b_forget_bias[j], i)
        else:
            x = _odd_layer(x, odd_ln_g[j], odd_w_in[j], odd_w_out[j], c_q_norm_g[j],
                           c_k_norm_g[j], c_sinks[j], d_q_norm_g[j], d_k_norm_g[j],
                           d_rel_bias[j])
    return x
```

```python
import functools
import math

import numpy as np
import jax
import jax.numpy as jnp
from jax import lax
from jax.experimental import pallas as pl
from jax.experimental.pallas import tpu as pltpu

D_MODEL = 1024
CHUNK = 64
HEAD_DIM = 64
NORM_EPS = 1e-6

A_HEADS = 4
A_STREAMS = 2 * A_HEADS
A_VDIM = 2 * HEAD_DIM
B_HEADS = 8
C_HEADS = 8
C_KV_HEADS = 2
C_GROUP = C_HEADS // C_KV_HEADS
WIN_CHUNKS = 2
D_HEADS = 8
D_LEFT_CHUNKS = 8
REL_MAX = 256

P_EVEN = 8 * 512 + B_HEADS
P_ODD = 512 + 128 + 128 + 512 + 4 * 512

LOG2E = 1.4426950408889634
QK_SCALE = HEAD_DIM ** -0.5 * LOG2E
NEG = -1e30

LANES = 128
KPAD = 128
BF16_ROWS = 16

PROJ_TOKENS = 512
ATT_TQ = 512
ATT_TK = 512
B_HEADS_PER_STEP = 2
BAND_TQ = 128
VMEM_LIMIT = 56 * 1024 * 1024

f32 = jnp.float32
bf16 = jnp.bfloat16


def _split3(v):
    hi = v.astype(bf16).astype(f32)
    r = v - hi
    mid = r.astype(bf16).astype(f32)
    lo = (r - mid).astype(bf16).astype(f32)
    return hi, mid, lo


def _silu(z):
    return z * (1.0 / (1.0 + jnp.exp(-z)))


def _rms_rows(x_ref, g_ref):
    x = x_ref[0]
    ms = jnp.mean(x * x, axis=-1, keepdims=True)
    return (x * lax.rsqrt(ms + NORM_EPS) * g_ref[...]).astype(bf16)


def _proj_t(wt_ref, r0, r1, xn):
    return lax.dot_general(wt_ref[r0:r1, :], xn, (((1,), (1,)), ((), ())),
                           preferred_element_type=f32)


def _head_norm(z_t, gain_col, mult):
    n = z_t.shape[0] // HEAD_DIM
    z3 = z_t.reshape(n, HEAD_DIM, z_t.shape[1])
    ms = jnp.mean(z3 * z3, axis=1, keepdims=True)
    return z3 * lax.rsqrt(ms + NORM_EPS) * (gain_col[...] * mult)[None]


def _store_heads(o_ref, z_t):
    o_ref[0] = z_t.reshape(o_ref.shape[1], o_ref.shape[2], z_t.shape[1]).astype(bf16)


def _store_keys(k_ref, kn, aug_fn):
    n, _, t = kn.shape
    zeros = jnp.zeros((KPAD - HEAD_DIM - BF16_ROWS, t), f32)
    for s in range(n):
        blk = jnp.concatenate([kn[s], aug_fn(s), zeros], axis=0)
        k_ref[0, s] = blk.T.astype(bf16)


def _proj_even_kernel(x_ref, lng_ref, wt_ref, aqg_ref, akg_ref, bqg_ref, bkg_ref, bfb_ref,
                      aq_ref, ak_ref, av_ref, asg_ref, bq_ref, bk_ref, bv_ref, bsg_ref,
                      cum_ref):
    t = pl.program_id(1)
    tt = x_ref.shape[1]
    xn = _rms_rows(x_ref, lng_ref)

    aq_ref[0] = _head_norm(_proj_t(wt_ref, 0, 512, xn), aqg_ref, QK_SCALE).astype(bf16)

    row = lax.broadcasted_iota(jnp.int32, (BF16_ROWS, tt), 0)
    pos = t * tt + lax.broadcasted_iota(jnp.int32, (BF16_ROWS, tt), 1)
    pos_a = lax.shift_right_logical(pos, int(math.log2(CHUNK))).astype(f32)
    pos_b = lax.bitwise_and(pos, CHUNK - 1).astype(f32)
    aug_a = jnp.where(row < 3, pos_a, jnp.where(row < 6, pos_b, 0.0))
    akn = _head_norm(_proj_t(wt_ref, 512, 1024, xn), akg_ref, 1.0)
    _store_keys(ak_ref, akn, lambda s: aug_a)

    _store_heads(av_ref, _proj_t(wt_ref, 1024, 1536, xn))
    _store_heads(asg_ref, _silu(_proj_t(wt_ref, 1536, 2048, xn)))

    bq_ref[0] = _head_norm(_proj_t(wt_ref, 2048, 2560, xn), bqg_ref, QK_SCALE).astype(bf16)

    z = _proj_t(wt_ref, 4096, 4096 + BF16_ROWS, xn)[:B_HEADS] + bfb_ref[...]
    log_f = jnp.minimum(z, 0.0) - jnp.log(1.0 + jnp.exp(-jnp.abs(z)))
    tri = jnp.where(lax.broadcasted_iota(jnp.int32, (tt, tt), 0)
                    <= lax.broadcasted_iota(jnp.int32, (tt, tt), 1), 1.0, 0.0).astype(bf16)
    pieces = jnp.concatenate(_split3(log_f) + (jnp.zeros_like(log_f),), axis=0).astype(bf16)
    part = jnp.dot(pieces, tri, preferred_element_type=f32)
    local = part[:B_HEADS] + part[B_HEADS:2 * B_HEADS] + part[2 * B_HEADS:3 * B_HEADS]

    @pl.when(t == 0)
    def _():
        cum_ref[...] = jnp.zeros_like(cum_ref)

    cum = cum_ref[...] + local
    cum_ref[...] = cum[:, tt - 1:tt]
    g_hi, g_mid, g_lo = _split3(-LOG2E * cum)

    def aug_b(s):
        pick = lambda a: jnp.broadcast_to(a[s:s + 1], (BF16_ROWS, tt))
        return jnp.where(row == 0, pick(g_hi),
                         jnp.where(row == 1, pick(g_mid),
                                   jnp.where(row == 2, pick(g_lo), 0.0)))

    bkn = _head_norm(_proj_t(wt_ref, 2560, 3072, xn), bkg_ref, 1.0)
    _store_keys(bk_ref, bkn, aug_b)

    _store_heads(bv_ref, _proj_t(wt_ref, 3072, 3584, xn))
    _store_heads(bsg_ref, _silu(_proj_t(wt_ref, 3584, 4096, xn)))


def _proj_even(x, ln_g, wt, aqg, akg, bqg, bkg, bfb):
    bsz, seq, _ = x.shape
    tt = PROJ_TOKENS
    col = lambda n: pl.BlockSpec((n, 1), lambda b, t: (0, 0))
    fm = lambda n, d: pl.BlockSpec((1, n, d, tt), lambda b, t: (b, 0, 0, t))
    km = lambda n: pl.BlockSpec((1, n, tt, KPAD), lambda b, t: (b, 0, t, 0))
    fm_shape = lambda n, d: jax.ShapeDtypeStruct((bsz, n, d, seq), bf16)
    km_shape = lambda n: jax.ShapeDtypeStruct((bsz, n, seq, KPAD), bf16)
    return pl.pallas_call(
        _proj_even_kernel,
        grid=(bsz, seq // tt),
        in_specs=[
            pl.BlockSpec((1, tt, D_MODEL), lambda b, t: (b, t, 0)),
            pl.BlockSpec((1, D_MODEL), lambda b, t: (0, 0)),
            pl.BlockSpec(wt.shape, lambda b, t: (0, 0)),
            col(HEAD_DIM), col(HEAD_DIM), col(HEAD_DIM), col(HEAD_DIM), col(B_HEADS),
        ],
        out_specs=[fm(A_STREAMS, HEAD_DIM), km(A_STREAMS), fm(A_HEADS, A_VDIM), fm(A_HEADS, A_VDIM),
                   fm(B_HEADS, HEAD_DIM), km(B_HEADS), fm(B_HEADS, HEAD_DIM), fm(B_HEADS, HEAD_DIM)],
        out_shape=[fm_shape(A_STREAMS, HEAD_DIM), km_shape(A_STREAMS), fm_shape(A_HEADS, A_VDIM),
                   fm_shape(A_HEADS, A_VDIM), fm_shape(B_HEADS, HEAD_DIM), km_shape(B_HEADS),
                   fm_shape(B_HEADS, HEAD_DIM), fm_shape(B_HEADS, HEAD_DIM)],
        scratch_shapes=[pltpu.VMEM((B_HEADS, 1), f32)],
        compiler_params=pltpu.CompilerParams(
            dimension_semantics=("arbitrary", "arbitrary"), vmem_limit_bytes=VMEM_LIMIT),
        name="proj_even",
    )(x, ln_g, wt, aqg, akg, bqg, bkg, bfb)


def _proj_odd_kernel(x_ref, lng_ref, wt_ref, cqg_ref, ckg_ref, dqg_ref, dkg_ref,
                     cq_ref, ck_ref, cv_ref, csg_ref, dq_ref, dk_ref, dv_ref, dsg_ref):
    tt = x_ref.shape[1]
    xn = _rms_rows(x_ref, lng_ref)
    no_aug = lambda s: jnp.zeros((BF16_ROWS, tt), f32)

    cq_ref[0] = _head_norm(_proj_t(wt_ref, 0, 512, xn), cqg_ref, QK_SCALE).astype(bf16)
    _store_keys(ck_ref, _head_norm(_proj_t(wt_ref, 512, 640, xn), ckg_ref, 1.0), no_aug)
    _store_heads(cv_ref, _proj_t(wt_ref, 640, 768, xn))
    _store_heads(csg_ref, _silu(_proj_t(wt_ref, 768, 1280, xn)))
    dq_ref[0] = _head_norm(_proj_t(wt_ref, 1280, 1792, xn), dqg_ref, QK_SCALE).astype(bf16)
    _store_keys(dk_ref, _head_norm(_proj_t(wt_ref, 1792, 2304, xn), dkg_ref, 1.0), no_aug)
    _store_heads(dv_ref, _proj_t(wt_ref, 2304, 2816, xn))
    _store_heads(dsg_ref, _silu(_proj_t(wt_ref, 2816, 3328, xn)))


def _proj_odd(x, ln_g, wt, cqg, ckg, dqg, dkg):
    bsz, seq, _ = x.shape
    tt = PROJ_TOKENS
    col = lambda n: pl.BlockSpec((n, 1), lambda b, t: (0, 0))
    fm = lambda n: pl.BlockSpec((1, n, HEAD_DIM, tt), lambda b, t: (b, 0, 0, t))
    km = lambda n: pl.BlockSpec((1, n, tt, KPAD), lambda b, t: (b, 0, t, 0))
    fm_shape = lambda n: jax.ShapeDtypeStruct((bsz, n, HEAD_DIM, seq), bf16)
    km_shape = lambda n: jax.ShapeDtypeStruct((bsz, n, seq, KPAD), bf16)
    return pl.pallas_call(
        _proj_odd_kernel,
        grid=(bsz, seq // tt),
        in_specs=[
            pl.BlockSpec((1, tt, D_MODEL), lambda b, t: (b, t, 0)),
            pl.BlockSpec((1, D_MODEL), lambda b, t: (0, 0)),
            pl.BlockSpec(wt.shape, lambda b, t: (0, 0)),
            col(HEAD_DIM), col(HEAD_DIM), col(HEAD_DIM), col(HEAD_DIM),
        ],
        out_specs=[fm(C_HEADS), km(C_KV_HEADS), fm(C_KV_HEADS), fm(C_HEADS),
                   fm(D_HEADS), km(D_HEADS), fm(D_HEADS), fm(D_HEADS)],
        out_shape=[fm_shape(C_HEADS), km_shape(C_KV_HEADS), fm_shape(C_KV_HEADS), fm_shape(C_HEADS),
                   fm_shape(D_HEADS), km_shape(D_HEADS), fm_shape(D_HEADS), fm_shape(D_HEADS)],
        compiler_params=pltpu.CompilerParams(
            dimension_semantics=("arbitrary", "arbitrary"), vmem_limit_bytes=VMEM_LIMIT),
        name="proj_odd",
    )(x, ln_g, wt, cqg, ckg, dqg, dkg)


def _online_step(carry, s, v):
    m, l, acc = carry
    m_new = jnp.maximum(m, jnp.max(s, axis=0, keepdims=True))
    p = jnp.exp2(s - m_new)
    alpha = jnp.exp2(m - m_new)
    l = alpha * l + jnp.sum(p, axis=0, keepdims=True)
    acc = alpha * acc + jnp.dot(v, p.astype(bf16), preferred_element_type=f32)
    return m_new, l, acc


def _causal_sweep(chains, i):
    def update(carries, j, diag):
        out = []
        for (k_at, v_at, q_aug, diag_bias, _), carry in zip(chains, carries):
            s = jnp.dot(k_at(j), q_aug, preferred_element_type=f32)
            if diag:
                s = s + diag_bias
            out.append(_online_step(carry, s, v_at(j)))
        return tuple(out)

    init = tuple((jnp.full((1, q.shape[1]), NEG, f32), jnp.zeros((1, q.shape[1]), f32),
                  jnp.zeros((vdim, q.shape[1]), f32)) for _, _, q, _, vdim in chains)
    carries = lax.fori_loop(0, i, lambda j, c: update(c, j, False), init)
    carries = update(carries, i, True)
    return [(acc, l) for _, l, acc in carries]


def _augment_q(q, aug_col):
    tq = q.shape[1]
    aug = jnp.broadcast_to(aug_col, (KPAD - HEAD_DIM, tq)).astype(bf16)
    return jnp.concatenate([q, aug], axis=0)


def _attn_a_kernel(q_ref, k_ref, v_ref, sg_ref, dtab_ref, qaug_ref, subg_ref, lamv_ref, o_ref,
                   *, lam_init):
    i = pl.program_id(2)
    lv = lamv_ref[...]
    lam = (jnp.exp(jnp.sum(lv[0:1] * lv[1:2], axis=1, keepdims=True))
           - jnp.exp(jnp.sum(lv[2:3] * lv[3:4], axis=1, keepdims=True)) + lam_init)
    v_at = lambda j: v_ref[0, 0, :, pl.ds(pl.multiple_of(j * ATT_TK, ATT_TK), ATT_TK)]
    chains = []
    for c in range(2):
        k_at = lambda j, c=c: k_ref[0, c, pl.ds(pl.multiple_of(j * ATT_TK, ATT_TK), ATT_TK), :]
        chains.append((k_at, v_at, _augment_q(q_ref[0, c], qaug_ref[0]), dtab_ref[0], A_VDIM))
    outs = [acc * (1.0 / l) for acc, l in _causal_sweep(chains, i)]
    o = outs[0] - lam * outs[1]
    ms = jnp.mean(o * o, axis=0, keepdims=True)
    y = o * lax.rsqrt(ms + NORM_EPS) * (subg_ref[...] * (1.0 - lam_init))
    o_ref[0, 0] = (y * sg_ref[0, 0].astype(f32)).astype(bf16)


def _attn_a(aq, ak, av, asg, dtab, qaug, subg, lamv, lam_init):
    bsz, _, _, seq = aq.shape
    nq = seq // ATT_TQ
    return pl.pallas_call(
        functools.partial(_attn_a_kernel, lam_init=lam_init),
        grid=(bsz, A_HEADS, nq),
        in_specs=[
            pl.BlockSpec((1, 2, HEAD_DIM, ATT_TQ), lambda b, h, i: (b, h, 0, i)),
            pl.BlockSpec((1, 2, seq, KPAD), lambda b, h, i: (b, h, 0, 0)),
            pl.BlockSpec((1, 1, A_VDIM, seq), lambda b, h, i: (b, h, 0, 0)),
            pl.BlockSpec((1, 1, A_VDIM, ATT_TQ), lambda b, h, i: (b, h, 0, i)),
            pl.BlockSpec((1, ATT_TK, ATT_TQ), lambda b, h, i: (h, 0, 0)),
            pl.BlockSpec((1, KPAD - HEAD_DIM, 1), lambda b, h, i: (h, 0, 0)),
            pl.BlockSpec((A_VDIM, 1), lambda b, h, i: (0, 0)),
            pl.BlockSpec((4, HEAD_DIM), lambda b, h, i: (0, 0)),
        ],
        out_specs=pl.BlockSpec((1, 1, A_VDIM, ATT_TQ), lambda b, h, i: (b, h, 0, i)),
        out_shape=jax.ShapeDtypeStruct((bsz, A_HEADS, A_VDIM, seq), bf16),
        compiler_params=pltpu.CompilerParams(
            dimension_semantics=("arbitrary", "arbitrary", "arbitrary"),
            vmem_limit_bytes=VMEM_LIMIT),
        name="attn_a",
    )(aq, ak, av, asg, dtab, qaug, subg, lamv)


def _attn_b_kernel(q_ref, k_ref, v_ref, sg_ref, qaug_ref, o_ref):
    i = pl.program_id(2)
    causal = jnp.where(lax.broadcasted_iota(jnp.int32, (ATT_TK, ATT_TQ), 0)
                       <= lax.broadcasted_iota(jnp.int32, (ATT_TK, ATT_TQ), 1), 0.0, NEG)
    chains = []
    for h in range(B_HEADS_PER_STEP):
        k_at = lambda j, h=h: k_ref[0, h, pl.ds(pl.multiple_of(j * ATT_TK, ATT_TK), ATT_TK), :]
        v_at = lambda j, h=h: v_ref[0, h, :, pl.ds(pl.multiple_of(j * ATT_TK, ATT_TK), ATT_TK)]
        chains.append((k_at, v_at, _augment_q(q_ref[0, h], qaug_ref[...]), causal, HEAD_DIM))
    for h, (acc, l) in enumerate(_causal_sweep(chains, i)):
        o_ref[0, h] = (acc * (1.0 / l) * sg_ref[0, h].astype(f32)).astype(bf16)


def _attn_b(bq, bk, bv, bsg, qaug):
    bsz, _, _, seq = bq.shape
    nq = seq // ATT_TQ
    hs = B_HEADS_PER_STEP
    return pl.pallas_call(
        _attn_b_kernel,
        grid=(bsz, B_HEADS // hs, nq),
        in_specs=[
            pl.BlockSpec((1, hs, HEAD_DIM, ATT_TQ), lambda b, h, i: (b, h, 0, i)),
            pl.BlockSpec((1, hs, seq, KPAD), lambda b, h, i: (b, h, 0, 0)),
            pl.BlockSpec((1, hs, HEAD_DIM, seq), lambda b, h, i: (b, h, 0, 0)),
            pl.BlockSpec((1, hs, HEAD_DIM, ATT_TQ), lambda b, h, i: (b, h, 0, i)),
            pl.BlockSpec((KPAD - HEAD_DIM, 1), lambda b, h, i: (0, 0)),
        ],
        out_specs=pl.BlockSpec((1, hs, HEAD_DIM, ATT_TQ), lambda b, h, i: (b, h, 0, i)),
        out_shape=jax.ShapeDtypeStruct((bsz, B_HEADS, HEAD_DIM, seq), bf16),
        compiler_params=pltpu.CompilerParams(
            dimension_semantics=("arbitrary", "arbitrary", "arbitrary"),
            vmem_limit_bytes=VMEM_LIMIT),
        name="attn_b",
    )(bq, bk, bv, bsg, qaug)


def _band_kernel(q_ref, k_ref, v_ref, sg_ref, tab_ref, sink_ref, o_ref, *, group, back):
    i = pl.program_id(1)
    t = BAND_TQ
    band = (back + 1) * t
    k_start = pl.multiple_of(jnp.maximum(i - back, 0) * t, t)
    tab_start = pl.multiple_of(jnp.maximum(back - i, 0) * t, t)
    for h in range(k_ref.shape[1]):
        q = jnp.concatenate([q_ref[0, h * group + g] for g in range(group)], axis=1)
        q = jnp.concatenate([q, jnp.zeros((KPAD - HEAD_DIM, group * t), bf16)], axis=0)
        s = jnp.dot(k_ref[0, h, pl.ds(k_start, band), :], q, preferred_element_type=f32)
        s = s + tab_ref[h, pl.ds(tab_start, band), :]
        sink = sink_ref[h]
        m = jnp.maximum(jnp.max(s, axis=0, keepdims=True), sink)
        p = jnp.exp2(s - m)
        l = jnp.sum(p, axis=0, keepdims=True) + jnp.exp2(sink - m)
        o = jnp.dot(v_ref[0, h, :, pl.ds(k_start, band)], p.astype(bf16),
                    preferred_element_type=f32) * (1.0 / l)
        for g in range(group):
            hq = h * group + g
            o_ref[0, hq] = (o[:, g * t:(g + 1) * t] * sg_ref[0, hq].astype(f32)).astype(bf16)


def _band_attn(q, k, v, sg, tab, sink, group, back, name):
    bsz, nheads, _, seq = q.shape
    nkv = nheads // group
    t = BAND_TQ
    whole = lambda a: pl.BlockSpec(a.shape, lambda b, i: (0,) * a.ndim)
    return pl.pallas_call(
        functools.partial(_band_kernel, group=group, back=back),
        grid=(bsz, seq // t),
        in_specs=[
            pl.BlockSpec((1, nheads, HEAD_DIM, t), lambda b, i: (b, 0, 0, i)),
            pl.BlockSpec((1, nkv, seq, KPAD), lambda b, i: (b, 0, 0, 0)),
            pl.BlockSpec((1, nkv, HEAD_DIM, seq), lambda b, i: (b, 0, 0, 0)),
            pl.BlockSpec((1, nheads, HEAD_DIM, t), lambda b, i: (b, 0, 0, i)),
            whole(tab), whole(sink),
        ],
        out_specs=pl.BlockSpec((1, nheads, HEAD_DIM, t), lambda b, i: (b, 0, 0, i)),
        out_shape=jax.ShapeDtypeStruct((bsz, nheads, HEAD_DIM, seq), bf16),
        compiler_params=pltpu.CompilerParams(
            dimension_semantics=("arbitrary", "arbitrary"), vmem_limit_bytes=VMEM_LIMIT),
        name=name,
    )(q, k, v, sg, tab, sink)


def _out_proj_kernel(m1_ref, m2_ref, wt_ref, x_ref, o_ref):
    half = m1_ref.shape[1]
    y_t = (jnp.dot(wt_ref[:, :half], m1_ref[0], preferred_element_type=f32)
           + jnp.dot(wt_ref[:, half:], m2_ref[0], preferred_element_type=f32))
    o_ref[0] = x_ref[0] + y_t.T


def _out_proj(m1, m2, wt, x):
    bsz, seq, _ = x.shape
    tt = PROJ_TOKENS
    half = m1.shape[1]
    return pl.pallas_call(
        _out_proj_kernel,
        grid=(bsz, seq // tt),
        in_specs=[
            pl.BlockSpec((1, half, tt), lambda b, t: (b, 0, t)),
            pl.BlockSpec((1, half, tt), lambda b, t: (b, 0, t)),
            pl.BlockSpec(wt.shape, lambda b, t: (0, 0)),
            pl.BlockSpec((1, tt, D_MODEL), lambda b, t: (b, t, 0)),
        ],
        out_specs=pl.BlockSpec((1, tt, D_MODEL), lambda b, t: (b, t, 0)),
        out_shape=jax.ShapeDtypeStruct(x.shape, f32),
        compiler_params=pltpu.CompilerParams(
            dimension_semantics=("arbitrary", "arbitrary"), vmem_limit_bytes=VMEM_LIMIT),
        name="out_proj",
    )(m1, m2, wt, x)


def _alibi_slopes(n):
    return 2.0 ** (-8.0 * np.arange(1, n + 1, dtype=np.float64) / n)


def _np_split3(v):
    v = np.asarray(v, np.float32)
    to_bf = lambda a: a.astype(bf16).astype(np.float32)
    hi = to_bf(v)
    mid = to_bf(v - hi)
    lo = to_bf(v - hi - mid)
    return hi, mid, lo


def _a_tables():
    rate = (_alibi_slopes(A_HEADS) * LOG2E).astype(np.float32)
    qaug = np.zeros((A_HEADS, KPAD - HEAD_DIM, 1), np.float32)
    for idx, piece in enumerate(_np_split3(rate * CHUNK) + _np_split3(rate)):
        qaug[:, idx, 0] = piece
    kk = np.arange(ATT_TK)[:, None]
    qq = np.arange(ATT_TQ)[None, :]
    future = np.maximum(kk - qq, 0).astype(np.float32)
    corr = -2.0 * rate[:, None, None] * future[None]
    allowed = (kk // CHUNK) <= (qq // CHUNK)
    dtab = np.where(allowed[None], corr, NEG).astype(np.float32)
    return jnp.asarray(qaug), jnp.asarray(dtab)


def _band_frames(back):
    t = BAND_TQ
    k_pos = np.arange((back + 1) * t)[:, None]
    q_pos = back * t + np.arange(t)[None, :]
    chunk_diff = q_pos // CHUNK - k_pos // CHUNK
    return q_pos - k_pos, chunk_diff


def _c_tables(sinks):
    back = 1
    rel, chunk_diff = _band_frames(back)
    allowed = (chunk_diff >= 0) & (chunk_diff <= WIN_CHUNKS)
    slopes = _alibi_slopes(C_HEADS)
    per_head = np.where(allowed[None], -slopes[:, None, None] * np.abs(rel)[None] * LOG2E, NEG)
    tab = per_head.reshape(C_KV_HEADS, C_GROUP, *rel.shape).transpose(0, 2, 1, 3)
    tab = tab.reshape(C_KV_HEADS, rel.shape[0], C_GROUP * BAND_TQ).astype(np.float32)
    tab = np.concatenate([tab, np.full((C_KV_HEADS, back * BAND_TQ, tab.shape[2]), NEG, np.float32)], 1)
    sink = jnp.repeat(sinks.astype(f32) * LOG2E, BAND_TQ).reshape(C_KV_HEADS, 1, C_GROUP * BAND_TQ)
    return jnp.asarray(tab), sink, back


def _d_tables(rel_table):
    back = D_LEFT_CHUNKS * CHUNK // BAND_TQ
    t = BAND_TQ
    band = (back + 1) * t
    rel, chunk_diff = _band_frames(back)
    allowed = (chunk_diff >= 0) & (chunk_diff <= D_LEFT_CHUNKS)
    tbl = rel_table.astype(f32) * LOG2E
    n_lo = (t - 1) - (CHUNK - 1)
    n_hi = (band - 1) - REL_MAX
    diag = jnp.concatenate([jnp.repeat(tbl[:, :1], n_lo, axis=1), tbl,
                            jnp.repeat(tbl[:, -1:], n_hi, axis=1)], axis=1)
    m = diag.shape[1]
    skew = jnp.tile(diag, (1, band + 1))[:, :band * (m + 1)].reshape(D_HEADS, band, m + 1)[:, :, :t]
    bias = jnp.flip(skew, axis=1)
    tab = jnp.where(jnp.asarray(allowed)[None], bias, NEG)
    tab = jnp.concatenate([tab, jnp.full((D_HEADS, back * t, t), NEG, f32)], axis=1)
    sink = jnp.full((D_HEADS, 1, t), NEG, f32)
    return tab, sink, back


def _pad_rows(w_t, rows):
    return jnp.pad(w_t, ((0, rows - w_t.shape[0]), (0, 0)))


def _even_layer(x, ln_g, w_in, w_out, a_qn_g, a_kn_g, a_lq1, a_lk1, a_lq2, a_lk2, a_subln_g,
                b_qn_g, b_kn_g, b_f_bias, layer_idx):
    bsz, seq, _ = x.shape
    colv = lambda v: v.astype(f32).reshape(-1, 1)
    wt = _pad_rows(w_in.T.astype(bf16), 4096 + BF16_ROWS)
    aq, ak, av, asg, bq, bk, bv, bsg = _proj_even(
        x, ln_g.astype(f32).reshape(1, -1), wt, colv(a_qn_g), colv(a_kn_g), colv(b_qn_g),
        colv(b_kn_g), colv(b_f_bias))
    lam_init = 0.8 - 0.6 * math.exp(-0.3 * layer_idx)
    qaug_a, dtab = _a_tables()
    lamv = jnp.stack([a_lq1, a_lk1, a_lq2, a_lk2]).astype(f32)
    mix_a = _attn_a(aq, ak, av, asg, dtab, qaug_a, colv(a_subln_g), lamv, lam_init)
    qaug_b = np.zeros((KPAD - HEAD_DIM, 1), np.float32)
    qaug_b[:3] = 1.0
    mix_b = _attn_b(bq, bk, bv, bsg, jnp.asarray(qaug_b))
    return _out_proj(mix_a.reshape(bsz, -1, seq), mix_b.reshape(bsz, -1, seq),
                     w_out.T.astype(bf16), x)


def _odd_layer(x, ln_g, w_in, w_out, c_qn_g, c_kn_g, c_sinks, d_qn_g, d_kn_g, d_rel_bias):
    bsz, seq, _ = x.shape
    colv = lambda v: v.astype(f32).reshape(-1, 1)
    cq, ck, cv, csg, dq, dk, dv, dsg = _proj_odd(
        x, ln_g.astype(f32).reshape(1, -1), w_in.T.astype(bf16), colv(c_qn_g), colv(c_kn_g),
        colv(d_qn_g), colv(d_kn_g))
    tab_c, sink_c, back_c = _c_tables(c_sinks)
    mix_c = _band_attn(cq, ck, cv, csg, tab_c, sink_c, C_GROUP, back_c, "attn_c")
    tab_d, sink_d, back_d = _d_tables(d_rel_bias)
    mix_d = _band_attn(dq, dk, dv, dsg, tab_d, sink_d, 1, back_d, "attn_d")
    return _out_proj(mix_c.reshape(bsz, -1, seq), mix_d.reshape(bsz, -1, seq),
                     w_out.T.astype(bf16), x)


def kernel(x, even_ln_g, even_w_in, even_w_out, a_q_norm_g, a_k_norm_g, a_lambda_q1, a_lambda_k1, a_lambda_q2, a_lambda_k2, a_subln_g, b_q_norm_g, b_k_norm_g, b_forget_bias, odd_ln_g, odd_w_in, odd_w_out, c_q_norm_g, c_k_norm_g, c_sinks, d_q_norm_g, d_k_norm_g, d_rel_bias):
    depth = even_ln_g.shape[0] + odd_ln_g.shape[0]
    for i in range(depth):
        j = i // 2
        if i % 2 == 0:
            x = _even_layer(x, even_ln_g[j], even_w_in[j], even_w_out[j], a_q_norm_g[j],
                            a_k_norm_g[j], a_lambda_q1[j], a_lambda_k1[j], a_lambda_q2[j],
                            a_lambda_k2[j], a_subln_g[j], b_q_norm_g[j], b_k_norm_g[j],
                            b_forget_bias[j], i)
        else:
            x = _odd_layer(x, odd_ln_g[j], odd_w_in[j], odd_w_out[j], c_q_norm_g[j],
                           c_k_norm_g[j], c_sinks[j], d_q_norm_g[j], d_k_norm_g[j],
                           d_rel_bias[j])
    return x
```

```python
import functools
import math

import numpy as np
import jax
import jax.numpy as jnp
from jax import lax
from jax.experimental import pallas as pl
from jax.experimental.pallas import tpu as pltpu

D_MODEL = 1024
CHUNK = 64
HEAD_DIM = 64
NORM_EPS = 1e-6

A_HEADS = 4
A_STREAMS = 2 * A_HEADS
A_VDIM = 2 * HEAD_DIM
B_HEADS = 8
C_HEADS = 8
C_KV_HEADS = 2
C_GROUP = C_HEADS // C_KV_HEADS
WIN_CHUNKS = 2
D_HEADS = 8
D_LEFT_CHUNKS = 8
REL_MAX = 256

P_EVEN = 8 * 512 + B_HEADS
P_ODD = 512 + 128 + 128 + 512 + 4 * 512

LOG2E = 1.4426950408889634
QK_SCALE = HEAD_DIM ** -0.5 * LOG2E
NEG = -1e30

LANES = 128
KPAD = 128
BF16_ROWS = 16

PROJ_TOKENS = 512
ATT_TQ = 512
ATT_TK = 512
A_HEADS_PER_STEP = 2
B_HEADS_PER_STEP = 4
A_BIAS_ROWS = 6
B_BIAS_ROWS = 3
MAX_ROWS = 3
FIXED_MAX_LIMIT = 96.0
BAND_TQ = 128
VMEM_LIMIT = 56 * 1024 * 1024

f32 = jnp.float32
bf16 = jnp.bfloat16


def _split3(v):
    hi = v.astype(bf16).astype(f32)
    r = v - hi
    mid = r.astype(bf16).astype(f32)
    lo = (r - mid).astype(bf16).astype(f32)
    return hi, mid, lo


def _silu(z):
    return z * (1.0 / (1.0 + jnp.exp(-z)))


def _rms_rows(x_ref, g_ref):
    x = x_ref[0]
    ms = jnp.mean(x * x, axis=-1, keepdims=True)
    return (x * lax.rsqrt(ms + NORM_EPS) * g_ref[...]).astype(bf16)


def _proj_t(wt_ref, r0, r1, xn):
    return lax.dot_general(wt_ref[r0:r1, :], xn, (((1,), (1,)), ((), ())),
                           preferred_element_type=f32)


def _head_norm(z_t, gain_col, mult):
    n = z_t.shape[0] // HEAD_DIM
    z3 = z_t.reshape(n, HEAD_DIM, z_t.shape[1])
    ms = jnp.mean(z3 * z3, axis=1, keepdims=True)
    return z3 * lax.rsqrt(ms + NORM_EPS) * (gain_col[...] * mult)[None]


def _ones_rows(row, first):
    return jnp.where((row >= first) & (row < first + 3), 1.0, 0.0)


def _store_heads(o_ref, z_t):
    o_ref[0] = z_t.reshape(o_ref.shape[1], o_ref.shape[2], z_t.shape[1]).astype(bf16)


def _store_keys(k_ref, kn, aug_fn):
    n, _, t = kn.shape
    zeros = jnp.zeros((KPAD - HEAD_DIM - BF16_ROWS, t), f32)
    for s in range(n):
        blk = jnp.concatenate([kn[s], aug_fn(s), zeros], axis=0)
        k_ref[0, s] = blk.T.astype(bf16)


def _proj_even_kernel(x_ref, lng_ref, wt_ref, aqg_ref, akg_ref, bqg_ref, bkg_ref, bfb_ref,
                      aq_ref, ak_ref, av_ref, asg_ref, bq_ref, bk_ref, bv_ref, bsg_ref,
                      cum_ref):
    t = pl.program_id(1)
    tt = x_ref.shape[1]
    xn = _rms_rows(x_ref, lng_ref)

    aq_ref[0] = _head_norm(_proj_t(wt_ref, 0, 512, xn), aqg_ref, QK_SCALE).astype(bf16)

    row = lax.broadcasted_iota(jnp.int32, (BF16_ROWS, tt), 0)
    pos = t * tt + lax.broadcasted_iota(jnp.int32, (BF16_ROWS, tt), 1)
    pos_a = lax.shift_right_logical(pos, int(math.log2(CHUNK))).astype(f32)
    pos_b = lax.bitwise_and(pos, CHUNK - 1).astype(f32)
    aug_a = jnp.where(row < 3, pos_a, jnp.where(row < 6, pos_b, _ones_rows(row, A_BIAS_ROWS)))
    akn = _head_norm(_proj_t(wt_ref, 512, 1024, xn), akg_ref, 1.0)
    _store_keys(ak_ref, akn, lambda s: aug_a)

    _store_heads(av_ref, _proj_t(wt_ref, 1024, 1536, xn))
    _store_heads(asg_ref, _silu(_proj_t(wt_ref, 1536, 2048, xn)))

    bq_ref[0] = _head_norm(_proj_t(wt_ref, 2048, 2560, xn), bqg_ref, QK_SCALE).astype(bf16)

    z = _proj_t(wt_ref, 4096, 4096 + BF16_ROWS, xn)[:B_HEADS] + bfb_ref[...]
    log_f = jnp.minimum(z, 0.0) - jnp.log(1.0 + jnp.exp(-jnp.abs(z)))
    tri = jnp.where(lax.broadcasted_iota(jnp.int32, (tt, tt), 0)
                    <= lax.broadcasted_iota(jnp.int32, (tt, tt), 1), 1.0, 0.0).astype(bf16)
    pieces = jnp.concatenate(_split3(log_f) + (jnp.zeros_like(log_f),), axis=0).astype(bf16)
    part = jnp.dot(pieces, tri, preferred_element_type=f32)
    local = part[:B_HEADS] + part[B_HEADS:2 * B_HEADS] + part[2 * B_HEADS:3 * B_HEADS]

    @pl.when(t == 0)
    def _():
        cum_ref[...] = jnp.zeros_like(cum_ref)

    cum = cum_ref[...] + local
    cum_ref[...] = cum[:, tt - 1:tt]
    g_hi, g_mid, g_lo = _split3(-LOG2E * cum)

    def aug_b(s):
        pick = lambda a: jnp.broadcast_to(a[s:s + 1], (BF16_ROWS, tt))
        return jnp.where(row == 0, pick(g_hi),
                         jnp.where(row == 1, pick(g_mid),
                                   jnp.where(row == 2, pick(g_lo), _ones_rows(row, B_BIAS_ROWS))))

    bkn = _head_norm(_proj_t(wt_ref, 2560, 3072, xn), bkg_ref, 1.0)
    _store_keys(bk_ref, bkn, aug_b)

    _store_heads(bv_ref, _proj_t(wt_ref, 3072, 3584, xn))
    _store_heads(bsg_ref, _silu(_proj_t(wt_ref, 3584, 4096, xn)))


def _proj_even(x, ln_g, wt, aqg, akg, bqg, bkg, bfb):
    bsz, seq, _ = x.shape
    tt = PROJ_TOKENS
    col = lambda n: pl.BlockSpec((n, 1), lambda b, t: (0, 0))
    fm = lambda n, d: pl.BlockSpec((1, n, d, tt), lambda b, t: (b, 0, 0, t))
    km = lambda n: pl.BlockSpec((1, n, tt, KPAD), lambda b, t: (b, 0, t, 0))
    fm_shape = lambda n, d: jax.ShapeDtypeStruct((bsz, n, d, seq), bf16)
    km_shape = lambda n: jax.ShapeDtypeStruct((bsz, n, seq, KPAD), bf16)
    return pl.pallas_call(
        _proj_even_kernel,
        grid=(bsz, seq // tt),
        in_specs=[
            pl.BlockSpec((1, tt, D_MODEL), lambda b, t: (b, t, 0)),
            pl.BlockSpec((1, D_MODEL), lambda b, t: (0, 0)),
            pl.BlockSpec(wt.shape, lambda b, t: (0, 0)),
            col(HEAD_DIM), col(HEAD_DIM), col(HEAD_DIM), col(HEAD_DIM), col(B_HEADS),
        ],
        out_specs=[fm(A_STREAMS, HEAD_DIM), km(A_STREAMS), fm(A_HEADS, A_VDIM), fm(A_HEADS, A_VDIM),
                   fm(B_HEADS, HEAD_DIM), km(B_HEADS), fm(B_HEADS, HEAD_DIM), fm(B_HEADS, HEAD_DIM)],
        out_shape=[fm_shape(A_STREAMS, HEAD_DIM), km_shape(A_STREAMS), fm_shape(A_HEADS, A_VDIM),
                   fm_shape(A_HEADS, A_VDIM), fm_shape(B_HEADS, HEAD_DIM), km_shape(B_HEADS),
                   fm_shape(B_HEADS, HEAD_DIM), fm_shape(B_HEADS, HEAD_DIM)],
        scratch_shapes=[pltpu.VMEM((B_HEADS, 1), f32)],
        compiler_params=pltpu.CompilerParams(
            dimension_semantics=("arbitrary", "arbitrary"), vmem_limit_bytes=VMEM_LIMIT),
        name="proj_even",
    )(x, ln_g, wt, aqg, akg, bqg, bkg, bfb)


def _proj_odd_kernel(x_ref, lng_ref, wt_ref, cqg_ref, ckg_ref, dqg_ref, dkg_ref,
                     cq_ref, ck_ref, cv_ref, csg_ref, dq_ref, dk_ref, dv_ref, dsg_ref):
    tt = x_ref.shape[1]
    xn = _rms_rows(x_ref, lng_ref)
    no_aug = lambda s: jnp.zeros((BF16_ROWS, tt), f32)

    cq_ref[0] = _head_norm(_proj_t(wt_ref, 0, 512, xn), cqg_ref, QK_SCALE).astype(bf16)
    _store_keys(ck_ref, _head_norm(_proj_t(wt_ref, 512, 640, xn), ckg_ref, 1.0), no_aug)
    _store_heads(cv_ref, _proj_t(wt_ref, 640, 768, xn))
    _store_heads(csg_ref, _silu(_proj_t(wt_ref, 768, 1280, xn)))
    dq_ref[0] = _head_norm(_proj_t(wt_ref, 1280, 1792, xn), dqg_ref, QK_SCALE).astype(bf16)
    _store_keys(dk_ref, _head_norm(_proj_t(wt_ref, 1792, 2304, xn), dkg_ref, 1.0), no_aug)
    _store_heads(dv_ref, _proj_t(wt_ref, 2304, 2816, xn))
    _store_heads(dsg_ref, _silu(_proj_t(wt_ref, 2816, 3328, xn)))


def _proj_odd(x, ln_g, wt, cqg, ckg, dqg, dkg):
    bsz, seq, _ = x.shape
    tt = PROJ_TOKENS
    col = lambda n: pl.BlockSpec((n, 1), lambda b, t: (0, 0))
    fm = lambda n: pl.BlockSpec((1, n, HEAD_DIM, tt), lambda b, t: (b, 0, 0, t))
    km = lambda n: pl.BlockSpec((1, n, tt, KPAD), lambda b, t: (b, 0, t, 0))
    fm_shape = lambda n: jax.ShapeDtypeStruct((bsz, n, HEAD_DIM, seq), bf16)
    km_shape = lambda n: jax.ShapeDtypeStruct((bsz, n, seq, KPAD), bf16)
    return pl.pallas_call(
        _proj_odd_kernel,
        grid=(bsz, seq // tt),
        in_specs=[
            pl.BlockSpec((1, tt, D_MODEL), lambda b, t: (b, t, 0)),
            pl.BlockSpec((1, D_MODEL), lambda b, t: (0, 0)),
            pl.BlockSpec(wt.shape, lambda b, t: (0, 0)),
            col(HEAD_DIM), col(HEAD_DIM), col(HEAD_DIM), col(HEAD_DIM),
        ],
        out_specs=[fm(C_HEADS), km(C_KV_HEADS), fm(C_KV_HEADS), fm(C_HEADS),
                   fm(D_HEADS), km(D_HEADS), fm(D_HEADS), fm(D_HEADS)],
        out_shape=[fm_shape(C_HEADS), km_shape(C_KV_HEADS), fm_shape(C_KV_HEADS), fm_shape(C_HEADS),
                   fm_shape(D_HEADS), km_shape(D_HEADS), fm_shape(D_HEADS), fm_shape(D_HEADS)],
        compiler_params=pltpu.CompilerParams(
            dimension_semantics=("arbitrary", "arbitrary"), vmem_limit_bytes=VMEM_LIMIT),
        name="proj_odd",
    )(x, ln_g, wt, cqg, ckg, dqg, dkg)


def _online_step(carry, s, v):
    m, l, acc = carry
    m_new = jnp.maximum(m, jnp.max(s, axis=0, keepdims=True))
    p = jnp.exp2(s - m_new)
    alpha = jnp.exp2(m - m_new)
    l = alpha * l + jnp.sum(p, axis=0, keepdims=True)
    acc = alpha * acc + jnp.dot(v, p.astype(bf16), preferred_element_type=f32)
    return m_new, l, acc


def _colsum8(p):
    return p.reshape(p.shape[0] // 8, 8, p.shape[1]).sum(axis=0)


def _augment_q(q, aug_col, n_bias, m=None):
    tq = q.shape[1]
    aug = jnp.broadcast_to(aug_col, (KPAD - HEAD_DIM, tq))
    if m is not None:
        row = lax.broadcasted_iota(jnp.int32, aug.shape, 0)
        for r, piece in enumerate(_split3(-m)):
            aug = jnp.where(row == n_bias + r, piece, aug)
    return jnp.concatenate([q, aug.astype(bf16)], axis=0)


def _causal_sweep(chains, i, fixed_max):
    tq = chains[0][2].shape[1]
    d0 = pl.multiple_of(i * tq, tq)
    n_tiles = i * (tq // ATT_TK)
    tile = lambda j: (pl.multiple_of(j * ATT_TK, ATT_TK), ATT_TK)
    scores = [jnp.dot(k_at(d0, tq), _augment_q(q, aug_col, n_bias), preferred_element_type=f32)
              + diag_bias for k_at, _, q, aug_col, n_bias, diag_bias in chains]
    maxes = [jnp.max(s, axis=0, keepdims=True) for s in scores]
    probs = [jnp.exp2(s - m) for s, m in zip(scores, maxes)]
    diag = [(m, _colsum8(p), jnp.dot(v_at(d0, tq), p.astype(bf16), preferred_element_type=f32))
            for (_, v_at, *_), m, p in zip(chains, maxes, probs)]

    if fixed_max:
        q_aug = [_augment_q(q, aug_col, n_bias, m)
                 for (_, _, q, aug_col, n_bias, _), (m, _, _) in zip(chains, diag)]

        def body(j, carries):
            scores = [jnp.dot(k_at(*tile(j)), qa, preferred_element_type=f32)
                      for (k_at, *_), qa in zip(chains, q_aug)]
            probs = [jnp.exp2(s) for s in scores]
            return tuple((l + _colsum8(p),
                          acc + jnp.dot(v_at(*tile(j)), p.astype(bf16), preferred_element_type=f32))
                         for (_, v_at, *_), p, (l, acc) in zip(chains, probs, carries))

        carries = lax.fori_loop(0, n_tiles, body, tuple((l, acc) for _, l, acc in diag))
        return [(acc, jnp.sum(l, axis=0, keepdims=True)) for l, acc in carries]

    q_aug = [_augment_q(q, aug_col, n_bias) for _, _, q, aug_col, n_bias, _ in chains]

    def body(j, carries):
        scores = [jnp.dot(k_at(*tile(j)), qa, preferred_element_type=f32)
                  for (k_at, *_), qa in zip(chains, q_aug)]
        return tuple(_online_step(c, s, v_at(*tile(j)))
                     for (_, v_at, *_), s, c in zip(chains, scores, carries))

    init = tuple((m, jnp.sum(l, axis=0, keepdims=True), acc) for m, l, acc in diag)
    return [(acc, l) for _, l, acc in lax.fori_loop(0, n_tiles, body, init)]


def _either_sweep(fixed_ref, run):
    @pl.when(fixed_ref[0] != 0)
    def _():
        run(True)

    @pl.when(fixed_ref[0] == 0)
    def _():
        run(False)


def _attn_a_kernel(fixed_ref, q_ref, k_ref, v_ref, sg_ref, dtab_ref, qaug_ref, subg_ref, lamv_ref,
                   o_ref, *, lam_init):
    i = pl.program_id(2)
    chains = []
    for h in range(A_HEADS_PER_STEP):
        v_at = lambda start, size, h=h: v_ref[0, h, :, pl.ds(start, size)]
        for c in range(2):
            k_at = lambda start, size, s=2 * h + c: k_ref[0, s, pl.ds(start, size), :]
            chains.append((k_at, v_at, q_ref[0, 2 * h + c], qaug_ref[h], A_BIAS_ROWS, dtab_ref[h]))

    def run(fixed_max):
        lv = lamv_ref[...]
        lam = (jnp.exp(jnp.sum(lv[0:1] * lv[1:2], axis=1, keepdims=True))
               - jnp.exp(jnp.sum(lv[2:3] * lv[3:4], axis=1, keepdims=True)) + lam_init)
        outs = [acc * (1.0 / l) for acc, l in _causal_sweep(chains, i, fixed_max)]
        for h in range(A_HEADS_PER_STEP):
            o = outs[2 * h] - lam * outs[2 * h + 1]
            ms = jnp.mean(o * o, axis=0, keepdims=True)
            y = o * lax.rsqrt(ms + NORM_EPS) * (subg_ref[...] * (1.0 - lam_init))
            o_ref[0, h] = (y * sg_ref[0, h].astype(f32)).astype(bf16)

    _either_sweep(fixed_ref, run)


def _attn_a(fixed, aq, ak, av, asg, dtab, qaug, subg, lamv, lam_init):
    bsz, _, _, seq = aq.shape
    nq = seq // ATT_TQ
    hs = A_HEADS_PER_STEP
    return pl.pallas_call(
        functools.partial(_attn_a_kernel, lam_init=lam_init),
        grid=(bsz, A_HEADS // hs, nq),
        in_specs=[
            pl.BlockSpec(memory_space=pltpu.SMEM),
            pl.BlockSpec((1, 2 * hs, HEAD_DIM, ATT_TQ), lambda b, h, i: (b, h, 0, i)),
            pl.BlockSpec((1, 2 * hs, seq, KPAD), lambda b, h, i: (b, h, 0, 0)),
            pl.BlockSpec((1, hs, A_VDIM, seq), lambda b, h, i: (b, h, 0, 0)),
            pl.BlockSpec((1, hs, A_VDIM, ATT_TQ), lambda b, h, i: (b, h, 0, i)),
            pl.BlockSpec((hs, ATT_TQ, ATT_TQ), lambda b, h, i: (h, 0, 0)),
            pl.BlockSpec((hs, KPAD - HEAD_DIM, 1), lambda b, h, i: (h, 0, 0)),
            pl.BlockSpec((A_VDIM, 1), lambda b, h, i: (0, 0)),
            pl.BlockSpec((4, HEAD_DIM), lambda b, h, i: (0, 0)),
        ],
        out_specs=pl.BlockSpec((1, hs, A_VDIM, ATT_TQ), lambda b, h, i: (b, h, 0, i)),
        out_shape=jax.ShapeDtypeStruct((bsz, A_HEADS, A_VDIM, seq), bf16),
        compiler_params=pltpu.CompilerParams(
            dimension_semantics=("arbitrary", "arbitrary", "arbitrary"),
            vmem_limit_bytes=VMEM_LIMIT),
        name="attn_a",
    )(fixed, aq, ak, av, asg, dtab, qaug, subg, lamv)


def _attn_b_kernel(fixed_ref, q_ref, k_ref, v_ref, sg_ref, qaug_ref, o_ref):
    i = pl.program_id(2)

    def run(fixed_max):
        causal = jnp.where(lax.broadcasted_iota(jnp.int32, (ATT_TQ, ATT_TQ), 0)
                           <= lax.broadcasted_iota(jnp.int32, (ATT_TQ, ATT_TQ), 1), 0.0, NEG)
        chains = []
        for h in range(B_HEADS_PER_STEP):
            k_at = lambda start, size, h=h: k_ref[0, h, pl.ds(start, size), :]
            v_at = lambda start, size, h=h: v_ref[0, h, :, pl.ds(start, size)]
            chains.append((k_at, v_at, q_ref[0, h], qaug_ref[...], B_BIAS_ROWS, causal))
        for h, (acc, l) in enumerate(_causal_sweep(chains, i, fixed_max)):
            o_ref[0, h] = (acc * (1.0 / l) * sg_ref[0, h].astype(f32)).astype(bf16)

    _either_sweep(fixed_ref, run)


def _attn_b(fixed, bq, bk, bv, bsg, qaug):
    bsz, _, _, seq = bq.shape
    nq = seq // ATT_TQ
    hs = B_HEADS_PER_STEP
    return pl.pallas_call(
        _attn_b_kernel,
        grid=(bsz, B_HEADS // hs, nq),
        in_specs=[
            pl.BlockSpec(memory_space=pltpu.SMEM),
            pl.BlockSpec((1, hs, HEAD_DIM, ATT_TQ), lambda b, h, i: (b, h, 0, i)),
            pl.BlockSpec((1, hs, seq, KPAD), lambda b, h, i: (b, h, 0, 0)),
            pl.BlockSpec((1, hs, HEAD_DIM, seq), lambda b, h, i: (b, h, 0, 0)),
            pl.BlockSpec((1, hs, HEAD_DIM, ATT_TQ), lambda b, h, i: (b, h, 0, i)),
            pl.BlockSpec((KPAD - HEAD_DIM, 1), lambda b, h, i: (0, 0)),
        ],
        out_specs=pl.BlockSpec((1, hs, HEAD_DIM, ATT_TQ), lambda b, h, i: (b, h, 0, i)),
        out_shape=jax.ShapeDtypeStruct((bsz, B_HEADS, HEAD_DIM, seq), bf16),
        compiler_params=pltpu.CompilerParams(
            dimension_semantics=("arbitrary", "arbitrary", "arbitrary"),
            vmem_limit_bytes=VMEM_LIMIT),
        name="attn_b",
    )(fixed, bq, bk, bv, bsg, qaug)


def _band_kernel(q_ref, k_ref, v_ref, sg_ref, tab_ref, sink_ref, o_ref, *, group, back):
    i = pl.program_id(1)
    t = BAND_TQ
    band = (back + 1) * t
    k_start = pl.multiple_of(jnp.maximum(i - back, 0) * t, t)
    tab_start = pl.multiple_of(jnp.maximum(back - i, 0) * t, t)
    for h in range(k_ref.shape[1]):
        q = jnp.concatenate([q_ref[0, h * group + g] for g in range(group)], axis=1)
        q = jnp.concatenate([q, jnp.zeros((KPAD - HEAD_DIM, group * t), bf16)], axis=0)
        s = jnp.dot(k_ref[0, h, pl.ds(k_start, band), :], q, preferred_element_type=f32)
        s = s + tab_ref[h, pl.ds(tab_start, band), :]
        sink = sink_ref[h]
        m = jnp.maximum(jnp.max(s, axis=0, keepdims=True), sink)
        p = jnp.exp2(s - m)
        l = jnp.sum(p, axis=0, keepdims=True) + jnp.exp2(sink - m)
        o = jnp.dot(v_ref[0, h, :, pl.ds(k_start, band)], p.astype(bf16),
                    preferred_element_type=f32) * (1.0 / l)
        for g in range(group):
            hq = h * group + g
            o_ref[0, hq] = (o[:, g * t:(g + 1) * t] * sg_ref[0, hq].astype(f32)).astype(bf16)


def _band_attn(q, k, v, sg, tab, sink, group, back, name):
    bsz, nheads, _, seq = q.shape
    nkv = nheads // group
    t = BAND_TQ
    whole = lambda a: pl.BlockSpec(a.shape, lambda b, i: (0,) * a.ndim)
    return pl.pallas_call(
        functools.partial(_band_kernel, group=group, back=back),
        grid=(bsz, seq // t),
        in_specs=[
            pl.BlockSpec((1, nheads, HEAD_DIM, t), lambda b, i: (b, 0, 0, i)),
            pl.BlockSpec((1, nkv, seq, KPAD), lambda b, i: (b, 0, 0, 0)),
            pl.BlockSpec((1, nkv, HEAD_DIM, seq), lambda b, i: (b, 0, 0, 0)),
            pl.BlockSpec((1, nheads, HEAD_DIM, t), lambda b, i: (b, 0, 0, i)),
            whole(tab), whole(sink),
        ],
        out_specs=pl.BlockSpec((1, nheads, HEAD_DIM, t), lambda b, i: (b, 0, 0, i)),
        out_shape=jax.ShapeDtypeStruct((bsz, nheads, HEAD_DIM, seq), bf16),
        compiler_params=pltpu.CompilerParams(
            dimension_semantics=("arbitrary", "arbitrary"), vmem_limit_bytes=VMEM_LIMIT),
        name=name,
    )(q, k, v, sg, tab, sink)


def _out_proj_kernel(m1_ref, m2_ref, wt_ref, x_ref, o_ref):
    half = m1_ref.shape[1]
    y_t = (jnp.dot(wt_ref[:, :half], m1_ref[0], preferred_element_type=f32)
           + jnp.dot(wt_ref[:, half:], m2_ref[0], preferred_element_type=f32))
    o_ref[0] = x_ref[0] + y_t.T


def _out_proj(m1, m2, wt, x):
    bsz, seq, _ = x.shape
    tt = PROJ_TOKENS
    half = m1.shape[1]
    return pl.pallas_call(
        _out_proj_kernel,
        grid=(bsz, seq // tt),
        in_specs=[
            pl.BlockSpec((1, half, tt), lambda b, t: (b, 0, t)),
            pl.BlockSpec((1, half, tt), lambda b, t: (b, 0, t)),
            pl.BlockSpec(wt.shape, lambda b, t: (0, 0)),
            pl.BlockSpec((1, tt, D_MODEL), lambda b, t: (b, t, 0)),
        ],
        out_specs=pl.BlockSpec((1, tt, D_MODEL), lambda b, t: (b, t, 0)),
        out_shape=jax.ShapeDtypeStruct(x.shape, f32),
        compiler_params=pltpu.CompilerParams(
            dimension_semantics=("arbitrary", "arbitrary"), vmem_limit_bytes=VMEM_LIMIT),
        name="out_proj",
    )(m1, m2, wt, x)


def _alibi_slopes(n):
    return 2.0 ** (-8.0 * np.arange(1, n + 1, dtype=np.float64) / n)


def _np_split3(v):
    v = np.asarray(v, np.float32)
    to_bf = lambda a: a.astype(bf16).astype(np.float32)
    hi = to_bf(v)
    mid = to_bf(v - hi)
    lo = to_bf(v - hi - mid)
    return hi, mid, lo


def _a_tables():
    rate = (_alibi_slopes(A_HEADS) * LOG2E).astype(np.float32)
    qaug = np.zeros((A_HEADS, KPAD - HEAD_DIM, 1), np.float32)
    for idx, piece in enumerate(_np_split3(rate * CHUNK) + _np_split3(rate)):
        qaug[:, idx, 0] = piece
    kk = np.arange(ATT_TQ)[:, None]
    qq = np.arange(ATT_TQ)[None, :]
    future = np.maximum(kk - qq, 0).astype(np.float32)
    corr = -2.0 * rate[:, None, None] * future[None]
    allowed = (kk // CHUNK) <= (qq // CHUNK)
    dtab = np.where(allowed[None], corr, NEG).astype(np.float32)
    return jnp.asarray(qaug), jnp.asarray(dtab)


def _band_frames(back):
    t = BAND_TQ
    k_pos = np.arange((back + 1) * t)[:, None]
    q_pos = back * t + np.arange(t)[None, :]
    chunk_diff = q_pos // CHUNK - k_pos // CHUNK
    return q_pos - k_pos, chunk_diff


def _c_tables(sinks):
    back = 1
    rel, chunk_diff = _band_frames(back)
    allowed = (chunk_diff >= 0) & (chunk_diff <= WIN_CHUNKS)
    slopes = _alibi_slopes(C_HEADS)
    per_head = np.where(allowed[None], -slopes[:, None, None] * np.abs(rel)[None] * LOG2E, NEG)
    tab = per_head.reshape(C_KV_HEADS, C_GROUP, *rel.shape).transpose(0, 2, 1, 3)
    tab = tab.reshape(C_KV_HEADS, rel.shape[0], C_GROUP * BAND_TQ).astype(np.float32)
    tab = np.concatenate([tab, np.full((C_KV_HEADS, back * BAND_TQ, tab.shape[2]), NEG, np.float32)], 1)
    sink = jnp.repeat(sinks.astype(f32) * LOG2E, BAND_TQ).reshape(C_KV_HEADS, 1, C_GROUP * BAND_TQ)
    return jnp.asarray(tab), sink, back


def _d_tables(rel_table):
    back = D_LEFT_CHUNKS * CHUNK // BAND_TQ
    t = BAND_TQ
    band = (back + 1) * t
    rel, chunk_diff = _band_frames(back)
    allowed = (chunk_diff >= 0) & (chunk_diff <= D_LEFT_CHUNKS)
    tbl = rel_table.astype(f32) * LOG2E
    n_lo = (t - 1) - (CHUNK - 1)
    n_hi = (band - 1) - REL_MAX
    diag = jnp.concatenate([jnp.repeat(tbl[:, :1], n_lo, axis=1), tbl,
                            jnp.repeat(tbl[:, -1:], n_hi, axis=1)], axis=1)
    m = diag.shape[1]
    skew = jnp.tile(diag, (1, band + 1))[:, :band * (m + 1)].reshape(D_HEADS, band, m + 1)[:, :, :t]
    bias = jnp.flip(skew, axis=1)
    tab = jnp.where(jnp.asarray(allowed)[None], bias, NEG)
    tab = jnp.concatenate([tab, jnp.full((D_HEADS, back * t, t), NEG, f32)], axis=1)
    sink = jnp.full((D_HEADS, 1, t), NEG, f32)
    return tab, sink, back


def _fixed_max_ok(q_gain, k_gain):
    spread = (2.0 * 1.02 * QK_SCALE * HEAD_DIM
              * jnp.max(jnp.abs(q_gain.astype(f32))) * jnp.max(jnp.abs(k_gain.astype(f32))))
    return (spread <= FIXED_MAX_LIMIT).astype(jnp.int32).reshape(1)


def _pad_rows(w_t, rows):
    return jnp.pad(w_t, ((0, rows - w_t.shape[0]), (0, 0)))


def _even_layer(x, ln_g, w_in, w_out, a_qn_g, a_kn_g, a_lq1, a_lk1, a_lq2, a_lk2, a_subln_g,
                b_qn_g, b_kn_g, b_f_bias, layer_idx):
    bsz, seq, _ = x.shape
    colv = lambda v: v.astype(f32).reshape(-1, 1)
    wt = _pad_rows(w_in.T.astype(bf16), 4096 + BF16_ROWS)
    aq, ak, av, asg, bq, bk, bv, bsg = _proj_even(
        x, ln_g.astype(f32).reshape(1, -1), wt, colv(a_qn_g), colv(a_kn_g), colv(b_qn_g),
        colv(b_kn_g), colv(b_f_bias))
    lam_init = 0.8 - 0.6 * math.exp(-0.3 * layer_idx)
    qaug_a, dtab = _a_tables()
    lamv = jnp.stack([a_lq1, a_lk1, a_lq2, a_lk2]).astype(f32)
    mix_a = _attn_a(_fixed_max_ok(a_qn_g, a_kn_g), aq, ak, av, asg, dtab, qaug_a,
                    colv(a_subln_g), lamv, lam_init)
    qaug_b = np.zeros((KPAD - HEAD_DIM, 1), np.float32)
    qaug_b[:B_BIAS_ROWS] = 1.0
    mix_b = _attn_b(_fixed_max_ok(b_qn_g, b_kn_g), bq, bk, bv, bsg, jnp.asarray(qaug_b))
    return _out_proj(mix_a.reshape(bsz, -1, seq), mix_b.reshape(bsz, -1, seq),
                     w_out.T.astype(bf16), x)


def _odd_layer(x, ln_g, w_in, w_out, c_qn_g, c_kn_g, c_sinks, d_qn_g, d_kn_g, d_rel_bias):
    bsz, seq, _ = x.shape
    colv = lambda v: v.astype(f32).reshape(-1, 1)
    cq, ck, cv, csg, dq, dk, dv, dsg = _proj_odd(
        x, ln_g.astype(f32).reshape(1, -1), w_in.T.astype(bf16), colv(c_qn_g), colv(c_kn_g),
        colv(d_qn_g), colv(d_kn_g))
    tab_c, sink_c, back_c = _c_tables(c_sinks)
    mix_c = _band_attn(cq, ck, cv, csg, tab_c, sink_c, C_GROUP, back_c, "attn_c")
    tab_d, sink_d, back_d = _d_tables(d_rel_bias)
    mix_d = _band_attn(dq, dk, dv, dsg, tab_d, sink_d, 1, back_d, "attn_d")
    return _out_proj(mix_c.reshape(bsz, -1, seq), mix_d.reshape(bsz, -1, seq),
                     w_out.T.astype(bf16), x)


def kernel(x, even_ln_g, even_w_in, even_w_out, a_q_norm_g, a_k_norm_g, a_lambda_q1, a_lambda_k1, a_lambda_q2, a_lambda_k2, a_subln_g, b_q_norm_g, b_k_norm_g, b_forget_bias, odd_ln_g, odd_w_in, odd_w_out, c_q_norm_g, c_k_norm_g, c_sinks, d_q_norm_g, d_k_norm_g, d_rel_bias):
    depth = even_ln_g.shape[0] + odd_ln_g.shape[0]
    for i in range(depth):
        j = i // 2
        if i % 2 == 0:
            x = _even_layer(x, even_ln_g[j], even_w_in[j], even_w_out[j], a_q_norm_g[j],
                            a_k_norm_g[j], a_lambda_q1[j], a_lambda_k1[j], a_lambda_q2[j],
                            a_lambda_k2[j], a_subln_g[j], b_q_norm_g[j], b_k_norm_g[j],
                            b_forget_bias[j], i)
        else:
            x = _odd_layer(x, odd_ln_g[j], odd_w_in[j], odd_w_out[j], c_q_norm_g[j],
                           c_k_norm_g[j], c_sinks[j], d_q_norm_g[j], d_k_norm_g[j],
                           d_rel_bias[j])
    return x
```

```python
import functools
import math

import numpy as np
import jax
import jax.numpy as jnp
from jax import lax
from jax.experimental import pallas as pl
from jax.experimental.pallas import tpu as pltpu

D_MODEL = 1024
CHUNK = 64
HEAD_DIM = 64
NORM_EPS = 1e-6

A_HEADS = 4
A_STREAMS = 2 * A_HEADS
A_VDIM = 2 * HEAD_DIM
B_HEADS = 8
C_HEADS = 8
C_KV_HEADS = 2
C_GROUP = C_HEADS // C_KV_HEADS
WIN_CHUNKS = 2
D_HEADS = 8
D_LEFT_CHUNKS = 8
REL_MAX = 256

P_EVEN = 8 * 512 + B_HEADS
P_ODD = 512 + 128 + 128 + 512 + 4 * 512

LOG2E = 1.4426950408889634
QK_SCALE = HEAD_DIM ** -0.5 * LOG2E
NEG = -1e30

LANES = 128
KPAD = 128
BF16_ROWS = 16

PROJ_TOKENS = 512
ATT_TQ = 512
ATT_TK = 512
A_HEADS_PER_STEP = 2
B_HEADS_PER_STEP = 4
A_BIAS_ROWS = 6
B_BIAS_ROWS = 3
MAX_ROWS = 3
FIXED_MAX_LIMIT = 96.0
C_BAND_TQ = 128
D_BAND_TQ = 256
D_HEADS_PER_STEP = 4
VMEM_LIMIT = 56 * 1024 * 1024

f32 = jnp.float32
bf16 = jnp.bfloat16


def _split3(v):
    hi = v.astype(bf16).astype(f32)
    r = v - hi
    mid = r.astype(bf16).astype(f32)
    lo = (r - mid).astype(bf16).astype(f32)
    return hi, mid, lo


def _silu(z):
    return z * (1.0 / (1.0 + jnp.exp(-z)))


def _rms_rows(x_ref, g_ref):
    x = x_ref[0]
    ms = jnp.mean(x * x, axis=-1, keepdims=True)
    return (x * lax.rsqrt(ms + NORM_EPS) * g_ref[...]).astype(bf16)


def _proj_t(wt_ref, r0, r1, xn):
    return lax.dot_general(wt_ref[r0:r1, :], xn, (((1,), (1,)), ((), ())),
                           preferred_element_type=f32)


def _head_norm(z_t, gain_col, mult):
    n = z_t.shape[0] // HEAD_DIM
    z3 = z_t.reshape(n, HEAD_DIM, z_t.shape[1])
    ms = jnp.mean(z3 * z3, axis=1, keepdims=True)
    return z3 * lax.rsqrt(ms + NORM_EPS) * (gain_col[...] * mult)[None]


def _ones_rows(row, first):
    return jnp.where((row >= first) & (row < first + 3), 1.0, 0.0)


def _store_heads(o_ref, z_t):
    o_ref[0] = z_t.reshape(o_ref.shape[1], o_ref.shape[2], z_t.shape[1]).astype(bf16)


def _store_keys(k_ref, kn, aug_fn):
    n, _, t = kn.shape
    zeros = jnp.zeros((KPAD - HEAD_DIM - BF16_ROWS, t), f32)
    for s in range(n):
        blk = jnp.concatenate([kn[s], aug_fn(s), zeros], axis=0)
        k_ref[0, s] = blk.T.astype(bf16)


def _proj_even_kernel(x_ref, lng_ref, wt_ref, aqg_ref, akg_ref, bqg_ref, bkg_ref, bfb_ref,
                      aq_ref, ak_ref, av_ref, asg_ref, bq_ref, bk_ref, bv_ref, bsg_ref,
                      cum_ref):
    t = pl.program_id(1)
    tt = x_ref.shape[1]
    xn = _rms_rows(x_ref, lng_ref)

    aq_ref[0] = _head_norm(_proj_t(wt_ref, 0, 512, xn), aqg_ref, QK_SCALE).astype(bf16)

    row = lax.broadcasted_iota(jnp.int32, (BF16_ROWS, tt), 0)
    pos = t * tt + lax.broadcasted_iota(jnp.int32, (BF16_ROWS, tt), 1)
    pos_a = lax.shift_right_logical(pos, int(math.log2(CHUNK))).astype(f32)
    pos_b = lax.bitwise_and(pos, CHUNK - 1).astype(f32)
    aug_a = jnp.where(row < 3, pos_a, jnp.where(row < 6, pos_b, _ones_rows(row, A_BIAS_ROWS)))
    akn = _head_norm(_proj_t(wt_ref, 512, 1024, xn), akg_ref, 1.0)
    _store_keys(ak_ref, akn, lambda s: aug_a)

    _store_heads(av_ref, _proj_t(wt_ref, 1024, 1536, xn))
    _store_heads(asg_ref, _silu(_proj_t(wt_ref, 1536, 2048, xn)))

    bq_ref[0] = _head_norm(_proj_t(wt_ref, 2048, 2560, xn), bqg_ref, QK_SCALE).astype(bf16)

    z = _proj_t(wt_ref, 4096, 4096 + BF16_ROWS, xn)[:B_HEADS] + bfb_ref[...]
    log_f = jnp.minimum(z, 0.0) - jnp.log(1.0 + jnp.exp(-jnp.abs(z)))
    tri = jnp.where(lax.broadcasted_iota(jnp.int32, (tt, tt), 0)
                    <= lax.broadcasted_iota(jnp.int32, (tt, tt), 1), 1.0, 0.0).astype(bf16)
    pieces = jnp.concatenate(_split3(log_f) + (jnp.zeros_like(log_f),), axis=0).astype(bf16)
    part = jnp.dot(pieces, tri, preferred_element_type=f32)
    local = part[:B_HEADS] + part[B_HEADS:2 * B_HEADS] + part[2 * B_HEADS:3 * B_HEADS]

    @pl.when(t == 0)
    def _():
        cum_ref[...] = jnp.zeros_like(cum_ref)

    cum = cum_ref[...] + local
    cum_ref[...] = cum[:, tt - 1:tt]
    g_hi, g_mid, g_lo = _split3(-LOG2E * cum)

    def aug_b(s):
        pick = lambda a: jnp.broadcast_to(a[s:s + 1], (BF16_ROWS, tt))
        return jnp.where(row == 0, pick(g_hi),
                         jnp.where(row == 1, pick(g_mid),
                                   jnp.where(row == 2, pick(g_lo), _ones_rows(row, B_BIAS_ROWS))))

    bkn = _head_norm(_proj_t(wt_ref, 2560, 3072, xn), bkg_ref, 1.0)
    _store_keys(bk_ref, bkn, aug_b)

    _store_heads(bv_ref, _proj_t(wt_ref, 3072, 3584, xn))
    _store_heads(bsg_ref, _silu(_proj_t(wt_ref, 3584, 4096, xn)))


def _proj_even(x, ln_g, wt, aqg, akg, bqg, bkg, bfb):
    bsz, seq, _ = x.shape
    tt = PROJ_TOKENS
    col = lambda n: pl.BlockSpec((n, 1), lambda b, t: (0, 0))
    fm = lambda n, d: pl.BlockSpec((1, n, d, tt), lambda b, t: (b, 0, 0, t))
    km = lambda n: pl.BlockSpec((1, n, tt, KPAD), lambda b, t: (b, 0, t, 0))
    fm_shape = lambda n, d: jax.ShapeDtypeStruct((bsz, n, d, seq), bf16)
    km_shape = lambda n: jax.ShapeDtypeStruct((bsz, n, seq, KPAD), bf16)
    return pl.pallas_call(
        _proj_even_kernel,
        grid=(bsz, seq // tt),
        in_specs=[
            pl.BlockSpec((1, tt, D_MODEL), lambda b, t: (b, t, 0)),
            pl.BlockSpec((1, D_MODEL), lambda b, t: (0, 0)),
            pl.BlockSpec(wt.shape, lambda b, t: (0, 0)),
            col(HEAD_DIM), col(HEAD_DIM), col(HEAD_DIM), col(HEAD_DIM), col(B_HEADS),
        ],
        out_specs=[fm(A_STREAMS, HEAD_DIM), km(A_STREAMS), fm(A_HEADS, A_VDIM), fm(A_HEADS, A_VDIM),
                   fm(B_HEADS, HEAD_DIM), km(B_HEADS), fm(B_HEADS, HEAD_DIM), fm(B_HEADS, HEAD_DIM)],
        out_shape=[fm_shape(A_STREAMS, HEAD_DIM), km_shape(A_STREAMS), fm_shape(A_HEADS, A_VDIM),
                   fm_shape(A_HEADS, A_VDIM), fm_shape(B_HEADS, HEAD_DIM), km_shape(B_HEADS),
                   fm_shape(B_HEADS, HEAD_DIM), fm_shape(B_HEADS, HEAD_DIM)],
        scratch_shapes=[pltpu.VMEM((B_HEADS, 1), f32)],
        compiler_params=pltpu.CompilerParams(
            dimension_semantics=("arbitrary", "arbitrary"), vmem_limit_bytes=VMEM_LIMIT),
        name="proj_even",
    )(x, ln_g, wt, aqg, akg, bqg, bkg, bfb)


def _proj_odd_kernel(x_ref, lng_ref, wt_ref, cqg_ref, ckg_ref, dqg_ref, dkg_ref,
                     cq_ref, ck_ref, cv_ref, csg_ref, dq_ref, dk_ref, dv_ref, dsg_ref):
    tt = x_ref.shape[1]
    xn = _rms_rows(x_ref, lng_ref)
    no_aug = lambda s: jnp.zeros((BF16_ROWS, tt), f32)

    cq_ref[0] = _head_norm(_proj_t(wt_ref, 0, 512, xn), cqg_ref, QK_SCALE).astype(bf16)
    _store_keys(ck_ref, _head_norm(_proj_t(wt_ref, 512, 640, xn), ckg_ref, 1.0), no_aug)
    _store_heads(cv_ref, _proj_t(wt_ref, 640, 768, xn))
    _store_heads(csg_ref, _silu(_proj_t(wt_ref, 768, 1280, xn)))
    dq_ref[0] = _head_norm(_proj_t(wt_ref, 1280, 1792, xn), dqg_ref, QK_SCALE).astype(bf16)
    _store_keys(dk_ref, _head_norm(_proj_t(wt_ref, 1792, 2304, xn), dkg_ref, 1.0), no_aug)
    _store_heads(dv_ref, _proj_t(wt_ref, 2304, 2816, xn))
    _store_heads(dsg_ref, _silu(_proj_t(wt_ref, 2816, 3328, xn)))


def _proj_odd(x, ln_g, wt, cqg, ckg, dqg, dkg):
    bsz, seq, _ = x.shape
    tt = PROJ_TOKENS
    col = lambda n: pl.BlockSpec((n, 1), lambda b, t: (0, 0))
    fm = lambda n: pl.BlockSpec((1, n, HEAD_DIM, tt), lambda b, t: (b, 0, 0, t))
    km = lambda n: pl.BlockSpec((1, n, tt, KPAD), lambda b, t: (b, 0, t, 0))
    fm_shape = lambda n: jax.ShapeDtypeStruct((bsz, n, HEAD_DIM, seq), bf16)
    km_shape = lambda n: jax.ShapeDtypeStruct((bsz, n, seq, KPAD), bf16)
    return pl.pallas_call(
        _proj_odd_kernel,
        grid=(bsz, seq // tt),
        in_specs=[
            pl.BlockSpec((1, tt, D_MODEL), lambda b, t: (b, t, 0)),
            pl.BlockSpec((1, D_MODEL), lambda b, t: (0, 0)),
            pl.BlockSpec(wt.shape, lambda b, t: (0, 0)),
            col(HEAD_DIM), col(HEAD_DIM), col(HEAD_DIM), col(HEAD_DIM),
        ],
        out_specs=[fm(C_HEADS), km(C_KV_HEADS), fm(C_KV_HEADS), fm(C_HEADS),
                   fm(D_HEADS), km(D_HEADS), fm(D_HEADS), fm(D_HEADS)],
        out_shape=[fm_shape(C_HEADS), km_shape(C_KV_HEADS), fm_shape(C_KV_HEADS), fm_shape(C_HEADS),
                   fm_shape(D_HEADS), km_shape(D_HEADS), fm_shape(D_HEADS), fm_shape(D_HEADS)],
        compiler_params=pltpu.CompilerParams(
            dimension_semantics=("arbitrary", "arbitrary"), vmem_limit_bytes=VMEM_LIMIT),
        name="proj_odd",
    )(x, ln_g, wt, cqg, ckg, dqg, dkg)


def _online_step(carry, s, v):
    m, l, acc = carry
    m_new = jnp.maximum(m, jnp.max(s, axis=0, keepdims=True))
    p = jnp.exp2(s - m_new)
    alpha = jnp.exp2(m - m_new)
    l = alpha * l + jnp.sum(p, axis=0, keepdims=True)
    acc = alpha * acc + jnp.dot(v, p.astype(bf16), preferred_element_type=f32)
    return m_new, l, acc


def _colsum8(p):
    return p.reshape(p.shape[0] // 8, 8, p.shape[1]).sum(axis=0)


def _augment_q(q, aug_col, n_bias, m=None):
    tq = q.shape[1]
    aug = jnp.broadcast_to(aug_col, (KPAD - HEAD_DIM, tq))
    if m is not None:
        row = lax.broadcasted_iota(jnp.int32, aug.shape, 0)
        for r, piece in enumerate(_split3(-m)):
            aug = jnp.where(row == n_bias + r, piece, aug)
    return jnp.concatenate([q, aug.astype(bf16)], axis=0)


def _staggered(n, scores, finish):
    out, pending = [], scores(0)
    for c in range(1, n):
        nxt = scores(c)
        out.append(finish(c - 1, pending))
        pending = nxt
    out.append(finish(n - 1, pending))
    return out


def _diag_full(chains, d0, tq):
    def scores(c):
        k_at, _, q, aug_col, n_bias, diag_bias = chains[c]
        return jnp.dot(k_at(d0, tq), _augment_q(q, aug_col, n_bias),
                       preferred_element_type=f32) + diag_bias

    def finish(c, s):
        m = jnp.max(s, axis=0, keepdims=True)
        p = jnp.exp2(s - m)
        return m, _colsum8(p), jnp.dot(chains[c][1](d0, tq), p.astype(bf16), preferred_element_type=f32)

    return _staggered(len(chains), scores, finish)


def _diag_halves(chains, d0, tq):
    h = tq // 2
    d1 = pl.multiple_of(d0 + h, h)

    def scores(c):
        k_at, _, q, aug_col, n_bias, diag_bias = chains[c]
        qa = _augment_q(q, aug_col, n_bias)
        bias = diag_bias[:h, :h]
        return (jnp.dot(k_at(d0, h), qa[:, :h], preferred_element_type=f32) + bias,
                jnp.dot(k_at(d1, h), qa[:, h:], preferred_element_type=f32) + bias)

    def finish(c, s):
        k_at, v_at, q, aug_col, n_bias, _ = chains[c]
        m0 = jnp.max(s[0], axis=0, keepdims=True)
        m1 = jnp.max(s[1], axis=0, keepdims=True)
        qa = _augment_q(q, aug_col, n_bias, jnp.concatenate([m0, m1], axis=1))
        p00 = jnp.exp2(s[0] - m0)
        p11 = jnp.exp2(s[1] - m1)
        p01 = jnp.exp2(jnp.dot(k_at(d0, h), qa[:, h:], preferred_element_type=f32))
        l = jnp.concatenate([_colsum8(p00), _colsum8(p01) + _colsum8(p11)], axis=1)
        a0 = jnp.dot(v_at(d0, h), jnp.concatenate([p00, p01], axis=1).astype(bf16),
                     preferred_element_type=f32)
        a1 = jnp.dot(v_at(d1, h), p11.astype(bf16), preferred_element_type=f32)
        return qa, l, jnp.concatenate([a0[:, :h], a0[:, h:] + a1], axis=1)

    return _staggered(len(chains), scores, finish)


def _causal_sweep(chains, i, fixed_max):
    n = len(chains)
    tq = chains[0][2].shape[1]
    d0 = pl.multiple_of(i * tq, tq)
    n_tiles = i * (tq // ATT_TK)
    tile = lambda j: (pl.multiple_of(j * ATT_TK, ATT_TK), ATT_TK)

    if fixed_max:
        diag = _diag_halves(chains, d0, tq)

        def body(j, carries):
            def finish(c, s):
                p = jnp.exp2(s)
                l, acc = carries[c]
                return (l + _colsum8(p), acc + jnp.dot(chains[c][1](*tile(j)), p.astype(bf16),
                                                       preferred_element_type=f32))

            scores = lambda c: jnp.dot(chains[c][0](*tile(j)), diag[c][0], preferred_element_type=f32)
            return tuple(_staggered(n, scores, finish))

        carries = lax.fori_loop(0, n_tiles, body, tuple((l, acc) for _, l, acc in diag))
        return [(acc, jnp.sum(l, axis=0, keepdims=True)) for l, acc in carries]

    q_aug = [_augment_q(q, aug_col, n_bias) for _, _, q, aug_col, n_bias, _ in chains]

    def body(j, carries):
        scores = lambda c: jnp.dot(chains[c][0](*tile(j)), q_aug[c], preferred_element_type=f32)
        finish = lambda c, s: _online_step(carries[c], s, chains[c][1](*tile(j)))
        return tuple(_staggered(n, scores, finish))

    init = tuple((m, jnp.sum(l, axis=0, keepdims=True), acc) for m, l, acc in _diag_full(chains, d0, tq))
    return [(acc, l) for _, l, acc in lax.fori_loop(0, n_tiles, body, init)]


def _either_sweep(fixed_ref, run):
    @pl.when(fixed_ref[0] != 0)
    def _():
        run(True)

    @pl.when(fixed_ref[0] == 0)
    def _():
        run(False)


def _attn_a_kernel(fixed_ref, q_ref, k_ref, v_ref, sg_ref, dtab_ref, qaug_ref, subg_ref, lamv_ref,
                   o_ref, *, lam_init):
    i = pl.program_id(2)
    chains = []
    for h in range(A_HEADS_PER_STEP):
        v_at = lambda start, size, h=h: v_ref[0, h, :, pl.ds(start, size)]
        for c in range(2):
            k_at = lambda start, size, s=2 * h + c: k_ref[0, s, pl.ds(start, size), :]
            chains.append((k_at, v_at, q_ref[0, 2 * h + c], qaug_ref[h], A_BIAS_ROWS, dtab_ref[h]))

    def run(fixed_max):
        lv = lamv_ref[...]
        lam = (jnp.exp(jnp.sum(lv[0:1] * lv[1:2], axis=1, keepdims=True))
               - jnp.exp(jnp.sum(lv[2:3] * lv[3:4], axis=1, keepdims=True)) + lam_init)
        outs = [acc * (1.0 / l) for acc, l in _causal_sweep(chains, i, fixed_max)]
        for h in range(A_HEADS_PER_STEP):
            o = outs[2 * h] - lam * outs[2 * h + 1]
            ms = jnp.mean(o * o, axis=0, keepdims=True)
            y = o * lax.rsqrt(ms + NORM_EPS) * (subg_ref[...] * (1.0 - lam_init))
            o_ref[0, h] = (y * sg_ref[0, h].astype(f32)).astype(bf16)

    _either_sweep(fixed_ref, run)


def _attn_a(fixed, aq, ak, av, asg, dtab, qaug, subg, lamv, lam_init):
    bsz, _, _, seq = aq.shape
    nq = seq // ATT_TQ
    hs = A_HEADS_PER_STEP
    return pl.pallas_call(
        functools.partial(_attn_a_kernel, lam_init=lam_init),
        grid=(bsz, A_HEADS // hs, nq),
        in_specs=[
            pl.BlockSpec(memory_space=pltpu.SMEM),
            pl.BlockSpec((1, 2 * hs, HEAD_DIM, ATT_TQ), lambda b, h, i: (b, h, 0, i)),
            pl.BlockSpec((1, 2 * hs, seq, KPAD), lambda b, h, i: (b, h, 0, 0)),
            pl.BlockSpec((1, hs, A_VDIM, seq), lambda b, h, i: (b, h, 0, 0)),
            pl.BlockSpec((1, hs, A_VDIM, ATT_TQ), lambda b, h, i: (b, h, 0, i)),
            pl.BlockSpec((hs, ATT_TQ, ATT_TQ), lambda b, h, i: (h, 0, 0)),
            pl.BlockSpec((hs, KPAD - HEAD_DIM, 1), lambda b, h, i: (h, 0, 0)),
            pl.BlockSpec((A_VDIM, 1), lambda b, h, i: (0, 0)),
            pl.BlockSpec((4, HEAD_DIM), lambda b, h, i: (0, 0)),
        ],
        out_specs=pl.BlockSpec((1, hs, A_VDIM, ATT_TQ), lambda b, h, i: (b, h, 0, i)),
        out_shape=jax.ShapeDtypeStruct((bsz, A_HEADS, A_VDIM, seq), bf16),
        compiler_params=pltpu.CompilerParams(
            dimension_semantics=("arbitrary", "arbitrary", "arbitrary"),
            vmem_limit_bytes=VMEM_LIMIT),
        name="attn_a",
    )(fixed, aq, ak, av, asg, dtab, qaug, subg, lamv)


def _attn_b_kernel(fixed_ref, q_ref, k_ref, v_ref, sg_ref, qaug_ref, o_ref):
    i = pl.program_id(2)

    def run(fixed_max):
        causal = jnp.where(lax.broadcasted_iota(jnp.int32, (ATT_TQ, ATT_TQ), 0)
                           <= lax.broadcasted_iota(jnp.int32, (ATT_TQ, ATT_TQ), 1), 0.0, NEG)
        chains = []
        for h in range(B_HEADS_PER_STEP):
            k_at = lambda start, size, h=h: k_ref[0, h, pl.ds(start, size), :]
            v_at = lambda start, size, h=h: v_ref[0, h, :, pl.ds(start, size)]
            chains.append((k_at, v_at, q_ref[0, h], qaug_ref[...], B_BIAS_ROWS, causal))
        for h, (acc, l) in enumerate(_causal_sweep(chains, i, fixed_max)):
            o_ref[0, h] = (acc * (1.0 / l) * sg_ref[0, h].astype(f32)).astype(bf16)

    _either_sweep(fixed_ref, run)


def _attn_b(fixed, bq, bk, bv, bsg, qaug):
    bsz, _, _, seq = bq.shape
    nq = seq // ATT_TQ
    hs = B_HEADS_PER_STEP
    return pl.pallas_call(
        _attn_b_kernel,
        grid=(bsz, B_HEADS // hs, nq),
        in_specs=[
            pl.BlockSpec(memory_space=pltpu.SMEM),
            pl.BlockSpec((1, hs, HEAD_DIM, ATT_TQ), lambda b, h, i: (b, h, 0, i)),
            pl.BlockSpec((1, hs, seq, KPAD), lambda b, h, i: (b, h, 0, 0)),
            pl.BlockSpec((1, hs, HEAD_DIM, seq), lambda b, h, i: (b, h, 0, 0)),
            pl.BlockSpec((1, hs, HEAD_DIM, ATT_TQ), lambda b, h, i: (b, h, 0, i)),
            pl.BlockSpec((KPAD - HEAD_DIM, 1), lambda b, h, i: (0, 0)),
        ],
        out_specs=pl.BlockSpec((1, hs, HEAD_DIM, ATT_TQ), lambda b, h, i: (b, h, 0, i)),
        out_shape=jax.ShapeDtypeStruct((bsz, B_HEADS, HEAD_DIM, seq), bf16),
        compiler_params=pltpu.CompilerParams(
            dimension_semantics=("arbitrary", "arbitrary", "arbitrary"),
            vmem_limit_bytes=VMEM_LIMIT),
        name="attn_b",
    )(fixed, bq, bk, bv, bsg, qaug)


def _band_kernel(q_ref, k_ref, v_ref, sg_ref, tab_ref, sink_ref, o_ref, *, group, back, tq):
    i = pl.program_id(2)
    blocks = tq // LANES
    band = (back + blocks) * LANES
    k_start = pl.multiple_of(jnp.maximum(i * blocks - back, 0) * LANES, LANES)
    tab_start = pl.multiple_of(jnp.maximum(back - i * blocks, 0) * LANES, LANES)
    heads = range(k_ref.shape[1])
    pad = jnp.zeros((KPAD - HEAD_DIM, group * tq), bf16)
    qs = [jnp.concatenate([jnp.concatenate([q_ref[0, h * group + g] for g in range(group)], axis=1),
                           pad], axis=0) for h in heads]
    scores = [jnp.dot(k_ref[0, h, pl.ds(k_start, band), :], qs[h], preferred_element_type=f32)
              + tab_ref[h, pl.ds(tab_start, band), :] for h in heads]
    maxes = [jnp.maximum(jnp.max(s, axis=0, keepdims=True), sink_ref[h])
             for h, s in zip(heads, scores)]
    probs = [jnp.exp2(s - m) for s, m in zip(scores, maxes)]
    outs = [jnp.dot(v_ref[0, h, :, pl.ds(k_start, band)], p.astype(bf16), preferred_element_type=f32)
            for h, p in zip(heads, probs)]
    for h, p, m, o in zip(heads, probs, maxes, outs):
        l = jnp.sum(p, axis=0, keepdims=True) + jnp.exp2(sink_ref[h] - m)
        o = o * (1.0 / l)
        for g in range(group):
            hq = h * group + g
            o_ref[0, hq] = (o[:, g * tq:(g + 1) * tq] * sg_ref[0, hq].astype(f32)).astype(bf16)


def _band_attn(q, k, v, sg, tab, sink, group, back, tq, kv_per_step, name):
    bsz, nheads, _, seq = q.shape
    hs = kv_per_step
    return pl.pallas_call(
        functools.partial(_band_kernel, group=group, back=back, tq=tq),
        grid=(bsz, nheads // (group * hs), seq // tq),
        in_specs=[
            pl.BlockSpec((1, hs * group, HEAD_DIM, tq), lambda b, h, i: (b, h, 0, i)),
            pl.BlockSpec((1, hs, seq, KPAD), lambda b, h, i: (b, h, 0, 0)),
            pl.BlockSpec((1, hs, HEAD_DIM, seq), lambda b, h, i: (b, h, 0, 0)),
            pl.BlockSpec((1, hs * group, HEAD_DIM, tq), lambda b, h, i: (b, h, 0, i)),
            pl.BlockSpec((hs,) + tab.shape[1:], lambda b, h, i: (h, 0, 0)),
            pl.BlockSpec((hs,) + sink.shape[1:], lambda b, h, i: (h, 0, 0)),
        ],
        out_specs=pl.BlockSpec((1, hs * group, HEAD_DIM, tq), lambda b, h, i: (b, h, 0, i)),
        out_shape=jax.ShapeDtypeStruct((bsz, nheads, HEAD_DIM, seq), bf16),
        compiler_params=pltpu.CompilerParams(
            dimension_semantics=("arbitrary", "arbitrary", "arbitrary"),
            vmem_limit_bytes=VMEM_LIMIT),
        name=name,
    )(q, k, v, sg, tab, sink)


def _out_proj_kernel(m1_ref, m2_ref, wt_ref, x_ref, o_ref):
    half = m1_ref.shape[1]
    y_t = (jnp.dot(wt_ref[:, :half], m1_ref[0], preferred_element_type=f32)
           + jnp.dot(wt_ref[:, half:], m2_ref[0], preferred_element_type=f32))
    o_ref[0] = x_ref[0] + y_t.T


def _out_proj(m1, m2, wt, x):
    bsz, seq, _ = x.shape
    tt = PROJ_TOKENS
    half = m1.shape[1]
    return pl.pallas_call(
        _out_proj_kernel,
        grid=(bsz, seq // tt),
        in_specs=[
            pl.BlockSpec((1, half, tt), lambda b, t: (b, 0, t)),
            pl.BlockSpec((1, half, tt), lambda b, t: (b, 0, t)),
            pl.BlockSpec(wt.shape, lambda b, t: (0, 0)),
            pl.BlockSpec((1, tt, D_MODEL), lambda b, t: (b, t, 0)),
        ],
        out_specs=pl.BlockSpec((1, tt, D_MODEL), lambda b, t: (b, t, 0)),
        out_shape=jax.ShapeDtypeStruct(x.shape, f32),
        compiler_params=pltpu.CompilerParams(
            dimension_semantics=("arbitrary", "arbitrary"), vmem_limit_bytes=VMEM_LIMIT),
        name="out_proj",
    )(m1, m2, wt, x)


def _alibi_slopes(n):
    return 2.0 ** (-8.0 * np.arange(1, n + 1, dtype=np.float64) / n)


def _np_split3(v):
    v = np.asarray(v, np.float32)
    to_bf = lambda a: a.astype(bf16).astype(np.float32)
    hi = to_bf(v)
    mid = to_bf(v - hi)
    lo = to_bf(v - hi - mid)
    return hi, mid, lo


def _a_tables():
    rate = (_alibi_slopes(A_HEADS) * LOG2E).astype(np.float32)
    qaug = np.zeros((A_HEADS, KPAD - HEAD_DIM, 1), np.float32)
    for idx, piece in enumerate(_np_split3(rate * CHUNK) + _np_split3(rate)):
        qaug[:, idx, 0] = piece
    kk = np.arange(ATT_TQ)[:, None]
    qq = np.arange(ATT_TQ)[None, :]
    future = np.maximum(kk - qq, 0).astype(np.float32)
    corr = -2.0 * rate[:, None, None] * future[None]
    allowed = (kk // CHUNK) <= (qq // CHUNK)
    dtab = np.where(allowed[None], corr, NEG).astype(np.float32)
    return jnp.asarray(qaug), jnp.asarray(dtab)


def _band_frames(back, tq):
    k_pos = np.arange(back * LANES + tq)[:, None]
    q_pos = back * LANES + np.arange(tq)[None, :]
    return q_pos - k_pos, q_pos // CHUNK - k_pos // CHUNK


def _c_tables(sinks):
    back, tq = WIN_CHUNKS * CHUNK // LANES, C_BAND_TQ
    rel, chunk_diff = _band_frames(back, tq)
    allowed = (chunk_diff >= 0) & (chunk_diff <= WIN_CHUNKS)
    slopes = _alibi_slopes(C_HEADS)
    per_head = np.where(allowed[None], -slopes[:, None, None] * np.abs(rel)[None] * LOG2E, NEG)
    tab = per_head.reshape(C_KV_HEADS, C_GROUP, *rel.shape).transpose(0, 2, 1, 3)
    tab = tab.reshape(C_KV_HEADS, rel.shape[0], C_GROUP * tq).astype(np.float32)
    tab = np.concatenate([tab, np.full((C_KV_HEADS, back * LANES, tab.shape[2]), NEG, np.float32)], 1)
    sink = jnp.repeat(sinks.astype(f32) * LOG2E, tq).reshape(C_KV_HEADS, 1, C_GROUP * tq)
    return jnp.asarray(tab), sink, back


def _d_tables(rel_table):
    back, t = D_LEFT_CHUNKS * CHUNK // LANES, D_BAND_TQ
    band = back * LANES + t
    rel, chunk_diff = _band_frames(back, t)
    allowed = (chunk_diff >= 0) & (chunk_diff <= D_LEFT_CHUNKS)
    tbl = rel_table.astype(f32) * LOG2E
    n_lo = (t - 1) - (CHUNK - 1)
    n_hi = (band - 1) - REL_MAX
    diag = jnp.concatenate([jnp.repeat(tbl[:, :1], n_lo, axis=1), tbl,
                            jnp.repeat(tbl[:, -1:], n_hi, axis=1)], axis=1)
    m = diag.shape[1]
    skew = jnp.tile(diag, (1, band + 1))[:, :band * (m + 1)].reshape(D_HEADS, band, m + 1)[:, :, :t]
    bias = jnp.flip(skew, axis=1)
    tab = jnp.where(jnp.asarray(allowed)[None], bias, NEG)
    tab = jnp.concatenate([tab, jnp.full((D_HEADS, back * LANES, t), NEG, f32)], axis=1)
    sink = jnp.full((D_HEADS, 1, t), NEG, f32)
    return tab, sink, back


def _fixed_max_ok(q_gain, k_gain):
    spread = (2.0 * 1.02 * QK_SCALE * HEAD_DIM
              * jnp.max(jnp.abs(q_gain.astype(f32))) * jnp.max(jnp.abs(k_gain.astype(f32))))
    return (spread <= FIXED_MAX_LIMIT).astype(jnp.int32).reshape(1)


def _pad_rows(w_t, rows):
    return jnp.pad(w_t, ((0, rows - w_t.shape[0]), (0, 0)))


def _even_layer(x, ln_g, w_in, w_out, a_qn_g, a_kn_g, a_lq1, a_lk1, a_lq2, a_lk2, a_subln_g,
                b_qn_g, b_kn_g, b_f_bias, layer_idx):
    bsz, seq, _ = x.shape
    colv = lambda v: v.astype(f32).reshape(-1, 1)
    wt = _pad_rows(w_in.T.astype(bf16), 4096 + BF16_ROWS)
    aq, ak, av, asg, bq, bk, bv, bsg = _proj_even(
        x, ln_g.astype(f32).reshape(1, -1), wt, colv(a_qn_g), colv(a_kn_g), colv(b_qn_g),
        colv(b_kn_g), colv(b_f_bias))
    lam_init = 0.8 - 0.6 * math.exp(-0.3 * layer_idx)
    qaug_a, dtab = _a_tables()
    lamv = jnp.stack([a_lq1, a_lk1, a_lq2, a_lk2]).astype(f32)
    mix_a = _attn_a(_fixed_max_ok(a_qn_g, a_kn_g), aq, ak, av, asg, dtab, qaug_a,
                    colv(a_subln_g), lamv, lam_init)
    qaug_b = np.zeros((KPAD - HEAD_DIM, 1), np.float32)
    qaug_b[:B_BIAS_ROWS] = 1.0
    mix_b = _attn_b(_fixed_max_ok(b_qn_g, b_kn_g), bq, bk, bv, bsg, jnp.asarray(qaug_b))
    return _out_proj(mix_a.reshape(bsz, -1, seq), mix_b.reshape(bsz, -1, seq),
                     w_out.T.astype(bf16), x)


def _odd_layer(x, ln_g, w_in, w_out, c_qn_g, c_kn_g, c_sinks, d_qn_g, d_kn_g, d_rel_bias):
    bsz, seq, _ = x.shape
    colv = lambda v: v.astype(f32).reshape(-1, 1)
    cq, ck, cv, csg, dq, dk, dv, dsg = _proj_odd(
        x, ln_g.astype(f32).reshape(1, -1), w_in.T.astype(bf16), colv(c_qn_g), colv(c_kn_g),
        colv(d_qn_g), colv(d_kn_g))
    tab_c, sink_c, back_c = _c_tables(c_sinks)
    mix_c = _band_attn(cq, ck, cv, csg, tab_c, sink_c, C_GROUP, back_c, C_BAND_TQ, C_KV_HEADS, "attn_c")
    tab_d, sink_d, back_d = _d_tables(d_rel_bias)
    mix_d = _band_attn(dq, dk, dv, dsg, tab_d, sink_d, 1, back_d, D_BAND_TQ, D_HEADS_PER_STEP, "attn_d")
    return _out_proj(mix_c.reshape(bsz, -1, seq), mix_d.reshape(bsz, -1, seq),
                     w_out.T.astype(bf16), x)


def kernel(x, even_ln_g, even_w_in, even_w_out, a_q_norm_g, a_k_norm_g, a_lambda_q1, a_lambda_k1, a_lambda_q2, a_lambda_k2, a_subln_g, b_q_norm_g, b_k_norm_g, b_forget_bias, odd_ln_g, odd_w_in, odd_w_out, c_q_norm_g, c_k_norm_g, c_sinks, d_q_norm_g, d_k_norm_g, d_rel_bias):
    depth = even_ln_g.shape[0] + odd_ln_g.shape[0]
    for i in range(depth):
        j = i // 2
        if i % 2 == 0:
            x = _even_layer(x, even_ln_g[j], even_w_in[j], even_w_out[j], a_q_norm_g[j],
                            a_k_norm_g[j], a_lambda_q1[j], a_lambda_k1[j], a_lambda_q2[j],
                            a_lambda_k2[j], a_subln_g[j], b_q_norm_g[j], b_k_norm_g[j],
                            b_forget_bias[j], i)
        else:
            x = _odd_layer(x, odd_ln_g[j], odd_w_in[j], odd_w_out[j], c_q_norm_g[j],
                           c_k_norm_g[j], c_sinks[j], d_q_norm_g[j], d_k_norm_g[j],
                           d_rel_bias[j])
    return x
```

```python
import functools
import math

import numpy as np
import jax
import jax.numpy as jnp
from jax import lax
from jax.experimental import pallas as pl
from jax.experimental.pallas import tpu as pltpu

D_MODEL = 1024
CHUNK = 64
HEAD_DIM = 64
NORM_EPS = 1e-6

A_HEADS = 4
A_STREAMS = 2 * A_HEADS
A_VDIM = 2 * HEAD_DIM
B_HEADS = 8
C_HEADS = 8
C_KV_HEADS = 2
C_GROUP = C_HEADS // C_KV_HEADS
WIN_CHUNKS = 2
D_HEADS = 8
D_LEFT_CHUNKS = 8
REL_MAX = 256

P_EVEN = 8 * 512 + B_HEADS
P_ODD = 512 + 128 + 128 + 512 + 4 * 512

LOG2E = 1.4426950408889634
QK_SCALE = HEAD_DIM ** -0.5 * LOG2E
NEG = -1e30

LANES = 128
KPAD = 128
BF16_ROWS = 16

PROJ_TOKENS = 512
ATT_TQ = 512
ATT_TK = 512
A_HEADS_PER_STEP = 4
B_HEADS_PER_STEP = 8
A_BIAS_ROWS = 6
B_BIAS_ROWS = 3
MAX_ROWS = 3
FIXED_MAX_LIMIT = 96.0
C_BAND_TQ = 128
D_BAND_TQ = 256
D_HEADS_PER_STEP = 4
C_TILES_PER_STEP = 4
D_TILES_PER_STEP = 2
VMEM_LIMIT = 56 * 1024 * 1024

f32 = jnp.float32
bf16 = jnp.bfloat16


def _split3(v):
    hi = v.astype(bf16).astype(f32)
    r = v - hi
    mid = r.astype(bf16).astype(f32)
    lo = (r - mid).astype(bf16).astype(f32)
    return hi, mid, lo


def _silu(z):
    return z * (1.0 / (1.0 + jnp.exp(-z)))


def _rms_rows(x_ref, g_ref):
    x = x_ref[0]
    ms = jnp.mean(x * x, axis=-1, keepdims=True)
    return (x * lax.rsqrt(ms + NORM_EPS) * g_ref[...]).astype(bf16)


def _proj_t(wt_ref, r0, r1, xn):
    return lax.dot_general(wt_ref[r0:r1, :], xn, (((1,), (1,)), ((), ())),
                           preferred_element_type=f32)


def _head_norm(z_t, gain_col, mult):
    n = z_t.shape[0] // HEAD_DIM
    z3 = z_t.reshape(n, HEAD_DIM, z_t.shape[1])
    ms = jnp.mean(z3 * z3, axis=1, keepdims=True)
    return z3 * lax.rsqrt(ms + NORM_EPS) * (gain_col[...] * mult)[None]


def _ones_rows(row, first):
    return jnp.where((row >= first) & (row < first + 3), 1.0, 0.0)


def _store_heads(o_ref, z_t):
    o_ref[0] = z_t.reshape(o_ref.shape[1], o_ref.shape[2], z_t.shape[1]).astype(bf16)


def _store_keys(k_ref, kn, aug_fn):
    n, _, t = kn.shape
    zeros = jnp.zeros((KPAD - HEAD_DIM - BF16_ROWS, t), f32)
    for s in range(n):
        blk = jnp.concatenate([kn[s], aug_fn(s), zeros], axis=0)
        k_ref[0, s] = blk.T.astype(bf16)


def _proj_even_kernel(x_ref, lng_ref, wt_ref, aqg_ref, akg_ref, bqg_ref, bkg_ref, bfb_ref,
                      aq_ref, ak_ref, av_ref, asg_ref, bq_ref, bk_ref, bv_ref, bsg_ref,
                      cum_ref):
    t = pl.program_id(1)
    tt = x_ref.shape[1]
    xn = _rms_rows(x_ref, lng_ref)

    aq_ref[0] = _head_norm(_proj_t(wt_ref, 0, 512, xn), aqg_ref, QK_SCALE).astype(bf16)

    row = lax.broadcasted_iota(jnp.int32, (BF16_ROWS, tt), 0)
    pos = t * tt + lax.broadcasted_iota(jnp.int32, (BF16_ROWS, tt), 1)
    pos_a = lax.shift_right_logical(pos, int(math.log2(CHUNK))).astype(f32)
    pos_b = lax.bitwise_and(pos, CHUNK - 1).astype(f32)
    aug_a = jnp.where(row < 3, pos_a, jnp.where(row < 6, pos_b, _ones_rows(row, A_BIAS_ROWS)))
    akn = _head_norm(_proj_t(wt_ref, 512, 1024, xn), akg_ref, 1.0)
    _store_keys(ak_ref, akn, lambda s: aug_a)

    _store_heads(av_ref, _proj_t(wt_ref, 1024, 1536, xn))
    _store_heads(asg_ref, _silu(_proj_t(wt_ref, 1536, 2048, xn)))

    bq_ref[0] = _head_norm(_proj_t(wt_ref, 2048, 2560, xn), bqg_ref, QK_SCALE).astype(bf16)

    z = _proj_t(wt_ref, 4096, 4096 + BF16_ROWS, xn)[:B_HEADS] + bfb_ref[...]
    log_f = jnp.minimum(z, 0.0) - jnp.log(1.0 + jnp.exp(-jnp.abs(z)))
    tri = jnp.where(lax.broadcasted_iota(jnp.int32, (tt, tt), 0)
                    <= lax.broadcasted_iota(jnp.int32, (tt, tt), 1), 1.0, 0.0).astype(bf16)
    pieces = jnp.concatenate(_split3(log_f) + (jnp.zeros_like(log_f),), axis=0).astype(bf16)
    part = jnp.dot(pieces, tri, preferred_element_type=f32)
    local = part[:B_HEADS] + part[B_HEADS:2 * B_HEADS] + part[2 * B_HEADS:3 * B_HEADS]

    @pl.when(t == 0)
    def _():
        cum_ref[...] = jnp.zeros_like(cum_ref)

    cum = cum_ref[...] + local
    cum_ref[...] = cum[:, tt - 1:tt]
    g_hi, g_mid, g_lo = _split3(-LOG2E * cum)

    def aug_b(s):
        pick = lambda a: jnp.broadcast_to(a[s:s + 1], (BF16_ROWS, tt))
        return jnp.where(row == 0, pick(g_hi),
                         jnp.where(row == 1, pick(g_mid),
                                   jnp.where(row == 2, pick(g_lo), _ones_rows(row, B_BIAS_ROWS))))

    bkn = _head_norm(_proj_t(wt_ref, 2560, 3072, xn), bkg_ref, 1.0)
    _store_keys(bk_ref, bkn, aug_b)

    _store_heads(bv_ref, _proj_t(wt_ref, 3072, 3584, xn))
    _store_heads(bsg_ref, _silu(_proj_t(wt_ref, 3584, 4096, xn)))


def _proj_even(x, ln_g, wt, aqg, akg, bqg, bkg, bfb):
    bsz, seq, _ = x.shape
    tt = PROJ_TOKENS
    col = lambda n: pl.BlockSpec((n, 1), lambda b, t: (0, 0))
    fm = lambda n, d: pl.BlockSpec((1, n, d, tt), lambda b, t: (b, 0, 0, t))
    km = lambda n: pl.BlockSpec((1, n, tt, KPAD), lambda b, t: (b, 0, t, 0))
    fm_shape = lambda n, d: jax.ShapeDtypeStruct((bsz, n, d, seq), bf16)
    km_shape = lambda n: jax.ShapeDtypeStruct((bsz, n, seq, KPAD), bf16)
    return pl.pallas_call(
        _proj_even_kernel,
        grid=(bsz, seq // tt),
        in_specs=[
            pl.BlockSpec((1, tt, D_MODEL), lambda b, t: (b, t, 0)),
            pl.BlockSpec((1, D_MODEL), lambda b, t: (0, 0)),
            pl.BlockSpec(wt.shape, lambda b, t: (0, 0)),
            col(HEAD_DIM), col(HEAD_DIM), col(HEAD_DIM), col(HEAD_DIM), col(B_HEADS),
        ],
        out_specs=[fm(A_STREAMS, HEAD_DIM), km(A_STREAMS), fm(A_HEADS, A_VDIM), fm(A_HEADS, A_VDIM),
                   fm(B_HEADS, HEAD_DIM), km(B_HEADS), fm(B_HEADS, HEAD_DIM), fm(B_HEADS, HEAD_DIM)],
        out_shape=[fm_shape(A_STREAMS, HEAD_DIM), km_shape(A_STREAMS), fm_shape(A_HEADS, A_VDIM),
                   fm_shape(A_HEADS, A_VDIM), fm_shape(B_HEADS, HEAD_DIM), km_shape(B_HEADS),
                   fm_shape(B_HEADS, HEAD_DIM), fm_shape(B_HEADS, HEAD_DIM)],
        scratch_shapes=[pltpu.VMEM((B_HEADS, 1), f32)],
        compiler_params=pltpu.CompilerParams(
            dimension_semantics=("arbitrary", "arbitrary"), vmem_limit_bytes=VMEM_LIMIT),
        name="proj_even",
    )(x, ln_g, wt, aqg, akg, bqg, bkg, bfb)


def _proj_odd_kernel(x_ref, lng_ref, wt_ref, cqg_ref, ckg_ref, dqg_ref, dkg_ref,
                     cq_ref, ck_ref, cv_ref, csg_ref, dq_ref, dk_ref, dv_ref, dsg_ref):
    tt = x_ref.shape[1]
    xn = _rms_rows(x_ref, lng_ref)
    no_aug = lambda s: jnp.zeros((BF16_ROWS, tt), f32)

    cq_ref[0] = _head_norm(_proj_t(wt_ref, 0, 512, xn), cqg_ref, QK_SCALE).astype(bf16)
    _store_keys(ck_ref, _head_norm(_proj_t(wt_ref, 512, 640, xn), ckg_ref, 1.0), no_aug)
    _store_heads(cv_ref, _proj_t(wt_ref, 640, 768, xn))
    _store_heads(csg_ref, _silu(_proj_t(wt_ref, 768, 1280, xn)))
    dq_ref[0] = _head_norm(_proj_t(wt_ref, 1280, 1792, xn), dqg_ref, QK_SCALE).astype(bf16)
    _store_keys(dk_ref, _head_norm(_proj_t(wt_ref, 1792, 2304, xn), dkg_ref, 1.0), no_aug)
    _store_heads(dv_ref, _proj_t(wt_ref, 2304, 2816, xn))
    _store_heads(dsg_ref, _silu(_proj_t(wt_ref, 2816, 3328, xn)))


def _proj_odd(x, ln_g, wt, cqg, ckg, dqg, dkg):
    bsz, seq, _ = x.shape
    tt = PROJ_TOKENS
    col = lambda n: pl.BlockSpec((n, 1), lambda b, t: (0, 0))
    fm = lambda n: pl.BlockSpec((1, n, HEAD_DIM, tt), lambda b, t: (b, 0, 0, t))
    km = lambda n: pl.BlockSpec((1, n, tt, KPAD), lambda b, t: (b, 0, t, 0))
    fm_shape = lambda n: jax.ShapeDtypeStruct((bsz, n, HEAD_DIM, seq), bf16)
    km_shape = lambda n: jax.ShapeDtypeStruct((bsz, n, seq, KPAD), bf16)
    return pl.pallas_call(
        _proj_odd_kernel,
        grid=(bsz, seq // tt),
        in_specs=[
            pl.BlockSpec((1, tt, D_MODEL), lambda b, t: (b, t, 0)),
            pl.BlockSpec((1, D_MODEL), lambda b, t: (0, 0)),
            pl.BlockSpec(wt.shape, lambda b, t: (0, 0)),
            col(HEAD_DIM), col(HEAD_DIM), col(HEAD_DIM), col(HEAD_DIM),
        ],
        out_specs=[fm(C_HEADS), km(C_KV_HEADS), fm(C_KV_HEADS), fm(C_HEADS),
                   fm(D_HEADS), km(D_HEADS), fm(D_HEADS), fm(D_HEADS)],
        out_shape=[fm_shape(C_HEADS), km_shape(C_KV_HEADS), fm_shape(C_KV_HEADS), fm_shape(C_HEADS),
                   fm_shape(D_HEADS), km_shape(D_HEADS), fm_shape(D_HEADS), fm_shape(D_HEADS)],
        compiler_params=pltpu.CompilerParams(
            dimension_semantics=("arbitrary", "arbitrary"), vmem_limit_bytes=VMEM_LIMIT),
        name="proj_odd",
    )(x, ln_g, wt, cqg, ckg, dqg, dkg)


def _online_step(carry, s, v):
    m, l, acc = carry
    m_new = jnp.maximum(m, jnp.max(s, axis=0, keepdims=True))
    p = jnp.exp2(s - m_new)
    alpha = jnp.exp2(m - m_new)
    l = alpha * l + jnp.sum(p, axis=0, keepdims=True)
    acc = alpha * acc + jnp.dot(v, p.astype(bf16), preferred_element_type=f32)
    return m_new, l, acc


def _colsum8(p):
    return p.reshape(p.shape[0] // 8, 8, p.shape[1]).sum(axis=0)


def _augment_q(q, aug_col, n_bias, m=None):
    tq = q.shape[1]
    aug = jnp.broadcast_to(aug_col, (KPAD - HEAD_DIM, tq))
    if m is not None:
        row = lax.broadcasted_iota(jnp.int32, aug.shape, 0)
        for r, piece in enumerate(_split3(-m)):
            aug = jnp.where(row == n_bias + r, piece, aug)
    return jnp.concatenate([q, aug.astype(bf16)], axis=0)


def _staggered(n, scores, finish):
    out, pending = [], scores(0)
    for c in range(1, n):
        nxt = scores(c)
        out.append(finish(c - 1, pending))
        pending = nxt
    out.append(finish(n - 1, pending))
    return out


def _diag_full(chains, d0, tq):
    def scores(c):
        k_at, _, q, aug_col, n_bias, diag_bias = chains[c]
        return jnp.dot(k_at(d0, tq), _augment_q(q, aug_col, n_bias),
                       preferred_element_type=f32) + diag_bias

    def finish(c, s):
        m = jnp.max(s, axis=0, keepdims=True)
        p = jnp.exp2(s - m)
        return m, _colsum8(p), jnp.dot(chains[c][1](d0, tq), p.astype(bf16), preferred_element_type=f32)

    return _staggered(len(chains), scores, finish)


def _diag_halves(chains, d0, tq):
    h = tq // 2
    d1 = pl.multiple_of(d0 + h, h)

    def scores(c):
        k_at, _, q, aug_col, n_bias, diag_bias = chains[c]
        qa = _augment_q(q, aug_col, n_bias)
        bias = diag_bias[:h, :h]
        return (jnp.dot(k_at(d0, h), qa[:, :h], preferred_element_type=f32) + bias,
                jnp.dot(k_at(d1, h), qa[:, h:], preferred_element_type=f32) + bias)

    def finish(c, s):
        k_at, v_at, q, aug_col, n_bias, _ = chains[c]
        m0 = jnp.max(s[0], axis=0, keepdims=True)
        m1 = jnp.max(s[1], axis=0, keepdims=True)
        qa = _augment_q(q, aug_col, n_bias, jnp.concatenate([m0, m1], axis=1))
        p00 = jnp.exp2(s[0] - m0)
        p11 = jnp.exp2(s[1] - m1)
        p01 = jnp.exp2(jnp.dot(k_at(d0, h), qa[:, h:], preferred_element_type=f32))
        l = jnp.concatenate([_colsum8(p00), _colsum8(p01) + _colsum8(p11)], axis=1)
        a0 = jnp.dot(v_at(d0, h), jnp.concatenate([p00, p01], axis=1).astype(bf16),
                     preferred_element_type=f32)
        a1 = jnp.dot(v_at(d1, h), p11.astype(bf16), preferred_element_type=f32)
        return qa, l, jnp.concatenate([a0[:, :h], a0[:, h:] + a1], axis=1)

    return _staggered(len(chains), scores, finish)


def _causal_sweep(chains, i, fixed_max):
    n = len(chains)
    tq = chains[0][2].shape[1]
    d0 = pl.multiple_of(i * tq, tq)
    n_tiles = i * (tq // ATT_TK)
    tile = lambda j: (pl.multiple_of(j * ATT_TK, ATT_TK), ATT_TK)

    if fixed_max:
        diag = _diag_halves(chains, d0, tq)

        def body(j, carries):
            def finish(c, s):
                p = jnp.exp2(s)
                l, acc = carries[c]
                return (l + _colsum8(p), acc + jnp.dot(chains[c][1](*tile(j)), p.astype(bf16),
                                                       preferred_element_type=f32))

            scores = lambda c: jnp.dot(chains[c][0](*tile(j)), diag[c][0], preferred_element_type=f32)
            return tuple(_staggered(n, scores, finish))

        carries = lax.fori_loop(0, n_tiles, body, tuple((l, acc) for _, l, acc in diag))
        return [(acc, jnp.sum(l, axis=0, keepdims=True)) for l, acc in carries]

    q_aug = [_augment_q(q, aug_col, n_bias) for _, _, q, aug_col, n_bias, _ in chains]

    def body(j, carries):
        scores = lambda c: jnp.dot(chains[c][0](*tile(j)), q_aug[c], preferred_element_type=f32)
        finish = lambda c, s: _online_step(carries[c], s, chains[c][1](*tile(j)))
        return tuple(_staggered(n, scores, finish))

    init = tuple((m, jnp.sum(l, axis=0, keepdims=True), acc) for m, l, acc in _diag_full(chains, d0, tq))
    return [(acc, l) for _, l, acc in lax.fori_loop(0, n_tiles, body, init)]


def _either_sweep(fixed_ref, run):
    @pl.when(fixed_ref[0] != 0)
    def _():
        run(True)

    @pl.when(fixed_ref[0] == 0)
    def _():
        run(False)


def _attn_a_kernel(fixed_ref, q_ref, k_ref, v_ref, sg_ref, dtab_ref, qaug_ref, subg_ref, lamv_ref,
                   o_ref, *, lam_init):
    i = pl.program_id(2)
    chains = []
    for h in range(A_HEADS_PER_STEP):
        v_at = lambda start, size, h=h: v_ref[0, h, :, pl.ds(start, size)]
        for c in range(2):
            k_at = lambda start, size, s=2 * h + c: k_ref[0, s, pl.ds(start, size), :]
            chains.append((k_at, v_at, q_ref[0, 2 * h + c], qaug_ref[h], A_BIAS_ROWS, dtab_ref[h]))

    def run(fixed_max):
        lv = lamv_ref[...]
        lam = (jnp.exp(jnp.sum(lv[0:1] * lv[1:2], axis=1, keepdims=True))
               - jnp.exp(jnp.sum(lv[2:3] * lv[3:4], axis=1, keepdims=True)) + lam_init)
        outs = [acc * (1.0 / l) for acc, l in _causal_sweep(chains, i, fixed_max)]
        for h in range(A_HEADS_PER_STEP):
            o = outs[2 * h] - lam * outs[2 * h + 1]
            ms = jnp.mean(o * o, axis=0, keepdims=True)
            y = o * lax.rsqrt(ms + NORM_EPS) * (subg_ref[...] * (1.0 - lam_init))
            o_ref[0, h] = (y * sg_ref[0, h].astype(f32)).astype(bf16)

    _either_sweep(fixed_ref, run)


def _attn_a(fixed, aq, ak, av, asg, dtab, qaug, subg, lamv, lam_init):
    bsz, _, _, seq = aq.shape
    nq = seq // ATT_TQ
    hs = A_HEADS_PER_STEP
    return pl.pallas_call(
        functools.partial(_attn_a_kernel, lam_init=lam_init),
        grid=(bsz, A_HEADS // hs, nq),
        in_specs=[
            pl.BlockSpec(memory_space=pltpu.SMEM),
            pl.BlockSpec((1, 2 * hs, HEAD_DIM, ATT_TQ), lambda b, h, i: (b, h, 0, i)),
            pl.BlockSpec((1, 2 * hs, seq, KPAD), lambda b, h, i: (b, h, 0, 0)),
            pl.BlockSpec((1, hs, A_VDIM, seq), lambda b, h, i: (b, h, 0, 0)),
            pl.BlockSpec((1, hs, A_VDIM, ATT_TQ), lambda b, h, i: (b, h, 0, i)),
            pl.BlockSpec((hs, ATT_TQ, ATT_TQ), lambda b, h, i: (h, 0, 0)),
            pl.BlockSpec((hs, KPAD - HEAD_DIM, 1), lambda b, h, i: (h, 0, 0)),
            pl.BlockSpec((A_VDIM, 1), lambda b, h, i: (0, 0)),
            pl.BlockSpec((4, HEAD_DIM), lambda b, h, i: (0, 0)),
        ],
        out_specs=pl.BlockSpec((1, hs, A_VDIM, ATT_TQ), lambda b, h, i: (b, h, 0, i)),
        out_shape=jax.ShapeDtypeStruct((bsz, A_HEADS, A_VDIM, seq), bf16),
        compiler_params=pltpu.CompilerParams(
            dimension_semantics=("arbitrary", "arbitrary", "arbitrary"),
            vmem_limit_bytes=VMEM_LIMIT),
        name="attn_a",
    )(fixed, aq, ak, av, asg, dtab, qaug, subg, lamv)


def _attn_b_kernel(fixed_ref, q_ref, k_ref, v_ref, sg_ref, qaug_ref, o_ref):
    i = pl.program_id(2)

    def run(fixed_max):
        causal = jnp.where(lax.broadcasted_iota(jnp.int32, (ATT_TQ, ATT_TQ), 0)
                           <= lax.broadcasted_iota(jnp.int32, (ATT_TQ, ATT_TQ), 1), 0.0, NEG)
        chains = []
        for h in range(B_HEADS_PER_STEP):
            k_at = lambda start, size, h=h: k_ref[0, h, pl.ds(start, size), :]
            v_at = lambda start, size, h=h: v_ref[0, h, :, pl.ds(start, size)]
            chains.append((k_at, v_at, q_ref[0, h], qaug_ref[...], B_BIAS_ROWS, causal))
        for h, (acc, l) in enumerate(_causal_sweep(chains, i, fixed_max)):
            o_ref[0, h] = (acc * (1.0 / l) * sg_ref[0, h].astype(f32)).astype(bf16)

    _either_sweep(fixed_ref, run)


def _attn_b(fixed, bq, bk, bv, bsg, qaug):
    bsz, _, _, seq = bq.shape
    nq = seq // ATT_TQ
    hs = B_HEADS_PER_STEP
    return pl.pallas_call(
        _attn_b_kernel,
        grid=(bsz, B_HEADS // hs, nq),
        in_specs=[
            pl.BlockSpec(memory_space=pltpu.SMEM),
            pl.BlockSpec((1, hs, HEAD_DIM, ATT_TQ), lambda b, h, i: (b, h, 0, i)),
            pl.BlockSpec((1, hs, seq, KPAD), lambda b, h, i: (b, h, 0, 0)),
            pl.BlockSpec((1, hs, HEAD_DIM, seq), lambda b, h, i: (b, h, 0, 0)),
            pl.BlockSpec((1, hs, HEAD_DIM, ATT_TQ), lambda b, h, i: (b, h, 0, i)),
            pl.BlockSpec((KPAD - HEAD_DIM, 1), lambda b, h, i: (0, 0)),
        ],
        out_specs=pl.BlockSpec((1, hs, HEAD_DIM, ATT_TQ), lambda b, h, i: (b, h, 0, i)),
        out_shape=jax.ShapeDtypeStruct((bsz, B_HEADS, HEAD_DIM, seq), bf16),
        compiler_params=pltpu.CompilerParams(
            dimension_semantics=("arbitrary", "arbitrary", "arbitrary"),
            vmem_limit_bytes=VMEM_LIMIT),
        name="attn_b",
    )(fixed, bq, bk, bv, bsg, qaug)


def _band_kernel(q_ref, k_ref, v_ref, sg_ref, tab_ref, sink_ref, o_ref, *, group, back, tq):
    blocks = tq // LANES
    band = (back + blocks) * LANES
    tiles = q_ref.shape[3] // tq
    pad = jnp.zeros((KPAD - HEAD_DIM, group * tq), bf16)
    work = [(u, h) for u in range(tiles) for h in range(k_ref.shape[1])]

    def window(u):
        first = (pl.program_id(2) * tiles + u) * blocks
        return (pl.multiple_of(jnp.maximum(first - back, 0) * LANES, LANES),
                pl.multiple_of(jnp.maximum(back - first, 0) * LANES, LANES))

    def scores(w):
        u, h = work[w]
        k_start, tab_start = window(u)
        q = jnp.concatenate([q_ref[0, h * group + g, :, u * tq:(u + 1) * tq] for g in range(group)],
                            axis=1)
        s = jnp.dot(k_ref[0, h, pl.ds(k_start, band), :], jnp.concatenate([q, pad], axis=0),
                    preferred_element_type=f32)
        return s + tab_ref[h, pl.ds(tab_start, band), :]

    def finish(w, s):
        u, h = work[w]
        k_start, _ = window(u)
        m = jnp.maximum(jnp.max(s, axis=0, keepdims=True), sink_ref[h])
        p = jnp.exp2(s - m)
        l = jnp.sum(p, axis=0, keepdims=True) + jnp.exp2(sink_ref[h] - m)
        o = jnp.dot(v_ref[0, h, :, pl.ds(k_start, band)], p.astype(bf16),
                    preferred_element_type=f32) * (1.0 / l)
        for g in range(group):
            hq = h * group + g
            gate = sg_ref[0, hq, :, u * tq:(u + 1) * tq].astype(f32)
            o_ref[0, hq, :, u * tq:(u + 1) * tq] = (o[:, g * tq:(g + 1) * tq] * gate).astype(bf16)

    _staggered(len(work), scores, finish)


def _band_attn(q, k, v, sg, tab, sink, group, back, tq, kv_per_step, tiles_per_step, name):
    bsz, nheads, _, seq = q.shape
    hs = kv_per_step
    tile = tq * tiles_per_step
    return pl.pallas_call(
        functools.partial(_band_kernel, group=group, back=back, tq=tq),
        grid=(bsz, nheads // (group * hs), seq // tile),
        in_specs=[
            pl.BlockSpec((1, hs * group, HEAD_DIM, tile), lambda b, h, i: (b, h, 0, i)),
            pl.BlockSpec((1, hs, seq, KPAD), lambda b, h, i: (b, h, 0, 0)),
            pl.BlockSpec((1, hs, HEAD_DIM, seq), lambda b, h, i: (b, h, 0, 0)),
            pl.BlockSpec((1, hs * group, HEAD_DIM, tile), lambda b, h, i: (b, h, 0, i)),
            pl.BlockSpec((hs,) + tab.shape[1:], lambda b, h, i: (h, 0, 0)),
            pl.BlockSpec((hs,) + sink.shape[1:], lambda b, h, i: (h, 0, 0)),
        ],
        out_specs=pl.BlockSpec((1, hs * group, HEAD_DIM, tile), lambda b, h, i: (b, h, 0, i)),
        out_shape=jax.ShapeDtypeStruct((bsz, nheads, HEAD_DIM, seq), bf16),
        compiler_params=pltpu.CompilerParams(
            dimension_semantics=("arbitrary", "arbitrary", "arbitrary"),
            vmem_limit_bytes=VMEM_LIMIT),
        name=name,
    )(q, k, v, sg, tab, sink)


def _out_proj_kernel(m1_ref, m2_ref, wt_ref, x_ref, o_ref):
    half = m1_ref.shape[1]
    y_t = (jnp.dot(wt_ref[:, :half], m1_ref[0], preferred_element_type=f32)
           + jnp.dot(wt_ref[:, half:], m2_ref[0], preferred_element_type=f32))
    o_ref[0] = x_ref[0] + y_t.T


def _out_proj(m1, m2, wt, x):
    bsz, seq, _ = x.shape
    tt = PROJ_TOKENS
    half = m1.shape[1]
    return pl.pallas_call(
        _out_proj_kernel,
        grid=(bsz, seq // tt),
        in_specs=[
            pl.BlockSpec((1, half, tt), lambda b, t: (b, 0, t)),
            pl.BlockSpec((1, half, tt), lambda b, t: (b, 0, t)),
            pl.BlockSpec(wt.shape, lambda b, t: (0, 0)),
            pl.BlockSpec((1, tt, D_MODEL), lambda b, t: (b, t, 0)),
        ],
        out_specs=pl.BlockSpec((1, tt, D_MODEL), lambda b, t: (b, t, 0)),
        out_shape=jax.ShapeDtypeStruct(x.shape, f32),
        compiler_params=pltpu.CompilerParams(
            dimension_semantics=("arbitrary", "arbitrary"), vmem_limit_bytes=VMEM_LIMIT),
        name="out_proj",
    )(m1, m2, wt, x)


def _alibi_slopes(n):
    return 2.0 ** (-8.0 * np.arange(1, n + 1, dtype=np.float64) / n)


def _np_split3(v):
    v = np.asarray(v, np.float32)
    to_bf = lambda a: a.astype(bf16).astype(np.float32)
    hi = to_bf(v)
    mid = to_bf(v - hi)
    lo = to_bf(v - hi - mid)
    return hi, mid, lo


def _a_tables():
    rate = (_alibi_slopes(A_HEADS) * LOG2E).astype(np.float32)
    qaug = np.zeros((A_HEADS, KPAD - HEAD_DIM, 1), np.float32)
    for idx, piece in enumerate(_np_split3(rate * CHUNK) + _np_split3(rate)):
        qaug[:, idx, 0] = piece
    kk = np.arange(ATT_TQ)[:, None]
    qq = np.arange(ATT_TQ)[None, :]
    future = np.maximum(kk - qq, 0).astype(np.float32)
    corr = -2.0 * rate[:, None, None] * future[None]
    allowed = (kk // CHUNK) <= (qq // CHUNK)
    dtab = np.where(allowed[None], corr, NEG).astype(np.float32)
    return jnp.asarray(qaug), jnp.asarray(dtab)


def _band_frames(back, tq):
    k_pos = np.arange(back * LANES + tq)[:, None]
    q_pos = back * LANES + np.arange(tq)[None, :]
    return q_pos - k_pos, q_pos // CHUNK - k_pos // CHUNK


def _c_tables(sinks):
    back, tq = WIN_CHUNKS * CHUNK // LANES, C_BAND_TQ
    rel, chunk_diff = _band_frames(back, tq)
    allowed = (chunk_diff >= 0) & (chunk_diff <= WIN_CHUNKS)
    slopes = _alibi_slopes(C_HEADS)
    per_head = np.where(allowed[None], -slopes[:, None, None] * np.abs(rel)[None] * LOG2E, NEG)
    tab = per_head.reshape(C_KV_HEADS, C_GROUP, *rel.shape).transpose(0, 2, 1, 3)
    tab = tab.reshape(C_KV_HEADS, rel.shape[0], C_GROUP * tq).astype(np.float32)
    tab = np.concatenate([tab, np.full((C_KV_HEADS, back * LANES, tab.shape[2]), NEG, np.float32)], 1)
    sink = jnp.repeat(sinks.astype(f32) * LOG2E, tq).reshape(C_KV_HEADS, 1, C_GROUP * tq)
    return jnp.asarray(tab), sink, back


def _d_tables(rel_table):
    back, t = D_LEFT_CHUNKS * CHUNK // LANES, D_BAND_TQ
    band = back * LANES + t
    rel, chunk_diff = _band_frames(back, t)
    allowed = (chunk_diff >= 0) & (chunk_diff <= D_LEFT_CHUNKS)
    tbl = rel_table.astype(f32) * LOG2E
    n_lo = (t - 1) - (CHUNK - 1)
    n_hi = (band - 1) - REL_MAX
    diag = jnp.concatenate([jnp.broadcast_to(tbl[:, :1], (D_HEADS, n_lo)), tbl,
                            jnp.broadcast_to(tbl[:, -1:], (D_HEADS, n_hi))], axis=1)
    m = t + LANES - 1
    blocks = []
    for kb in range(band // LANES):
        lo = rel[kb * LANES:(kb + 1) * LANES].min()
        if lo >= REL_MAX:
            blocks.append(jnp.broadcast_to(tbl[:, -1:, None], (D_HEADS, LANES, t)))
            continue
        window = diag[:, band - (kb + 1) * LANES:band - (kb + 1) * LANES + m]
        skew = jnp.broadcast_to(window[:, None, :], (D_HEADS, LANES + 1, m)).reshape(D_HEADS, -1)
        skew = skew[:, :LANES * (m + 1)].reshape(D_HEADS, LANES, m + 1)[:, :, :t]
        blocks.append(jnp.flip(skew, axis=1))
    tab = jnp.where(jnp.asarray(allowed)[None], jnp.concatenate(blocks, axis=1), NEG)
    tab = jnp.concatenate([tab, jnp.full((D_HEADS, back * LANES, t), NEG, f32)], axis=1)
    sink = jnp.full((D_HEADS, 1, t), NEG, f32)
    return tab, sink, back


def _fixed_max_ok(q_gain, k_gain):
    spread = (2.0 * 1.02 * QK_SCALE * HEAD_DIM
              * jnp.max(jnp.abs(q_gain.astype(f32))) * jnp.max(jnp.abs(k_gain.astype(f32))))
    return (spread <= FIXED_MAX_LIMIT).astype(jnp.int32).reshape(1)


def _pad_rows(w_t, rows):
    return jnp.pad(w_t, ((0, rows - w_t.shape[0]), (0, 0)))


def _even_layer(x, ln_g, w_in, w_out, a_qn_g, a_kn_g, a_lq1, a_lk1, a_lq2, a_lk2, a_subln_g,
                b_qn_g, b_kn_g, b_f_bias, layer_idx):
    bsz, seq, _ = x.shape
    colv = lambda v: v.astype(f32).reshape(-1, 1)
    wt = _pad_rows(w_in.T.astype(bf16), 4096 + BF16_ROWS)
    aq, ak, av, asg, bq, bk, bv, bsg = _proj_even(
        x, ln_g.astype(f32).reshape(1, -1), wt, colv(a_qn_g), colv(a_kn_g), colv(b_qn_g),
        colv(b_kn_g), colv(b_f_bias))
    lam_init = 0.8 - 0.6 * math.exp(-0.3 * layer_idx)
    qaug_a, dtab = _a_tables()
    lamv = jnp.stack([a_lq1, a_lk1, a_lq2, a_lk2]).astype(f32)
    mix_a = _attn_a(_fixed_max_ok(a_qn_g, a_kn_g), aq, ak, av, asg, dtab, qaug_a,
                    colv(a_subln_g), lamv, lam_init)
    qaug_b = np.zeros((KPAD - HEAD_DIM, 1), np.float32)
    qaug_b[:B_BIAS_ROWS] = 1.0
    mix_b = _attn_b(_fixed_max_ok(b_qn_g, b_kn_g), bq, bk, bv, bsg, jnp.asarray(qaug_b))
    return _out_proj(mix_a.reshape(bsz, -1, seq), mix_b.reshape(bsz, -1, seq),
                     w_out.T.astype(bf16), x)


def _odd_layer(x, ln_g, w_in, w_out, c_qn_g, c_kn_g, c_sinks, d_qn_g, d_kn_g, d_rel_bias):
    bsz, seq, _ = x.shape
    colv = lambda v: v.astype(f32).reshape(-1, 1)
    cq, ck, cv, csg, dq, dk, dv, dsg = _proj_odd(
        x, ln_g.astype(f32).reshape(1, -1), w_in.T.astype(bf16), colv(c_qn_g), colv(c_kn_g),
        colv(d_qn_g), colv(d_kn_g))
    tab_c, sink_c, back_c = _c_tables(c_sinks)
    mix_c = _band_attn(cq, ck, cv, csg, tab_c, sink_c, C_GROUP, back_c, C_BAND_TQ, C_KV_HEADS,
                       C_TILES_PER_STEP, "attn_c")
    tab_d, sink_d, back_d = _d_tables(d_rel_bias)
    mix_d = _band_attn(dq, dk, dv, dsg, tab_d, sink_d, 1, back_d, D_BAND_TQ, D_HEADS_PER_STEP,
                       D_TILES_PER_STEP, "attn_d")
    return _out_proj(mix_c.reshape(bsz, -1, seq), mix_d.reshape(bsz, -1, seq),
                     w_out.T.astype(bf16), x)


def kernel(x, even_ln_g, even_w_in, even_w_out, a_q_norm_g, a_k_norm_g, a_lambda_q1, a_lambda_k1, a_lambda_q2, a_lambda_k2, a_subln_g, b_q_norm_g, b_k_norm_g, b_forget_bias, odd_ln_g, odd_w_in, odd_w_out, c_q_norm_g, c_k_norm_g, c_sinks, d_q_norm_g, d_k_norm_g, d_rel_bias):
    depth = even_ln_g.shape[0] + odd_ln_g.shape[0]
    for i in range(depth):
        j = i // 2
        if i % 2 == 0:
            x = _even_layer(x, even_ln_g[j], even_w_in[j], even_w_out[j], a_q_norm_g[j],
                            a_k_norm_g[j], a_lambda_q1[j], a_lambda_k1[j], a_lambda_q2[j],
                            a_lambda_k2[j], a_subln_g[j], b_q_norm_g[j], b_k_norm_g[j],
                            b_forget_bias[j], i)
        else:
            x = _odd_layer(x, odd_ln_g[j], odd_w_in[j], odd_w_out[j], c_q_norm_g[j],
                           c_k_norm_g[j], c_sinks[j], d_q_norm_g[j], d_k_norm_g[j],
                           d_rel_bias[j])
    return x
```

```python
import functools
import math

import numpy as np
import jax
import jax.numpy as jnp
from jax import lax
from jax.experimental import pallas as pl
from jax.experimental.pallas import tpu as pltpu

D_MODEL = 1024
CHUNK = 64
HEAD_DIM = 64
NORM_EPS = 1e-6

A_HEADS = 4
A_STREAMS = 2 * A_HEADS
A_VDIM = 2 * HEAD_DIM
B_HEADS = 8
C_HEADS = 8
C_KV_HEADS = 2
C_GROUP = C_HEADS // C_KV_HEADS
WIN_CHUNKS = 2
D_HEADS = 8
D_LEFT_CHUNKS = 8
REL_MAX = 256

P_EVEN = 8 * 512 + B_HEADS
P_ODD = 512 + 128 + 128 + 512 + 4 * 512

LOG2E = 1.4426950408889634
QK_SCALE = HEAD_DIM ** -0.5 * LOG2E
NEG = -1e30
A_RATES = (2.0 ** (-8.0 * np.arange(1, A_HEADS + 1) / A_HEADS) * LOG2E).astype(np.float32)

LANES = 128
KPAD = 128
BF16_ROWS = 16

PROJ_TOKENS = 512
ATT_TQ = 512
ATT_TK = 512
A_HEADS_PER_STEP = 4
B_HEADS_PER_STEP = 8
A_BIAS_ROWS = 6
B_BIAS_ROWS = 3
MAX_ROWS = 3
FIXED_MAX_LIMIT = 96.0
C_BAND_TQ = 128
D_BAND_TQ = 256
D_HEADS_PER_STEP = 4
C_TILES_PER_STEP = 4
D_TILES_PER_STEP = 2
VMEM_LIMIT = 56 * 1024 * 1024

f32 = jnp.float32
bf16 = jnp.bfloat16


def _split3(v):
    hi = v.astype(bf16).astype(f32)
    r = v - hi
    mid = r.astype(bf16).astype(f32)
    lo = (r - mid).astype(bf16).astype(f32)
    return hi, mid, lo


def _silu(z):
    return z * (1.0 / (1.0 + jnp.exp(-z)))


def _rms_rows(x_ref, g_ref):
    x = x_ref[0]
    ms = jnp.mean(x * x, axis=-1, keepdims=True)
    return (x * lax.rsqrt(ms + NORM_EPS) * g_ref[...]).astype(bf16)


def _proj_t(wt_ref, r0, r1, xn):
    return lax.dot_general(wt_ref[r0:r1, :], xn, (((1,), (1,)), ((), ())),
                           preferred_element_type=f32)


def _head_norm(z_t, gain_col, mult):
    n = z_t.shape[0] // HEAD_DIM
    z3 = z_t.reshape(n, HEAD_DIM, z_t.shape[1])
    ms = jnp.mean(z3 * z3, axis=1, keepdims=True)
    return z3 * lax.rsqrt(ms + NORM_EPS) * (gain_col[...] * mult)[None]


def _ones_rows(row, first):
    return jnp.where((row >= first) & (row < first + 3), 1.0, 0.0)


def _store_heads(o_ref, z_t):
    o_ref[0] = z_t.reshape(o_ref.shape[1], o_ref.shape[2], z_t.shape[1]).astype(bf16)


def _store_keys(k_ref, kn, aug_fn):
    n, _, t = kn.shape
    zeros = jnp.zeros((KPAD - HEAD_DIM - BF16_ROWS, t), f32)
    for s in range(n):
        blk = jnp.concatenate([kn[s], aug_fn(s), zeros], axis=0)
        k_ref[0, s] = blk.T.astype(bf16)


def _proj_even_kernel(x_ref, lng_ref, wt_ref, aqg_ref, akg_ref, bqg_ref, bkg_ref, bfb_ref,
                      aq_ref, ak_ref, av_ref, asg_ref, bq_ref, bk_ref, bv_ref, bsg_ref,
                      am_ref, bm_ref, cum_ref):
    t = pl.program_id(1)
    tt = x_ref.shape[1]
    xn = _rms_rows(x_ref, lng_ref)
    row = lax.broadcasted_iota(jnp.int32, (BF16_ROWS, tt), 0)

    z = _proj_t(wt_ref, 4096, 4096 + BF16_ROWS, xn)[:B_HEADS] + bfb_ref[...]
    log_f = jnp.minimum(z, 0.0) - jnp.log(1.0 + jnp.exp(-jnp.abs(z)))
    tri = jnp.where(lax.broadcasted_iota(jnp.int32, (tt, tt), 0)
                    <= lax.broadcasted_iota(jnp.int32, (tt, tt), 1), 1.0, 0.0).astype(bf16)
    pieces = jnp.concatenate(_split3(log_f) + (jnp.zeros_like(log_f),), axis=0).astype(bf16)
    part = jnp.dot(pieces, tri, preferred_element_type=f32)
    local = part[:B_HEADS] + part[B_HEADS:2 * B_HEADS] + part[2 * B_HEADS:3 * B_HEADS]

    @pl.when(t == 0)
    def _():
        cum_ref[...] = jnp.zeros_like(cum_ref)

    cum = cum_ref[...] + local
    cum_ref[...] = cum[:, tt - 1:tt]
    gate = -LOG2E * cum
    g_hi, g_mid, g_lo = _split3(gate)

    def aug_b(s):
        pick = lambda a: jnp.broadcast_to(a[s:s + 1], (BF16_ROWS, tt))
        return jnp.where(row == 0, pick(g_hi),
                         jnp.where(row == 1, pick(g_mid),
                                   jnp.where(row == 2, pick(g_lo), _ones_rows(row, B_BIAS_ROWS))))

    pos = t * tt + lax.broadcasted_iota(jnp.int32, (BF16_ROWS, tt), 1)
    pos_a = lax.shift_right_logical(pos, int(math.log2(CHUNK))).astype(f32)
    pos_b = lax.bitwise_and(pos, CHUNK - 1).astype(f32)
    aug_a = jnp.where(row < 3, pos_a, jnp.where(row < 6, pos_b, _ones_rows(row, A_BIAS_ROWS)))
    aqn = _head_norm(_proj_t(wt_ref, 0, 512, xn), aqg_ref, QK_SCALE)
    aq_ref[0] = aqn.astype(bf16)
    akn = _head_norm(_proj_t(wt_ref, 512, 1024, xn), akg_ref, 1.0)
    _store_keys(ak_ref, akn, lambda s: aug_a)
    self_a = jnp.sum(aqn * akn, axis=1, keepdims=True)
    for s in range(A_STREAMS):
        am_ref[0, s] = self_a[s] + float(A_RATES[s // 2]) * pos[:1].astype(f32)
    _store_heads(av_ref, _proj_t(wt_ref, 1024, 1536, xn))
    _store_heads(asg_ref, _silu(_proj_t(wt_ref, 1536, 2048, xn)))

    bqn = _head_norm(_proj_t(wt_ref, 2048, 2560, xn), bqg_ref, QK_SCALE)
    bq_ref[0] = bqn.astype(bf16)
    bkn = _head_norm(_proj_t(wt_ref, 2560, 3072, xn), bkg_ref, 1.0)
    _store_keys(bk_ref, bkn, aug_b)
    self_b = jnp.sum(bqn * bkn, axis=1, keepdims=True)
    for s in range(B_HEADS):
        bm_ref[0, s] = self_b[s] + gate[s:s + 1]
    _store_heads(bv_ref, _proj_t(wt_ref, 3072, 3584, xn))
    _store_heads(bsg_ref, _silu(_proj_t(wt_ref, 3584, 4096, xn)))


def _proj_even(x, ln_g, wt, aqg, akg, bqg, bkg, bfb):
    bsz, seq, _ = x.shape
    tt = PROJ_TOKENS
    col = lambda n: pl.BlockSpec((n, 1), lambda b, t: (0, 0))
    fm = lambda n, d: pl.BlockSpec((1, n, d, tt), lambda b, t: (b, 0, 0, t))
    km = lambda n: pl.BlockSpec((1, n, tt, KPAD), lambda b, t: (b, 0, t, 0))
    fm_shape = lambda n, d: jax.ShapeDtypeStruct((bsz, n, d, seq), bf16)
    km_shape = lambda n: jax.ShapeDtypeStruct((bsz, n, seq, KPAD), bf16)
    return pl.pallas_call(
        _proj_even_kernel,
        grid=(bsz, seq // tt),
        in_specs=[
            pl.BlockSpec((1, tt, D_MODEL), lambda b, t: (b, t, 0)),
            pl.BlockSpec((1, D_MODEL), lambda b, t: (0, 0)),
            pl.BlockSpec(wt.shape, lambda b, t: (0, 0)),
            col(HEAD_DIM), col(HEAD_DIM), col(HEAD_DIM), col(HEAD_DIM), col(B_HEADS),
        ],
        out_specs=[fm(A_STREAMS, HEAD_DIM), km(A_STREAMS), fm(A_HEADS, A_VDIM), fm(A_HEADS, A_VDIM),
                   fm(B_HEADS, HEAD_DIM), km(B_HEADS), fm(B_HEADS, HEAD_DIM), fm(B_HEADS, HEAD_DIM),
                   fm(A_STREAMS, 1), fm(B_HEADS, 1)],
        out_shape=[fm_shape(A_STREAMS, HEAD_DIM), km_shape(A_STREAMS), fm_shape(A_HEADS, A_VDIM),
                   fm_shape(A_HEADS, A_VDIM), fm_shape(B_HEADS, HEAD_DIM), km_shape(B_HEADS),
                   fm_shape(B_HEADS, HEAD_DIM), fm_shape(B_HEADS, HEAD_DIM),
                   jax.ShapeDtypeStruct((bsz, A_STREAMS, 1, seq), f32),
                   jax.ShapeDtypeStruct((bsz, B_HEADS, 1, seq), f32)],
        scratch_shapes=[pltpu.VMEM((B_HEADS, 1), f32)],
        compiler_params=pltpu.CompilerParams(
            dimension_semantics=("arbitrary", "arbitrary"), vmem_limit_bytes=VMEM_LIMIT),
        name="proj_even",
    )(x, ln_g, wt, aqg, akg, bqg, bkg, bfb)


def _proj_odd_kernel(x_ref, lng_ref, wt_ref, cqg_ref, ckg_ref, dqg_ref, dkg_ref,
                     cq_ref, ck_ref, cv_ref, csg_ref, dq_ref, dk_ref, dv_ref, dsg_ref,
                     cm_ref, dm_ref):
    tt = x_ref.shape[1]
    xn = _rms_rows(x_ref, lng_ref)
    ones = _ones_rows(lax.broadcasted_iota(jnp.int32, (BF16_ROWS, tt), 0), 0)
    aug = lambda s: ones

    cqn = _head_norm(_proj_t(wt_ref, 0, 512, xn), cqg_ref, QK_SCALE)
    cq_ref[0] = cqn.astype(bf16)
    ckn = _head_norm(_proj_t(wt_ref, 512, 640, xn), ckg_ref, 1.0)
    _store_keys(ck_ref, ckn, aug)
    cm_ref[0] = jnp.sum(cqn.reshape(C_KV_HEADS, C_GROUP, HEAD_DIM, tt) * ckn[:, None], axis=2,
                        keepdims=True).reshape(C_HEADS, 1, tt)
    _store_heads(cv_ref, _proj_t(wt_ref, 640, 768, xn))
    _store_heads(csg_ref, _silu(_proj_t(wt_ref, 768, 1280, xn)))
    dqn = _head_norm(_proj_t(wt_ref, 1280, 1792, xn), dqg_ref, QK_SCALE)
    dq_ref[0] = dqn.astype(bf16)
    dkn = _head_norm(_proj_t(wt_ref, 1792, 2304, xn), dkg_ref, 1.0)
    _store_keys(dk_ref, dkn, aug)
    dm_ref[0] = jnp.sum(dqn * dkn, axis=1, keepdims=True)
    _store_heads(dv_ref, _proj_t(wt_ref, 2304, 2816, xn))
    _store_heads(dsg_ref, _silu(_proj_t(wt_ref, 2816, 3328, xn)))


def _proj_odd(x, ln_g, wt, cqg, ckg, dqg, dkg):
    bsz, seq, _ = x.shape
    tt = PROJ_TOKENS
    col = lambda n: pl.BlockSpec((n, 1), lambda b, t: (0, 0))
    fm = lambda n: pl.BlockSpec((1, n, HEAD_DIM, tt), lambda b, t: (b, 0, 0, t))
    km = lambda n: pl.BlockSpec((1, n, tt, KPAD), lambda b, t: (b, 0, t, 0))
    fm_shape = lambda n: jax.ShapeDtypeStruct((bsz, n, HEAD_DIM, seq), bf16)
    km_shape = lambda n: jax.ShapeDtypeStruct((bsz, n, seq, KPAD), bf16)
    return pl.pallas_call(
        _proj_odd_kernel,
        grid=(bsz, seq // tt),
        in_specs=[
            pl.BlockSpec((1, tt, D_MODEL), lambda b, t: (b, t, 0)),
            pl.BlockSpec((1, D_MODEL), lambda b, t: (0, 0)),
            pl.BlockSpec(wt.shape, lambda b, t: (0, 0)),
            col(HEAD_DIM), col(HEAD_DIM), col(HEAD_DIM), col(HEAD_DIM),
        ],
        out_specs=[fm(C_HEADS), km(C_KV_HEADS), fm(C_KV_HEADS), fm(C_HEADS),
                   fm(D_HEADS), km(D_HEADS), fm(D_HEADS), fm(D_HEADS),
                   pl.BlockSpec((1, C_HEADS, 1, tt), lambda b, t: (b, 0, 0, t)),
                   pl.BlockSpec((1, D_HEADS, 1, tt), lambda b, t: (b, 0, 0, t))],
        out_shape=[fm_shape(C_HEADS), km_shape(C_KV_HEADS), fm_shape(C_KV_HEADS), fm_shape(C_HEADS),
                   fm_shape(D_HEADS), km_shape(D_HEADS), fm_shape(D_HEADS), fm_shape(D_HEADS),
                   jax.ShapeDtypeStruct((bsz, C_HEADS, 1, seq), f32),
                   jax.ShapeDtypeStruct((bsz, D_HEADS, 1, seq), f32)],
        compiler_params=pltpu.CompilerParams(
            dimension_semantics=("arbitrary", "arbitrary"), vmem_limit_bytes=VMEM_LIMIT),
        name="proj_odd",
    )(x, ln_g, wt, cqg, ckg, dqg, dkg)


def _online_step(carry, s, v):
    m, l, acc = carry
    m_new = jnp.maximum(m, jnp.max(s, axis=0, keepdims=True))
    p = jnp.exp2(s - m_new)
    alpha = jnp.exp2(m - m_new)
    l = alpha * l + jnp.sum(p, axis=0, keepdims=True)
    acc = alpha * acc + jnp.dot(v, p.astype(bf16), preferred_element_type=f32)
    return m_new, l, acc


def _colsum8(p):
    return p.reshape(p.shape[0] // 8, 8, p.shape[1]).sum(axis=0)


def _augment_q(q, aug_col, n_bias, m=None):
    tq = q.shape[1]
    aug = jnp.broadcast_to(aug_col, (KPAD - HEAD_DIM, tq))
    if m is not None:
        row = lax.broadcasted_iota(jnp.int32, aug.shape, 0)
        for r, piece in enumerate(_split3(-m)):
            aug = jnp.where(row == n_bias + r, piece, aug)
    return jnp.concatenate([q, aug.astype(bf16)], axis=0)


def _staggered(n, scores, finish):
    out, pending = [], scores(0)
    for c in range(1, n):
        nxt = scores(c)
        out.append(finish(c - 1, pending))
        pending = nxt
    out.append(finish(n - 1, pending))
    return out


def _diag_full(chains, d0, tq):
    def scores(c):
        k_at, _, q, aug_col, n_bias, diag_bias, _ = chains[c]
        return jnp.dot(k_at(d0, tq), _augment_q(q, aug_col, n_bias),
                       preferred_element_type=f32) + diag_bias

    def finish(c, s):
        m = jnp.max(s, axis=0, keepdims=True)
        p = jnp.exp2(s - m)
        return m, _colsum8(p), jnp.dot(chains[c][1](d0, tq), p.astype(bf16), preferred_element_type=f32)

    return _staggered(len(chains), scores, finish)


def _diag_halves(chains, q_aug, d0, tq):
    h = tq // 2
    d1 = pl.multiple_of(d0 + h, h)

    def scores(c):
        k_at = chains[c][0]
        return (jnp.dot(k_at(d0, h), q_aug[c], preferred_element_type=f32),
                jnp.dot(k_at(d1, h), q_aug[c][:, h:], preferred_element_type=f32))

    def finish(c, s):
        v_at, bias = chains[c][1], chains[c][5][:h, :h]
        p0 = jnp.concatenate([jnp.exp2(s[0][:, :h] + bias), jnp.exp2(s[0][:, h:])], axis=1)
        p1 = jnp.exp2(s[1] + bias)
        l0 = _colsum8(p0)
        a0 = jnp.dot(v_at(d0, h), p0.astype(bf16), preferred_element_type=f32)
        a1 = jnp.dot(v_at(d1, h), p1.astype(bf16), preferred_element_type=f32)
        return (jnp.concatenate([l0[:, :h], l0[:, h:] + _colsum8(p1)], axis=1),
                jnp.concatenate([a0[:, :h], a0[:, h:] + a1], axis=1))

    return _staggered(len(chains), scores, finish)


def _causal_sweep(chains, i, fixed_max):
    n = len(chains)
    tq = chains[0][2].shape[1]
    d0 = pl.multiple_of(i * tq, tq)
    n_tiles = i * (tq // ATT_TK)
    tile = lambda j: (pl.multiple_of(j * ATT_TK, ATT_TK), ATT_TK)

    if fixed_max:
        q_aug = [_augment_q(q, aug_col, n_bias, m) for _, _, q, aug_col, n_bias, _, m in chains]

        def body(j, carries):
            def finish(c, s):
                p = jnp.exp2(s)
                l, acc = carries[c]
                return (l + _colsum8(p), acc + jnp.dot(chains[c][1](*tile(j)), p.astype(bf16),
                                                       preferred_element_type=f32))

            scores = lambda c: jnp.dot(chains[c][0](*tile(j)), q_aug[c], preferred_element_type=f32)
            return tuple(_staggered(n, scores, finish))

        carries = lax.fori_loop(0, n_tiles, body, tuple(_diag_halves(chains, q_aug, d0, tq)))
        return [(acc, jnp.sum(l, axis=0, keepdims=True)) for l, acc in carries]

    q_aug = [_augment_q(q, aug_col, n_bias) for _, _, q, aug_col, n_bias, _, _ in chains]

    def body(j, carries):
        scores = lambda c: jnp.dot(chains[c][0](*tile(j)), q_aug[c], preferred_element_type=f32)
        finish = lambda c, s: _online_step(carries[c], s, chains[c][1](*tile(j)))
        return tuple(_staggered(n, scores, finish))

    init = tuple((m, jnp.sum(l, axis=0, keepdims=True), acc) for m, l, acc in _diag_full(chains, d0, tq))
    return [(acc, l) for _, l, acc in lax.fori_loop(0, n_tiles, body, init)]


def _either_sweep(fixed_ref, run):
    @pl.when(fixed_ref[0] != 0)
    def _():
        run(True)

    @pl.when(fixed_ref[0] == 0)
    def _():
        run(False)


def _attn_a_kernel(fixed_ref, q_ref, k_ref, v_ref, sg_ref, m_ref, dtab_ref, qaug_ref, subg_ref,
                   lamv_ref, o_ref, *, lam_init):
    i = pl.program_id(2)
    chains = []
    for h in range(A_HEADS_PER_STEP):
        v_at = lambda start, size, h=h: v_ref[0, h, :, pl.ds(start, size)]
        for c in range(2):
            s = 2 * h + c
            k_at = lambda start, size, s=s: k_ref[0, s, pl.ds(start, size), :]
            chains.append((k_at, v_at, q_ref[0, s], qaug_ref[h], A_BIAS_ROWS, dtab_ref[h],
                           m_ref[0, s]))

    def run(fixed_max):
        lv = lamv_ref[...]
        lam = (jnp.exp(jnp.sum(lv[0:1] * lv[1:2], axis=1, keepdims=True))
               - jnp.exp(jnp.sum(lv[2:3] * lv[3:4], axis=1, keepdims=True)) + lam_init)
        outs = [acc * (1.0 / l) for acc, l in _causal_sweep(chains, i, fixed_max)]
        for h in range(A_HEADS_PER_STEP):
            o = outs[2 * h] - lam * outs[2 * h + 1]
            ms = jnp.mean(o * o, axis=0, keepdims=True)
            y = o * lax.rsqrt(ms + NORM_EPS) * (subg_ref[...] * (1.0 - lam_init))
            o_ref[0, h] = (y * sg_ref[0, h].astype(f32)).astype(bf16)

    _either_sweep(fixed_ref, run)


def _attn_a(fixed, aq, ak, av, asg, am, dtab, qaug, subg, lamv, lam_init):
    bsz, _, _, seq = aq.shape
    nq = seq // ATT_TQ
    hs = A_HEADS_PER_STEP
    return pl.pallas_call(
        functools.partial(_attn_a_kernel, lam_init=lam_init),
        grid=(bsz, A_HEADS // hs, nq),
        in_specs=[
            pl.BlockSpec(memory_space=pltpu.SMEM),
            pl.BlockSpec((1, 2 * hs, HEAD_DIM, ATT_TQ), lambda b, h, i: (b, h, 0, i)),
            pl.BlockSpec((1, 2 * hs, seq, KPAD), lambda b, h, i: (b, h, 0, 0)),
            pl.BlockSpec((1, hs, A_VDIM, seq), lambda b, h, i: (b, h, 0, 0)),
            pl.BlockSpec((1, hs, A_VDIM, ATT_TQ), lambda b, h, i: (b, h, 0, i)),
            pl.BlockSpec((1, 2 * hs, 1, ATT_TQ), lambda b, h, i: (b, h, 0, i)),
            pl.BlockSpec((hs, ATT_TQ, ATT_TQ), lambda b, h, i: (h, 0, 0)),
            pl.BlockSpec((hs, KPAD - HEAD_DIM, 1), lambda b, h, i: (h, 0, 0)),
            pl.BlockSpec((A_VDIM, 1), lambda b, h, i: (0, 0)),
            pl.BlockSpec((4, HEAD_DIM), lambda b, h, i: (0, 0)),
        ],
        out_specs=pl.BlockSpec((1, hs, A_VDIM, ATT_TQ), lambda b, h, i: (b, h, 0, i)),
        out_shape=jax.ShapeDtypeStruct((bsz, A_HEADS, A_VDIM, seq), bf16),
        compiler_params=pltpu.CompilerParams(
            dimension_semantics=("arbitrary", "arbitrary", "arbitrary"),
            vmem_limit_bytes=VMEM_LIMIT),
        name="attn_a",
    )(fixed, aq, ak, av, asg, am, dtab, qaug, subg, lamv)


def _attn_b_kernel(fixed_ref, q_ref, k_ref, v_ref, sg_ref, m_ref, qaug_ref, o_ref):
    i = pl.program_id(2)

    def run(fixed_max):
        causal = jnp.where(lax.broadcasted_iota(jnp.int32, (ATT_TQ, ATT_TQ), 0)
                           <= lax.broadcasted_iota(jnp.int32, (ATT_TQ, ATT_TQ), 1), 0.0, NEG)
        chains = []
        for h in range(B_HEADS_PER_STEP):
            k_at = lambda start, size, h=h: k_ref[0, h, pl.ds(start, size), :]
            v_at = lambda start, size, h=h: v_ref[0, h, :, pl.ds(start, size)]
            chains.append((k_at, v_at, q_ref[0, h], qaug_ref[...], B_BIAS_ROWS, causal, m_ref[0, h]))
        for h, (acc, l) in enumerate(_causal_sweep(chains, i, fixed_max)):
            o_ref[0, h] = (acc * (1.0 / l) * sg_ref[0, h].astype(f32)).astype(bf16)

    _either_sweep(fixed_ref, run)


def _attn_b(fixed, bq, bk, bv, bsg, bm, qaug):
    bsz, _, _, seq = bq.shape
    nq = seq // ATT_TQ
    hs = B_HEADS_PER_STEP
    return pl.pallas_call(
        _attn_b_kernel,
        grid=(bsz, B_HEADS // hs, nq),
        in_specs=[
            pl.BlockSpec(memory_space=pltpu.SMEM),
            pl.BlockSpec((1, hs, HEAD_DIM, ATT_TQ), lambda b, h, i: (b, h, 0, i)),
            pl.BlockSpec((1, hs, seq, KPAD), lambda b, h, i: (b, h, 0, 0)),
            pl.BlockSpec((1, hs, HEAD_DIM, seq), lambda b, h, i: (b, h, 0, 0)),
            pl.BlockSpec((1, hs, HEAD_DIM, ATT_TQ), lambda b, h, i: (b, h, 0, i)),
            pl.BlockSpec((1, hs, 1, ATT_TQ), lambda b, h, i: (b, h, 0, i)),
            pl.BlockSpec((KPAD - HEAD_DIM, 1), lambda b, h, i: (0, 0)),
        ],
        out_specs=pl.BlockSpec((1, hs, HEAD_DIM, ATT_TQ), lambda b, h, i: (b, h, 0, i)),
        out_shape=jax.ShapeDtypeStruct((bsz, B_HEADS, HEAD_DIM, seq), bf16),
        compiler_params=pltpu.CompilerParams(
            dimension_semantics=("arbitrary", "arbitrary", "arbitrary"),
            vmem_limit_bytes=VMEM_LIMIT),
        name="attn_b",
    )(fixed, bq, bk, bv, bsg, bm, qaug)


def _band_kernel(fixed_ref, q_ref, k_ref, v_ref, sg_ref, m_ref, tab_ref, sink_ref, o_ref,
                 *, group, back, tq):
    blocks = tq // LANES
    band = (back + blocks) * LANES
    tiles = q_ref.shape[3] // tq
    work = [(u, h) for u in range(tiles) for h in range(k_ref.shape[1])]
    grouped = lambda ref, u, h: jnp.concatenate(
        [ref[0, h * group + g, :, u * tq:(u + 1) * tq] for g in range(group)], axis=1)

    def window(u):
        first = (pl.program_id(2) * tiles + u) * blocks
        return (pl.multiple_of(jnp.maximum(first - back, 0) * LANES, LANES),
                pl.multiple_of(jnp.maximum(back - first, 0) * LANES, LANES))

    def run(fixed_max):
        def scores(w):
            u, h = work[w]
            k_start, tab_start = window(u)
            q = _augment_q(grouped(q_ref, u, h), 0.0, 0, grouped(m_ref, u, h) if fixed_max else None)
            s = jnp.dot(k_ref[0, h, pl.ds(k_start, band), :], q, preferred_element_type=f32)
            return s + tab_ref[h, pl.ds(tab_start, band), :]

        def finish(w, s):
            u, h = work[w]
            k_start, _ = window(u)
            if fixed_max:
                m = grouped(m_ref, u, h)
                p = jnp.exp2(s)
            else:
                m = jnp.maximum(jnp.max(s, axis=0, keepdims=True), sink_ref[h])
                p = jnp.exp2(s - m)
            l = jnp.sum(p, axis=0, keepdims=True) + jnp.exp2(sink_ref[h] - m)
            o = jnp.dot(v_ref[0, h, :, pl.ds(k_start, band)], p.astype(bf16),
                        preferred_element_type=f32) * (1.0 / l)
            for g in range(group):
                hq = h * group + g
                gate = sg_ref[0, hq, :, u * tq:(u + 1) * tq].astype(f32)
                o_ref[0, hq, :, u * tq:(u + 1) * tq] = (o[:, g * tq:(g + 1) * tq] * gate).astype(bf16)

        _staggered(len(work), scores, finish)

    _either_sweep(fixed_ref, run)


def _band_attn(fixed, q, k, v, sg, m, tab, sink, group, back, tq, kv_per_step, tiles_per_step, name):
    bsz, nheads, _, seq = q.shape
    hs = kv_per_step
    tile = tq * tiles_per_step
    return pl.pallas_call(
        functools.partial(_band_kernel, group=group, back=back, tq=tq),
        grid=(bsz, nheads // (group * hs), seq // tile),
        in_specs=[
            pl.BlockSpec(memory_space=pltpu.SMEM),
            pl.BlockSpec((1, hs * group, HEAD_DIM, tile), lambda b, h, i: (b, h, 0, i)),
            pl.BlockSpec((1, hs, seq, KPAD), lambda b, h, i: (b, h, 0, 0)),
            pl.BlockSpec((1, hs, HEAD_DIM, seq), lambda b, h, i: (b, h, 0, 0)),
            pl.BlockSpec((1, hs * group, HEAD_DIM, tile), lambda b, h, i: (b, h, 0, i)),
            pl.BlockSpec((1, hs * group, 1, tile), lambda b, h, i: (b, h, 0, i)),
            pl.BlockSpec((hs,) + tab.shape[1:], lambda b, h, i: (h, 0, 0)),
            pl.BlockSpec((hs,) + sink.shape[1:], lambda b, h, i: (h, 0, 0)),
        ],
        out_specs=pl.BlockSpec((1, hs * group, HEAD_DIM, tile), lambda b, h, i: (b, h, 0, i)),
        out_shape=jax.ShapeDtypeStruct((bsz, nheads, HEAD_DIM, seq), bf16),
        compiler_params=pltpu.CompilerParams(
            dimension_semantics=("arbitrary", "arbitrary", "arbitrary"),
            vmem_limit_bytes=VMEM_LIMIT),
        name=name,
    )(fixed, q, k, v, sg, m, tab, sink)


def _out_proj_kernel(m1_ref, m2_ref, wt_ref, x_ref, o_ref):
    half = m1_ref.shape[1]
    y_t = (jnp.dot(wt_ref[:, :half], m1_ref[0], preferred_element_type=f32)
           + jnp.dot(wt_ref[:, half:], m2_ref[0], preferred_element_type=f32))
    o_ref[0] = x_ref[0] + y_t.T


def _out_proj(m1, m2, wt, x):
    bsz, seq, _ = x.shape
    tt = PROJ_TOKENS
    half = m1.shape[1]
    return pl.pallas_call(
        _out_proj_kernel,
        grid=(bsz, seq // tt),
        in_specs=[
            pl.BlockSpec((1, half, tt), lambda b, t: (b, 0, t)),
            pl.BlockSpec((1, half, tt), lambda b, t: (b, 0, t)),
            pl.BlockSpec(wt.shape, lambda b, t: (0, 0)),
            pl.BlockSpec((1, tt, D_MODEL), lambda b, t: (b, t, 0)),
        ],
        out_specs=pl.BlockSpec((1, tt, D_MODEL), lambda b, t: (b, t, 0)),
        out_shape=jax.ShapeDtypeStruct(x.shape, f32),
        compiler_params=pltpu.CompilerParams(
            dimension_semantics=("arbitrary", "arbitrary"), vmem_limit_bytes=VMEM_LIMIT),
        name="out_proj",
    )(m1, m2, wt, x)


def _alibi_slopes(n):
    return 2.0 ** (-8.0 * np.arange(1, n + 1, dtype=np.float64) / n)


def _np_split3(v):
    v = np.asarray(v, np.float32)
    to_bf = lambda a: a.astype(bf16).astype(np.float32)
    hi = to_bf(v)
    mid = to_bf(v - hi)
    lo = to_bf(v - hi - mid)
    return hi, mid, lo


def _a_tables():
    rate = A_RATES
    qaug = np.zeros((A_HEADS, KPAD - HEAD_DIM, 1), np.float32)
    for idx, piece in enumerate(_np_split3(rate * CHUNK) + _np_split3(rate)):
        qaug[:, idx, 0] = piece
    kk = np.arange(ATT_TQ)[:, None]
    qq = np.arange(ATT_TQ)[None, :]
    future = np.maximum(kk - qq, 0).astype(np.float32)
    corr = -2.0 * rate[:, None, None] * future[None]
    allowed = (kk // CHUNK) <= (qq // CHUNK)
    dtab = np.where(allowed[None], corr, NEG).astype(np.float32)
    return jnp.asarray(qaug), jnp.asarray(dtab)


def _band_frames(back, tq):
    k_pos = np.arange(back * LANES + tq)[:, None]
    q_pos = back * LANES + np.arange(tq)[None, :]
    return q_pos - k_pos, q_pos // CHUNK - k_pos // CHUNK


def _c_tables(sinks):
    back, tq = WIN_CHUNKS * CHUNK // LANES, C_BAND_TQ
    rel, chunk_diff = _band_frames(back, tq)
    allowed = (chunk_diff >= 0) & (chunk_diff <= WIN_CHUNKS)
    slopes = _alibi_slopes(C_HEADS)
    per_head = np.where(allowed[None], -slopes[:, None, None] * np.abs(rel)[None] * LOG2E, NEG)
    tab = per_head.reshape(C_KV_HEADS, C_GROUP, *rel.shape).transpose(0, 2, 1, 3)
    tab = tab.reshape(C_KV_HEADS, rel.shape[0], C_GROUP * tq).astype(np.float32)
    tab = np.concatenate([tab, np.full((C_KV_HEADS, back * LANES, tab.shape[2]), NEG, np.float32)], 1)
    sink = jnp.repeat(sinks.astype(f32) * LOG2E, tq).reshape(C_KV_HEADS, 1, C_GROUP * tq)
    return jnp.asarray(tab), sink, back


def _d_tables(rel_table):
    back, t = D_LEFT_CHUNKS * CHUNK // LANES, D_BAND_TQ
    band = back * LANES + t
    rel, chunk_diff = _band_frames(back, t)
    allowed = (chunk_diff >= 0) & (chunk_diff <= D_LEFT_CHUNKS)
    tbl = rel_table.astype(f32) * LOG2E
    n_lo = (t - 1) - (CHUNK - 1)
    n_hi = (band - 1) - REL_MAX
    diag = jnp.concatenate([jnp.broadcast_to(tbl[:, :1], (D_HEADS, n_lo)), tbl,
                            jnp.broadcast_to(tbl[:, -1:], (D_HEADS, n_hi))], axis=1)
    m = t + LANES - 1
    blocks = []
    for kb in range(band // LANES):
        lo = rel[kb * LANES:(kb + 1) * LANES].min()
        if lo >= REL_MAX:
            blocks.append(jnp.broadcast_to(tbl[:, -1:, None], (D_HEADS, LANES, t)))
            continue
        window = diag[:, band - (kb + 1) * LANES:band - (kb + 1) * LANES + m]
        skew = jnp.broadcast_to(window[:, None, :], (D_HEADS, LANES + 1, m)).reshape(D_HEADS, -1)
        skew = skew[:, :LANES * (m + 1)].reshape(D_HEADS, LANES, m + 1)[:, :, :t]
        blocks.append(jnp.flip(skew, axis=1))
    tab = jnp.where(jnp.asarray(allowed)[None], jnp.concatenate(blocks, axis=1), NEG)
    tab = jnp.concatenate([tab, jnp.full((D_HEADS, back * LANES, t), NEG, f32)], axis=1)
    sink = jnp.full((D_HEADS, 1, t), NEG, f32)
    return tab, sink, back


def _fixed_max_ok(q_gain, k_gain, bias_range=0.0):
    spread = (2.0 * 1.02 * QK_SCALE * HEAD_DIM
              * jnp.max(jnp.abs(q_gain.astype(f32))) * jnp.max(jnp.abs(k_gain.astype(f32))))
    return (spread + bias_range <= FIXED_MAX_LIMIT).astype(jnp.int32).reshape(1)


def _pad_rows(w_t, rows):
    return jnp.pad(w_t, ((0, rows - w_t.shape[0]), (0, 0)))


def _even_layer(x, ln_g, w_in, w_out, a_qn_g, a_kn_g, a_lq1, a_lk1, a_lq2, a_lk2, a_subln_g,
                b_qn_g, b_kn_g, b_f_bias, layer_idx):
    bsz, seq, _ = x.shape
    colv = lambda v: v.astype(f32).reshape(-1, 1)
    wt = _pad_rows(w_in.T.astype(bf16), 4096 + BF16_ROWS)
    aq, ak, av, asg, bq, bk, bv, bsg, am, bm = _proj_even(
        x, ln_g.astype(f32).reshape(1, -1), wt, colv(a_qn_g), colv(a_kn_g), colv(b_qn_g),
        colv(b_kn_g), colv(b_f_bias))
    lam_init = 0.8 - 0.6 * math.exp(-0.3 * layer_idx)
    qaug_a, dtab = _a_tables()
    lamv = jnp.stack([a_lq1, a_lk1, a_lq2, a_lk2]).astype(f32)
    mix_a = _attn_a(_fixed_max_ok(a_qn_g, a_kn_g), aq, ak, av, asg, am, dtab, qaug_a,
                    colv(a_subln_g), lamv, lam_init)
    qaug_b = np.zeros((KPAD - HEAD_DIM, 1), np.float32)
    qaug_b[:B_BIAS_ROWS] = 1.0
    mix_b = _attn_b(_fixed_max_ok(b_qn_g, b_kn_g), bq, bk, bv, bsg, bm, jnp.asarray(qaug_b))
    return _out_proj(mix_a.reshape(bsz, -1, seq), mix_b.reshape(bsz, -1, seq),
                     w_out.T.astype(bf16), x)


def _odd_layer(x, ln_g, w_in, w_out, c_qn_g, c_kn_g, c_sinks, d_qn_g, d_kn_g, d_rel_bias):
    bsz, seq, _ = x.shape
    colv = lambda v: v.astype(f32).reshape(-1, 1)
    cq, ck, cv, csg, dq, dk, dv, dsg, cm, dm = _proj_odd(
        x, ln_g.astype(f32).reshape(1, -1), w_in.T.astype(bf16), colv(c_qn_g), colv(c_kn_g),
        colv(d_qn_g), colv(d_kn_g))
    tab_c, sink_c, back_c = _c_tables(c_sinks)
    fixed_c = _fixed_max_ok(c_qn_g, c_kn_g, LOG2E * jnp.maximum(jnp.max(c_sinks.astype(f32)), 0.0))
    mix_c = _band_attn(fixed_c, cq, ck, cv, csg, cm, tab_c, sink_c, C_GROUP, back_c, C_BAND_TQ,
                       C_KV_HEADS, C_TILES_PER_STEP, "attn_c")
    tab_d, sink_d, back_d = _d_tables(d_rel_bias)
    fixed_d = _fixed_max_ok(d_qn_g, d_kn_g, LOG2E * jnp.max(jnp.abs(d_rel_bias.astype(f32))))
    mix_d = _band_attn(fixed_d, dq, dk, dv, dsg, dm, tab_d, sink_d, 1, back_d, D_BAND_TQ,
                       D_HEADS_PER_STEP, D_TILES_PER_STEP, "attn_d")
    return _out_proj(mix_c.reshape(bsz, -1, seq), mix_d.reshape(bsz, -1, seq),
                     w_out.T.astype(bf16), x)


def kernel(x, even_ln_g, even_w_in, even_w_out, a_q_norm_g, a_k_norm_g, a_lambda_q1, a_lambda_k1, a_lambda_q2, a_lambda_k2, a_subln_g, b_q_norm_g, b_k_norm_g, b_forget_bias, odd_ln_g, odd_w_in, odd_w_out, c_q_norm_g, c_k_norm_g, c_sinks, d_q_norm_g, d_k_norm_g, d_rel_bias):
    depth = even_ln_g.shape[0] + odd_ln_g.shape[0]
    for i in range(depth):
        j = i // 2
        if i % 2 == 0:
            x = _even_layer(x, even_ln_g[j], even_w_in[j], even_w_out[j], a_q_norm_g[j],
                            a_k_norm_g[j], a_lambda_q1[j], a_lambda_k1[j], a_lambda_q2[j],
                            a_lambda_k2[j], a_subln_g[j], b_q_norm_g[j], b_k_norm_g[j],
                            b_forget_bias[j], i)
        else:
            x = _odd_layer(x, odd_ln_g[j], odd_w_in[j], odd_w_out[j], c_q_norm_g[j],
                           c_k_norm_g[j], c_sinks[j], d_q_norm_g[j], d_k_norm_g[j],
                           d_rel_bias[j])
    return x
```

```python
import functools
import math

import numpy as np
import jax
import jax.numpy as jnp
from jax import lax
from jax.experimental import pallas as pl
from jax.experimental.pallas import tpu as pltpu

D_MODEL = 1024
CHUNK = 64
HEAD_DIM = 64
NORM_EPS = 1e-6

A_HEADS = 4
A_STREAMS = 2 * A_HEADS
A_VDIM = 2 * HEAD_DIM
B_HEADS = 8
C_HEADS = 8
C_KV_HEADS = 2
C_GROUP = C_HEADS // C_KV_HEADS
WIN_CHUNKS = 2
D_HEADS = 8
D_LEFT_CHUNKS = 8
REL_MAX = 256

P_EVEN = 8 * 512 + B_HEADS
P_ODD = 512 + 128 + 128 + 512 + 4 * 512

LOG2E = 1.4426950408889634
QK_SCALE = HEAD_DIM ** -0.5 * LOG2E
NEG = -1e30
A_RATES = (2.0 ** (-8.0 * np.arange(1, A_HEADS + 1) / A_HEADS) * LOG2E).astype(np.float32)

LANES = 128
KPAD = 128
BF16_ROWS = 16

PROJ_TOKENS = 512
ATT_TQ = 512
ATT_TK = 512
A_HEADS_PER_STEP = 4
B_HEADS_PER_STEP = 8
A_BIAS_ROWS = 6
B_BIAS_ROWS = 3
MAX_ROWS = 3
FIXED_MAX_LIMIT = 96.0
C_BAND_TQ = 128
D_BAND_TQ = 256
D_HEADS_PER_STEP = 4
C_TILES_PER_STEP = 4
D_TILES_PER_STEP = 2
VMEM_LIMIT = 56 * 1024 * 1024

f32 = jnp.float32
bf16 = jnp.bfloat16


def _split3(v):
    hi = v.astype(bf16).astype(f32)
    r = v - hi
    mid = r.astype(bf16).astype(f32)
    lo = (r - mid).astype(bf16).astype(f32)
    return hi, mid, lo


def _silu(z):
    return z * (1.0 / (1.0 + jnp.exp(-z)))


def _rms_rows(x, g_ref):
    ms = jnp.mean(x * x, axis=-1, keepdims=True)
    return (x * lax.rsqrt(ms + NORM_EPS) * g_ref[...]).astype(bf16)


def _residual_add(m1_ref, m2_ref, wt_ref, x_ref):
    half = m1_ref.shape[1]
    y_t = (jnp.dot(wt_ref[:, :half], m1_ref[0], preferred_element_type=f32)
           + jnp.dot(wt_ref[:, half:], m2_ref[0], preferred_element_type=f32))
    return x_ref[0] + y_t.T


def _proj_t(wt_ref, r0, r1, xn):
    return lax.dot_general(wt_ref[r0:r1, :], xn, (((1,), (1,)), ((), ())),
                           preferred_element_type=f32)


def _head_norm(z_t, gain_col, mult):
    n = z_t.shape[0] // HEAD_DIM
    z3 = z_t.reshape(n, HEAD_DIM, z_t.shape[1])
    ms = jnp.mean(z3 * z3, axis=1, keepdims=True)
    return z3 * lax.rsqrt(ms + NORM_EPS) * (gain_col[...] * mult)[None]


def _ones_rows(row, first):
    return jnp.where((row >= first) & (row < first + 3), 1.0, 0.0)


def _store_heads(o_ref, z_t):
    o_ref[0] = z_t.reshape(o_ref.shape[1], o_ref.shape[2], z_t.shape[1]).astype(bf16)


def _store_keys(k_ref, kn, aug_fn):
    n, _, t = kn.shape
    zeros = jnp.zeros((KPAD - HEAD_DIM - BF16_ROWS, t), f32)
    for s in range(n):
        blk = jnp.concatenate([kn[s], aug_fn(s), zeros], axis=0)
        k_ref[0, s] = blk.T.astype(bf16)


def _proj_even_kernel(x_ref, lng_ref, wt_ref, aqg_ref, akg_ref, bqg_ref, bkg_ref, bfb_ref,
                      aq_ref, ak_ref, av_ref, asg_ref, bq_ref, bk_ref, bv_ref, bsg_ref,
                      am_ref, bm_ref, cum_ref):
    t = pl.program_id(1)
    tt = x_ref.shape[1]
    xn = _rms_rows(x_ref[0], lng_ref)
    row = lax.broadcasted_iota(jnp.int32, (BF16_ROWS, tt), 0)

    z = _proj_t(wt_ref, 4096, 4096 + BF16_ROWS, xn)[:B_HEADS] + bfb_ref[...]
    log_f = jnp.minimum(z, 0.0) - jnp.log(1.0 + jnp.exp(-jnp.abs(z)))
    tri = jnp.where(lax.broadcasted_iota(jnp.int32, (tt, tt), 0)
                    <= lax.broadcasted_iota(jnp.int32, (tt, tt), 1), 1.0, 0.0).astype(bf16)
    pieces = jnp.concatenate(_split3(log_f) + (jnp.zeros_like(log_f),), axis=0).astype(bf16)
    part = jnp.dot(pieces, tri, preferred_element_type=f32)
    local = part[:B_HEADS] + part[B_HEADS:2 * B_HEADS] + part[2 * B_HEADS:3 * B_HEADS]

    @pl.when(t == 0)
    def _():
        cum_ref[...] = jnp.zeros_like(cum_ref)

    cum = cum_ref[...] + local
    cum_ref[...] = cum[:, tt - 1:tt]
    gate = -LOG2E * cum
    g_hi, g_mid, g_lo = _split3(gate)

    def aug_b(s):
        pick = lambda a: jnp.broadcast_to(a[s:s + 1], (BF16_ROWS, tt))
        return jnp.where(row == 0, pick(g_hi),
                         jnp.where(row == 1, pick(g_mid),
                                   jnp.where(row == 2, pick(g_lo), _ones_rows(row, B_BIAS_ROWS))))

    pos = t * tt + lax.broadcasted_iota(jnp.int32, (BF16_ROWS, tt), 1)
    pos_a = lax.shift_right_logical(pos, int(math.log2(CHUNK))).astype(f32)
    pos_b = lax.bitwise_and(pos, CHUNK - 1).astype(f32)
    aug_a = jnp.where(row < 3, pos_a, jnp.where(row < 6, pos_b, _ones_rows(row, A_BIAS_ROWS)))
    aqn = _head_norm(_proj_t(wt_ref, 0, 512, xn), aqg_ref, QK_SCALE)
    aq_ref[0] = aqn.astype(bf16)
    akn = _head_norm(_proj_t(wt_ref, 512, 1024, xn), akg_ref, 1.0)
    _store_keys(ak_ref, akn, lambda s: aug_a)
    self_a = jnp.sum(aqn * akn, axis=1, keepdims=True)
    for s in range(A_STREAMS):
        am_ref[0, s] = self_a[s] + float(A_RATES[s // 2]) * pos[:1].astype(f32)
    _store_heads(av_ref, _proj_t(wt_ref, 1024, 1536, xn))
    _store_heads(asg_ref, _silu(_proj_t(wt_ref, 1536, 2048, xn)))

    bqn = _head_norm(_proj_t(wt_ref, 2048, 2560, xn), bqg_ref, QK_SCALE)
    bq_ref[0] = bqn.astype(bf16)
    bkn = _head_norm(_proj_t(wt_ref, 2560, 3072, xn), bkg_ref, 1.0)
    _store_keys(bk_ref, bkn, aug_b)
    self_b = jnp.sum(bqn * bkn, axis=1, keepdims=True)
    for s in range(B_HEADS):
        bm_ref[0, s] = self_b[s] + gate[s:s + 1]
    _store_heads(bv_ref, _proj_t(wt_ref, 3072, 3584, xn))
    _store_heads(bsg_ref, _silu(_proj_t(wt_ref, 3584, 4096, xn)))


def _proj_even(x, ln_g, wt, aqg, akg, bqg, bkg, bfb):
    bsz, seq, _ = x.shape
    tt = PROJ_TOKENS
    col = lambda n: pl.BlockSpec((n, 1), lambda b, t: (0, 0))
    fm = lambda n, d: pl.BlockSpec((1, n, d, tt), lambda b, t: (b, 0, 0, t))
    km = lambda n: pl.BlockSpec((1, n, tt, KPAD), lambda b, t: (b, 0, t, 0))
    fm_shape = lambda n, d: jax.ShapeDtypeStruct((bsz, n, d, seq), bf16)
    km_shape = lambda n: jax.ShapeDtypeStruct((bsz, n, seq, KPAD), bf16)
    return pl.pallas_call(
        _proj_even_kernel,
        grid=(bsz, seq // tt),
        in_specs=[
            pl.BlockSpec((1, tt, D_MODEL), lambda b, t: (b, t, 0)),
            pl.BlockSpec((1, D_MODEL), lambda b, t: (0, 0)),
            pl.BlockSpec(wt.shape, lambda b, t: (0, 0)),
            col(HEAD_DIM), col(HEAD_DIM), col(HEAD_DIM), col(HEAD_DIM), col(B_HEADS),
        ],
        out_specs=[fm(A_STREAMS, HEAD_DIM), km(A_STREAMS), fm(A_HEADS, A_VDIM), fm(A_HEADS, A_VDIM),
                   fm(B_HEADS, HEAD_DIM), km(B_HEADS), fm(B_HEADS, HEAD_DIM), fm(B_HEADS, HEAD_DIM),
                   fm(A_STREAMS, 1), fm(B_HEADS, 1)],
        out_shape=[fm_shape(A_STREAMS, HEAD_DIM), km_shape(A_STREAMS), fm_shape(A_HEADS, A_VDIM),
                   fm_shape(A_HEADS, A_VDIM), fm_shape(B_HEADS, HEAD_DIM), km_shape(B_HEADS),
                   fm_shape(B_HEADS, HEAD_DIM), fm_shape(B_HEADS, HEAD_DIM),
                   jax.ShapeDtypeStruct((bsz, A_STREAMS, 1, seq), f32),
                   jax.ShapeDtypeStruct((bsz, B_HEADS, 1, seq), f32)],
        scratch_shapes=[pltpu.VMEM((B_HEADS, 1), f32)],
        compiler_params=pltpu.CompilerParams(
            dimension_semantics=("arbitrary", "arbitrary"), vmem_limit_bytes=VMEM_LIMIT),
        name="proj_even",
    )(x, ln_g, wt, aqg, akg, bqg, bkg, bfb)


def _proj_odd_kernel(m1_ref, m2_ref, wo_ref, x_ref, lng_ref, wt_ref, cqg_ref, ckg_ref, dqg_ref,
                     dkg_ref, x1_ref, cq_ref, ck_ref, cv_ref, csg_ref, dq_ref, dk_ref, dv_ref,
                     dsg_ref, cm_ref, dm_ref):
    tt = x_ref.shape[1]
    x1 = _residual_add(m1_ref, m2_ref, wo_ref, x_ref)
    x1_ref[0] = x1
    xn = _rms_rows(x1, lng_ref)
    ones = _ones_rows(lax.broadcasted_iota(jnp.int32, (BF16_ROWS, tt), 0), 0)
    aug = lambda s: ones

    cqn = _head_norm(_proj_t(wt_ref, 0, 512, xn), cqg_ref, QK_SCALE)
    cq_ref[0] = cqn.astype(bf16)
    ckn = _head_norm(_proj_t(wt_ref, 512, 640, xn), ckg_ref, 1.0)
    _store_keys(ck_ref, ckn, aug)
    cm_ref[0] = jnp.sum(cqn.reshape(C_KV_HEADS, C_GROUP, HEAD_DIM, tt) * ckn[:, None], axis=2,
                        keepdims=True).reshape(C_HEADS, 1, tt)
    _store_heads(cv_ref, _proj_t(wt_ref, 640, 768, xn))
    _store_heads(csg_ref, _silu(_proj_t(wt_ref, 768, 1280, xn)))
    dqn = _head_norm(_proj_t(wt_ref, 1280, 1792, xn), dqg_ref, QK_SCALE)
    dq_ref[0] = dqn.astype(bf16)
    dkn = _head_norm(_proj_t(wt_ref, 1792, 2304, xn), dkg_ref, 1.0)
    _store_keys(dk_ref, dkn, aug)
    dm_ref[0] = jnp.sum(dqn * dkn, axis=1, keepdims=True)
    _store_heads(dv_ref, _proj_t(wt_ref, 2304, 2816, xn))
    _store_heads(dsg_ref, _silu(_proj_t(wt_ref, 2816, 3328, xn)))


def _proj_odd(m1, m2, wo_t, x, ln_g, wt, cqg, ckg, dqg, dkg):
    bsz, seq, _ = x.shape
    tt = PROJ_TOKENS
    half = m1.shape[1]
    col = lambda n: pl.BlockSpec((n, 1), lambda b, t: (0, 0))
    rows = pl.BlockSpec((1, tt, D_MODEL), lambda b, t: (b, t, 0))
    fm = lambda n: pl.BlockSpec((1, n, HEAD_DIM, tt), lambda b, t: (b, 0, 0, t))
    km = lambda n: pl.BlockSpec((1, n, tt, KPAD), lambda b, t: (b, 0, t, 0))
    fm_shape = lambda n: jax.ShapeDtypeStruct((bsz, n, HEAD_DIM, seq), bf16)
    km_shape = lambda n: jax.ShapeDtypeStruct((bsz, n, seq, KPAD), bf16)
    return pl.pallas_call(
        _proj_odd_kernel,
        grid=(bsz, seq // tt),
        in_specs=[
            pl.BlockSpec((1, half, tt), lambda b, t: (b, 0, t)),
            pl.BlockSpec((1, half, tt), lambda b, t: (b, 0, t)),
            pl.BlockSpec(wo_t.shape, lambda b, t: (0, 0)),
            rows,
            pl.BlockSpec((1, D_MODEL), lambda b, t: (0, 0)),
            pl.BlockSpec(wt.shape, lambda b, t: (0, 0)),
            col(HEAD_DIM), col(HEAD_DIM), col(HEAD_DIM), col(HEAD_DIM),
        ],
        out_specs=[rows, fm(C_HEADS), km(C_KV_HEADS), fm(C_KV_HEADS), fm(C_HEADS),
                   fm(D_HEADS), km(D_HEADS), fm(D_HEADS), fm(D_HEADS),
                   pl.BlockSpec((1, C_HEADS, 1, tt), lambda b, t: (b, 0, 0, t)),
                   pl.BlockSpec((1, D_HEADS, 1, tt), lambda b, t: (b, 0, 0, t))],
        out_shape=[jax.ShapeDtypeStruct(x.shape, f32),
                   fm_shape(C_HEADS), km_shape(C_KV_HEADS), fm_shape(C_KV_HEADS), fm_shape(C_HEADS),
                   fm_shape(D_HEADS), km_shape(D_HEADS), fm_shape(D_HEADS), fm_shape(D_HEADS),
                   jax.ShapeDtypeStruct((bsz, C_HEADS, 1, seq), f32),
                   jax.ShapeDtypeStruct((bsz, D_HEADS, 1, seq), f32)],
        compiler_params=pltpu.CompilerParams(
            dimension_semantics=("arbitrary", "arbitrary"), vmem_limit_bytes=VMEM_LIMIT),
        name="proj_odd",
    )(m1, m2, wo_t, x, ln_g, wt, cqg, ckg, dqg, dkg)


def _online_step(carry, s, v):
    m, l, acc = carry
    m_new = jnp.maximum(m, jnp.max(s, axis=0, keepdims=True))
    p = jnp.exp2(s - m_new)
    alpha = jnp.exp2(m - m_new)
    l = alpha * l + jnp.sum(p, axis=0, keepdims=True)
    acc = alpha * acc + jnp.dot(v, p.astype(bf16), preferred_element_type=f32)
    return m_new, l, acc


def _colsum8(p):
    return p.reshape(p.shape[0] // 8, 8, p.shape[1]).sum(axis=0)


def _augment_q(q, aug_col, n_bias, m=None):
    tq = q.shape[1]
    aug = jnp.broadcast_to(aug_col, (KPAD - HEAD_DIM, tq))
    if m is not None:
        row = lax.broadcasted_iota(jnp.int32, aug.shape, 0)
        for r, piece in enumerate(_split3(-m)):
            aug = jnp.where(row == n_bias + r, piece, aug)
    return jnp.concatenate([q, aug.astype(bf16)], axis=0)


def _staggered(n, scores, finish):
    out, pending = [], scores(0)
    for c in range(1, n):
        nxt = scores(c)
        out.append(finish(c - 1, pending))
        pending = nxt
    out.append(finish(n - 1, pending))
    return out


def _diag_full(chains, d0, tq):
    def scores(c):
        k_at, _, q, aug_col, n_bias, diag_bias, _ = chains[c]
        return jnp.dot(k_at(d0, tq), _augment_q(q, aug_col, n_bias),
                       preferred_element_type=f32) + diag_bias

    def finish(c, s):
        m = jnp.max(s, axis=0, keepdims=True)
        p = jnp.exp2(s - m)
        return m, _colsum8(p), jnp.dot(chains[c][1](d0, tq), p.astype(bf16), preferred_element_type=f32)

    return _staggered(len(chains), scores, finish)


def _diag_halves(chains, q_aug, d0, tq):
    h = tq // 2
    d1 = pl.multiple_of(d0 + h, h)

    def scores(c):
        k_at = chains[c][0]
        return (jnp.dot(k_at(d0, h), q_aug[c], preferred_element_type=f32),
                jnp.dot(k_at(d1, h), q_aug[c][:, h:], preferred_element_type=f32))

    def finish(c, s):
        v_at, bias = chains[c][1], chains[c][5][:h, :h]
        p0 = jnp.concatenate([jnp.exp2(s[0][:, :h] + bias), jnp.exp2(s[0][:, h:])], axis=1)
        p1 = jnp.exp2(s[1] + bias)
        l0 = _colsum8(p0)
        a0 = jnp.dot(v_at(d0, h), p0.astype(bf16), preferred_element_type=f32)
        a1 = jnp.dot(v_at(d1, h), p1.astype(bf16), preferred_element_type=f32)
        return (jnp.concatenate([l0[:, :h], l0[:, h:] + _colsum8(p1)], axis=1),
                jnp.concatenate([a0[:, :h], a0[:, h:] + a1], axis=1))

    return _staggered(len(chains), scores, finish)


def _causal_sweep(chains, i, fixed_max):
    n = len(chains)
    tq = chains[0][2].shape[1]
    d0 = pl.multiple_of(i * tq, tq)
    n_tiles = i * (tq // ATT_TK)
    tile = lambda j: (pl.multiple_of(j * ATT_TK, ATT_TK), ATT_TK)

    if fixed_max:
        q_aug = [_augment_q(q, aug_col, n_bias, m) for _, _, q, aug_col, n_bias, _, m in chains]

        def body(j, carries):
            def finish(c, s):
                p = jnp.exp2(s)
                l, acc = carries[c]
                return (l + _colsum8(p), acc + jnp.dot(chains[c][1](*tile(j)), p.astype(bf16),
                                                       preferred_element_type=f32))

            scores = lambda c: jnp.dot(chains[c][0](*tile(j)), q_aug[c], preferred_element_type=f32)
            return tuple(_staggered(n, scores, finish))

        carries = lax.fori_loop(0, n_tiles, body, tuple(_diag_halves(chains, q_aug, d0, tq)))
        return [(acc, jnp.sum(l, axis=0, keepdims=True)) for l, acc in carries]

    q_aug = [_augment_q(q, aug_col, n_bias) for _, _, q, aug_col, n_bias, _, _ in chains]

    def body(j, carries):
        scores = lambda c: jnp.dot(chains[c][0](*tile(j)), q_aug[c], preferred_element_type=f32)
        finish = lambda c, s: _online_step(carries[c], s, chains[c][1](*tile(j)))
        return tuple(_staggered(n, scores, finish))

    init = tuple((m, jnp.sum(l, axis=0, keepdims=True), acc) for m, l, acc in _diag_full(chains, d0, tq))
    return [(acc, l) for _, l, acc in lax.fori_loop(0, n_tiles, body, init)]


def _either_sweep(fixed_ref, run):
    @pl.when(fixed_ref[0] != 0)
    def _():
        run(True)

    @pl.when(fixed_ref[0] == 0)
    def _():
        run(False)


def _attn_a_kernel(fixed_ref, q_ref, k_ref, v_ref, sg_ref, m_ref, dtab_ref, qaug_ref, subg_ref,
                   lamv_ref, o_ref, *, lam_init):
    i = pl.program_id(2)
    chains = []
    for h in range(A_HEADS_PER_STEP):
        v_at = lambda start, size, h=h: v_ref[0, h, :, pl.ds(start, size)]
        for c in range(2):
            s = 2 * h + c
            k_at = lambda start, size, s=s: k_ref[0, s, pl.ds(start, size), :]
            chains.append((k_at, v_at, q_ref[0, s], qaug_ref[h], A_BIAS_ROWS, dtab_ref[h],
                           m_ref[0, s]))

    def run(fixed_max):
        lv = lamv_ref[...]
        lam = (jnp.exp(jnp.sum(lv[0:1] * lv[1:2], axis=1, keepdims=True))
               - jnp.exp(jnp.sum(lv[2:3] * lv[3:4], axis=1, keepdims=True)) + lam_init)
        outs = [acc * (1.0 / l) for acc, l in _causal_sweep(chains, i, fixed_max)]
        for h in range(A_HEADS_PER_STEP):
            o = outs[2 * h] - lam * outs[2 * h + 1]
            ms = jnp.mean(o * o, axis=0, keepdims=True)
            y = o * lax.rsqrt(ms + NORM_EPS) * (subg_ref[...] * (1.0 - lam_init))
            o_ref[0, h] = (y * sg_ref[0, h].astype(f32)).astype(bf16)

    _either_sweep(fixed_ref, run)


def _attn_a(fixed, aq, ak, av, asg, am, dtab, qaug, subg, lamv, lam_init):
    bsz, _, _, seq = aq.shape
    nq = seq // ATT_TQ
    hs = A_HEADS_PER_STEP
    return pl.pallas_call(
        functools.partial(_attn_a_kernel, lam_init=lam_init),
        grid=(bsz, A_HEADS // hs, nq),
        in_specs=[
            pl.BlockSpec(memory_space=pltpu.SMEM),
            pl.BlockSpec((1, 2 * hs, HEAD_DIM, ATT_TQ), lambda b, h, i: (b, h, 0, i)),
            pl.BlockSpec((1, 2 * hs, seq, KPAD), lambda b, h, i: (b, h, 0, 0)),
            pl.BlockSpec((1, hs, A_VDIM, seq), lambda b, h, i: (b, h, 0, 0)),
            pl.BlockSpec((1, hs, A_VDIM, ATT_TQ), lambda b, h, i: (b, h, 0, i)),
            pl.BlockSpec((1, 2 * hs, 1, ATT_TQ), lambda b, h, i: (b, h, 0, i)),
            pl.BlockSpec((hs, ATT_TQ, ATT_TQ), lambda b, h, i: (h, 0, 0)),
            pl.BlockSpec((hs, KPAD - HEAD_DIM, 1), lambda b, h, i: (h, 0, 0)),
            pl.BlockSpec((A_VDIM, 1), lambda b, h, i: (0, 0)),
            pl.BlockSpec((4, HEAD_DIM), lambda b, h, i: (0, 0)),
        ],
        out_specs=pl.BlockSpec((1, hs, A_VDIM, ATT_TQ), lambda b, h, i: (b, h, 0, i)),
        out_shape=jax.ShapeDtypeStruct((bsz, A_HEADS, A_VDIM, seq), bf16),
        compiler_params=pltpu.CompilerParams(
            dimension_semantics=("arbitrary", "arbitrary", "arbitrary"),
            vmem_limit_bytes=VMEM_LIMIT),
        name="attn_a",
    )(fixed, aq, ak, av, asg, am, dtab, qaug, subg, lamv)


def _attn_b_kernel(fixed_ref, q_ref, k_ref, v_ref, sg_ref, m_ref, qaug_ref, o_ref):
    i = pl.program_id(2)

    def run(fixed_max):
        causal = jnp.where(lax.broadcasted_iota(jnp.int32, (ATT_TQ, ATT_TQ), 0)
                           <= lax.broadcasted_iota(jnp.int32, (ATT_TQ, ATT_TQ), 1), 0.0, NEG)
        chains = []
        for h in range(B_HEADS_PER_STEP):
            k_at = lambda start, size, h=h: k_ref[0, h, pl.ds(start, size), :]
            v_at = lambda start, size, h=h: v_ref[0, h, :, pl.ds(start, size)]
            chains.append((k_at, v_at, q_ref[0, h], qaug_ref[...], B_BIAS_ROWS, causal, m_ref[0, h]))
        for h, (acc, l) in enumerate(_causal_sweep(chains, i, fixed_max)):
            o_ref[0, h] = (acc * (1.0 / l) * sg_ref[0, h].astype(f32)).astype(bf16)

    _either_sweep(fixed_ref, run)


def _attn_b(fixed, bq, bk, bv, bsg, bm, qaug):
    bsz, _, _, seq = bq.shape
    nq = seq // ATT_TQ
    hs = B_HEADS_PER_STEP
    return pl.pallas_call(
        _attn_b_kernel,
        grid=(bsz, B_HEADS // hs, nq),
        in_specs=[
            pl.BlockSpec(memory_space=pltpu.SMEM),
            pl.BlockSpec((1, hs, HEAD_DIM, ATT_TQ), lambda b, h, i: (b, h, 0, i)),
            pl.BlockSpec((1, hs, seq, KPAD), lambda b, h, i: (b, h, 0, 0)),
            pl.BlockSpec((1, hs, HEAD_DIM, seq), lambda b, h, i: (b, h, 0, 0)),
            pl.BlockSpec((1, hs, HEAD_DIM, ATT_TQ), lambda b, h, i: (b, h, 0, i)),
            pl.BlockSpec((1, hs, 1, ATT_TQ), lambda b, h, i: (b, h, 0, i)),
            pl.BlockSpec((KPAD - HEAD_DIM, 1), lambda b, h, i: (0, 0)),
        ],
        out_specs=pl.BlockSpec((1, hs, HEAD_DIM, ATT_TQ), lambda b, h, i: (b, h, 0, i)),
        out_shape=jax.ShapeDtypeStruct((bsz, B_HEADS, HEAD_DIM, seq), bf16),
        compiler_params=pltpu.CompilerParams(
            dimension_semantics=("arbitrary", "arbitrary", "arbitrary"),
            vmem_limit_bytes=VMEM_LIMIT),
        name="attn_b",
    )(fixed, bq, bk, bv, bsg, bm, qaug)


def _band_kernel(fixed_ref, q_ref, k_ref, v_ref, sg_ref, m_ref, tab_ref, sink_ref, o_ref,
                 *, group, back, tq):
    blocks = tq // LANES
    band = (back + blocks) * LANES
    tiles = q_ref.shape[3] // tq
    work = [(u, h) for u in range(tiles) for h in range(k_ref.shape[1])]
    grouped = lambda ref, u, h: jnp.concatenate(
        [ref[0, h * group + g, :, u * tq:(u + 1) * tq] for g in range(group)], axis=1)

    def window(u):
        first = (pl.program_id(2) * tiles + u) * blocks
        return (pl.multiple_of(jnp.maximum(first - back, 0) * LANES, LANES),
                pl.multiple_of(jnp.maximum(back - first, 0) * LANES, LANES))

    def run(fixed_max):
        def scores(w):
            u, h = work[w]
            k_start, tab_start = window(u)
            q = _augment_q(grouped(q_ref, u, h), 0.0, 0, grouped(m_ref, u, h) if fixed_max else None)
            s = jnp.dot(k_ref[0, h, pl.ds(k_start, band), :], q, preferred_element_type=f32)
            return s + tab_ref[h, pl.ds(tab_start, band), :]

        def finish(w, s):
            u, h = work[w]
            k_start, _ = window(u)
            if fixed_max:
                m = grouped(m_ref, u, h)
                p = jnp.exp2(s)
            else:
                m = jnp.maximum(jnp.max(s, axis=0, keepdims=True), sink_ref[h])
                p = jnp.exp2(s - m)
            l = jnp.sum(p, axis=0, keepdims=True) + jnp.exp2(sink_ref[h] - m)
            o = jnp.dot(v_ref[0, h, :, pl.ds(k_start, band)], p.astype(bf16),
                        preferred_element_type=f32) * (1.0 / l)
            for g in range(group):
                hq = h * group + g
                gate = sg_ref[0, hq, :, u * tq:(u + 1) * tq].astype(f32)
                o_ref[0, hq, :, u * tq:(u + 1) * tq] = (o[:, g * tq:(g + 1) * tq] * gate).astype(bf16)

        _staggered(len(work), scores, finish)

    _either_sweep(fixed_ref, run)


def _band_attn(fixed, q, k, v, sg, m, tab, sink, group, back, tq, kv_per_step, tiles_per_step, name):
    bsz, nheads, _, seq = q.shape
    hs = kv_per_step
    tile = tq * tiles_per_step
    return pl.pallas_call(
        functools.partial(_band_kernel, group=group, back=back, tq=tq),
        grid=(bsz, nheads // (group * hs), seq // tile),
        in_specs=[
            pl.BlockSpec(memory_space=pltpu.SMEM),
            pl.BlockSpec((1, hs * group, HEAD_DIM, tile), lambda b, h, i: (b, h, 0, i)),
            pl.BlockSpec((1, hs, seq, KPAD), lambda b, h, i: (b, h, 0, 0)),
            pl.BlockSpec((1, hs, HEAD_DIM, seq), lambda b, h, i: (b, h, 0, 0)),
            pl.BlockSpec((1, hs * group, HEAD_DIM, tile), lambda b, h, i: (b, h, 0, i)),
            pl.BlockSpec((1, hs * group, 1, tile), lambda b, h, i: (b, h, 0, i)),
            pl.BlockSpec((hs,) + tab.shape[1:], lambda b, h, i: (h, 0, 0)),
            pl.BlockSpec((hs,) + sink.shape[1:], lambda b, h, i: (h, 0, 0)),
        ],
        out_specs=pl.BlockSpec((1, hs * group, HEAD_DIM, tile), lambda b, h, i: (b, h, 0, i)),
        out_shape=jax.ShapeDtypeStruct((bsz, nheads, HEAD_DIM, seq), bf16),
        compiler_params=pltpu.CompilerParams(
            dimension_semantics=("arbitrary", "arbitrary", "arbitrary"),
            vmem_limit_bytes=VMEM_LIMIT),
        name=name,
    )(fixed, q, k, v, sg, m, tab, sink)


def _out_proj_kernel(m1_ref, m2_ref, wt_ref, x_ref, o_ref):
    o_ref[0] = _residual_add(m1_ref, m2_ref, wt_ref, x_ref)


def _out_proj(m1, m2, wt, x):
    bsz, seq, _ = x.shape
    tt = PROJ_TOKENS
    half = m1.shape[1]
    return pl.pallas_call(
        _out_proj_kernel,
        grid=(bsz, seq // tt),
        in_specs=[
            pl.BlockSpec((1, half, tt), lambda b, t: (b, 0, t)),
            pl.BlockSpec((1, half, tt), lambda b, t: (b, 0, t)),
            pl.BlockSpec(wt.shape, lambda b, t: (0, 0)),
            pl.BlockSpec((1, tt, D_MODEL), lambda b, t: (b, t, 0)),
        ],
        out_specs=pl.BlockSpec((1, tt, D_MODEL), lambda b, t: (b, t, 0)),
        out_shape=jax.ShapeDtypeStruct(x.shape, f32),
        compiler_params=pltpu.CompilerParams(
            dimension_semantics=("arbitrary", "arbitrary"), vmem_limit_bytes=VMEM_LIMIT),
        name="out_proj",
    )(m1, m2, wt, x)


def _alibi_slopes(n):
    return 2.0 ** (-8.0 * np.arange(1, n + 1, dtype=np.float64) / n)


def _np_split3(v):
    v = np.asarray(v, np.float32)
    to_bf = lambda a: a.astype(bf16).astype(np.float32)
    hi = to_bf(v)
    mid = to_bf(v - hi)
    lo = to_bf(v - hi - mid)
    return hi, mid, lo


def _a_tables():
    rate = A_RATES
    qaug = np.zeros((A_HEADS, KPAD - HEAD_DIM, 1), np.float32)
    for idx, piece in enumerate(_np_split3(rate * CHUNK) + _np_split3(rate)):
        qaug[:, idx, 0] = piece
    kk = np.arange(ATT_TQ)[:, None]
    qq = np.arange(ATT_TQ)[None, :]
    future = np.maximum(kk - qq, 0).astype(np.float32)
    corr = -2.0 * rate[:, None, None] * future[None]
    allowed = (kk // CHUNK) <= (qq // CHUNK)
    dtab = np.where(allowed[None], corr, NEG).astype(np.float32)
    return jnp.asarray(qaug), jnp.asarray(dtab)


def _band_frames(back, tq):
    k_pos = np.arange(back * LANES + tq)[:, None]
    q_pos = back * LANES + np.arange(tq)[None, :]
    return q_pos - k_pos, q_pos // CHUNK - k_pos // CHUNK


def _c_tables(sinks):
    back, tq = WIN_CHUNKS * CHUNK // LANES, C_BAND_TQ
    rel, chunk_diff = _band_frames(back, tq)
    allowed = (chunk_diff >= 0) & (chunk_diff <= WIN_CHUNKS)
    slopes = _alibi_slopes(C_HEADS)
    per_head = np.where(allowed[None], -slopes[:, None, None] * np.abs(rel)[None] * LOG2E, NEG)
    tab = per_head.reshape(C_KV_HEADS, C_GROUP, *rel.shape).transpose(0, 2, 1, 3)
    tab = tab.reshape(C_KV_HEADS, rel.shape[0], C_GROUP * tq).astype(np.float32)
    tab = np.concatenate([tab, np.full((C_KV_HEADS, back * LANES, tab.shape[2]), NEG, np.float32)], 1)
    sink = jnp.repeat(sinks.astype(f32) * LOG2E, tq).reshape(C_KV_HEADS, 1, C_GROUP * tq)
    return jnp.asarray(tab), sink, back


def _d_tables(rel_table):
    back, t = D_LEFT_CHUNKS * CHUNK // LANES, D_BAND_TQ
    band = back * LANES + t
    rel, chunk_diff = _band_frames(back, t)
    allowed = (chunk_diff >= 0) & (chunk_diff <= D_LEFT_CHUNKS)
    tbl = rel_table.astype(f32) * LOG2E
    n_lo = (t - 1) - (CHUNK - 1)
    n_hi = (band - 1) - REL_MAX
    diag = jnp.concatenate([jnp.broadcast_to(tbl[:, :1], (D_HEADS, n_lo)), tbl,
                            jnp.broadcast_to(tbl[:, -1:], (D_HEADS, n_hi))], axis=1)
    m = t + LANES - 1
    blocks = []
    for kb in range(band // LANES):
        lo = rel[kb * LANES:(kb + 1) * LANES].min()
        if lo >= REL_MAX:
            blocks.append(jnp.broadcast_to(tbl[:, -1:, None], (D_HEADS, LANES, t)))
            continue
        window = diag[:, band - (kb + 1) * LANES:band - (kb + 1) * LANES + m]
        skew = jnp.broadcast_to(window[:, None, :], (D_HEADS, LANES + 1, m)).reshape(D_HEADS, -1)
        skew = skew[:, :LANES * (m + 1)].reshape(D_HEADS, LANES, m + 1)[:, :, :t]
        blocks.append(jnp.flip(skew, axis=1))
    tab = jnp.where(jnp.asarray(allowed)[None], jnp.concatenate(blocks, axis=1), NEG)
    tab = jnp.concatenate([tab, jnp.full((D_HEADS, back * LANES, t), NEG, f32)], axis=1)
    sink = jnp.full((D_HEADS, 1, t), NEG, f32)
    return tab, sink, back


def _fixed_max_ok(q_gain, k_gain, bias_range=0.0):
    spread = (2.0 * 1.02 * QK_SCALE * HEAD_DIM
              * jnp.max(jnp.abs(q_gain.astype(f32))) * jnp.max(jnp.abs(k_gain.astype(f32))))
    return (spread + bias_range <= FIXED_MAX_LIMIT).astype(jnp.int32).reshape(1)


def _pad_rows(w_t, rows):
    return jnp.pad(w_t, ((0, rows - w_t.shape[0]), (0, 0)))


def _even_layer(x, ln_g, w_in, w_out, a_qn_g, a_kn_g, a_lq1, a_lk1, a_lq2, a_lk2, a_subln_g,
                b_qn_g, b_kn_g, b_f_bias, layer_idx):
    bsz, seq, _ = x.shape
    colv = lambda v: v.astype(f32).reshape(-1, 1)
    wt = _pad_rows(w_in.T.astype(bf16), 4096 + BF16_ROWS)
    aq, ak, av, asg, bq, bk, bv, bsg, am, bm = _proj_even(
        x, ln_g.astype(f32).reshape(1, -1), wt, colv(a_qn_g), colv(a_kn_g), colv(b_qn_g),
        colv(b_kn_g), colv(b_f_bias))
    lam_init = 0.8 - 0.6 * math.exp(-0.3 * layer_idx)
    qaug_a, dtab = _a_tables()
    lamv = jnp.stack([a_lq1, a_lk1, a_lq2, a_lk2]).astype(f32)
    mix_a = _attn_a(_fixed_max_ok(a_qn_g, a_kn_g), aq, ak, av, asg, am, dtab, qaug_a,
                    colv(a_subln_g), lamv, lam_init)
    qaug_b = np.zeros((KPAD - HEAD_DIM, 1), np.float32)
    qaug_b[:B_BIAS_ROWS] = 1.0
    mix_b = _attn_b(_fixed_max_ok(b_qn_g, b_kn_g), bq, bk, bv, bsg, bm, jnp.asarray(qaug_b))
    return mix_a.reshape(bsz, -1, seq), mix_b.reshape(bsz, -1, seq), w_out.T.astype(bf16), x


def _odd_layer(pending, ln_g, w_in, w_out, c_qn_g, c_kn_g, c_sinks, d_qn_g, d_kn_g, d_rel_bias):
    bsz, seq, _ = pending[3].shape
    colv = lambda v: v.astype(f32).reshape(-1, 1)
    x, cq, ck, cv, csg, dq, dk, dv, dsg, cm, dm = _proj_odd(
        *pending, ln_g.astype(f32).reshape(1, -1), w_in.T.astype(bf16), colv(c_qn_g), colv(c_kn_g),
        colv(d_qn_g), colv(d_kn_g))
    tab_c, sink_c, back_c = _c_tables(c_sinks)
    fixed_c = _fixed_max_ok(c_qn_g, c_kn_g, LOG2E * jnp.maximum(jnp.max(c_sinks.astype(f32)), 0.0))
    mix_c = _band_attn(fixed_c, cq, ck, cv, csg, cm, tab_c, sink_c, C_GROUP, back_c, C_BAND_TQ,
                       C_KV_HEADS, C_TILES_PER_STEP, "attn_c")
    tab_d, sink_d, back_d = _d_tables(d_rel_bias)
    fixed_d = _fixed_max_ok(d_qn_g, d_kn_g, LOG2E * jnp.max(jnp.abs(d_rel_bias.astype(f32))))
    mix_d = _band_attn(fixed_d, dq, dk, dv, dsg, dm, tab_d, sink_d, 1, back_d, D_BAND_TQ,
                       D_HEADS_PER_STEP, D_TILES_PER_STEP, "attn_d")
    return mix_c.reshape(bsz, -1, seq), mix_d.reshape(bsz, -1, seq), w_out.T.astype(bf16), x


def kernel(x, even_ln_g, even_w_in, even_w_out, a_q_norm_g, a_k_norm_g, a_lambda_q1, a_lambda_k1, a_lambda_q2, a_lambda_k2, a_subln_g, b_q_norm_g, b_k_norm_g, b_forget_bias, odd_ln_g, odd_w_in, odd_w_out, c_q_norm_g, c_k_norm_g, c_sinks, d_q_norm_g, d_k_norm_g, d_rel_bias):
    depth = even_ln_g.shape[0] + odd_ln_g.shape[0]
    pending = None
    for i in range(depth):
        j = i // 2
        if i % 2 == 0:
            if pending is not None:
                x = _out_proj(*pending)
            pending = _even_layer(x, even_ln_g[j], even_w_in[j], even_w_out[j], a_q_norm_g[j],
                                  a_k_norm_g[j], a_lambda_q1[j], a_lambda_k1[j], a_lambda_q2[j],
                                  a_lambda_k2[j], a_subln_g[j], b_q_norm_g[j], b_k_norm_g[j],
                                  b_forget_bias[j], i)
        else:
            pending = _odd_layer(pending, odd_ln_g[j], odd_w_in[j], odd_w_out[j], c_q_norm_g[j],
                                 c_k_norm_g[j], c_sinks[j], d_q_norm_g[j], d_k_norm_g[j],
                                 d_rel_bias[j])
    return _out_proj(*pending)
```

```python
import functools
import math

import numpy as np
import jax
import jax.numpy as jnp
from jax import lax
from jax.experimental import pallas as pl
from jax.experimental.pallas import tpu as pltpu

D_MODEL = 1024
CHUNK = 64
HEAD_DIM = 64
NORM_EPS = 1e-6

A_HEADS = 4
A_STREAMS = 2 * A_HEADS
A_VDIM = 2 * HEAD_DIM
B_HEADS = 8
C_HEADS = 8
C_KV_HEADS = 2
C_GROUP = C_HEADS // C_KV_HEADS
WIN_CHUNKS = 2
D_HEADS = 8
D_LEFT_CHUNKS = 8
REL_MAX = 256

P_EVEN = 8 * 512 + B_HEADS
P_ODD = 512 + 128 + 128 + 512 + 4 * 512

LOG2E = 1.4426950408889634
QK_SCALE = HEAD_DIM ** -0.5 * LOG2E
NEG = -1e30
A_RATES = (2.0 ** (-8.0 * np.arange(1, A_HEADS + 1) / A_HEADS) * LOG2E).astype(np.float32)

LANES = 128
KPAD = 128
BF16_ROWS = 16

PROJ_TOKENS = 512
ATT_TQ = 512
ATT_TK = 512
A_HEADS_PER_STEP = 4
B_HEADS_PER_STEP = 8
A_BIAS_ROWS = 6
B_BIAS_ROWS = 3
MAX_ROWS = 3
FIXED_MAX_LIMIT = 96.0
C_BAND_TQ = 128
D_BAND_TQ = 256
D_HEADS_PER_STEP = 4
C_TILES_PER_STEP = 4
D_TILES_PER_STEP = 4
VMEM_LIMIT = 56 * 1024 * 1024

f32 = jnp.float32
bf16 = jnp.bfloat16


def _split3(v):
    hi = v.astype(bf16).astype(f32)
    r = v - hi
    mid = r.astype(bf16).astype(f32)
    lo = (r - mid).astype(bf16).astype(f32)
    return hi, mid, lo


def _silu(z):
    return z * (1.0 / (1.0 + jnp.exp(-z)))


def _rms_rows(x, g_ref):
    ms = jnp.mean(x * x, axis=-1, keepdims=True)
    return (x * lax.rsqrt(ms + NORM_EPS) * g_ref[...]).astype(bf16)


def _residual_add(m1_ref, m2_ref, wt_ref, x_ref):
    half = m1_ref.shape[1]
    y_t = (jnp.dot(wt_ref[:, :half], m1_ref[0], preferred_element_type=f32)
           + jnp.dot(wt_ref[:, half:], m2_ref[0], preferred_element_type=f32))
    return x_ref[0] + y_t.T


def _proj_t(wt_ref, r0, r1, xn):
    return lax.dot_general(wt_ref[r0:r1, :], xn, (((1,), (1,)), ((), ())),
                           preferred_element_type=f32)


def _head_norm(z_t, gain_col, mult):
    n = z_t.shape[0] // HEAD_DIM
    z3 = z_t.reshape(n, HEAD_DIM, z_t.shape[1])
    ms = jnp.mean(z3 * z3, axis=1, keepdims=True)
    return z3 * lax.rsqrt(ms + NORM_EPS) * (gain_col[...] * mult)[None]


def _ones_rows(row, first):
    return jnp.where((row >= first) & (row < first + 3), 1.0, 0.0)


def _store_heads(o_ref, z_t):
    o_ref[0] = z_t.reshape(o_ref.shape[1], o_ref.shape[2], z_t.shape[1]).astype(bf16)


def _store_keys(k_ref, kn, aug_fn):
    n, _, t = kn.shape
    zeros = jnp.zeros((KPAD - HEAD_DIM - BF16_ROWS, t), f32)
    for s in range(n):
        blk = jnp.concatenate([kn[s], aug_fn(s), zeros], axis=0)
        k_ref[0, s] = blk.T.astype(bf16)


def _proj_even_kernel(x_ref, lng_ref, wt_ref, aqg_ref, akg_ref, bqg_ref, bkg_ref, bfb_ref,
                      aq_ref, ak_ref, av_ref, asg_ref, bq_ref, bk_ref, bv_ref, bsg_ref,
                      am_ref, bm_ref, cum_ref):
    t = pl.program_id(1)
    tt = x_ref.shape[1]
    xn = _rms_rows(x_ref[0], lng_ref)
    row = lax.broadcasted_iota(jnp.int32, (BF16_ROWS, tt), 0)

    z = _proj_t(wt_ref, 4096, 4096 + BF16_ROWS, xn)[:B_HEADS] + bfb_ref[...]
    log_f = jnp.minimum(z, 0.0) - jnp.log(1.0 + jnp.exp(-jnp.abs(z)))
    tri = jnp.where(lax.broadcasted_iota(jnp.int32, (tt, tt), 0)
                    <= lax.broadcasted_iota(jnp.int32, (tt, tt), 1), 1.0, 0.0).astype(bf16)
    pieces = jnp.concatenate(_split3(log_f) + (jnp.zeros_like(log_f),), axis=0).astype(bf16)
    part = jnp.dot(pieces, tri, preferred_element_type=f32)
    local = part[:B_HEADS] + part[B_HEADS:2 * B_HEADS] + part[2 * B_HEADS:3 * B_HEADS]

    @pl.when(t == 0)
    def _():
        cum_ref[...] = jnp.zeros_like(cum_ref)

    cum = cum_ref[...] + local
    cum_ref[...] = cum[:, tt - 1:tt]
    gate = -LOG2E * cum
    g_hi, g_mid, g_lo = _split3(gate)

    def aug_b(s):
        pick = lambda a: jnp.broadcast_to(a[s:s + 1], (BF16_ROWS, tt))
        return jnp.where(row == 0, pick(g_hi),
                         jnp.where(row == 1, pick(g_mid),
                                   jnp.where(row == 2, pick(g_lo), _ones_rows(row, B_BIAS_ROWS))))

    pos = t * tt + lax.broadcasted_iota(jnp.int32, (BF16_ROWS, tt), 1)
    pos_a = lax.shift_right_logical(pos, int(math.log2(CHUNK))).astype(f32)
    pos_b = lax.bitwise_and(pos, CHUNK - 1).astype(f32)
    aug_a = jnp.where(row < 3, pos_a, jnp.where(row < 6, pos_b, _ones_rows(row, A_BIAS_ROWS)))
    aqn = _head_norm(_proj_t(wt_ref, 0, 512, xn), aqg_ref, QK_SCALE)
    aq_ref[0] = aqn.astype(bf16)
    akn = _head_norm(_proj_t(wt_ref, 512, 1024, xn), akg_ref, 1.0)
    _store_keys(ak_ref, akn, lambda s: aug_a)
    self_a = jnp.sum(aqn * akn, axis=1, keepdims=True)
    for s in range(A_STREAMS):
        am_ref[0, s] = self_a[s] + float(A_RATES[s // 2]) * pos[:1].astype(f32)
    _store_heads(asg_ref, _silu(_proj_t(wt_ref, 1536, 2048, xn)))

    bqn = _head_norm(_proj_t(wt_ref, 2048, 2560, xn), bqg_ref, QK_SCALE)
    bq_ref[0] = bqn.astype(bf16)
    bkn = _head_norm(_proj_t(wt_ref, 2560, 3072, xn), bkg_ref, 1.0)
    _store_keys(bk_ref, bkn, aug_b)
    self_b = jnp.sum(bqn * bkn, axis=1, keepdims=True)
    for s in range(B_HEADS):
        bm_ref[0, s] = self_b[s] + gate[s:s + 1]
    _store_heads(bsg_ref, _silu(_proj_t(wt_ref, 3584, 4096, xn)))
    _store_heads(av_ref, _proj_t(wt_ref, 1024, 1536, xn))
    _store_heads(bv_ref, _proj_t(wt_ref, 3072, 3584, xn))


def _proj_even(x, ln_g, wt, aqg, akg, bqg, bkg, bfb):
    bsz, seq, _ = x.shape
    tt = PROJ_TOKENS
    col = lambda n: pl.BlockSpec((n, 1), lambda b, t: (0, 0))
    fm = lambda n, d: pl.BlockSpec((1, n, d, tt), lambda b, t: (b, 0, 0, t))
    km = lambda n: pl.BlockSpec((1, n, tt, KPAD), lambda b, t: (b, 0, t, 0))
    fm_shape = lambda n, d: jax.ShapeDtypeStruct((bsz, n, d, seq), bf16)
    km_shape = lambda n: jax.ShapeDtypeStruct((bsz, n, seq, KPAD), bf16)
    return pl.pallas_call(
        _proj_even_kernel,
        grid=(bsz, seq // tt),
        in_specs=[
            pl.BlockSpec((1, tt, D_MODEL), lambda b, t: (b, t, 0)),
            pl.BlockSpec((1, D_MODEL), lambda b, t: (0, 0)),
            pl.BlockSpec(wt.shape, lambda b, t: (0, 0)),
            col(HEAD_DIM), col(HEAD_DIM), col(HEAD_DIM), col(HEAD_DIM), col(B_HEADS),
        ],
        out_specs=[fm(A_STREAMS, HEAD_DIM), km(A_STREAMS), fm(A_HEADS, A_VDIM), fm(A_HEADS, A_VDIM),
                   fm(B_HEADS, HEAD_DIM), km(B_HEADS), fm(B_HEADS, HEAD_DIM), fm(B_HEADS, HEAD_DIM),
                   fm(A_STREAMS, 1), fm(B_HEADS, 1)],
        out_shape=[fm_shape(A_STREAMS, HEAD_DIM), km_shape(A_STREAMS), fm_shape(A_HEADS, A_VDIM),
                   fm_shape(A_HEADS, A_VDIM), fm_shape(B_HEADS, HEAD_DIM), km_shape(B_HEADS),
                   fm_shape(B_HEADS, HEAD_DIM), fm_shape(B_HEADS, HEAD_DIM),
                   jax.ShapeDtypeStruct((bsz, A_STREAMS, 1, seq), f32),
                   jax.ShapeDtypeStruct((bsz, B_HEADS, 1, seq), f32)],
        scratch_shapes=[pltpu.VMEM((B_HEADS, 1), f32)],
        compiler_params=pltpu.CompilerParams(
            dimension_semantics=("arbitrary", "arbitrary"), vmem_limit_bytes=VMEM_LIMIT),
        name="proj_even",
    )(x, ln_g, wt, aqg, akg, bqg, bkg, bfb)


def _proj_odd_kernel(m1_ref, m2_ref, wo_ref, x_ref, lng_ref, wt_ref, cqg_ref, ckg_ref, dqg_ref,
                     dkg_ref, x1_ref, cq_ref, ck_ref, cv_ref, csg_ref, dq_ref, dk_ref, dv_ref,
                     dsg_ref, cm_ref, dm_ref):
    tt = x_ref.shape[1]
    x1 = _residual_add(m1_ref, m2_ref, wo_ref, x_ref)
    x1_ref[0] = x1
    xn = _rms_rows(x1, lng_ref)
    ones = _ones_rows(lax.broadcasted_iota(jnp.int32, (BF16_ROWS, tt), 0), 0)
    aug = lambda s: ones

    cqn = _head_norm(_proj_t(wt_ref, 0, 512, xn), cqg_ref, QK_SCALE)
    cq_ref[0] = cqn.astype(bf16)
    ckn = _head_norm(_proj_t(wt_ref, 512, 640, xn), ckg_ref, 1.0)
    _store_keys(ck_ref, ckn, aug)
    cm_ref[0] = jnp.sum(cqn.reshape(C_KV_HEADS, C_GROUP, HEAD_DIM, tt) * ckn[:, None], axis=2,
                        keepdims=True).reshape(C_HEADS, 1, tt)
    _store_heads(cv_ref, _proj_t(wt_ref, 640, 768, xn))
    _store_heads(csg_ref, _silu(_proj_t(wt_ref, 768, 1280, xn)))
    dqn = _head_norm(_proj_t(wt_ref, 1280, 1792, xn), dqg_ref, QK_SCALE)
    dq_ref[0] = dqn.astype(bf16)
    dkn = _head_norm(_proj_t(wt_ref, 1792, 2304, xn), dkg_ref, 1.0)
    _store_keys(dk_ref, dkn, aug)
    dm_ref[0] = jnp.sum(dqn * dkn, axis=1, keepdims=True)
    _store_heads(dsg_ref, _silu(_proj_t(wt_ref, 2816, 3328, xn)))
    _store_heads(dv_ref, _proj_t(wt_ref, 2304, 2816, xn))


def _proj_odd(m1, m2, wo_t, x, ln_g, wt, cqg, ckg, dqg, dkg):
    bsz, seq, _ = x.shape
    tt = PROJ_TOKENS
    half = m1.shape[1]
    col = lambda n: pl.BlockSpec((n, 1), lambda b, t: (0, 0))
    rows = pl.BlockSpec((1, tt, D_MODEL), lambda b, t: (b, t, 0))
    fm = lambda n: pl.BlockSpec((1, n, HEAD_DIM, tt), lambda b, t: (b, 0, 0, t))
    km = lambda n: pl.BlockSpec((1, n, tt, KPAD), lambda b, t: (b, 0, t, 0))
    fm_shape = lambda n: jax.ShapeDtypeStruct((bsz, n, HEAD_DIM, seq), bf16)
    km_shape = lambda n: jax.ShapeDtypeStruct((bsz, n, seq, KPAD), bf16)
    return pl.pallas_call(
        _proj_odd_kernel,
        grid=(bsz, seq // tt),
        in_specs=[
            pl.BlockSpec((1, half, tt), lambda b, t: (b, 0, t)),
            pl.BlockSpec((1, half, tt), lambda b, t: (b, 0, t)),
            pl.BlockSpec(wo_t.shape, lambda b, t: (0, 0)),
            rows,
            pl.BlockSpec((1, D_MODEL), lambda b, t: (0, 0)),
            pl.BlockSpec(wt.shape, lambda b, t: (0, 0)),
            col(HEAD_DIM), col(HEAD_DIM), col(HEAD_DIM), col(HEAD_DIM),
        ],
        out_specs=[rows, fm(C_HEADS), km(C_KV_HEADS), fm(C_KV_HEADS), fm(C_HEADS),
                   fm(D_HEADS), km(D_HEADS), fm(D_HEADS), fm(D_HEADS),
                   pl.BlockSpec((1, C_HEADS, 1, tt), lambda b, t: (b, 0, 0, t)),
                   pl.BlockSpec((1, D_HEADS, 1, tt), lambda b, t: (b, 0, 0, t))],
        out_shape=[jax.ShapeDtypeStruct(x.shape, f32),
                   fm_shape(C_HEADS), km_shape(C_KV_HEADS), fm_shape(C_KV_HEADS), fm_shape(C_HEADS),
                   fm_shape(D_HEADS), km_shape(D_HEADS), fm_shape(D_HEADS), fm_shape(D_HEADS),
                   jax.ShapeDtypeStruct((bsz, C_HEADS, 1, seq), f32),
                   jax.ShapeDtypeStruct((bsz, D_HEADS, 1, seq), f32)],
        compiler_params=pltpu.CompilerParams(
            dimension_semantics=("arbitrary", "arbitrary"), vmem_limit_bytes=VMEM_LIMIT),
        name="proj_odd",
    )(m1, m2, wo_t, x, ln_g, wt, cqg, ckg, dqg, dkg)


def _online_step(carry, s, v):
    m, l, acc = carry
    m_new = jnp.maximum(m, jnp.max(s, axis=0, keepdims=True))
    p = jnp.exp2(s - m_new)
    alpha = jnp.exp2(m - m_new)
    l = alpha * l + jnp.sum(p, axis=0, keepdims=True)
    acc = alpha * acc + jnp.dot(v, p.astype(bf16), preferred_element_type=f32)
    return m_new, l, acc


def _colsum8(p):
    return p.reshape(p.shape[0] // 8, 8, p.shape[1]).sum(axis=0)


def _augment_q(q, aug_col, n_bias, m=None):
    tq = q.shape[1]
    aug = jnp.broadcast_to(aug_col, (KPAD - HEAD_DIM, tq))
    if m is not None:
        row = lax.broadcasted_iota(jnp.int32, aug.shape, 0)
        for r, piece in enumerate(_split3(-m)):
            aug = jnp.where(row == n_bias + r, piece, aug)
    return jnp.concatenate([q, aug.astype(bf16)], axis=0)


def _staggered(n, scores, finish):
    out, pending = [], scores(0)
    for c in range(1, n):
        nxt = scores(c)
        out.append(finish(c - 1, pending))
        pending = nxt
    out.append(finish(n - 1, pending))
    return out


def _diag_full(chains, d0, tq):
    def scores(c):
        k_at, _, q, aug_col, n_bias, diag_bias, _ = chains[c]
        return jnp.dot(k_at(d0, tq), _augment_q(q, aug_col, n_bias),
                       preferred_element_type=f32) + diag_bias

    def finish(c, s):
        m = jnp.max(s, axis=0, keepdims=True)
        p = jnp.exp2(s - m)
        return m, _colsum8(p), jnp.dot(chains[c][1](d0, tq), p.astype(bf16), preferred_element_type=f32)

    return _staggered(len(chains), scores, finish)


def _diag_halves(chains, q_aug, d0, tq):
    h = tq // 2
    d1 = pl.multiple_of(d0 + h, h)

    def scores(c):
        k_at = chains[c][0]
        return (jnp.dot(k_at(d0, h), q_aug[c], preferred_element_type=f32),
                jnp.dot(k_at(d1, h), q_aug[c][:, h:], preferred_element_type=f32))

    def finish(c, s):
        v_at, bias = chains[c][1], chains[c][5][:h, :h]
        p0 = jnp.concatenate([jnp.exp2(s[0][:, :h] + bias), jnp.exp2(s[0][:, h:])], axis=1)
        p1 = jnp.exp2(s[1] + bias)
        l0 = _colsum8(p0)
        a0 = jnp.dot(v_at(d0, h), p0.astype(bf16), preferred_element_type=f32)
        a1 = jnp.dot(v_at(d1, h), p1.astype(bf16), preferred_element_type=f32)
        return (jnp.concatenate([l0[:, :h], l0[:, h:] + _colsum8(p1)], axis=1),
                jnp.concatenate([a0[:, :h], a0[:, h:] + a1], axis=1))

    return _staggered(len(chains), scores, finish)


def _causal_sweep(chains, i, fixed_max):
    n = len(chains)
    tq = chains[0][2].shape[1]
    d0 = pl.multiple_of(i * tq, tq)
    n_tiles = i * (tq // ATT_TK)
    tile = lambda j: (pl.multiple_of(j * ATT_TK, ATT_TK), ATT_TK)

    if fixed_max:
        q_aug = [_augment_q(q, aug_col, n_bias, m) for _, _, q, aug_col, n_bias, _, m in chains]

        def body(j, carries):
            def finish(c, s):
                p = jnp.exp2(s)
                l, acc = carries[c]
                return (l + _colsum8(p), acc + jnp.dot(chains[c][1](*tile(j)), p.astype(bf16),
                                                       preferred_element_type=f32))

            scores = lambda c: jnp.dot(chains[c][0](*tile(j)), q_aug[c], preferred_element_type=f32)
            return tuple(_staggered(n, scores, finish))

        carries = lax.fori_loop(0, n_tiles, body, tuple(_diag_halves(chains, q_aug, d0, tq)))
        return [(acc, jnp.sum(l, axis=0, keepdims=True)) for l, acc in carries]

    q_aug = [_augment_q(q, aug_col, n_bias) for _, _, q, aug_col, n_bias, _, _ in chains]

    def body(j, carries):
        scores = lambda c: jnp.dot(chains[c][0](*tile(j)), q_aug[c], preferred_element_type=f32)
        finish = lambda c, s: _online_step(carries[c], s, chains[c][1](*tile(j)))
        return tuple(_staggered(n, scores, finish))

    init = tuple((m, jnp.sum(l, axis=0, keepdims=True), acc) for m, l, acc in _diag_full(chains, d0, tq))
    return [(acc, l) for _, l, acc in lax.fori_loop(0, n_tiles, body, init)]


def _either_sweep(fixed_ref, run):
    @pl.when(fixed_ref[0] != 0)
    def _():
        run(True)

    @pl.when(fixed_ref[0] == 0)
    def _():
        run(False)


def _attn_a_kernel(fixed_ref, q_ref, k_ref, v_ref, sg_ref, m_ref, dtab_ref, qaug_ref, subg_ref,
                   lamv_ref, o_ref, *, lam_init):
    i = pl.program_id(2)
    chains = []
    for h in range(A_HEADS_PER_STEP):
        v_at = lambda start, size, h=h: v_ref[0, h, :, pl.ds(start, size)]
        for c in range(2):
            s = 2 * h + c
            k_at = lambda start, size, s=s: k_ref[0, s, pl.ds(start, size), :]
            chains.append((k_at, v_at, q_ref[0, s], qaug_ref[h], A_BIAS_ROWS, dtab_ref[h],
                           m_ref[0, s]))

    def run(fixed_max):
        lv = lamv_ref[...]
        lam = (jnp.exp(jnp.sum(lv[0:1] * lv[1:2], axis=1, keepdims=True))
               - jnp.exp(jnp.sum(lv[2:3] * lv[3:4], axis=1, keepdims=True)) + lam_init)
        outs = [acc * (1.0 / l) for acc, l in _causal_sweep(chains, i, fixed_max)]
        for h in range(A_HEADS_PER_STEP):
            o = outs[2 * h] - lam * outs[2 * h + 1]
            ms = jnp.mean(o * o, axis=0, keepdims=True)
            y = o * lax.rsqrt(ms + NORM_EPS) * (subg_ref[...] * (1.0 - lam_init))
            o_ref[0, h] = (y * sg_ref[0, h].astype(f32)).astype(bf16)

    _either_sweep(fixed_ref, run)


def _attn_a(fixed, aq, ak, av, asg, am, dtab, qaug, subg, lamv, lam_init):
    bsz, _, _, seq = aq.shape
    nq = seq // ATT_TQ
    hs = A_HEADS_PER_STEP
    return pl.pallas_call(
        functools.partial(_attn_a_kernel, lam_init=lam_init),
        grid=(bsz, A_HEADS // hs, nq),
        in_specs=[
            pl.BlockSpec(memory_space=pltpu.SMEM),
            pl.BlockSpec((1, 2 * hs, HEAD_DIM, ATT_TQ), lambda b, h, i: (b, h, 0, i)),
            pl.BlockSpec((1, 2 * hs, seq, KPAD), lambda b, h, i: (b, h, 0, 0)),
            pl.BlockSpec((1, hs, A_VDIM, seq), lambda b, h, i: (b, h, 0, 0)),
            pl.BlockSpec((1, hs, A_VDIM, ATT_TQ), lambda b, h, i: (b, h, 0, i)),
            pl.BlockSpec((1, 2 * hs, 1, ATT_TQ), lambda b, h, i: (b, h, 0, i)),
            pl.BlockSpec((hs, ATT_TQ, ATT_TQ), lambda b, h, i: (h, 0, 0)),
            pl.BlockSpec((hs, KPAD - HEAD_DIM, 1), lambda b, h, i: (h, 0, 0)),
            pl.BlockSpec((A_VDIM, 1), lambda b, h, i: (0, 0)),
            pl.BlockSpec((4, HEAD_DIM), lambda b, h, i: (0, 0)),
        ],
        out_specs=pl.BlockSpec((1, hs, A_VDIM, ATT_TQ), lambda b, h, i: (b, h, 0, i)),
        out_shape=jax.ShapeDtypeStruct((bsz, A_HEADS, A_VDIM, seq), bf16),
        compiler_params=pltpu.CompilerParams(
            dimension_semantics=("arbitrary", "arbitrary", "arbitrary"),
            vmem_limit_bytes=VMEM_LIMIT),
        name="attn_a",
    )(fixed, aq, ak, av, asg, am, dtab, qaug, subg, lamv)


def _attn_b_kernel(fixed_ref, q_ref, k_ref, v_ref, sg_ref, m_ref, qaug_ref, o_ref):
    i = pl.program_id(2)

    def run(fixed_max):
        causal = jnp.where(lax.broadcasted_iota(jnp.int32, (ATT_TQ, ATT_TQ), 0)
                           <= lax.broadcasted_iota(jnp.int32, (ATT_TQ, ATT_TQ), 1), 0.0, NEG)
        chains = []
        for h in range(B_HEADS_PER_STEP):
            k_at = lambda start, size, h=h: k_ref[0, h, pl.ds(start, size), :]
            v_at = lambda start, size, h=h: v_ref[0, h, :, pl.ds(start, size)]
            chains.append((k_at, v_at, q_ref[0, h], qaug_ref[...], B_BIAS_ROWS, causal, m_ref[0, h]))
        for h, (acc, l) in enumerate(_causal_sweep(chains, i, fixed_max)):
            o_ref[0, h] = (acc * (1.0 / l) * sg_ref[0, h].astype(f32)).astype(bf16)

    _either_sweep(fixed_ref, run)


def _attn_b(fixed, bq, bk, bv, bsg, bm, qaug):
    bsz, _, _, seq = bq.shape
    nq = seq // ATT_TQ
    hs = B_HEADS_PER_STEP
    return pl.pallas_call(
        _attn_b_kernel,
        grid=(bsz, B_HEADS // hs, nq),
        in_specs=[
            pl.BlockSpec(memory_space=pltpu.SMEM),
            pl.BlockSpec((1, hs, HEAD_DIM, ATT_TQ), lambda b, h, i: (b, h, 0, i)),
            pl.BlockSpec((1, hs, seq, KPAD), lambda b, h, i: (b, h, 0, 0)),
            pl.BlockSpec((1, hs, HEAD_DIM, seq), lambda b, h, i: (b, h, 0, 0)),
            pl.BlockSpec((1, hs, HEAD_DIM, ATT_TQ), lambda b, h, i: (b, h, 0, i)),
            pl.BlockSpec((1, hs, 1, ATT_TQ), lambda b, h, i: (b, h, 0, i)),
            pl.BlockSpec((KPAD - HEAD_DIM, 1), lambda b, h, i: (0, 0)),
        ],
        out_specs=pl.BlockSpec((1, hs, HEAD_DIM, ATT_TQ), lambda b, h, i: (b, h, 0, i)),
        out_shape=jax.ShapeDtypeStruct((bsz, B_HEADS, HEAD_DIM, seq), bf16),
        compiler_params=pltpu.CompilerParams(
            dimension_semantics=("arbitrary", "arbitrary", "arbitrary"),
            vmem_limit_bytes=VMEM_LIMIT),
        name="attn_b",
    )(fixed, bq, bk, bv, bsg, bm, qaug)


def _band_kernel(fixed_ref, q_ref, k_ref, v_ref, sg_ref, m_ref, tab_ref, sink_ref, o_ref,
                 *, group, back, tq):
    blocks = tq // LANES
    band = (back + blocks) * LANES
    tiles = q_ref.shape[3] // tq
    work = [(u, h) for u in range(tiles) for h in range(k_ref.shape[1])]
    grouped = lambda ref, u, h: jnp.concatenate(
        [ref[0, h * group + g, :, u * tq:(u + 1) * tq] for g in range(group)], axis=1)

    def window(u):
        first = (pl.program_id(2) * tiles + u) * blocks
        return (pl.multiple_of(jnp.maximum(first - back, 0) * LANES, LANES),
                pl.multiple_of(jnp.maximum(back - first, 0) * LANES, LANES))

    def run(fixed_max):
        def scores(w):
            u, h = work[w]
            k_start, tab_start = window(u)
            q = _augment_q(grouped(q_ref, u, h), 0.0, 0, grouped(m_ref, u, h) if fixed_max else None)
            s = jnp.dot(k_ref[0, h, pl.ds(k_start, band), :], q, preferred_element_type=f32)
            return s + tab_ref[h, pl.ds(tab_start, band), :]

        def finish(w, s):
            u, h = work[w]
            k_start, _ = window(u)
            if fixed_max:
                m = grouped(m_ref, u, h)
                p = jnp.exp2(s)
            else:
                m = jnp.maximum(jnp.max(s, axis=0, keepdims=True), sink_ref[h])
                p = jnp.exp2(s - m)
            l = jnp.sum(p, axis=0, keepdims=True) + jnp.exp2(sink_ref[h] - m)
            o = jnp.dot(v_ref[0, h, :, pl.ds(k_start, band)], p.astype(bf16),
                        preferred_element_type=f32) * (1.0 / l)
            for g in range(group):
                hq = h * group + g
                gate = sg_ref[0, hq, :, u * tq:(u + 1) * tq].astype(f32)
                o_ref[0, hq, :, u * tq:(u + 1) * tq] = (o[:, g * tq:(g + 1) * tq] * gate).astype(bf16)

        _staggered(len(work), scores, finish)

    _either_sweep(fixed_ref, run)


def _band_attn(fixed, q, k, v, sg, m, tab, sink, group, back, tq, kv_per_step, tiles_per_step, name):
    bsz, nheads, _, seq = q.shape
    hs = kv_per_step
    tile = tq * tiles_per_step
    return pl.pallas_call(
        functools.partial(_band_kernel, group=group, back=back, tq=tq),
        grid=(bsz, nheads // (group * hs), seq // tile),
        in_specs=[
            pl.BlockSpec(memory_space=pltpu.SMEM),
            pl.BlockSpec((1, hs * group, HEAD_DIM, tile), lambda b, h, i: (b, h, 0, i)),
            pl.BlockSpec((1, hs, seq, KPAD), lambda b, h, i: (b, h, 0, 0)),
            pl.BlockSpec((1, hs, HEAD_DIM, seq), lambda b, h, i: (b, h, 0, 0)),
            pl.BlockSpec((1, hs * group, HEAD_DIM, tile), lambda b, h, i: (b, h, 0, i)),
            pl.BlockSpec((1, hs * group, 1, tile), lambda b, h, i: (b, h, 0, i)),
            pl.BlockSpec((hs,) + tab.shape[1:], lambda b, h, i: (h, 0, 0)),
            pl.BlockSpec((hs,) + sink.shape[1:], lambda b, h, i: (h, 0, 0)),
        ],
        out_specs=pl.BlockSpec((1, hs * group, HEAD_DIM, tile), lambda b, h, i: (b, h, 0, i)),
        out_shape=jax.ShapeDtypeStruct((bsz, nheads, HEAD_DIM, seq), bf16),
        compiler_params=pltpu.CompilerParams(
            dimension_semantics=("arbitrary", "arbitrary", "arbitrary"),
            vmem_limit_bytes=VMEM_LIMIT),
        name=name,
    )(fixed, q, k, v, sg, m, tab, sink)


def _out_proj_kernel(m1_ref, m2_ref, wt_ref, x_ref, o_ref):
    o_ref[0] = _residual_add(m1_ref, m2_ref, wt_ref, x_ref)


def _out_proj(m1, m2, wt, x):
    bsz, seq, _ = x.shape
    tt = PROJ_TOKENS
    half = m1.shape[1]
    return pl.pallas_call(
        _out_proj_kernel,
        grid=(bsz, seq // tt),
        in_specs=[
            pl.BlockSpec((1, half, tt), lambda b, t: (b, 0, t)),
            pl.BlockSpec((1, half, tt), lambda b, t: (b, 0, t)),
            pl.BlockSpec(wt.shape, lambda b, t: (0, 0)),
            pl.BlockSpec((1, tt, D_MODEL), lambda b, t: (b, t, 0)),
        ],
        out_specs=pl.BlockSpec((1, tt, D_MODEL), lambda b, t: (b, t, 0)),
        out_shape=jax.ShapeDtypeStruct(x.shape, f32),
        compiler_params=pltpu.CompilerParams(
            dimension_semantics=("arbitrary", "arbitrary"), vmem_limit_bytes=VMEM_LIMIT),
        name="out_proj",
    )(m1, m2, wt, x)


def _alibi_slopes(n):
    return 2.0 ** (-8.0 * np.arange(1, n + 1, dtype=np.float64) / n)


def _np_split3(v):
    v = np.asarray(v, np.float32)
    to_bf = lambda a: a.astype(bf16).astype(np.float32)
    hi = to_bf(v)
    mid = to_bf(v - hi)
    lo = to_bf(v - hi - mid)
    return hi, mid, lo


def _a_tables():
    rate = A_RATES
    qaug = np.zeros((A_HEADS, KPAD - HEAD_DIM, 1), np.float32)
    for idx, piece in enumerate(_np_split3(rate * CHUNK) + _np_split3(rate)):
        qaug[:, idx, 0] = piece
    kk = np.arange(ATT_TQ)[:, None]
    qq = np.arange(ATT_TQ)[None, :]
    future = np.maximum(kk - qq, 0).astype(np.float32)
    corr = -2.0 * rate[:, None, None] * future[None]
    allowed = (kk // CHUNK) <= (qq // CHUNK)
    dtab = np.where(allowed[None], corr, NEG).astype(np.float32)
    return jnp.asarray(qaug), jnp.asarray(dtab)


def _band_frames(back, tq):
    k_pos = np.arange(back * LANES + tq)[:, None]
    q_pos = back * LANES + np.arange(tq)[None, :]
    return q_pos - k_pos, q_pos // CHUNK - k_pos // CHUNK


def _c_tables(sinks):
    back, tq = WIN_CHUNKS * CHUNK // LANES, C_BAND_TQ
    rel, chunk_diff = _band_frames(back, tq)
    allowed = (chunk_diff >= 0) & (chunk_diff <= WIN_CHUNKS)
    slopes = _alibi_slopes(C_HEADS)
    per_head = np.where(allowed[None], -slopes[:, None, None] * np.abs(rel)[None] * LOG2E, NEG)
    tab = per_head.reshape(C_KV_HEADS, C_GROUP, *rel.shape).transpose(0, 2, 1, 3)
    tab = tab.reshape(C_KV_HEADS, rel.shape[0], C_GROUP * tq).astype(np.float32)
    tab = np.concatenate([tab, np.full((C_KV_HEADS, back * LANES, tab.shape[2]), NEG, np.float32)], 1)
    sink = jnp.repeat(sinks.astype(f32) * LOG2E, tq).reshape(C_KV_HEADS, 1, C_GROUP * tq)
    return jnp.asarray(tab), sink, back


def _d_tables(rel_table):
    back, t = D_LEFT_CHUNKS * CHUNK // LANES, D_BAND_TQ
    band = back * LANES + t
    rel, chunk_diff = _band_frames(back, t)
    allowed = (chunk_diff >= 0) & (chunk_diff <= D_LEFT_CHUNKS)
    tbl = rel_table.astype(f32) * LOG2E
    n_lo = (t - 1) - (CHUNK - 1)
    n_hi = (band - 1) - REL_MAX
    diag = jnp.concatenate([jnp.broadcast_to(tbl[:, :1], (D_HEADS, n_lo)), tbl,
                            jnp.broadcast_to(tbl[:, -1:], (D_HEADS, n_hi))], axis=1)
    m = t + LANES - 1
    blocks = []
    for kb in range(band // LANES):
        lo = rel[kb * LANES:(kb + 1) * LANES].min()
        if lo >= REL_MAX:
            blocks.append(jnp.broadcast_to(tbl[:, -1:, None], (D_HEADS, LANES, t)))
            continue
        window = diag[:, band - (kb + 1) * LANES:band - (kb + 1) * LANES + m]
        skew = jnp.broadcast_to(window[:, None, :], (D_HEADS, LANES + 1, m)).reshape(D_HEADS, -1)
        skew = skew[:, :LANES * (m + 1)].reshape(D_HEADS, LANES, m + 1)[:, :, :t]
        blocks.append(jnp.flip(skew, axis=1))
    tab = jnp.where(jnp.asarray(allowed)[None], jnp.concatenate(blocks, axis=1), NEG)
    tab = jnp.concatenate([tab, jnp.full((D_HEADS, back * LANES, t), NEG, f32)], axis=1)
    sink = jnp.full((D_HEADS, 1, t), NEG, f32)
    return tab, sink, back


def _fixed_max_ok(q_gain, k_gain, bias_range=0.0):
    spread = (2.0 * 1.02 * QK_SCALE * HEAD_DIM
              * jnp.max(jnp.abs(q_gain.astype(f32))) * jnp.max(jnp.abs(k_gain.astype(f32))))
    return (spread + bias_range <= FIXED_MAX_LIMIT).astype(jnp.int32).reshape(1)


def _pad_rows(w_t, rows):
    return jnp.pad(w_t, ((0, rows - w_t.shape[0]), (0, 0)))


def _even_layer(x, ln_g, w_in, w_out, a_qn_g, a_kn_g, a_lq1, a_lk1, a_lq2, a_lk2, a_subln_g,
                b_qn_g, b_kn_g, b_f_bias, layer_idx):
    bsz, seq, _ = x.shape
    colv = lambda v: v.astype(f32).reshape(-1, 1)
    wt = _pad_rows(w_in.T.astype(bf16), 4096 + BF16_ROWS)
    aq, ak, av, asg, bq, bk, bv, bsg, am, bm = _proj_even(
        x, ln_g.astype(f32).reshape(1, -1), wt, colv(a_qn_g), colv(a_kn_g), colv(b_qn_g),
        colv(b_kn_g), colv(b_f_bias))
    lam_init = 0.8 - 0.6 * math.exp(-0.3 * layer_idx)
    qaug_a, dtab = _a_tables()
    lamv = jnp.stack([a_lq1, a_lk1, a_lq2, a_lk2]).astype(f32)
    mix_a = _attn_a(_fixed_max_ok(a_qn_g, a_kn_g), aq, ak, av, asg, am, dtab, qaug_a,
                    colv(a_subln_g), lamv, lam_init)
    qaug_b = np.zeros((KPAD - HEAD_DIM, 1), np.float32)
    qaug_b[:B_BIAS_ROWS] = 1.0
    mix_b = _attn_b(_fixed_max_ok(b_qn_g, b_kn_g), bq, bk, bv, bsg, bm, jnp.asarray(qaug_b))
    return mix_a.reshape(bsz, -1, seq), mix_b.reshape(bsz, -1, seq), w_out.T.astype(bf16), x


def _odd_layer(pending, ln_g, w_in, w_out, c_qn_g, c_kn_g, c_sinks, d_qn_g, d_kn_g, d_rel_bias):
    bsz, seq, _ = pending[3].shape
    colv = lambda v: v.astype(f32).reshape(-1, 1)
    x, cq, ck, cv, csg, dq, dk, dv, dsg, cm, dm = _proj_odd(
        *pending, ln_g.astype(f32).reshape(1, -1), w_in.T.astype(bf16), colv(c_qn_g), colv(c_kn_g),
        colv(d_qn_g), colv(d_kn_g))
    tab_c, sink_c, back_c = _c_tables(c_sinks)
    fixed_c = _fixed_max_ok(c_qn_g, c_kn_g, LOG2E * jnp.maximum(jnp.max(c_sinks.astype(f32)), 0.0))
    mix_c = _band_attn(fixed_c, cq, ck, cv, csg, cm, tab_c, sink_c, C_GROUP, back_c, C_BAND_TQ,
                       C_KV_HEADS, C_TILES_PER_STEP, "attn_c")
    tab_d, sink_d, back_d = _d_tables(d_rel_bias)
    fixed_d = _fixed_max_ok(d_qn_g, d_kn_g, LOG2E * jnp.max(jnp.abs(d_rel_bias.astype(f32))))
    mix_d = _band_attn(fixed_d, dq, dk, dv, dsg, dm, tab_d, sink_d, 1, back_d, D_BAND_TQ,
                       D_HEADS_PER_STEP, D_TILES_PER_STEP, "attn_d")
    return mix_c.reshape(bsz, -1, seq), mix_d.reshape(bsz, -1, seq), w_out.T.astype(bf16), x


def kernel(x, even_ln_g, even_w_in, even_w_out, a_q_norm_g, a_k_norm_g, a_lambda_q1, a_lambda_k1, a_lambda_q2, a_lambda_k2, a_subln_g, b_q_norm_g, b_k_norm_g, b_forget_bias, odd_ln_g, odd_w_in, odd_w_out, c_q_norm_g, c_k_norm_g, c_sinks, d_q_norm_g, d_k_norm_g, d_rel_bias):
    depth = even_ln_g.shape[0] + odd_ln_g.shape[0]
    pending = None
    for i in range(depth):
        j = i // 2
        if i % 2 == 0:
            if pending is not None:
                x = _out_proj(*pending)
            pending = _even_layer(x, even_ln_g[j], even_w_in[j], even_w_out[j], a_q_norm_g[j],
                                  a_k_norm_g[j], a_lambda_q1[j], a_lambda_k1[j], a_lambda_q2[j],
                                  a_lambda_k2[j], a_subln_g[j], b_q_norm_g[j], b_k_norm_g[j],
                                  b_forget_bias[j], i)
        else:
            pending = _odd_layer(pending, odd_ln_g[j], odd_w_in[j], odd_w_out[j], c_q_norm_g[j],
                                 c_k_norm_g[j], c_sinks[j], d_q_norm_g[j], d_k_norm_g[j],
                                 d_rel_bias[j])
    return _out_proj(*pending)
```

```python
import functools
import math

import numpy as np
import jax
import jax.numpy as jnp
from jax import lax
from jax.experimental import pallas as pl
from jax.experimental.pallas import tpu as pltpu

D_MODEL = 1024
CHUNK = 64
HEAD_DIM = 64
NORM_EPS = 1e-6

A_HEADS = 4
A_STREAMS = 2 * A_HEADS
A_VDIM = 2 * HEAD_DIM
B_HEADS = 8
C_HEADS = 8
C_KV_HEADS = 2
C_GROUP = C_HEADS // C_KV_HEADS
WIN_CHUNKS = 2
D_HEADS = 8
D_LEFT_CHUNKS = 8
REL_MAX = 256

P_EVEN = 8 * 512 + B_HEADS
P_ODD = 512 + 128 + 128 + 512 + 4 * 512

LOG2E = 1.4426950408889634
QK_SCALE = HEAD_DIM ** -0.5 * LOG2E
NEG = -1e30
A_RATES = (2.0 ** (-8.0 * np.arange(1, A_HEADS + 1) / A_HEADS) * LOG2E).astype(np.float32)

LANES = 128
KPAD = 128
BF16_ROWS = 16

PROJ_TOKENS = 512
ATT_TQ = 512
ATT_TK = 512
A_HEADS_PER_STEP = 4
B_HEADS_PER_STEP = 8
A_BIAS_ROWS = 6
B_BIAS_ROWS = 3
MAX_ROWS = 3
FIXED_MAX_LIMIT = 96.0
C_BAND_TQ = 128
D_BAND_TQ = 256
D_HEADS_PER_STEP = 4
C_TILES_PER_STEP = 4
D_TILES_PER_STEP = 4
VMEM_LIMIT = 56 * 1024 * 1024

f32 = jnp.float32
bf16 = jnp.bfloat16


def _split3(v):
    hi = v.astype(bf16).astype(f32)
    r = v - hi
    mid = r.astype(bf16).astype(f32)
    lo = (r - mid).astype(bf16).astype(f32)
    return hi, mid, lo


def _silu(z):
    return z * (1.0 / (1.0 + jnp.exp(-z)))


def _rms_rows(x, g_ref):
    ms = jnp.mean(x * x, axis=-1, keepdims=True)
    return (x * lax.rsqrt(ms + NORM_EPS) * g_ref[...]).astype(bf16)


def _residual_add(m1_ref, m2_ref, wt_ref, x_ref):
    half = m1_ref.shape[1]
    y_t = (jnp.dot(wt_ref[:, :half], m1_ref[0], preferred_element_type=f32)
           + jnp.dot(wt_ref[:, half:], m2_ref[0], preferred_element_type=f32))
    return x_ref[0] + y_t.T


def _proj_t(wt_ref, r0, r1, xn):
    return lax.dot_general(wt_ref[r0:r1, :], xn, (((1,), (1,)), ((), ())),
                           preferred_element_type=f32)


def _head_norm(z_t, gain_col, mult):
    n = z_t.shape[0] // HEAD_DIM
    z3 = z_t.reshape(n, HEAD_DIM, z_t.shape[1])
    ms = jnp.mean(z3 * z3, axis=1, keepdims=True)
    return z3 * lax.rsqrt(ms + NORM_EPS) * (gain_col[...] * mult)[None]


def _ones_rows(row, first):
    return jnp.where((row >= first) & (row < first + 3), 1.0, 0.0)


def _store_heads(o_ref, z_t):
    o_ref[0] = z_t.reshape(o_ref.shape[1], o_ref.shape[2], z_t.shape[1]).astype(bf16)


def _store_keys(k_ref, kn, aug_fn):
    n, _, t = kn.shape
    zeros = jnp.zeros((KPAD - HEAD_DIM - BF16_ROWS, t), f32)
    for s in range(n):
        blk = jnp.concatenate([kn[s], aug_fn(s), zeros], axis=0)
        k_ref[0, s] = blk.T.astype(bf16)


def _proj_even_kernel(x_ref, lng_ref, wt_ref, aqg_ref, akg_ref, bqg_ref, bkg_ref, bfb_ref, tri_ref,
                      aq_ref, ak_ref, av_ref, asg_ref, bq_ref, bk_ref, bv_ref, bsg_ref,
                      am_ref, bm_ref, cum_ref):
    t = pl.program_id(1)
    tt = x_ref.shape[1]
    xn = _rms_rows(x_ref[0], lng_ref)
    row = lax.broadcasted_iota(jnp.int32, (BF16_ROWS, tt), 0)

    z = _proj_t(wt_ref, 4096, 4096 + BF16_ROWS, xn)[:B_HEADS] + bfb_ref[...]
    aqn = _head_norm(_proj_t(wt_ref, 0, 512, xn), aqg_ref, QK_SCALE)
    aq_ref[0] = aqn.astype(bf16)

    pos = t * tt + lax.broadcasted_iota(jnp.int32, (BF16_ROWS, tt), 1)
    pos_a = lax.shift_right_logical(pos, int(math.log2(CHUNK))).astype(f32)
    pos_b = lax.bitwise_and(pos, CHUNK - 1).astype(f32)
    aug_a = jnp.where(row < 3, pos_a, jnp.where(row < 6, pos_b, _ones_rows(row, A_BIAS_ROWS)))
    akn = _head_norm(_proj_t(wt_ref, 512, 1024, xn), akg_ref, 1.0)
    _store_keys(ak_ref, akn, lambda s: aug_a)
    self_a = jnp.sum(aqn * akn, axis=1, keepdims=True)
    for s in range(A_STREAMS):
        am_ref[0, s] = self_a[s] + float(A_RATES[s // 2]) * pos[:1].astype(f32)

    _store_heads(asg_ref, _silu(_proj_t(wt_ref, 1536, 2048, xn)))
    bqn = _head_norm(_proj_t(wt_ref, 2048, 2560, xn), bqg_ref, QK_SCALE)
    bq_ref[0] = bqn.astype(bf16)

    log_f = jnp.minimum(z, 0.0) - jnp.log(1.0 + jnp.exp(-jnp.abs(z)))
    pieces = jnp.concatenate(_split3(log_f) + (jnp.zeros_like(log_f),), axis=0).astype(bf16)
    part = jnp.dot(pieces, tri_ref[...], preferred_element_type=f32)
    local = part[:B_HEADS] + part[B_HEADS:2 * B_HEADS] + part[2 * B_HEADS:3 * B_HEADS]

    @pl.when(t == 0)
    def _():
        cum_ref[...] = jnp.zeros_like(cum_ref)

    cum = cum_ref[...] + local
    cum_ref[...] = cum[:, tt - 1:tt]
    gate = -LOG2E * cum
    g_hi, g_mid, g_lo = _split3(gate)

    def aug_b(s):
        pick = lambda a: jnp.broadcast_to(a[s:s + 1], (BF16_ROWS, tt))
        return jnp.where(row == 0, pick(g_hi),
                         jnp.where(row == 1, pick(g_mid),
                                   jnp.where(row == 2, pick(g_lo), _ones_rows(row, B_BIAS_ROWS))))

    bkn = _head_norm(_proj_t(wt_ref, 2560, 3072, xn), bkg_ref, 1.0)
    _store_keys(bk_ref, bkn, aug_b)
    self_b = jnp.sum(bqn * bkn, axis=1, keepdims=True)
    for s in range(B_HEADS):
        bm_ref[0, s] = self_b[s] + gate[s:s + 1]
    _store_heads(bsg_ref, _silu(_proj_t(wt_ref, 3584, 4096, xn)))
    _store_heads(av_ref, _proj_t(wt_ref, 1024, 1536, xn))
    _store_heads(bv_ref, _proj_t(wt_ref, 3072, 3584, xn))


def _proj_even(x, ln_g, wt, aqg, akg, bqg, bkg, bfb):
    bsz, seq, _ = x.shape
    tt = PROJ_TOKENS
    col = lambda n: pl.BlockSpec((n, 1), lambda b, t: (0, 0))
    fm = lambda n, d: pl.BlockSpec((1, n, d, tt), lambda b, t: (b, 0, 0, t))
    km = lambda n: pl.BlockSpec((1, n, tt, KPAD), lambda b, t: (b, 0, t, 0))
    fm_shape = lambda n, d: jax.ShapeDtypeStruct((bsz, n, d, seq), bf16)
    km_shape = lambda n: jax.ShapeDtypeStruct((bsz, n, seq, KPAD), bf16)
    return pl.pallas_call(
        _proj_even_kernel,
        grid=(bsz, seq // tt),
        in_specs=[
            pl.BlockSpec((1, tt, D_MODEL), lambda b, t: (b, t, 0)),
            pl.BlockSpec((1, D_MODEL), lambda b, t: (0, 0)),
            pl.BlockSpec(wt.shape, lambda b, t: (0, 0)),
            col(HEAD_DIM), col(HEAD_DIM), col(HEAD_DIM), col(HEAD_DIM), col(B_HEADS),
            pl.BlockSpec((tt, tt), lambda b, t: (0, 0)),
        ],
        out_specs=[fm(A_STREAMS, HEAD_DIM), km(A_STREAMS), fm(A_HEADS, A_VDIM), fm(A_HEADS, A_VDIM),
                   fm(B_HEADS, HEAD_DIM), km(B_HEADS), fm(B_HEADS, HEAD_DIM), fm(B_HEADS, HEAD_DIM),
                   fm(A_STREAMS, 1), fm(B_HEADS, 1)],
        out_shape=[fm_shape(A_STREAMS, HEAD_DIM), km_shape(A_STREAMS), fm_shape(A_HEADS, A_VDIM),
                   fm_shape(A_HEADS, A_VDIM), fm_shape(B_HEADS, HEAD_DIM), km_shape(B_HEADS),
                   fm_shape(B_HEADS, HEAD_DIM), fm_shape(B_HEADS, HEAD_DIM),
                   jax.ShapeDtypeStruct((bsz, A_STREAMS, 1, seq), f32),
                   jax.ShapeDtypeStruct((bsz, B_HEADS, 1, seq), f32)],
        scratch_shapes=[pltpu.VMEM((B_HEADS, 1), f32)],
        compiler_params=pltpu.CompilerParams(
            dimension_semantics=("arbitrary", "arbitrary"), vmem_limit_bytes=VMEM_LIMIT),
        name="proj_even",
    )(x, ln_g, wt, aqg, akg, bqg, bkg, bfb, jnp.asarray(np.triu(np.ones((tt, tt), np.float32)), bf16))


def _proj_odd_kernel(m1_ref, m2_ref, wo_ref, x_ref, lng_ref, wt_ref, cqg_ref, ckg_ref, dqg_ref,
                     dkg_ref, x1_ref, cq_ref, ck_ref, cv_ref, csg_ref, dq_ref, dk_ref, dv_ref,
                     dsg_ref, cm_ref, dm_ref):
    tt = x_ref.shape[1]
    x1 = _residual_add(m1_ref, m2_ref, wo_ref, x_ref)
    x1_ref[0] = x1
    xn = _rms_rows(x1, lng_ref)
    ones = _ones_rows(lax.broadcasted_iota(jnp.int32, (BF16_ROWS, tt), 0), 0)
    aug = lambda s: ones

    cqn = _head_norm(_proj_t(wt_ref, 0, 512, xn), cqg_ref, QK_SCALE)
    cq_ref[0] = cqn.astype(bf16)
    ckn = _head_norm(_proj_t(wt_ref, 512, 640, xn), ckg_ref, 1.0)
    _store_keys(ck_ref, ckn, aug)
    cm_ref[0] = jnp.sum(cqn.reshape(C_KV_HEADS, C_GROUP, HEAD_DIM, tt) * ckn[:, None], axis=2,
                        keepdims=True).reshape(C_HEADS, 1, tt)
    _store_heads(cv_ref, _proj_t(wt_ref, 640, 768, xn))
    _store_heads(csg_ref, _silu(_proj_t(wt_ref, 768, 1280, xn)))
    dqn = _head_norm(_proj_t(wt_ref, 1280, 1792, xn), dqg_ref, QK_SCALE)
    dq_ref[0] = dqn.astype(bf16)
    dkn = _head_norm(_proj_t(wt_ref, 1792, 2304, xn), dkg_ref, 1.0)
    _store_keys(dk_ref, dkn, aug)
    dm_ref[0] = jnp.sum(dqn * dkn, axis=1, keepdims=True)
    _store_heads(dsg_ref, _silu(_proj_t(wt_ref, 2816, 3328, xn)))
    _store_heads(dv_ref, _proj_t(wt_ref, 2304, 2816, xn))


def _proj_odd(m1, m2, wo_t, x, ln_g, wt, cqg, ckg, dqg, dkg):
    bsz, seq, _ = x.shape
    tt = PROJ_TOKENS
    half = m1.shape[1]
    col = lambda n: pl.BlockSpec((n, 1), lambda b, t: (0, 0))
    rows = pl.BlockSpec((1, tt, D_MODEL), lambda b, t: (b, t, 0))
    fm = lambda n: pl.BlockSpec((1, n, HEAD_DIM, tt), lambda b, t: (b, 0, 0, t))
    km = lambda n: pl.BlockSpec((1, n, tt, KPAD), lambda b, t: (b, 0, t, 0))
    fm_shape = lambda n: jax.ShapeDtypeStruct((bsz, n, HEAD_DIM, seq), bf16)
    km_shape = lambda n: jax.ShapeDtypeStruct((bsz, n, seq, KPAD), bf16)
    return pl.pallas_call(
        _proj_odd_kernel,
        grid=(bsz, seq // tt),
        in_specs=[
            pl.BlockSpec((1, half, tt), lambda b, t: (b, 0, t)),
            pl.BlockSpec((1, half, tt), lambda b, t: (b, 0, t)),
            pl.BlockSpec(wo_t.shape, lambda b, t: (0, 0)),
            rows,
            pl.BlockSpec((1, D_MODEL), lambda b, t: (0, 0)),
            pl.BlockSpec(wt.shape, lambda b, t: (0, 0)),
            col(HEAD_DIM), col(HEAD_DIM), col(HEAD_DIM), col(HEAD_DIM),
        ],
        out_specs=[rows, fm(C_HEADS), km(C_KV_HEADS), fm(C_KV_HEADS), fm(C_HEADS),
                   fm(D_HEADS), km(D_HEADS), fm(D_HEADS), fm(D_HEADS),
                   pl.BlockSpec((1, C_HEADS, 1, tt), lambda b, t: (b, 0, 0, t)),
                   pl.BlockSpec((1, D_HEADS, 1, tt), lambda b, t: (b, 0, 0, t))],
        out_shape=[jax.ShapeDtypeStruct(x.shape, f32),
                   fm_shape(C_HEADS), km_shape(C_KV_HEADS), fm_shape(C_KV_HEADS), fm_shape(C_HEADS),
                   fm_shape(D_HEADS), km_shape(D_HEADS), fm_shape(D_HEADS), fm_shape(D_HEADS),
                   jax.ShapeDtypeStruct((bsz, C_HEADS, 1, seq), f32),
                   jax.ShapeDtypeStruct((bsz, D_HEADS, 1, seq), f32)],
        compiler_params=pltpu.CompilerParams(
            dimension_semantics=("arbitrary", "arbitrary"), vmem_limit_bytes=VMEM_LIMIT),
        name="proj_odd",
    )(m1, m2, wo_t, x, ln_g, wt, cqg, ckg, dqg, dkg)


def _online_step(carry, s, v):
    m, l, acc = carry
    m_new = jnp.maximum(m, jnp.max(s, axis=0, keepdims=True))
    p = jnp.exp2(s - m_new)
    alpha = jnp.exp2(m - m_new)
    l = alpha * l + jnp.sum(p, axis=0, keepdims=True)
    acc = alpha * acc + jnp.dot(v, p.astype(bf16), preferred_element_type=f32)
    return m_new, l, acc


def _colsum8(p):
    return p.reshape(p.shape[0] // 8, 8, p.shape[1]).sum(axis=0)


def _augment_q(q, aug_col, n_bias, m=None):
    tq = q.shape[1]
    aug = jnp.broadcast_to(aug_col, (KPAD - HEAD_DIM, tq))
    if m is not None:
        row = lax.broadcasted_iota(jnp.int32, aug.shape, 0)
        for r, piece in enumerate(_split3(-m)):
            aug = jnp.where(row == n_bias + r, piece, aug)
    return jnp.concatenate([q, aug.astype(bf16)], axis=0)


def _staggered(n, scores, finish):
    out, pending = [], scores(0)
    for c in range(1, n):
        nxt = scores(c)
        out.append(finish(c - 1, pending))
        pending = nxt
    out.append(finish(n - 1, pending))
    return out


def _diag_full(chains, d0, tq):
    def scores(c):
        k_at, _, q, aug_col, n_bias, diag_bias, _ = chains[c]
        return jnp.dot(k_at(d0, tq), _augment_q(q, aug_col, n_bias),
                       preferred_element_type=f32) + diag_bias

    def finish(c, s):
        m = jnp.max(s, axis=0, keepdims=True)
        p = jnp.exp2(s - m)
        return m, _colsum8(p), jnp.dot(chains[c][1](d0, tq), p.astype(bf16), preferred_element_type=f32)

    return _staggered(len(chains), scores, finish)


def _diag_halves(chains, q_aug, d0, tq):
    h = tq // 2
    d1 = pl.multiple_of(d0 + h, h)

    def scores(c):
        k_at = chains[c][0]
        return (jnp.dot(k_at(d0, h), q_aug[c], preferred_element_type=f32),
                jnp.dot(k_at(d1, h), q_aug[c][:, h:], preferred_element_type=f32))

    def finish(c, s):
        v_at, bias = chains[c][1], chains[c][5][:h, :h]
        p0 = jnp.concatenate([jnp.exp2(s[0][:, :h] + bias), jnp.exp2(s[0][:, h:])], axis=1)
        p1 = jnp.exp2(s[1] + bias)
        l0 = _colsum8(p0)
        a0 = jnp.dot(v_at(d0, h), p0.astype(bf16), preferred_element_type=f32)
        a1 = jnp.dot(v_at(d1, h), p1.astype(bf16), preferred_element_type=f32)
        return (jnp.concatenate([l0[:, :h], l0[:, h:] + _colsum8(p1)], axis=1),
                jnp.concatenate([a0[:, :h], a0[:, h:] + a1], axis=1))

    return _staggered(len(chains), scores, finish)


def _causal_sweep(chains, i, fixed_max, l_s, acc_s):
    n = len(chains)
    tq = chains[0][2].shape[1]
    d0 = pl.multiple_of(i * tq, tq)
    n_tiles = i * (tq // ATT_TK)
    tile = lambda j: (pl.multiple_of(j * ATT_TK, ATT_TK), ATT_TK)

    if fixed_max:
        q_aug = [_augment_q(q, aug_col, n_bias, m) for _, _, q, aug_col, n_bias, _, m in chains]

        for c, (l, acc) in enumerate(_diag_halves(chains, q_aug, d0, tq)):
            l_s[c] = l
            acc_s[c] = acc

        def body(j, carry):
            def finish(c, s):
                p = jnp.exp2(s)
                l_s[c] += _colsum8(p)
                acc_s[c] += jnp.dot(chains[c][1](*tile(j)), p.astype(bf16), preferred_element_type=f32)

            scores = lambda c: jnp.dot(chains[c][0](*tile(j)), q_aug[c], preferred_element_type=f32)
            _staggered(n, scores, finish)
            return carry

        lax.fori_loop(0, n_tiles, body, 0)
        return [(acc_s[c], jnp.sum(l_s[c], axis=0, keepdims=True)) for c in range(n)]

    q_aug = [_augment_q(q, aug_col, n_bias) for _, _, q, aug_col, n_bias, _, _ in chains]

    def body(j, carries):
        scores = lambda c: jnp.dot(chains[c][0](*tile(j)), q_aug[c], preferred_element_type=f32)
        finish = lambda c, s: _online_step(carries[c], s, chains[c][1](*tile(j)))
        return tuple(_staggered(n, scores, finish))

    init = tuple((m, jnp.sum(l, axis=0, keepdims=True), acc) for m, l, acc in _diag_full(chains, d0, tq))
    return [(acc, l) for _, l, acc in lax.fori_loop(0, n_tiles, body, init)]


def _either_sweep(fixed_ref, run):
    @pl.when(fixed_ref[0] != 0)
    def _():
        run(True)

    @pl.when(fixed_ref[0] == 0)
    def _():
        run(False)


def _attn_a_kernel(fixed_ref, q_ref, k_ref, v_ref, sg_ref, m_ref, dtab_ref, qaug_ref, subg_ref,
                   lamv_ref, o_ref, l_s, acc_s, *, lam_init):
    i = pl.program_id(2)
    chains = []
    for h in range(A_HEADS_PER_STEP):
        v_at = lambda start, size, h=h: v_ref[0, h, :, pl.ds(start, size)]
        for c in range(2):
            s = 2 * h + c
            k_at = lambda start, size, s=s: k_ref[0, s, pl.ds(start, size), :]
            chains.append((k_at, v_at, q_ref[0, s], qaug_ref[h], A_BIAS_ROWS, dtab_ref[h],
                           m_ref[0, s]))

    def run(fixed_max):
        lv = lamv_ref[...]
        lam = (jnp.exp(jnp.sum(lv[0:1] * lv[1:2], axis=1, keepdims=True))
               - jnp.exp(jnp.sum(lv[2:3] * lv[3:4], axis=1, keepdims=True)) + lam_init)
        outs = [acc * (1.0 / l) for acc, l in _causal_sweep(chains, i, fixed_max, l_s, acc_s)]
        for h in range(A_HEADS_PER_STEP):
            o = outs[2 * h] - lam * outs[2 * h + 1]
            ms = jnp.mean(o * o, axis=0, keepdims=True)
            y = o * lax.rsqrt(ms + NORM_EPS) * (subg_ref[...] * (1.0 - lam_init))
            o_ref[0, h] = (y * sg_ref[0, h].astype(f32)).astype(bf16)

    _either_sweep(fixed_ref, run)


def _attn_a(fixed, aq, ak, av, asg, am, dtab, qaug, subg, lamv, lam_init):
    bsz, _, _, seq = aq.shape
    nq = seq // ATT_TQ
    hs = A_HEADS_PER_STEP
    return pl.pallas_call(
        functools.partial(_attn_a_kernel, lam_init=lam_init),
        grid=(bsz, A_HEADS // hs, nq),
        in_specs=[
            pl.BlockSpec(memory_space=pltpu.SMEM),
            pl.BlockSpec((1, 2 * hs, HEAD_DIM, ATT_TQ), lambda b, h, i: (b, h, 0, i)),
            pl.BlockSpec((1, 2 * hs, seq, KPAD), lambda b, h, i: (b, h, 0, 0)),
            pl.BlockSpec((1, hs, A_VDIM, seq), lambda b, h, i: (b, h, 0, 0)),
            pl.BlockSpec((1, hs, A_VDIM, ATT_TQ), lambda b, h, i: (b, h, 0, i)),
            pl.BlockSpec((1, 2 * hs, 1, ATT_TQ), lambda b, h, i: (b, h, 0, i)),
            pl.BlockSpec((hs, ATT_TQ, ATT_TQ), lambda b, h, i: (h, 0, 0)),
            pl.BlockSpec((hs, KPAD - HEAD_DIM, 1), lambda b, h, i: (h, 0, 0)),
            pl.BlockSpec((A_VDIM, 1), lambda b, h, i: (0, 0)),
            pl.BlockSpec((4, HEAD_DIM), lambda b, h, i: (0, 0)),
        ],
        out_specs=pl.BlockSpec((1, hs, A_VDIM, ATT_TQ), lambda b, h, i: (b, h, 0, i)),
        out_shape=jax.ShapeDtypeStruct((bsz, A_HEADS, A_VDIM, seq), bf16),
        scratch_shapes=[pltpu.VMEM((2 * hs, 8, ATT_TQ), f32), pltpu.VMEM((2 * hs, A_VDIM, ATT_TQ), f32)],
        compiler_params=pltpu.CompilerParams(
            dimension_semantics=("arbitrary", "arbitrary", "arbitrary"),
            vmem_limit_bytes=VMEM_LIMIT),
        name="attn_a",
    )(fixed, aq, ak, av, asg, am, dtab, qaug, subg, lamv)


def _attn_b_kernel(fixed_ref, q_ref, k_ref, v_ref, sg_ref, m_ref, qaug_ref, o_ref, l_s, acc_s):
    i = pl.program_id(2)

    def run(fixed_max):
        causal = jnp.where(lax.broadcasted_iota(jnp.int32, (ATT_TQ, ATT_TQ), 0)
                           <= lax.broadcasted_iota(jnp.int32, (ATT_TQ, ATT_TQ), 1), 0.0, NEG)
        chains = []
        for h in range(B_HEADS_PER_STEP):
            k_at = lambda start, size, h=h: k_ref[0, h, pl.ds(start, size), :]
            v_at = lambda start, size, h=h: v_ref[0, h, :, pl.ds(start, size)]
            chains.append((k_at, v_at, q_ref[0, h], qaug_ref[...], B_BIAS_ROWS, causal, m_ref[0, h]))
        for h, (acc, l) in enumerate(_causal_sweep(chains, i, fixed_max, l_s, acc_s)):
            o_ref[0, h] = (acc * (1.0 / l) * sg_ref[0, h].astype(f32)).astype(bf16)

    _either_sweep(fixed_ref, run)


def _attn_b(fixed, bq, bk, bv, bsg, bm, qaug):
    bsz, _, _, seq = bq.shape
    nq = seq // ATT_TQ
    hs = B_HEADS_PER_STEP
    return pl.pallas_call(
        _attn_b_kernel,
        grid=(bsz, B_HEADS // hs, nq),
        in_specs=[
            pl.BlockSpec(memory_space=pltpu.SMEM),
            pl.BlockSpec((1, hs, HEAD_DIM, ATT_TQ), lambda b, h, i: (b, h, 0, i)),
            pl.BlockSpec((1, hs, seq, KPAD), lambda b, h, i: (b, h, 0, 0)),
            pl.BlockSpec((1, hs, HEAD_DIM, seq), lambda b, h, i: (b, h, 0, 0)),
            pl.BlockSpec((1, hs, HEAD_DIM, ATT_TQ), lambda b, h, i: (b, h, 0, i)),
            pl.BlockSpec((1, hs, 1, ATT_TQ), lambda b, h, i: (b, h, 0, i)),
            pl.BlockSpec((KPAD - HEAD_DIM, 1), lambda b, h, i: (0, 0)),
        ],
        out_specs=pl.BlockSpec((1, hs, HEAD_DIM, ATT_TQ), lambda b, h, i: (b, h, 0, i)),
        out_shape=jax.ShapeDtypeStruct((bsz, B_HEADS, HEAD_DIM, seq), bf16),
        scratch_shapes=[pltpu.VMEM((hs, 8, ATT_TQ), f32), pltpu.VMEM((hs, HEAD_DIM, ATT_TQ), f32)],
        compiler_params=pltpu.CompilerParams(
            dimension_semantics=("arbitrary", "arbitrary", "arbitrary"),
            vmem_limit_bytes=VMEM_LIMIT),
        name="attn_b",
    )(fixed, bq, bk, bv, bsg, bm, qaug)


def _band_kernel(fixed_ref, q_ref, k_ref, v_ref, sg_ref, m_ref, tab_ref, sink_ref, o_ref,
                 *, group, back, tq):
    blocks = tq // LANES
    band = (back + blocks) * LANES
    tiles = q_ref.shape[3] // tq
    work = [(u, h) for u in range(tiles) for h in range(k_ref.shape[1])]
    grouped = lambda ref, u, h: jnp.concatenate(
        [ref[0, h * group + g, :, u * tq:(u + 1) * tq] for g in range(group)], axis=1)

    def window(u):
        first = (pl.program_id(2) * tiles + u) * blocks
        return (pl.multiple_of(jnp.maximum(first - back, 0) * LANES, LANES),
                pl.multiple_of(jnp.maximum(back - first, 0) * LANES, LANES))

    def run(fixed_max):
        def scores(w):
            u, h = work[w]
            k_start, tab_start = window(u)
            q = _augment_q(grouped(q_ref, u, h), 0.0, 0, grouped(m_ref, u, h) if fixed_max else None)
            s = jnp.dot(k_ref[0, h, pl.ds(k_start, band), :], q, preferred_element_type=f32)
            return s + tab_ref[h, pl.ds(tab_start, band), :]

        def finish(w, s):
            u, h = work[w]
            k_start, _ = window(u)
            if fixed_max:
                m = grouped(m_ref, u, h)
                p = jnp.exp2(s)
            else:
                m = jnp.maximum(jnp.max(s, axis=0, keepdims=True), sink_ref[h])
                p = jnp.exp2(s - m)
            l = jnp.sum(p, axis=0, keepdims=True) + jnp.exp2(sink_ref[h] - m)
            o = jnp.dot(v_ref[0, h, :, pl.ds(k_start, band)], p.astype(bf16),
                        preferred_element_type=f32) * (1.0 / l)
            for g in range(group):
                hq = h * group + g
                gate = sg_ref[0, hq, :, u * tq:(u + 1) * tq].astype(f32)
                o_ref[0, hq, :, u * tq:(u + 1) * tq] = (o[:, g * tq:(g + 1) * tq] * gate).astype(bf16)

        _staggered(len(work), scores, finish)

    _either_sweep(fixed_ref, run)


def _band_attn(fixed, q, k, v, sg, m, tab, sink, group, back, tq, kv_per_step, tiles_per_step, name):
    bsz, nheads, _, seq = q.shape
    hs = kv_per_step
    tile = tq * tiles_per_step
    return pl.pallas_call(
        functools.partial(_band_kernel, group=group, back=back, tq=tq),
        grid=(bsz, nheads // (group * hs), seq // tile),
        in_specs=[
            pl.BlockSpec(memory_space=pltpu.SMEM),
            pl.BlockSpec((1, hs * group, HEAD_DIM, tile), lambda b, h, i: (b, h, 0, i)),
            pl.BlockSpec((1, hs, seq, KPAD), lambda b, h, i: (b, h, 0, 0)),
            pl.BlockSpec((1, hs, HEAD_DIM, seq), lambda b, h, i: (b, h, 0, 0)),
            pl.BlockSpec((1, hs * group, HEAD_DIM, tile), lambda b, h, i: (b, h, 0, i)),
            pl.BlockSpec((1, hs * group, 1, tile), lambda b, h, i: (b, h, 0, i)),
            pl.BlockSpec((hs,) + tab.shape[1:], lambda b, h, i: (h, 0, 0)),
            pl.BlockSpec((hs,) + sink.shape[1:], lambda b, h, i: (h, 0, 0)),
        ],
        out_specs=pl.BlockSpec((1, hs * group, HEAD_DIM, tile), lambda b, h, i: (b, h, 0, i)),
        out_shape=jax.ShapeDtypeStruct((bsz, nheads, HEAD_DIM, seq), bf16),
        compiler_params=pltpu.CompilerParams(
            dimension_semantics=("arbitrary", "arbitrary", "arbitrary"),
            vmem_limit_bytes=VMEM_LIMIT),
        name=name,
    )(fixed, q, k, v, sg, m, tab, sink)


def _out_proj_kernel(m1_ref, m2_ref, wt_ref, x_ref, o_ref):
    o_ref[0] = _residual_add(m1_ref, m2_ref, wt_ref, x_ref)


def _out_proj(m1, m2, wt, x):
    bsz, seq, _ = x.shape
    tt = PROJ_TOKENS
    half = m1.shape[1]
    return pl.pallas_call(
        _out_proj_kernel,
        grid=(bsz, seq // tt),
        in_specs=[
            pl.BlockSpec((1, half, tt), lambda b, t: (b, 0, t)),
            pl.BlockSpec((1, half, tt), lambda b, t: (b, 0, t)),
            pl.BlockSpec(wt.shape, lambda b, t: (0, 0)),
            pl.BlockSpec((1, tt, D_MODEL), lambda b, t: (b, t, 0)),
        ],
        out_specs=pl.BlockSpec((1, tt, D_MODEL), lambda b, t: (b, t, 0)),
        out_shape=jax.ShapeDtypeStruct(x.shape, f32),
        compiler_params=pltpu.CompilerParams(
            dimension_semantics=("arbitrary", "arbitrary"), vmem_limit_bytes=VMEM_LIMIT),
        name="out_proj",
    )(m1, m2, wt, x)


def _alibi_slopes(n):
    return 2.0 ** (-8.0 * np.arange(1, n + 1, dtype=np.float64) / n)


def _np_split3(v):
    v = np.asarray(v, np.float32)
    to_bf = lambda a: a.astype(bf16).astype(np.float32)
    hi = to_bf(v)
    mid = to_bf(v - hi)
    lo = to_bf(v - hi - mid)
    return hi, mid, lo


def _a_tables():
    rate = A_RATES
    qaug = np.zeros((A_HEADS, KPAD - HEAD_DIM, 1), np.float32)
    for idx, piece in enumerate(_np_split3(rate * CHUNK) + _np_split3(rate)):
        qaug[:, idx, 0] = piece
    kk = np.arange(ATT_TQ)[:, None]
    qq = np.arange(ATT_TQ)[None, :]
    future = np.maximum(kk - qq, 0).astype(np.float32)
    corr = -2.0 * rate[:, None, None] * future[None]
    allowed = (kk // CHUNK) <= (qq // CHUNK)
    dtab = np.where(allowed[None], corr, NEG).astype(np.float32)
    return jnp.asarray(qaug), jnp.asarray(dtab)


def _band_frames(back, tq):
    k_pos = np.arange(back * LANES + tq)[:, None]
    q_pos = back * LANES + np.arange(tq)[None, :]
    return q_pos - k_pos, q_pos // CHUNK - k_pos // CHUNK


def _c_tables(sinks):
    back, tq = WIN_CHUNKS * CHUNK // LANES, C_BAND_TQ
    rel, chunk_diff = _band_frames(back, tq)
    allowed = (chunk_diff >= 0) & (chunk_diff <= WIN_CHUNKS)
    slopes = _alibi_slopes(C_HEADS)
    per_head = np.where(allowed[None], -slopes[:, None, None] * np.abs(rel)[None] * LOG2E, NEG)
    tab = per_head.reshape(C_KV_HEADS, C_GROUP, *rel.shape).transpose(0, 2, 1, 3)
    tab = tab.reshape(C_KV_HEADS, rel.shape[0], C_GROUP * tq).astype(np.float32)
    tab = np.concatenate([tab, np.full((C_KV_HEADS, back * LANES, tab.shape[2]), NEG, np.float32)], 1)
    sink = jnp.repeat(sinks.astype(f32) * LOG2E, tq).reshape(C_KV_HEADS, 1, C_GROUP * tq)
    return jnp.asarray(tab), sink, back


def _d_tables(rel_table):
    back, t = D_LEFT_CHUNKS * CHUNK // LANES, D_BAND_TQ
    band = back * LANES + t
    rel, chunk_diff = _band_frames(back, t)
    allowed = (chunk_diff >= 0) & (chunk_diff <= D_LEFT_CHUNKS)
    tbl = rel_table.astype(f32) * LOG2E
    n_lo = (t - 1) - (CHUNK - 1)
    n_hi = (band - 1) - REL_MAX
    diag = jnp.concatenate([jnp.broadcast_to(tbl[:, :1], (D_HEADS, n_lo)), tbl,
                            jnp.broadcast_to(tbl[:, -1:], (D_HEADS, n_hi))], axis=1)
    m = t + LANES - 1
    blocks = []
    for kb in range(band // LANES):
        lo = rel[kb * LANES:(kb + 1) * LANES].min()
        if lo >= REL_MAX:
            blocks.append(jnp.broadcast_to(tbl[:, -1:, None], (D_HEADS, LANES, t)))
            continue
        window = diag[:, band - (kb + 1) * LANES:band - (kb + 1) * LANES + m]
        skew = jnp.broadcast_to(window[:, None, :], (D_HEADS, LANES + 1, m)).reshape(D_HEADS, -1)
        skew = skew[:, :LANES * (m + 1)].reshape(D_HEADS, LANES, m + 1)[:, :, :t]
        blocks.append(jnp.flip(skew, axis=1))
    tab = jnp.where(jnp.asarray(allowed)[None], jnp.concatenate(blocks, axis=1), NEG)
    tab = jnp.concatenate([tab, jnp.full((D_HEADS, back * LANES, t), NEG, f32)], axis=1)
    sink = jnp.full((D_HEADS, 1, t), NEG, f32)
    return tab, sink, back


def _fixed_max_ok(q_gain, k_gain, bias_range=0.0):
    spread = (2.0 * 1.02 * QK_SCALE * HEAD_DIM
              * jnp.max(jnp.abs(q_gain.astype(f32))) * jnp.max(jnp.abs(k_gain.astype(f32))))
    return (spread + bias_range <= FIXED_MAX_LIMIT).astype(jnp.int32).reshape(1)


def _pad_rows(w_t, rows):
    return jnp.pad(w_t, ((0, rows - w_t.shape[0]), (0, 0)))


def _even_layer(x, ln_g, w_in, w_out, a_qn_g, a_kn_g, a_lq1, a_lk1, a_lq2, a_lk2, a_subln_g,
                b_qn_g, b_kn_g, b_f_bias, layer_idx):
    bsz, seq, _ = x.shape
    colv = lambda v: v.astype(f32).reshape(-1, 1)
    wt = _pad_rows(w_in.T.astype(bf16), 4096 + BF16_ROWS)
    aq, ak, av, asg, bq, bk, bv, bsg, am, bm = _proj_even(
        x, ln_g.astype(f32).reshape(1, -1), wt, colv(a_qn_g), colv(a_kn_g), colv(b_qn_g),
        colv(b_kn_g), colv(b_f_bias))
    lam_init = 0.8 - 0.6 * math.exp(-0.3 * layer_idx)
    qaug_a, dtab = _a_tables()
    lamv = jnp.stack([a_lq1, a_lk1, a_lq2, a_lk2]).astype(f32)
    mix_a = _attn_a(_fixed_max_ok(a_qn_g, a_kn_g), aq, ak, av, asg, am, dtab, qaug_a,
                    colv(a_subln_g), lamv, lam_init)
    qaug_b = np.zeros((KPAD - HEAD_DIM, 1), np.float32)
    qaug_b[:B_BIAS_ROWS] = 1.0
    mix_b = _attn_b(_fixed_max_ok(b_qn_g, b_kn_g), bq, bk, bv, bsg, bm, jnp.asarray(qaug_b))
    return mix_a.reshape(bsz, -1, seq), mix_b.reshape(bsz, -1, seq), w_out.T.astype(bf16), x


def _odd_layer(pending, ln_g, w_in, w_out, c_qn_g, c_kn_g, c_sinks, d_qn_g, d_kn_g, d_rel_bias):
    bsz, seq, _ = pending[3].shape
    colv = lambda v: v.astype(f32).reshape(-1, 1)
    x, cq, ck, cv, csg, dq, dk, dv, dsg, cm, dm = _proj_odd(
        *pending, ln_g.astype(f32).reshape(1, -1), w_in.T.astype(bf16), colv(c_qn_g), colv(c_kn_g),
        colv(d_qn_g), colv(d_kn_g))
    tab_c, sink_c, back_c = _c_tables(c_sinks)
    fixed_c = _fixed_max_ok(c_qn_g, c_kn_g, LOG2E * jnp.maximum(jnp.max(c_sinks.astype(f32)), 0.0))
    mix_c = _band_attn(fixed_c, cq, ck, cv, csg, cm, tab_c, sink_c, C_GROUP, back_c, C_BAND_TQ,
                       C_KV_HEADS, C_TILES_PER_STEP, "attn_c")
    tab_d, sink_d, back_d = _d_tables(d_rel_bias)
    fixed_d = _fixed_max_ok(d_qn_g, d_kn_g, LOG2E * jnp.max(jnp.abs(d_rel_bias.astype(f32))))
    mix_d = _band_attn(fixed_d, dq, dk, dv, dsg, dm, tab_d, sink_d, 1, back_d, D_BAND_TQ,
                       D_HEADS_PER_STEP, D_TILES_PER_STEP, "attn_d")
    return mix_c.reshape(bsz, -1, seq), mix_d.reshape(bsz, -1, seq), w_out.T.astype(bf16), x


def kernel(x, even_ln_g, even_w_in, even_w_out, a_q_norm_g, a_k_norm_g, a_lambda_q1, a_lambda_k1, a_lambda_q2, a_lambda_k2, a_subln_g, b_q_norm_g, b_k_norm_g, b_forget_bias, odd_ln_g, odd_w_in, odd_w_out, c_q_norm_g, c_k_norm_g, c_sinks, d_q_norm_g, d_k_norm_g, d_rel_bias):
    depth = even_ln_g.shape[0] + odd_ln_g.shape[0]
    seq = x.shape[1]
    assert x.shape[2] == D_MODEL and even_w_in.shape[2] == P_EVEN and odd_w_in.shape[2] == P_ODD
    for tile in (PROJ_TOKENS, ATT_TQ, C_BAND_TQ * C_TILES_PER_STEP, D_BAND_TQ * D_TILES_PER_STEP):
        assert seq % tile == 0, (seq, tile)
    pending = None
    for i in range(depth):
        j = i // 2
        if i % 2 == 0:
            if pending is not None:
                x = _out_proj(*pending)
            pending = _even_layer(x, even_ln_g[j], even_w_in[j], even_w_out[j], a_q_norm_g[j],
                                  a_k_norm_g[j], a_lambda_q1[j], a_lambda_k1[j], a_lambda_q2[j],
                                  a_lambda_k2[j], a_subln_g[j], b_q_norm_g[j], b_k_norm_g[j],
                                  b_forget_bias[j], i)
        else:
            pending = _odd_layer(pending, odd_ln_g[j], odd_w_in[j], odd_w_out[j], c_q_norm_g[j],
                                 c_k_norm_g[j], c_sinks[j], d_q_norm_g[j], d_k_norm_g[j],
                                 d_rel_bias[j])
    return _out_proj(*pending)
```

```python
import functools
import math

import numpy as np
import jax
import jax.numpy as jnp
from jax import lax
from jax.experimental import pallas as pl
from jax.experimental.pallas import tpu as pltpu

D_MODEL = 1024
CHUNK = 64
HEAD_DIM = 64
NORM_EPS = 1e-6

A_HEADS = 4
A_STREAMS = 2 * A_HEADS
A_VDIM = 2 * HEAD_DIM
B_HEADS = 8
C_HEADS = 8
C_KV_HEADS = 2
C_GROUP = C_HEADS // C_KV_HEADS
WIN_CHUNKS = 2
D_HEADS = 8
D_LEFT_CHUNKS = 8
REL_MAX = 256

P_EVEN = 8 * 512 + B_HEADS
P_ODD = 512 + 128 + 128 + 512 + 4 * 512

LOG2E = 1.4426950408889634
QK_SCALE = HEAD_DIM ** -0.5 * LOG2E
NEG = -1e30
A_RATES = (2.0 ** (-8.0 * np.arange(1, A_HEADS + 1) / A_HEADS) * LOG2E).astype(np.float32)

LANES = 128
KPAD = 128
BF16_ROWS = 16

PROJ_TOKENS = 512
OUT_TOKENS = 1024
ATT_TQ = 512
ATT_TK = 512
A_HEADS_PER_STEP = 4
B_HEADS_PER_STEP = 8
A_BIAS_ROWS = 6
B_BIAS_ROWS = 3
MAX_ROWS = 3
FIXED_MAX_LIMIT = 96.0
C_BAND_TQ = 128
D_BAND_TQ = 256
D_HEADS_PER_STEP = 4
C_TILES_PER_STEP = 4
D_TILES_PER_STEP = 4
VMEM_LIMIT = 56 * 1024 * 1024

f32 = jnp.float32
bf16 = jnp.bfloat16


def _split3(v):
    hi = v.astype(bf16).astype(f32)
    r = v - hi
    mid = r.astype(bf16).astype(f32)
    lo = (r - mid).astype(bf16).astype(f32)
    return hi, mid, lo


def _silu(z):
    return z * (1.0 / (1.0 + jnp.exp(-z)))


def _rms_rows(x, g_ref):
    ms = jnp.mean(x * x, axis=-1, keepdims=True)
    return (x * lax.rsqrt(ms + NORM_EPS) * g_ref[...]).astype(bf16)


def _residual_add(m1_ref, m2_ref, wt_ref, x_ref):
    half = m1_ref.shape[1]
    y_t = (jnp.dot(wt_ref[:, :half], m1_ref[0], preferred_element_type=f32)
           + jnp.dot(wt_ref[:, half:], m2_ref[0], preferred_element_type=f32))
    return x_ref[0] + y_t.T


def _proj_t(wt_ref, r0, r1, xn):
    return lax.dot_general(wt_ref[r0:r1, :], xn, (((1,), (1,)), ((), ())),
                           preferred_element_type=f32)


def _head_norm(z_t, gain_col, mult):
    n = z_t.shape[0] // HEAD_DIM
    z3 = z_t.reshape(n, HEAD_DIM, z_t.shape[1])
    ms = jnp.mean(z3 * z3, axis=1, keepdims=True)
    return z3 * lax.rsqrt(ms + NORM_EPS) * (gain_col[...] * mult)[None]


def _ones_rows(row, first):
    return jnp.where((row >= first) & (row < first + 3), 1.0, 0.0)


def _store_heads(o_ref, z_t):
    o_ref[0] = z_t.reshape(o_ref.shape[1], o_ref.shape[2], z_t.shape[1]).astype(bf16)


def _store_keys(k_ref, kn, aug_fn):
    n, _, t = kn.shape
    zeros = jnp.zeros((KPAD - HEAD_DIM - BF16_ROWS, t), f32)
    for s in range(n):
        blk = jnp.concatenate([kn[s], aug_fn(s), zeros], axis=0)
        k_ref[0, s] = blk.T.astype(bf16)


def _proj_even_kernel(x_ref, lng_ref, wt_ref, wf_ref, aqg_ref, akg_ref, bqg_ref, bkg_ref, bfb_ref,
                      tri_ref, aq_ref, ak_ref, av_ref, asg_ref, bq_ref, bk_ref, bv_ref, bsg_ref,
                      am_ref, bm_ref, cum_ref):
    t = pl.program_id(1)
    tt = x_ref.shape[1]
    xn = _rms_rows(x_ref[0], lng_ref)
    row = lax.broadcasted_iota(jnp.int32, (BF16_ROWS, tt), 0)

    z = _proj_t(wf_ref, 0, BF16_ROWS, xn)[:B_HEADS] + bfb_ref[...]
    aqn = _head_norm(_proj_t(wt_ref, 0, 512, xn), aqg_ref, QK_SCALE)
    aq_ref[0] = aqn.astype(bf16)

    pos = t * tt + lax.broadcasted_iota(jnp.int32, (BF16_ROWS, tt), 1)
    pos_a = lax.shift_right_logical(pos, int(math.log2(CHUNK))).astype(f32)
    pos_b = lax.bitwise_and(pos, CHUNK - 1).astype(f32)
    aug_a = jnp.where(row < 3, pos_a, jnp.where(row < 6, pos_b, _ones_rows(row, A_BIAS_ROWS)))
    akn = _head_norm(_proj_t(wt_ref, 512, 1024, xn), akg_ref, 1.0)
    _store_keys(ak_ref, akn, lambda s: aug_a)
    self_a = jnp.sum(aqn * akn, axis=1, keepdims=True)
    for s in range(A_STREAMS):
        am_ref[0, s] = self_a[s] + float(A_RATES[s // 2]) * pos[:1].astype(f32)

    _store_heads(asg_ref, _silu(_proj_t(wt_ref, 1536, 2048, xn)))
    bqn = _head_norm(_proj_t(wt_ref, 2048, 2560, xn), bqg_ref, QK_SCALE)
    bq_ref[0] = bqn.astype(bf16)

    log_f = jnp.minimum(z, 0.0) - jnp.log(1.0 + jnp.exp(-jnp.abs(z)))
    pieces = jnp.concatenate(_split3(log_f) + (jnp.zeros_like(log_f),), axis=0).astype(bf16)
    part = jnp.dot(pieces, tri_ref[...], preferred_element_type=f32)
    local = part[:B_HEADS] + part[B_HEADS:2 * B_HEADS] + part[2 * B_HEADS:3 * B_HEADS]

    @pl.when(t == 0)
    def _():
        cum_ref[...] = jnp.zeros_like(cum_ref)

    cum = cum_ref[...] + local
    cum_ref[...] = cum[:, tt - 1:tt]
    gate = -LOG2E * cum
    g_hi, g_mid, g_lo = _split3(gate)

    def aug_b(s):
        pick = lambda a: jnp.broadcast_to(a[s:s + 1], (BF16_ROWS, tt))
        return jnp.where(row == 0, pick(g_hi),
                         jnp.where(row == 1, pick(g_mid),
                                   jnp.where(row == 2, pick(g_lo), _ones_rows(row, B_BIAS_ROWS))))

    bkn = _head_norm(_proj_t(wt_ref, 2560, 3072, xn), bkg_ref, 1.0)
    _store_keys(bk_ref, bkn, aug_b)
    self_b = jnp.sum(bqn * bkn, axis=1, keepdims=True)
    for s in range(B_HEADS):
        bm_ref[0, s] = self_b[s] + gate[s:s + 1]
    _store_heads(bsg_ref, _silu(_proj_t(wt_ref, 3584, 4096, xn)))
    _store_heads(av_ref, _proj_t(wt_ref, 1024, 1536, xn))
    _store_heads(bv_ref, _proj_t(wt_ref, 3072, 3584, xn))


def _proj_even(x, ln_g, wt, wf, aqg, akg, bqg, bkg, bfb):
    bsz, seq, _ = x.shape
    tt = PROJ_TOKENS
    col = lambda n: pl.BlockSpec((n, 1), lambda b, t: (0, 0))
    fm = lambda n, d: pl.BlockSpec((1, n, d, tt), lambda b, t: (b, 0, 0, t))
    km = lambda n: pl.BlockSpec((1, n, tt, KPAD), lambda b, t: (b, 0, t, 0))
    fm_shape = lambda n, d: jax.ShapeDtypeStruct((bsz, n, d, seq), bf16)
    km_shape = lambda n: jax.ShapeDtypeStruct((bsz, n, seq, KPAD), bf16)
    return pl.pallas_call(
        _proj_even_kernel,
        grid=(bsz, seq // tt),
        in_specs=[
            pl.BlockSpec((1, tt, D_MODEL), lambda b, t: (b, t, 0)),
            pl.BlockSpec((1, D_MODEL), lambda b, t: (0, 0)),
            pl.BlockSpec(wt.shape, lambda b, t: (0, 0)),
            pl.BlockSpec(wf.shape, lambda b, t: (0, 0)),
            col(HEAD_DIM), col(HEAD_DIM), col(HEAD_DIM), col(HEAD_DIM), col(B_HEADS),
            pl.BlockSpec((tt, tt), lambda b, t: (0, 0)),
        ],
        out_specs=[fm(A_STREAMS, HEAD_DIM), km(A_STREAMS), fm(A_HEADS, A_VDIM), fm(A_HEADS, A_VDIM),
                   fm(B_HEADS, HEAD_DIM), km(B_HEADS), fm(B_HEADS, HEAD_DIM), fm(B_HEADS, HEAD_DIM),
                   fm(A_STREAMS, 1), fm(B_HEADS, 1)],
        out_shape=[fm_shape(A_STREAMS, HEAD_DIM), km_shape(A_STREAMS), fm_shape(A_HEADS, A_VDIM),
                   fm_shape(A_HEADS, A_VDIM), fm_shape(B_HEADS, HEAD_DIM), km_shape(B_HEADS),
                   fm_shape(B_HEADS, HEAD_DIM), fm_shape(B_HEADS, HEAD_DIM),
                   jax.ShapeDtypeStruct((bsz, A_STREAMS, 1, seq), f32),
                   jax.ShapeDtypeStruct((bsz, B_HEADS, 1, seq), f32)],
        scratch_shapes=[pltpu.VMEM((B_HEADS, 1), f32)],
        compiler_params=pltpu.CompilerParams(
            dimension_semantics=("arbitrary", "arbitrary"), vmem_limit_bytes=VMEM_LIMIT),
        name="proj_even",
    )(x, ln_g, wt, wf, aqg, akg, bqg, bkg, bfb,
      jnp.asarray(np.triu(np.ones((tt, tt), np.float32)), bf16))


def _proj_odd_kernel(m1_ref, m2_ref, wo_ref, x_ref, lng_ref, wt_ref, cqg_ref, ckg_ref, dqg_ref,
                     dkg_ref, x1_ref, cq_ref, ck_ref, cv_ref, csg_ref, dq_ref, dk_ref, dv_ref,
                     dsg_ref, cm_ref, dm_ref):
    tt = x_ref.shape[1]
    x1 = _residual_add(m1_ref, m2_ref, wo_ref, x_ref)
    x1_ref[0] = x1
    xn = _rms_rows(x1, lng_ref)
    ones = _ones_rows(lax.broadcasted_iota(jnp.int32, (BF16_ROWS, tt), 0), 0)
    aug = lambda s: ones

    cqn = _head_norm(_proj_t(wt_ref, 0, 512, xn), cqg_ref, QK_SCALE)
    cq_ref[0] = cqn.astype(bf16)
    ckn = _head_norm(_proj_t(wt_ref, 512, 640, xn), ckg_ref, 1.0)
    _store_keys(ck_ref, ckn, aug)
    cm_ref[0] = jnp.sum(cqn.reshape(C_KV_HEADS, C_GROUP, HEAD_DIM, tt) * ckn[:, None], axis=2,
                        keepdims=True).reshape(C_HEADS, 1, tt)
    _store_heads(cv_ref, _proj_t(wt_ref, 640, 768, xn))
    _store_heads(csg_ref, _silu(_proj_t(wt_ref, 768, 1280, xn)))
    dqn = _head_norm(_proj_t(wt_ref, 1280, 1792, xn), dqg_ref, QK_SCALE)
    dq_ref[0] = dqn.astype(bf16)
    dkn = _head_norm(_proj_t(wt_ref, 1792, 2304, xn), dkg_ref, 1.0)
    _store_keys(dk_ref, dkn, aug)
    dm_ref[0] = jnp.sum(dqn * dkn, axis=1, keepdims=True)
    _store_heads(dsg_ref, _silu(_proj_t(wt_ref, 2816, 3328, xn)))
    _store_heads(dv_ref, _proj_t(wt_ref, 2304, 2816, xn))


def _proj_odd(m1, m2, wo_t, x, ln_g, wt, cqg, ckg, dqg, dkg):
    bsz, seq, _ = x.shape
    tt = PROJ_TOKENS
    half = m1.shape[1]
    col = lambda n: pl.BlockSpec((n, 1), lambda b, t: (0, 0))
    rows = pl.BlockSpec((1, tt, D_MODEL), lambda b, t: (b, t, 0))
    fm = lambda n: pl.BlockSpec((1, n, HEAD_DIM, tt), lambda b, t: (b, 0, 0, t))
    km = lambda n: pl.BlockSpec((1, n, tt, KPAD), lambda b, t: (b, 0, t, 0))
    fm_shape = lambda n: jax.ShapeDtypeStruct((bsz, n, HEAD_DIM, seq), bf16)
    km_shape = lambda n: jax.ShapeDtypeStruct((bsz, n, seq, KPAD), bf16)
    return pl.pallas_call(
        _proj_odd_kernel,
        grid=(bsz, seq // tt),
        in_specs=[
            pl.BlockSpec((1, half, tt), lambda b, t: (b, 0, t)),
            pl.BlockSpec((1, half, tt), lambda b, t: (b, 0, t)),
            pl.BlockSpec(wo_t.shape, lambda b, t: (0, 0)),
            rows,
            pl.BlockSpec((1, D_MODEL), lambda b, t: (0, 0)),
            pl.BlockSpec(wt.shape, lambda b, t: (0, 0)),
            col(HEAD_DIM), col(HEAD_DIM), col(HEAD_DIM), col(HEAD_DIM),
        ],
        out_specs=[rows, fm(C_HEADS), km(C_KV_HEADS), fm(C_KV_HEADS), fm(C_HEADS),
                   fm(D_HEADS), km(D_HEADS), fm(D_HEADS), fm(D_HEADS),
                   pl.BlockSpec((1, C_HEADS, 1, tt), lambda b, t: (b, 0, 0, t)),
                   pl.BlockSpec((1, D_HEADS, 1, tt), lambda b, t: (b, 0, 0, t))],
        out_shape=[jax.ShapeDtypeStruct(x.shape, f32),
                   fm_shape(C_HEADS), km_shape(C_KV_HEADS), fm_shape(C_KV_HEADS), fm_shape(C_HEADS),
                   fm_shape(D_HEADS), km_shape(D_HEADS), fm_shape(D_HEADS), fm_shape(D_HEADS),
                   jax.ShapeDtypeStruct((bsz, C_HEADS, 1, seq), f32),
                   jax.ShapeDtypeStruct((bsz, D_HEADS, 1, seq), f32)],
        compiler_params=pltpu.CompilerParams(
            dimension_semantics=("arbitrary", "arbitrary"), vmem_limit_bytes=VMEM_LIMIT),
        name="proj_odd",
    )(m1, m2, wo_t, x, ln_g, wt, cqg, ckg, dqg, dkg)


def _online_step(carry, s, v):
    m, l, acc = carry
    m_new = jnp.maximum(m, jnp.max(s, axis=0, keepdims=True))
    p = jnp.exp2(s - m_new)
    alpha = jnp.exp2(m - m_new)
    l = alpha * l + jnp.sum(p, axis=0, keepdims=True)
    acc = alpha * acc + jnp.dot(v, p.astype(bf16), preferred_element_type=f32)
    return m_new, l, acc


def _colsum8(p):
    return p.reshape(p.shape[0] // 8, 8, p.shape[1]).sum(axis=0)


def _augment_q(q, aug_col, n_bias, m=None):
    tq = q.shape[1]
    aug = jnp.broadcast_to(aug_col, (KPAD - HEAD_DIM, tq))
    if m is not None:
        row = lax.broadcasted_iota(jnp.int32, aug.shape, 0)
        for r, piece in enumerate(_split3(-m)):
            aug = jnp.where(row == n_bias + r, piece, aug)
    return jnp.concatenate([q, aug.astype(bf16)], axis=0)


def _staggered(n, scores, finish):
    out, pending = [], scores(0)
    for c in range(1, n):
        nxt = scores(c)
        out.append(finish(c - 1, pending))
        pending = nxt
    out.append(finish(n - 1, pending))
    return out


def _diag_full(chains, d0, tq):
    def scores(c):
        k_at, _, q, aug_col, n_bias, diag_bias, _ = chains[c]
        return jnp.dot(k_at(d0, tq), _augment_q(q, aug_col, n_bias),
                       preferred_element_type=f32) + diag_bias

    def finish(c, s):
        m = jnp.max(s, axis=0, keepdims=True)
        p = jnp.exp2(s - m)
        return m, _colsum8(p), jnp.dot(chains[c][1](d0, tq), p.astype(bf16), preferred_element_type=f32)

    return _staggered(len(chains), scores, finish)


def _diag_halves(chains, q_aug, d0, tq):
    h = tq // 2
    d1 = pl.multiple_of(d0 + h, h)

    def scores(c):
        k_at = chains[c][0]
        return (jnp.dot(k_at(d0, h), q_aug[c], preferred_element_type=f32),
                jnp.dot(k_at(d1, h), q_aug[c][:, h:], preferred_element_type=f32))

    def finish(c, s):
        v_at, bias = chains[c][1], chains[c][5][:h, :h]
        p0 = jnp.concatenate([jnp.exp2(s[0][:, :h] + bias), jnp.exp2(s[0][:, h:])], axis=1)
        p1 = jnp.exp2(s[1] + bias)
        l0 = _colsum8(p0)
        a0 = jnp.dot(v_at(d0, h), p0.astype(bf16), preferred_element_type=f32)
        a1 = jnp.dot(v_at(d1, h), p1.astype(bf16), preferred_element_type=f32)
        return (jnp.concatenate([l0[:, :h], l0[:, h:] + _colsum8(p1)], axis=1),
                jnp.concatenate([a0[:, :h], a0[:, h:] + a1], axis=1))

    return _staggered(len(chains), scores, finish)


def _causal_sweep(chains, i, fixed_max, l_s, acc_s):
    n = len(chains)
    tq = chains[0][2].shape[1]
    d0 = pl.multiple_of(i * tq, tq)
    n_tiles = i * (tq // ATT_TK)
    tile = lambda j: (pl.multiple_of(j * ATT_TK, ATT_TK), ATT_TK)

    if fixed_max:
        q_aug = [_augment_q(q, aug_col, n_bias, m) for _, _, q, aug_col, n_bias, _, m in chains]

        for c, (l, acc) in enumerate(_diag_halves(chains, q_aug, d0, tq)):
            l_s[c] = l
            acc_s[c] = acc

        def body(j, carry):
            def finish(c, s):
                p = jnp.exp2(s)
                l_s[c] += _colsum8(p)
                acc_s[c] += jnp.dot(chains[c][1](*tile(j)), p.astype(bf16), preferred_element_type=f32)

            scores = lambda c: jnp.dot(chains[c][0](*tile(j)), q_aug[c], preferred_element_type=f32)
            _staggered(n, scores, finish)
            return carry

        lax.fori_loop(0, n_tiles, body, 0)
        return [(acc_s[c], jnp.sum(l_s[c], axis=0, keepdims=True)) for c in range(n)]

    q_aug = [_augment_q(q, aug_col, n_bias) for _, _, q, aug_col, n_bias, _, _ in chains]

    def body(j, carries):
        scores = lambda c: jnp.dot(chains[c][0](*tile(j)), q_aug[c], preferred_element_type=f32)
        finish = lambda c, s: _online_step(carries[c], s, chains[c][1](*tile(j)))
        return tuple(_staggered(n, scores, finish))

    init = tuple((m, jnp.sum(l, axis=0, keepdims=True), acc) for m, l, acc in _diag_full(chains, d0, tq))
    return [(acc, l) for _, l, acc in lax.fori_loop(0, n_tiles, body, init)]


def _either_sweep(fixed_ref, run):
    @pl.when(fixed_ref[0] != 0)
    def _():
        run(True)

    @pl.when(fixed_ref[0] == 0)
    def _():
        run(False)


def _attn_a_kernel(fixed_ref, q_ref, k_ref, v_ref, sg_ref, m_ref, dtab_ref, qaug_ref, subg_ref,
                   lamv_ref, o_ref, l_s, acc_s, *, lam_init):
    i = pl.program_id(2)
    chains = []
    for h in range(A_HEADS_PER_STEP):
        v_at = lambda start, size, h=h: v_ref[0, h, :, pl.ds(start, size)]
        for c in range(2):
            s = 2 * h + c
            k_at = lambda start, size, s=s: k_ref[0, s, pl.ds(start, size), :]
            chains.append((k_at, v_at, q_ref[0, s], qaug_ref[h], A_BIAS_ROWS, dtab_ref[h],
                           m_ref[0, s]))

    def run(fixed_max):
        lv = lamv_ref[...]
        lam = (jnp.exp(jnp.sum(lv[0:1] * lv[1:2], axis=1, keepdims=True))
               - jnp.exp(jnp.sum(lv[2:3] * lv[3:4], axis=1, keepdims=True)) + lam_init)
        outs = [acc * (1.0 / l) for acc, l in _causal_sweep(chains, i, fixed_max, l_s, acc_s)]
        for h in range(A_HEADS_PER_STEP):
            o = outs[2 * h] - lam * outs[2 * h + 1]
            ms = jnp.mean(o * o, axis=0, keepdims=True)
            y = o * lax.rsqrt(ms + NORM_EPS) * (subg_ref[...] * (1.0 - lam_init))
            o_ref[0, h] = (y * sg_ref[0, h].astype(f32)).astype(bf16)

    _either_sweep(fixed_ref, run)


def _attn_a(fixed, aq, ak, av, asg, am, dtab, qaug, subg, lamv, lam_init):
    bsz, _, _, seq = aq.shape
    nq = seq // ATT_TQ
    hs = A_HEADS_PER_STEP
    return pl.pallas_call(
        functools.partial(_attn_a_kernel, lam_init=lam_init),
        grid=(bsz, A_HEADS // hs, nq),
        in_specs=[
            pl.BlockSpec(memory_space=pltpu.SMEM),
            pl.BlockSpec((1, 2 * hs, HEAD_DIM, ATT_TQ), lambda b, h, i: (b, h, 0, i)),
            pl.BlockSpec((1, 2 * hs, seq, KPAD), lambda b, h, i: (b, h, 0, 0)),
            pl.BlockSpec((1, hs, A_VDIM, seq), lambda b, h, i: (b, h, 0, 0)),
            pl.BlockSpec((1, hs, A_VDIM, ATT_TQ), lambda b, h, i: (b, h, 0, i)),
            pl.BlockSpec((1, 2 * hs, 1, ATT_TQ), lambda b, h, i: (b, h, 0, i)),
            pl.BlockSpec((hs, ATT_TQ, ATT_TQ), lambda b, h, i: (h, 0, 0)),
            pl.BlockSpec((hs, KPAD - HEAD_DIM, 1), lambda b, h, i: (h, 0, 0)),
            pl.BlockSpec((A_VDIM, 1), lambda b, h, i: (0, 0)),
            pl.BlockSpec((4, HEAD_DIM), lambda b, h, i: (0, 0)),
        ],
        out_specs=pl.BlockSpec((1, hs, A_VDIM, ATT_TQ), lambda b, h, i: (b, h, 0, i)),
        out_shape=jax.ShapeDtypeStruct((bsz, A_HEADS, A_VDIM, seq), bf16),
        scratch_shapes=[pltpu.VMEM((2 * hs, 8, ATT_TQ), f32), pltpu.VMEM((2 * hs, A_VDIM, ATT_TQ), f32)],
        compiler_params=pltpu.CompilerParams(
            dimension_semantics=("arbitrary", "arbitrary", "arbitrary"),
            vmem_limit_bytes=VMEM_LIMIT),
        name="attn_a",
    )(fixed, aq, ak, av, asg, am, dtab, qaug, subg, lamv)


def _attn_b_kernel(fixed_ref, q_ref, k_ref, v_ref, sg_ref, m_ref, qaug_ref, o_ref, l_s, acc_s):
    i = pl.program_id(2)

    def run(fixed_max):
        causal = jnp.where(lax.broadcasted_iota(jnp.int32, (ATT_TQ, ATT_TQ), 0)
                           <= lax.broadcasted_iota(jnp.int32, (ATT_TQ, ATT_TQ), 1), 0.0, NEG)
        chains = []
        for h in range(B_HEADS_PER_STEP):
            k_at = lambda start, size, h=h: k_ref[0, h, pl.ds(start, size), :]
            v_at = lambda start, size, h=h: v_ref[0, h, :, pl.ds(start, size)]
            chains.append((k_at, v_at, q_ref[0, h], qaug_ref[...], B_BIAS_ROWS, causal, m_ref[0, h]))
        for h, (acc, l) in enumerate(_causal_sweep(chains, i, fixed_max, l_s, acc_s)):
            o_ref[0, h] = (acc * (1.0 / l) * sg_ref[0, h].astype(f32)).astype(bf16)

    _either_sweep(fixed_ref, run)


def _attn_b(fixed, bq, bk, bv, bsg, bm, qaug):
    bsz, _, _, seq = bq.shape
    nq = seq // ATT_TQ
    hs = B_HEADS_PER_STEP
    return pl.pallas_call(
        _attn_b_kernel,
        grid=(bsz, B_HEADS // hs, nq),
        in_specs=[
            pl.BlockSpec(memory_space=pltpu.SMEM),
            pl.BlockSpec((1, hs, HEAD_DIM, ATT_TQ), lambda b, h, i: (b, h, 0, i)),
            pl.BlockSpec((1, hs, seq, KPAD), lambda b, h, i: (b, h, 0, 0)),
            pl.BlockSpec((1, hs, HEAD_DIM, seq), lambda b, h, i: (b, h, 0, 0)),
            pl.BlockSpec((1, hs, HEAD_DIM, ATT_TQ), lambda b, h, i: (b, h, 0, i)),
            pl.BlockSpec((1, hs, 1, ATT_TQ), lambda b, h, i: (b, h, 0, i)),
            pl.BlockSpec((KPAD - HEAD_DIM, 1), lambda b, h, i: (0, 0)),
        ],
        out_specs=pl.BlockSpec((1, hs, HEAD_DIM, ATT_TQ), lambda b, h, i: (b, h, 0, i)),
        out_shape=jax.ShapeDtypeStruct((bsz, B_HEADS, HEAD_DIM, seq), bf16),
        scratch_shapes=[pltpu.VMEM((hs, 8, ATT_TQ), f32), pltpu.VMEM((hs, HEAD_DIM, ATT_TQ), f32)],
        compiler_params=pltpu.CompilerParams(
            dimension_semantics=("arbitrary", "arbitrary", "arbitrary"),
            vmem_limit_bytes=VMEM_LIMIT),
        name="attn_b",
    )(fixed, bq, bk, bv, bsg, bm, qaug)


def _band_kernel(fixed_ref, q_ref, k_ref, v_ref, sg_ref, m_ref, tab_ref, sink_ref, o_ref,
                 *, group, back, tq):
    blocks = tq // LANES
    band = (back + blocks) * LANES
    tiles = q_ref.shape[3] // tq
    work = [(u, h) for u in range(tiles) for h in range(k_ref.shape[1])]
    grouped = lambda ref, u, h: jnp.concatenate(
        [ref[0, h * group + g, :, u * tq:(u + 1) * tq] for g in range(group)], axis=1)

    def window(u):
        first = (pl.program_id(2) * tiles + u) * blocks
        return (pl.multiple_of(jnp.maximum(first - back, 0) * LANES, LANES),
                pl.multiple_of(jnp.maximum(back - first, 0) * LANES, LANES))

    def run(fixed_max):
        def scores(w):
            u, h = work[w]
            k_start, tab_start = window(u)
            q = _augment_q(grouped(q_ref, u, h), 0.0, 0, grouped(m_ref, u, h) if fixed_max else None)
            s = jnp.dot(k_ref[0, h, pl.ds(k_start, band), :], q, preferred_element_type=f32)
            return s + tab_ref[h, pl.ds(tab_start, band), :]

        def finish(w, s):
            u, h = work[w]
            k_start, _ = window(u)
            if fixed_max:
                m = grouped(m_ref, u, h)
                p = jnp.exp2(s)
            else:
                m = jnp.maximum(jnp.max(s, axis=0, keepdims=True), sink_ref[h])
                p = jnp.exp2(s - m)
            l = jnp.sum(p, axis=0, keepdims=True) + jnp.exp2(sink_ref[h] - m)
            o = jnp.dot(v_ref[0, h, :, pl.ds(k_start, band)], p.astype(bf16),
                        preferred_element_type=f32) * (1.0 / l)
            for g in range(group):
                hq = h * group + g
                gate = sg_ref[0, hq, :, u * tq:(u + 1) * tq].astype(f32)
                o_ref[0, hq, :, u * tq:(u + 1) * tq] = (o[:, g * tq:(g + 1) * tq] * gate).astype(bf16)

        _staggered(len(work), scores, finish)

    _either_sweep(fixed_ref, run)


def _band_attn(fixed, q, k, v, sg, m, tab, sink, group, back, tq, kv_per_step, tiles_per_step, name):
    bsz, nheads, _, seq = q.shape
    hs = kv_per_step
    tile = tq * tiles_per_step
    return pl.pallas_call(
        functools.partial(_band_kernel, group=group, back=back, tq=tq),
        grid=(bsz, nheads // (group * hs), seq // tile),
        in_specs=[
            pl.BlockSpec(memory_space=pltpu.SMEM),
            pl.BlockSpec((1, hs * group, HEAD_DIM, tile), lambda b, h, i: (b, h, 0, i)),
            pl.BlockSpec((1, hs, seq, KPAD), lambda b, h, i: (b, h, 0, 0)),
            pl.BlockSpec((1, hs, HEAD_DIM, seq), lambda b, h, i: (b, h, 0, 0)),
            pl.BlockSpec((1, hs * group, HEAD_DIM, tile), lambda b, h, i: (b, h, 0, i)),
            pl.BlockSpec((1, hs * group, 1, tile), lambda b, h, i: (b, h, 0, i)),
            pl.BlockSpec((hs,) + tab.shape[1:], lambda b, h, i: (h, 0, 0)),
            pl.BlockSpec((hs,) + sink.shape[1:], lambda b, h, i: (h, 0, 0)),
        ],
        out_specs=pl.BlockSpec((1, hs * group, HEAD_DIM, tile), lambda b, h, i: (b, h, 0, i)),
        out_shape=jax.ShapeDtypeStruct((bsz, nheads, HEAD_DIM, seq), bf16),
        compiler_params=pltpu.CompilerParams(
            dimension_semantics=("arbitrary", "arbitrary", "arbitrary"),
            vmem_limit_bytes=VMEM_LIMIT),
        name=name,
    )(fixed, q, k, v, sg, m, tab, sink)


def _out_proj_kernel(m1_ref, m2_ref, wt_ref, x_ref, o_ref):
    o_ref[0] = _residual_add(m1_ref, m2_ref, wt_ref, x_ref)


def _out_proj(m1, m2, wt, x):
    bsz, seq, _ = x.shape
    tt = OUT_TOKENS
    half = m1.shape[1]
    return pl.pallas_call(
        _out_proj_kernel,
        grid=(bsz, seq // tt),
        in_specs=[
            pl.BlockSpec((1, half, tt), lambda b, t: (b, 0, t)),
            pl.BlockSpec((1, half, tt), lambda b, t: (b, 0, t)),
            pl.BlockSpec(wt.shape, lambda b, t: (0, 0)),
            pl.BlockSpec((1, tt, D_MODEL), lambda b, t: (b, t, 0)),
        ],
        out_specs=pl.BlockSpec((1, tt, D_MODEL), lambda b, t: (b, t, 0)),
        out_shape=jax.ShapeDtypeStruct(x.shape, f32),
        compiler_params=pltpu.CompilerParams(
            dimension_semantics=("arbitrary", "arbitrary"), vmem_limit_bytes=VMEM_LIMIT),
        name="out_proj",
    )(m1, m2, wt, x)


def _alibi_slopes(n):
    return 2.0 ** (-8.0 * np.arange(1, n + 1, dtype=np.float64) / n)


def _np_split3(v):
    v = np.asarray(v, np.float32)
    to_bf = lambda a: a.astype(bf16).astype(np.float32)
    hi = to_bf(v)
    mid = to_bf(v - hi)
    lo = to_bf(v - hi - mid)
    return hi, mid, lo


def _a_tables():
    rate = A_RATES
    qaug = np.zeros((A_HEADS, KPAD - HEAD_DIM, 1), np.float32)
    for idx, piece in enumerate(_np_split3(rate * CHUNK) + _np_split3(rate)):
        qaug[:, idx, 0] = piece
    kk = np.arange(ATT_TQ)[:, None]
    qq = np.arange(ATT_TQ)[None, :]
    future = np.maximum(kk - qq, 0).astype(np.float32)
    corr = -2.0 * rate[:, None, None] * future[None]
    allowed = (kk // CHUNK) <= (qq // CHUNK)
    dtab = np.where(allowed[None], corr, NEG).astype(np.float32)
    return jnp.asarray(qaug), jnp.asarray(dtab)


def _band_frames(back, tq):
    k_pos = np.arange(back * LANES + tq)[:, None]
    q_pos = back * LANES + np.arange(tq)[None, :]
    return q_pos - k_pos, q_pos // CHUNK - k_pos // CHUNK


def _c_tables(sinks):
    back, tq = WIN_CHUNKS * CHUNK // LANES, C_BAND_TQ
    rel, chunk_diff = _band_frames(back, tq)
    allowed = (chunk_diff >= 0) & (chunk_diff <= WIN_CHUNKS)
    slopes = _alibi_slopes(C_HEADS)
    per_head = np.where(allowed[None], -slopes[:, None, None] * np.abs(rel)[None] * LOG2E, NEG)
    tab = per_head.reshape(C_KV_HEADS, C_GROUP, *rel.shape).transpose(0, 2, 1, 3)
    tab = tab.reshape(C_KV_HEADS, rel.shape[0], C_GROUP * tq).astype(np.float32)
    tab = np.concatenate([tab, np.full((C_KV_HEADS, back * LANES, tab.shape[2]), NEG, np.float32)], 1)
    sink = jnp.repeat(sinks.astype(f32) * LOG2E, tq).reshape(C_KV_HEADS, 1, C_GROUP * tq)
    return jnp.asarray(tab), sink, back


def _d_tables(rel_table):
    back, t = D_LEFT_CHUNKS * CHUNK // LANES, D_BAND_TQ
    band = back * LANES + t
    rel, chunk_diff = _band_frames(back, t)
    allowed = (chunk_diff >= 0) & (chunk_diff <= D_LEFT_CHUNKS)
    tbl = rel_table.astype(f32) * LOG2E
    n_lo = (t - 1) - (CHUNK - 1)
    n_hi = (band - 1) - REL_MAX
    diag = jnp.concatenate([jnp.broadcast_to(tbl[:, :1], (D_HEADS, n_lo)), tbl,
                            jnp.broadcast_to(tbl[:, -1:], (D_HEADS, n_hi))], axis=1)
    m = t + LANES - 1
    blocks = []
    for kb in range(band // LANES):
        lo = rel[kb * LANES:(kb + 1) * LANES].min()
        if lo >= REL_MAX:
            blocks.append(jnp.broadcast_to(tbl[:, -1:, None], (D_HEADS, LANES, t)))
            continue
        window = diag[:, band - (kb + 1) * LANES:band - (kb + 1) * LANES + m]
        skew = jnp.broadcast_to(window[:, None, :], (D_HEADS, LANES + 1, m)).reshape(D_HEADS, -1)
        skew = skew[:, :LANES * (m + 1)].reshape(D_HEADS, LANES, m + 1)[:, :, :t]
        blocks.append(jnp.flip(skew, axis=1))
    blocks += [jnp.zeros((D_HEADS, LANES, t), f32)] * back
    allowed = np.concatenate([allowed, np.zeros((back * LANES, t), bool)], axis=0)
    tab = jnp.where(jnp.asarray(allowed)[None], jnp.concatenate(blocks, axis=1), NEG)
    sink = jnp.full((D_HEADS, 1, t), NEG, f32)
    return tab, sink, back


def _fixed_max_ok(q_gain, k_gain, bias_range=0.0):
    spread = (2.0 * 1.02 * QK_SCALE * HEAD_DIM
              * jnp.max(jnp.abs(q_gain.astype(f32))) * jnp.max(jnp.abs(k_gain.astype(f32))))
    return (spread + bias_range <= FIXED_MAX_LIMIT).astype(jnp.int32).reshape(1)


def _pad_rows(w_t, rows):
    return jnp.pad(w_t, ((0, rows - w_t.shape[0]), (0, 0)))


def _even_layer(x, ln_g, w_in, w_out, a_qn_g, a_kn_g, a_lq1, a_lk1, a_lq2, a_lk2, a_subln_g,
                b_qn_g, b_kn_g, b_f_bias, layer_idx):
    bsz, seq, _ = x.shape
    colv = lambda v: v.astype(f32).reshape(-1, 1)
    n_wide = P_EVEN - B_HEADS
    wt = w_in[:, :n_wide].T.astype(bf16)
    wf = _pad_rows(w_in[:, n_wide:].T.astype(bf16), BF16_ROWS)
    aq, ak, av, asg, bq, bk, bv, bsg, am, bm = _proj_even(
        x, ln_g.astype(f32).reshape(1, -1), wt, wf, colv(a_qn_g), colv(a_kn_g), colv(b_qn_g),
        colv(b_kn_g), colv(b_f_bias))
    lam_init = 0.8 - 0.6 * math.exp(-0.3 * layer_idx)
    qaug_a, dtab = _a_tables()
    lamv = jnp.stack([a_lq1, a_lk1, a_lq2, a_lk2]).astype(f32)
    mix_a = _attn_a(_fixed_max_ok(a_qn_g, a_kn_g), aq, ak, av, asg, am, dtab, qaug_a,
                    colv(a_subln_g), lamv, lam_init)
    qaug_b = np.zeros((KPAD - HEAD_DIM, 1), np.float32)
    qaug_b[:B_BIAS_ROWS] = 1.0
    mix_b = _attn_b(_fixed_max_ok(b_qn_g, b_kn_g), bq, bk, bv, bsg, bm, jnp.asarray(qaug_b))
    return mix_a.reshape(bsz, -1, seq), mix_b.reshape(bsz, -1, seq), w_out.T.astype(bf16), x


def _odd_layer(pending, ln_g, w_in, w_out, c_qn_g, c_kn_g, c_sinks, d_qn_g, d_kn_g, d_rel_bias):
    bsz, seq, _ = pending[3].shape
    colv = lambda v: v.astype(f32).reshape(-1, 1)
    x, cq, ck, cv, csg, dq, dk, dv, dsg, cm, dm = _proj_odd(
        *pending, ln_g.astype(f32).reshape(1, -1), w_in.T.astype(bf16), colv(c_qn_g), colv(c_kn_g),
        colv(d_qn_g), colv(d_kn_g))
    tab_c, sink_c, back_c = _c_tables(c_sinks)
    fixed_c = _fixed_max_ok(c_qn_g, c_kn_g, LOG2E * jnp.maximum(jnp.max(c_sinks.astype(f32)), 0.0))
    mix_c = _band_attn(fixed_c, cq, ck, cv, csg, cm, tab_c, sink_c, C_GROUP, back_c, C_BAND_TQ,
                       C_KV_HEADS, C_TILES_PER_STEP, "attn_c")
    tab_d, sink_d, back_d = _d_tables(d_rel_bias)
    fixed_d = _fixed_max_ok(d_qn_g, d_kn_g, LOG2E * jnp.max(jnp.abs(d_rel_bias.astype(f32))))
    mix_d = _band_attn(fixed_d, dq, dk, dv, dsg, dm, tab_d, sink_d, 1, back_d, D_BAND_TQ,
                       D_HEADS_PER_STEP, D_TILES_PER_STEP, "attn_d")
    return mix_c.reshape(bsz, -1, seq), mix_d.reshape(bsz, -1, seq), w_out.T.astype(bf16), x


def kernel(x, even_ln_g, even_w_in, even_w_out, a_q_norm_g, a_k_norm_g, a_lambda_q1, a_lambda_k1, a_lambda_q2, a_lambda_k2, a_subln_g, b_q_norm_g, b_k_norm_g, b_forget_bias, odd_ln_g, odd_w_in, odd_w_out, c_q_norm_g, c_k_norm_g, c_sinks, d_q_norm_g, d_k_norm_g, d_rel_bias):
    depth = even_ln_g.shape[0] + odd_ln_g.shape[0]
    seq = x.shape[1]
    assert x.shape[2] == D_MODEL and even_w_in.shape[2] == P_EVEN and odd_w_in.shape[2] == P_ODD
    for tile in (PROJ_TOKENS, OUT_TOKENS, ATT_TQ, C_BAND_TQ * C_TILES_PER_STEP,
                 D_BAND_TQ * D_TILES_PER_STEP):
        assert seq % tile == 0, (seq, tile)
    pending = None
    for i in range(depth):
        j = i // 2
        if i % 2 == 0:
            if pending is not None:
                x = _out_proj(*pending)
            pending = _even_layer(x, even_ln_g[j], even_w_in[j], even_w_out[j], a_q_norm_g[j],
                                  a_k_norm_g[j], a_lambda_q1[j], a_lambda_k1[j], a_lambda_q2[j],
                                  a_lambda_k2[j], a_subln_g[j], b_q_norm_g[j], b_k_norm_g[j],
                                  b_forget_bias[j], i)
        else:
            pending = _odd_layer(pending, odd_ln_g[j], odd_w_in[j], odd_w_out[j], c_q_norm_g[j],
                                 c_k_norm_g[j], c_sinks[j], d_q_norm_g[j], d_k_norm_g[j],
                                 d_rel_bias[j])
    return _out_proj(*pending)
```

```python
import functools
import math

import numpy as np
import jax
import jax.numpy as jnp
from jax import lax
from jax.experimental import pallas as pl
from jax.experimental.pallas import tpu as pltpu

D_MODEL = 1024
CHUNK = 64
HEAD_DIM = 64
NORM_EPS = 1e-6

A_HEADS = 4
A_STREAMS = 2 * A_HEADS
A_VDIM = 2 * HEAD_DIM
B_HEADS = 8
C_HEADS = 8
C_KV_HEADS = 2
C_GROUP = C_HEADS // C_KV_HEADS
WIN_CHUNKS = 2
D_HEADS = 8
D_LEFT_CHUNKS = 8
REL_MAX = 256


def _row_ranges(names, sizes):
    stops = np.cumsum(sizes)
    return {n: (int(b - w), int(b)) for n, w, b in zip(names, sizes, stops)}


EVEN = _row_ranges(("aq", "ak", "av", "ag", "bq", "bk", "bv", "bg", "bf"),
                   (A_STREAMS * HEAD_DIM,) * 2 + (A_HEADS * A_VDIM,) * 2 + (B_HEADS * HEAD_DIM,) * 4
                   + (B_HEADS,))
ODD = _row_ranges(("cq", "ck", "cv", "cg", "dq", "dk", "dv", "dg"),
                  (C_HEADS * HEAD_DIM, C_KV_HEADS * HEAD_DIM, C_KV_HEADS * HEAD_DIM, C_HEADS * HEAD_DIM)
                  + (D_HEADS * HEAD_DIM,) * 4)
P_EVEN = EVEN["bf"][1]
P_ODD = ODD["dg"][1]

LOG2E = 1.4426950408889634
QK_SCALE = HEAD_DIM ** -0.5 * LOG2E
NEG = -1e30
A_RATES = (2.0 ** (-8.0 * np.arange(1, A_HEADS + 1) / A_HEADS) * LOG2E).astype(np.float32)

LANES = 128
KPAD = 128
BF16_ROWS = 16

PROJ_TOKENS = 512
OUT_TOKENS = 1024
ATT_TQ = 512
ATT_TK = 512
A_HEADS_PER_STEP = 4
B_HEADS_PER_STEP = 8
A_BIAS_ROWS = 6
B_BIAS_ROWS = 3
MAX_ROWS = 3
FIXED_MAX_LIMIT = 96.0
C_BAND_TQ = 128
D_BAND_TQ = 256
D_HEADS_PER_STEP = 4
C_TILES_PER_STEP = 4
D_TILES_PER_STEP = 4
VMEM_LIMIT = 56 * 1024 * 1024

f32 = jnp.float32
bf16 = jnp.bfloat16


def _split3(v):
    hi = v.astype(bf16).astype(f32)
    r = v - hi
    mid = r.astype(bf16).astype(f32)
    lo = (r - mid).astype(bf16).astype(f32)
    return hi, mid, lo


def _silu(z):
    return z * (1.0 / (1.0 + jnp.exp(-z)))


def _rms_rows(x, g_ref):
    ms = jnp.mean(x * x, axis=-1, keepdims=True)
    return (x * lax.rsqrt(ms + NORM_EPS) * g_ref[...]).astype(bf16)


def _residual_add(m1_ref, m2_ref, wt_ref, x_ref):
    half = m1_ref.shape[1]
    y_t = (jnp.dot(wt_ref[:, :half], m1_ref[0], preferred_element_type=f32)
           + jnp.dot(wt_ref[:, half:], m2_ref[0], preferred_element_type=f32))
    return x_ref[0] + y_t.T


def _proj_t(wt_ref, rows, xn):
    return lax.dot_general(wt_ref[rows[0]:rows[1], :], xn, (((1,), (1,)), ((), ())),
                           preferred_element_type=f32)


def _head_norm(z_t, gain_col, mult):
    n = z_t.shape[0] // HEAD_DIM
    z3 = z_t.reshape(n, HEAD_DIM, z_t.shape[1])
    ms = jnp.mean(z3 * z3, axis=1, keepdims=True)
    return z3 * lax.rsqrt(ms + NORM_EPS) * (gain_col[...] * mult)[None]


def _ones_rows(row, first):
    return jnp.where((row >= first) & (row < first + MAX_ROWS), 1.0, 0.0)


def _store_heads(o_ref, z_t):
    o_ref[0] = z_t.reshape(o_ref.shape[1], o_ref.shape[2], z_t.shape[1]).astype(bf16)


def _store_keys(k_ref, kn, aug_fn):
    n, _, t = kn.shape
    zeros = jnp.zeros((KPAD - HEAD_DIM - BF16_ROWS, t), f32)
    for s in range(n):
        blk = jnp.concatenate([kn[s], aug_fn(s), zeros], axis=0)
        k_ref[0, s] = blk.T.astype(bf16)


def _proj_even_kernel(x_ref, lng_ref, wt_ref, wf_ref, aqg_ref, akg_ref, bqg_ref, bkg_ref, bfb_ref,
                      tri_ref, aq_ref, ak_ref, av_ref, asg_ref, bq_ref, bk_ref, bv_ref, bsg_ref,
                      am_ref, bm_ref, cum_ref):
    t = pl.program_id(1)
    tt = x_ref.shape[1]
    xn = _rms_rows(x_ref[0], lng_ref)
    row = lax.broadcasted_iota(jnp.int32, (BF16_ROWS, tt), 0)

    z = _proj_t(wf_ref, (0, BF16_ROWS), xn)[:B_HEADS] + bfb_ref[...]
    aqn = _head_norm(_proj_t(wt_ref, EVEN["aq"], xn), aqg_ref, QK_SCALE)
    aq_ref[0] = aqn.astype(bf16)

    pos = t * tt + lax.broadcasted_iota(jnp.int32, (BF16_ROWS, tt), 1)
    pos_a = lax.shift_right_logical(pos, int(math.log2(CHUNK))).astype(f32)
    pos_b = lax.bitwise_and(pos, CHUNK - 1).astype(f32)
    aug_a = jnp.where(row < 3, pos_a, jnp.where(row < 6, pos_b, _ones_rows(row, A_BIAS_ROWS)))
    akn = _head_norm(_proj_t(wt_ref, EVEN["ak"], xn), akg_ref, 1.0)
    _store_keys(ak_ref, akn, lambda s: aug_a)
    self_a = jnp.sum(aqn * akn, axis=1, keepdims=True)
    for s in range(A_STREAMS):
        am_ref[0, s] = self_a[s] + float(A_RATES[s // 2]) * pos[:1].astype(f32)

    _store_heads(asg_ref, _silu(_proj_t(wt_ref, EVEN["ag"], xn)))
    bqn = _head_norm(_proj_t(wt_ref, EVEN["bq"], xn), bqg_ref, QK_SCALE)
    bq_ref[0] = bqn.astype(bf16)

    log_f = jnp.minimum(z, 0.0) - jnp.log(1.0 + jnp.exp(-jnp.abs(z)))
    pieces = jnp.concatenate(_split3(log_f) + (jnp.zeros_like(log_f),), axis=0).astype(bf16)
    part = jnp.dot(pieces, tri_ref[...], preferred_element_type=f32)
    local = part[:B_HEADS] + part[B_HEADS:2 * B_HEADS] + part[2 * B_HEADS:3 * B_HEADS]

    @pl.when(t == 0)
    def _():
        cum_ref[...] = jnp.zeros_like(cum_ref)

    cum = cum_ref[...] + local
    cum_ref[...] = cum[:, tt - 1:tt]
    gate = -LOG2E * cum
    g_hi, g_mid, g_lo = _split3(gate)

    def aug_b(s):
        pick = lambda a: jnp.broadcast_to(a[s:s + 1], (BF16_ROWS, tt))
        return jnp.where(row == 0, pick(g_hi),
                         jnp.where(row == 1, pick(g_mid),
                                   jnp.where(row == 2, pick(g_lo), _ones_rows(row, B_BIAS_ROWS))))

    bkn = _head_norm(_proj_t(wt_ref, EVEN["bk"], xn), bkg_ref, 1.0)
    _store_keys(bk_ref, bkn, aug_b)
    self_b = jnp.sum(bqn * bkn, axis=1, keepdims=True)
    for s in range(B_HEADS):
        bm_ref[0, s] = self_b[s] + gate[s:s + 1]
    _store_heads(bsg_ref, _silu(_proj_t(wt_ref, EVEN["bg"], xn)))
    _store_heads(av_ref, _proj_t(wt_ref, EVEN["av"], xn))
    _store_heads(bv_ref, _proj_t(wt_ref, EVEN["bv"], xn))


def _proj_even(x, ln_g, wt, wf, aqg, akg, bqg, bkg, bfb):
    bsz, seq, _ = x.shape
    tt = PROJ_TOKENS
    col = lambda n: pl.BlockSpec((n, 1), lambda b, t: (0, 0))
    fm = lambda n, d: pl.BlockSpec((1, n, d, tt), lambda b, t: (b, 0, 0, t))
    km = lambda n: pl.BlockSpec((1, n, tt, KPAD), lambda b, t: (b, 0, t, 0))
    fm_shape = lambda n, d: jax.ShapeDtypeStruct((bsz, n, d, seq), bf16)
    km_shape = lambda n: jax.ShapeDtypeStruct((bsz, n, seq, KPAD), bf16)
    return pl.pallas_call(
        _proj_even_kernel,
        grid=(bsz, seq // tt),
        in_specs=[
            pl.BlockSpec((1, tt, D_MODEL), lambda b, t: (b, t, 0)),
            pl.BlockSpec((1, D_MODEL), lambda b, t: (0, 0)),
            pl.BlockSpec(wt.shape, lambda b, t: (0, 0)),
            pl.BlockSpec(wf.shape, lambda b, t: (0, 0)),
            col(HEAD_DIM), col(HEAD_DIM), col(HEAD_DIM), col(HEAD_DIM), col(B_HEADS),
            pl.BlockSpec((tt, tt), lambda b, t: (0, 0)),
        ],
        out_specs=[fm(A_STREAMS, HEAD_DIM), km(A_STREAMS), fm(A_HEADS, A_VDIM), fm(A_HEADS, A_VDIM),
                   fm(B_HEADS, HEAD_DIM), km(B_HEADS), fm(B_HEADS, HEAD_DIM), fm(B_HEADS, HEAD_DIM),
                   fm(A_STREAMS, 1), fm(B_HEADS, 1)],
        out_shape=[fm_shape(A_STREAMS, HEAD_DIM), km_shape(A_STREAMS), fm_shape(A_HEADS, A_VDIM),
                   fm_shape(A_HEADS, A_VDIM), fm_shape(B_HEADS, HEAD_DIM), km_shape(B_HEADS),
                   fm_shape(B_HEADS, HEAD_DIM), fm_shape(B_HEADS, HEAD_DIM),
                   jax.ShapeDtypeStruct((bsz, A_STREAMS, 1, seq), f32),
                   jax.ShapeDtypeStruct((bsz, B_HEADS, 1, seq), f32)],
        scratch_shapes=[pltpu.VMEM((B_HEADS, 1), f32)],
        compiler_params=pltpu.CompilerParams(
            dimension_semantics=("arbitrary", "arbitrary"), vmem_limit_bytes=VMEM_LIMIT),
        name="proj_even",
    )(x, ln_g, wt, wf, aqg, akg, bqg, bkg, bfb,
      jnp.asarray(np.triu(np.ones((tt, tt), np.float32)), bf16))


def _proj_odd_kernel(m1_ref, m2_ref, wo_ref, x_ref, lng_ref, wt_ref, cqg_ref, ckg_ref, dqg_ref,
                     dkg_ref, x1_ref, cq_ref, ck_ref, cv_ref, csg_ref, dq_ref, dk_ref, dv_ref,
                     dsg_ref, cm_ref, dm_ref):
    tt = x_ref.shape[1]
    x1 = _residual_add(m1_ref, m2_ref, wo_ref, x_ref)
    x1_ref[0] = x1
    xn = _rms_rows(x1, lng_ref)
    ones = _ones_rows(lax.broadcasted_iota(jnp.int32, (BF16_ROWS, tt), 0), 0)
    aug = lambda s: ones

    cqn = _head_norm(_proj_t(wt_ref, ODD["cq"], xn), cqg_ref, QK_SCALE)
    cq_ref[0] = cqn.astype(bf16)
    ckn = _head_norm(_proj_t(wt_ref, ODD["ck"], xn), ckg_ref, 1.0)
    _store_keys(ck_ref, ckn, aug)
    cm_ref[0] = jnp.sum(cqn.reshape(C_KV_HEADS, C_GROUP, HEAD_DIM, tt) * ckn[:, None], axis=2,
                        keepdims=True).reshape(C_HEADS, 1, tt)
    _store_heads(cv_ref, _proj_t(wt_ref, ODD["cv"], xn))
    _store_heads(csg_ref, _silu(_proj_t(wt_ref, ODD["cg"], xn)))
    dqn = _head_norm(_proj_t(wt_ref, ODD["dq"], xn), dqg_ref, QK_SCALE)
    dq_ref[0] = dqn.astype(bf16)
    dkn = _head_norm(_proj_t(wt_ref, ODD["dk"], xn), dkg_ref, 1.0)
    _store_keys(dk_ref, dkn, aug)
    dm_ref[0] = jnp.sum(dqn * dkn, axis=1, keepdims=True)
    _store_heads(dsg_ref, _silu(_proj_t(wt_ref, ODD["dg"], xn)))
    _store_heads(dv_ref, _proj_t(wt_ref, ODD["dv"], xn))


def _proj_odd(m1, m2, wo_t, x, ln_g, wt, cqg, ckg, dqg, dkg):
    bsz, seq, _ = x.shape
    tt = PROJ_TOKENS
    half = m1.shape[1]
    col = lambda n: pl.BlockSpec((n, 1), lambda b, t: (0, 0))
    rows = pl.BlockSpec((1, tt, D_MODEL), lambda b, t: (b, t, 0))
    fm = lambda n: pl.BlockSpec((1, n, HEAD_DIM, tt), lambda b, t: (b, 0, 0, t))
    km = lambda n: pl.BlockSpec((1, n, tt, KPAD), lambda b, t: (b, 0, t, 0))
    fm_shape = lambda n: jax.ShapeDtypeStruct((bsz, n, HEAD_DIM, seq), bf16)
    km_shape = lambda n: jax.ShapeDtypeStruct((bsz, n, seq, KPAD), bf16)
    return pl.pallas_call(
        _proj_odd_kernel,
        grid=(bsz, seq // tt),
        in_specs=[
            pl.BlockSpec((1, half, tt), lambda b, t: (b, 0, t)),
            pl.BlockSpec((1, half, tt), lambda b, t: (b, 0, t)),
            pl.BlockSpec(wo_t.shape, lambda b, t: (0, 0)),
            rows,
            pl.BlockSpec((1, D_MODEL), lambda b, t: (0, 0)),
            pl.BlockSpec(wt.shape, lambda b, t: (0, 0)),
            col(HEAD_DIM), col(HEAD_DIM), col(HEAD_DIM), col(HEAD_DIM),
        ],
        out_specs=[rows, fm(C_HEADS), km(C_KV_HEADS), fm(C_KV_HEADS), fm(C_HEADS),
                   fm(D_HEADS), km(D_HEADS), fm(D_HEADS), fm(D_HEADS),
                   pl.BlockSpec((1, C_HEADS, 1, tt), lambda b, t: (b, 0, 0, t)),
                   pl.BlockSpec((1, D_HEADS, 1, tt), lambda b, t: (b, 0, 0, t))],
        out_shape=[jax.ShapeDtypeStruct(x.shape, f32),
                   fm_shape(C_HEADS), km_shape(C_KV_HEADS), fm_shape(C_KV_HEADS), fm_shape(C_HEADS),
                   fm_shape(D_HEADS), km_shape(D_HEADS), fm_shape(D_HEADS), fm_shape(D_HEADS),
                   jax.ShapeDtypeStruct((bsz, C_HEADS, 1, seq), f32),
                   jax.ShapeDtypeStruct((bsz, D_HEADS, 1, seq), f32)],
        compiler_params=pltpu.CompilerParams(
            dimension_semantics=("arbitrary", "arbitrary"), vmem_limit_bytes=VMEM_LIMIT),
        name="proj_odd",
    )(m1, m2, wo_t, x, ln_g, wt, cqg, ckg, dqg, dkg)


def _online_step(carry, s, v):
    m, l, acc = carry
    m_new = jnp.maximum(m, jnp.max(s, axis=0, keepdims=True))
    p = jnp.exp2(s - m_new)
    alpha = jnp.exp2(m - m_new)
    l = alpha * l + jnp.sum(p, axis=0, keepdims=True)
    acc = alpha * acc + jnp.dot(v, p.astype(bf16), preferred_element_type=f32)
    return m_new, l, acc


def _colsum8(p):
    return p.reshape(p.shape[0] // 8, 8, p.shape[1]).sum(axis=0)


def _augment_q(q, aug_col, n_bias, m=None):
    tq = q.shape[1]
    aug = jnp.broadcast_to(aug_col, (KPAD - HEAD_DIM, tq))
    if m is not None:
        row = lax.broadcasted_iota(jnp.int32, aug.shape, 0)
        pieces = _split3(-m)
        for r in range(MAX_ROWS):
            aug = jnp.where(row == n_bias + r, pieces[r], aug)
    return jnp.concatenate([q, aug.astype(bf16)], axis=0)


def _staggered(n, scores, finish):
    out, pending = [], scores(0)
    for c in range(1, n):
        nxt = scores(c)
        out.append(finish(c - 1, pending))
        pending = nxt
    out.append(finish(n - 1, pending))
    return out


def _diag_full(chains, d0, tq):
    def scores(c):
        k_at, _, q, aug_col, n_bias, diag_bias, _ = chains[c]
        return jnp.dot(k_at(d0, tq), _augment_q(q, aug_col, n_bias),
                       preferred_element_type=f32) + diag_bias

    def finish(c, s):
        m = jnp.max(s, axis=0, keepdims=True)
        p = jnp.exp2(s - m)
        return m, _colsum8(p), jnp.dot(chains[c][1](d0, tq), p.astype(bf16), preferred_element_type=f32)

    return _staggered(len(chains), scores, finish)


def _diag_halves(chains, q_aug, d0, tq):
    h = tq // 2
    d1 = pl.multiple_of(d0 + h, h)

    def scores(c):
        k_at = chains[c][0]
        return (jnp.dot(k_at(d0, h), q_aug[c], preferred_element_type=f32),
                jnp.dot(k_at(d1, h), q_aug[c][:, h:], preferred_element_type=f32))

    def finish(c, s):
        v_at, bias = chains[c][1], chains[c][5][:h, :h]
        p0 = jnp.concatenate([jnp.exp2(s[0][:, :h] + bias), jnp.exp2(s[0][:, h:])], axis=1)
        p1 = jnp.exp2(s[1] + bias)
        l0 = _colsum8(p0)
        a0 = jnp.dot(v_at(d0, h), p0.astype(bf16), preferred_element_type=f32)
        a1 = jnp.dot(v_at(d1, h), p1.astype(bf16), preferred_element_type=f32)
        return (jnp.concatenate([l0[:, :h], l0[:, h:] + _colsum8(p1)], axis=1),
                jnp.concatenate([a0[:, :h], a0[:, h:] + a1], axis=1))

    return _staggered(len(chains), scores, finish)


def _causal_sweep(chains, i, fixed_max, l_s, acc_s):
    n = len(chains)
    tq = chains[0][2].shape[1]
    d0 = pl.multiple_of(i * tq, tq)
    n_tiles = i * (tq // ATT_TK)
    tile = lambda j: (pl.multiple_of(j * ATT_TK, ATT_TK), ATT_TK)

    if fixed_max:
        q_aug = [_augment_q(q, aug_col, n_bias, m) for _, _, q, aug_col, n_bias, _, m in chains]

        for c, (l, acc) in enumerate(_diag_halves(chains, q_aug, d0, tq)):
            l_s[c] = l
            acc_s[c] = acc

        def body(j, carry):
            def finish(c, s):
                p = jnp.exp2(s)
                l_s[c] += _colsum8(p)
                acc_s[c] += jnp.dot(chains[c][1](*tile(j)), p.astype(bf16), preferred_element_type=f32)

            scores = lambda c: jnp.dot(chains[c][0](*tile(j)), q_aug[c], preferred_element_type=f32)
            _staggered(n, scores, finish)
            return carry

        lax.fori_loop(0, n_tiles, body, 0)
        return [(acc_s[c], jnp.sum(l_s[c], axis=0, keepdims=True)) for c in range(n)]

    q_aug = [_augment_q(q, aug_col, n_bias) for _, _, q, aug_col, n_bias, _, _ in chains]

    def body(j, carries):
        scores = lambda c: jnp.dot(chains[c][0](*tile(j)), q_aug[c], preferred_element_type=f32)
        finish = lambda c, s: _online_step(carries[c], s, chains[c][1](*tile(j)))
        return tuple(_staggered(n, scores, finish))

    init = tuple((m, jnp.sum(l, axis=0, keepdims=True), acc) for m, l, acc in _diag_full(chains, d0, tq))
    return [(acc, l) for _, l, acc in lax.fori_loop(0, n_tiles, body, init)]


def _either_sweep(fixed_ref, run):
    @pl.when(fixed_ref[0] != 0)
    def _():
        run(True)

    @pl.when(fixed_ref[0] == 0)
    def _():
        run(False)


def _attn_a_kernel(fixed_ref, q_ref, k_ref, v_ref, sg_ref, m_ref, dtab_ref, qaug_ref, subg_ref,
                   lamv_ref, o_ref, l_s, acc_s, *, lam_init):
    i = pl.program_id(2)
    chains = []
    for h in range(A_HEADS_PER_STEP):
        v_at = lambda start, size, h=h: v_ref[0, h, :, pl.ds(start, size)]
        for c in range(2):
            s = 2 * h + c
            k_at = lambda start, size, s=s: k_ref[0, s, pl.ds(start, size), :]
            chains.append((k_at, v_at, q_ref[0, s], qaug_ref[h], A_BIAS_ROWS, dtab_ref[h],
                           m_ref[0, s]))

    def run(fixed_max):
        lv = lamv_ref[...]
        lam = (jnp.exp(jnp.sum(lv[0:1] * lv[1:2], axis=1, keepdims=True))
               - jnp.exp(jnp.sum(lv[2:3] * lv[3:4], axis=1, keepdims=True)) + lam_init)
        outs = [acc * (1.0 / l) for acc, l in _causal_sweep(chains, i, fixed_max, l_s, acc_s)]
        for h in range(A_HEADS_PER_STEP):
            o = outs[2 * h] - lam * outs[2 * h + 1]
            ms = jnp.mean(o * o, axis=0, keepdims=True)
            y = o * lax.rsqrt(ms + NORM_EPS) * (subg_ref[...] * (1.0 - lam_init))
            o_ref[0, h] = (y * sg_ref[0, h].astype(f32)).astype(bf16)

    _either_sweep(fixed_ref, run)


def _attn_a(fixed, aq, ak, av, asg, am, dtab, qaug, subg, lamv, lam_init):
    bsz, _, _, seq = aq.shape
    nq = seq // ATT_TQ
    hs = A_HEADS_PER_STEP
    return pl.pallas_call(
        functools.partial(_attn_a_kernel, lam_init=lam_init),
        grid=(bsz, A_HEADS // hs, nq),
        in_specs=[
            pl.BlockSpec(memory_space=pltpu.SMEM),
            pl.BlockSpec((1, 2 * hs, HEAD_DIM, ATT_TQ), lambda b, h, i: (b, h, 0, i)),
            pl.BlockSpec((1, 2 * hs, seq, KPAD), lambda b, h, i: (b, h, 0, 0)),
            pl.BlockSpec((1, hs, A_VDIM, seq), lambda b, h, i: (b, h, 0, 0)),
            pl.BlockSpec((1, hs, A_VDIM, ATT_TQ), lambda b, h, i: (b, h, 0, i)),
            pl.BlockSpec((1, 2 * hs, 1, ATT_TQ), lambda b, h, i: (b, h, 0, i)),
            pl.BlockSpec((hs, ATT_TQ, ATT_TQ), lambda b, h, i: (h, 0, 0)),
            pl.BlockSpec((hs, KPAD - HEAD_DIM, 1), lambda b, h, i: (h, 0, 0)),
            pl.BlockSpec((A_VDIM, 1), lambda b, h, i: (0, 0)),
            pl.BlockSpec((4, HEAD_DIM), lambda b, h, i: (0, 0)),
        ],
        out_specs=pl.BlockSpec((1, hs, A_VDIM, ATT_TQ), lambda b, h, i: (b, h, 0, i)),
        out_shape=jax.ShapeDtypeStruct((bsz, A_HEADS, A_VDIM, seq), bf16),
        scratch_shapes=[pltpu.VMEM((2 * hs, 8, ATT_TQ), f32), pltpu.VMEM((2 * hs, A_VDIM, ATT_TQ), f32)],
        compiler_params=pltpu.CompilerParams(
            dimension_semantics=("arbitrary", "arbitrary", "arbitrary"),
            vmem_limit_bytes=VMEM_LIMIT),
        name="attn_a",
    )(fixed, aq, ak, av, asg, am, dtab, qaug, subg, lamv)


def _attn_b_kernel(fixed_ref, q_ref, k_ref, v_ref, sg_ref, m_ref, qaug_ref, o_ref, l_s, acc_s):
    i = pl.program_id(2)

    def run(fixed_max):
        causal = jnp.where(lax.broadcasted_iota(jnp.int32, (ATT_TQ, ATT_TQ), 0)
                           <= lax.broadcasted_iota(jnp.int32, (ATT_TQ, ATT_TQ), 1), 0.0, NEG)
        chains = []
        for h in range(B_HEADS_PER_STEP):
            k_at = lambda start, size, h=h: k_ref[0, h, pl.ds(start, size), :]
            v_at = lambda start, size, h=h: v_ref[0, h, :, pl.ds(start, size)]
            chains.append((k_at, v_at, q_ref[0, h], qaug_ref[...], B_BIAS_ROWS, causal, m_ref[0, h]))
        for h, (acc, l) in enumerate(_causal_sweep(chains, i, fixed_max, l_s, acc_s)):
            o_ref[0, h] = (acc * (1.0 / l) * sg_ref[0, h].astype(f32)).astype(bf16)

    _either_sweep(fixed_ref, run)


def _attn_b(fixed, bq, bk, bv, bsg, bm, qaug):
    bsz, _, _, seq = bq.shape
    nq = seq // ATT_TQ
    hs = B_HEADS_PER_STEP
    return pl.pallas_call(
        _attn_b_kernel,
        grid=(bsz, B_HEADS // hs, nq),
        in_specs=[
            pl.BlockSpec(memory_space=pltpu.SMEM),
            pl.BlockSpec((1, hs, HEAD_DIM, ATT_TQ), lambda b, h, i: (b, h, 0, i)),
            pl.BlockSpec((1, hs, seq, KPAD), lambda b, h, i: (b, h, 0, 0)),
            pl.BlockSpec((1, hs, HEAD_DIM, seq), lambda b, h, i: (b, h, 0, 0)),
            pl.BlockSpec((1, hs, HEAD_DIM, ATT_TQ), lambda b, h, i: (b, h, 0, i)),
            pl.BlockSpec((1, hs, 1, ATT_TQ), lambda b, h, i: (b, h, 0, i)),
            pl.BlockSpec((KPAD - HEAD_DIM, 1), lambda b, h, i: (0, 0)),
        ],
        out_specs=pl.BlockSpec((1, hs, HEAD_DIM, ATT_TQ), lambda b, h, i: (b, h, 0, i)),
        out_shape=jax.ShapeDtypeStruct((bsz, B_HEADS, HEAD_DIM, seq), bf16),
        scratch_shapes=[pltpu.VMEM((hs, 8, ATT_TQ), f32), pltpu.VMEM((hs, HEAD_DIM, ATT_TQ), f32)],
        compiler_params=pltpu.CompilerParams(
            dimension_semantics=("arbitrary", "arbitrary", "arbitrary"),
            vmem_limit_bytes=VMEM_LIMIT),
        name="attn_b",
    )(fixed, bq, bk, bv, bsg, bm, qaug)


def _band_kernel(fixed_ref, q_ref, k_ref, v_ref, sg_ref, m_ref, tab_ref, sink_ref, o_ref,
                 *, group, back, tq):
    blocks = tq // LANES
    band = (back + blocks) * LANES
    tiles = q_ref.shape[3] // tq
    work = [(u, h) for u in range(tiles) for h in range(k_ref.shape[1])]
    grouped = lambda ref, u, h: jnp.concatenate(
        [ref[0, h * group + g, :, u * tq:(u + 1) * tq] for g in range(group)], axis=1)

    def window(u):
        first = (pl.program_id(2) * tiles + u) * blocks
        return (pl.multiple_of(jnp.maximum(first - back, 0) * LANES, LANES),
                pl.multiple_of(jnp.maximum(back - first, 0) * LANES, LANES))

    def run(fixed_max):
        def scores(w):
            u, h = work[w]
            k_start, tab_start = window(u)
            q = _augment_q(grouped(q_ref, u, h), 0.0, 0, grouped(m_ref, u, h) if fixed_max else None)
            s = jnp.dot(k_ref[0, h, pl.ds(k_start, band), :], q, preferred_element_type=f32)
            return s + tab_ref[h, pl.ds(tab_start, band), :]

        def finish(w, s):
            u, h = work[w]
            k_start, _ = window(u)
            if fixed_max:
                m = grouped(m_ref, u, h)
                p = jnp.exp2(s)
            else:
                m = jnp.maximum(jnp.max(s, axis=0, keepdims=True), sink_ref[h])
                p = jnp.exp2(s - m)
            l = jnp.sum(p, axis=0, keepdims=True) + jnp.exp2(sink_ref[h] - m)
            o = jnp.dot(v_ref[0, h, :, pl.ds(k_start, band)], p.astype(bf16),
                        preferred_element_type=f32) * (1.0 / l)
            for g in range(group):
                hq = h * group + g
                gate = sg_ref[0, hq, :, u * tq:(u + 1) * tq].astype(f32)
                o_ref[0, hq, :, u * tq:(u + 1) * tq] = (o[:, g * tq:(g + 1) * tq] * gate).astype(bf16)

        _staggered(len(work), scores, finish)

    _either_sweep(fixed_ref, run)


def _band_attn(fixed, q, k, v, sg, m, tab, sink, group, back, tq, kv_per_step, tiles_per_step, name):
    bsz, nheads, _, seq = q.shape
    hs = kv_per_step
    tile = tq * tiles_per_step
    return pl.pallas_call(
        functools.partial(_band_kernel, group=group, back=back, tq=tq),
        grid=(bsz, nheads // (group * hs), seq // tile),
        in_specs=[
            pl.BlockSpec(memory_space=pltpu.SMEM),
            pl.BlockSpec((1, hs * group, HEAD_DIM, tile), lambda b, h, i: (b, h, 0, i)),
            pl.BlockSpec((1, hs, seq, KPAD), lambda b, h, i: (b, h, 0, 0)),
            pl.BlockSpec((1, hs, HEAD_DIM, seq), lambda b, h, i: (b, h, 0, 0)),
            pl.BlockSpec((1, hs * group, HEAD_DIM, tile), lambda b, h, i: (b, h, 0, i)),
            pl.BlockSpec((1, hs * group, 1, tile), lambda b, h, i: (b, h, 0, i)),
            pl.BlockSpec((hs,) + tab.shape[1:], lambda b, h, i: (h, 0, 0)),
            pl.BlockSpec((hs,) + sink.shape[1:], lambda b, h, i: (h, 0, 0)),
        ],
        out_specs=pl.BlockSpec((1, hs * group, HEAD_DIM, tile), lambda b, h, i: (b, h, 0, i)),
        out_shape=jax.ShapeDtypeStruct((bsz, nheads, HEAD_DIM, seq), bf16),
        compiler_params=pltpu.CompilerParams(
            dimension_semantics=("arbitrary", "arbitrary", "arbitrary"),
            vmem_limit_bytes=VMEM_LIMIT),
        name=name,
    )(fixed, q, k, v, sg, m, tab, sink)


def _out_proj_kernel(m1_ref, m2_ref, wt_ref, x_ref, o_ref):
    o_ref[0] = _residual_add(m1_ref, m2_ref, wt_ref, x_ref)


def _out_proj(m1, m2, wt, x):
    bsz, seq, _ = x.shape
    tt = OUT_TOKENS
    half = m1.shape[1]
    return pl.pallas_call(
        _out_proj_kernel,
        grid=(bsz, seq // tt),
        in_specs=[
            pl.BlockSpec((1, half, tt), lambda b, t: (b, 0, t)),
            pl.BlockSpec((1, half, tt), lambda b, t: (b, 0, t)),
            pl.BlockSpec(wt.shape, lambda b, t: (0, 0)),
            pl.BlockSpec((1, tt, D_MODEL), lambda b, t: (b, t, 0)),
        ],
        out_specs=pl.BlockSpec((1, tt, D_MODEL), lambda b, t: (b, t, 0)),
        out_shape=jax.ShapeDtypeStruct(x.shape, f32),
        compiler_params=pltpu.CompilerParams(
            dimension_semantics=("arbitrary", "arbitrary"), vmem_limit_bytes=VMEM_LIMIT),
        name="out_proj",
    )(m1, m2, wt, x)


def _alibi_slopes(n):
    return 2.0 ** (-8.0 * np.arange(1, n + 1, dtype=np.float64) / n)


def _np_split3(v):
    v = np.asarray(v, np.float32)
    to_bf = lambda a: a.astype(bf16).astype(np.float32)
    hi = to_bf(v)
    mid = to_bf(v - hi)
    lo = to_bf(v - hi - mid)
    return hi, mid, lo


def _a_tables():
    rate = A_RATES
    qaug = np.zeros((A_HEADS, KPAD - HEAD_DIM, 1), np.float32)
    for idx, piece in enumerate(_np_split3(rate * CHUNK) + _np_split3(rate)):
        qaug[:, idx, 0] = piece
    kk = np.arange(ATT_TQ)[:, None]
    qq = np.arange(ATT_TQ)[None, :]
    future = np.maximum(kk - qq, 0).astype(np.float32)
    corr = -2.0 * rate[:, None, None] * future[None]
    allowed = (kk // CHUNK) <= (qq // CHUNK)
    dtab = np.where(allowed[None], corr, NEG).astype(np.float32)
    return jnp.asarray(qaug), jnp.asarray(dtab)


def _band_frames(back, tq):
    k_pos = np.arange(back * LANES + tq)[:, None]
    q_pos = back * LANES + np.arange(tq)[None, :]
    return q_pos - k_pos, q_pos // CHUNK - k_pos // CHUNK


def _c_tables(sinks):
    back, tq = WIN_CHUNKS * CHUNK // LANES, C_BAND_TQ
    rel, chunk_diff = _band_frames(back, tq)
    allowed = (chunk_diff >= 0) & (chunk_diff <= WIN_CHUNKS)
    slopes = _alibi_slopes(C_HEADS)
    per_head = np.where(allowed[None], -slopes[:, None, None] * np.abs(rel)[None] * LOG2E, NEG)
    tab = per_head.reshape(C_KV_HEADS, C_GROUP, *rel.shape).transpose(0, 2, 1, 3)
    tab = tab.reshape(C_KV_HEADS, rel.shape[0], C_GROUP * tq).astype(np.float32)
    tab = np.concatenate([tab, np.full((C_KV_HEADS, back * LANES, tab.shape[2]), NEG, np.float32)], 1)
    sink = jnp.repeat(sinks.astype(f32) * LOG2E, tq).reshape(C_KV_HEADS, 1, C_GROUP * tq)
    return jnp.asarray(tab), sink, back


def _d_tables(rel_table):
    back, t = D_LEFT_CHUNKS * CHUNK // LANES, D_BAND_TQ
    band = back * LANES + t
    rel, chunk_diff = _band_frames(back, t)
    allowed = (chunk_diff >= 0) & (chunk_diff <= D_LEFT_CHUNKS)
    tbl = rel_table.astype(f32) * LOG2E
    n_lo = (t - 1) - (CHUNK - 1)
    n_hi = (band - 1) - REL_MAX
    diag = jnp.concatenate([jnp.broadcast_to(tbl[:, :1], (D_HEADS, n_lo)), tbl,
                            jnp.broadcast_to(tbl[:, -1:], (D_HEADS, n_hi))], axis=1)
    m = t + LANES - 1
    blocks = []
    for kb in range(band // LANES):
        lo = rel[kb * LANES:(kb + 1) * LANES].min()
        if lo >= REL_MAX:
            blocks.append(jnp.broadcast_to(tbl[:, -1:, None], (D_HEADS, LANES, t)))
            continue
        window = diag[:, band - (kb + 1) * LANES:band - (kb + 1) * LANES + m]
        skew = jnp.broadcast_to(window[:, None, :], (D_HEADS, LANES + 1, m)).reshape(D_HEADS, -1)
        skew = skew[:, :LANES * (m + 1)].reshape(D_HEADS, LANES, m + 1)[:, :, :t]
        blocks.append(jnp.flip(skew, axis=1))
    blocks += [jnp.zeros((D_HEADS, LANES, t), f32)] * back
    allowed = np.concatenate([allowed, np.zeros((back * LANES, t), bool)], axis=0)
    tab = jnp.where(jnp.asarray(allowed)[None], jnp.concatenate(blocks, axis=1), NEG)
    sink = jnp.full((D_HEADS, 1, t), NEG, f32)
    return tab, sink, back


def _fixed_max_ok(q_gain, k_gain, bias_range=0.0):
    spread = (2.0 * 1.02 * QK_SCALE * HEAD_DIM
              * jnp.max(jnp.abs(q_gain.astype(f32))) * jnp.max(jnp.abs(k_gain.astype(f32))))
    return (spread + bias_range <= FIXED_MAX_LIMIT).astype(jnp.int32).reshape(1)


def _pad_rows(w_t, rows):
    return jnp.pad(w_t, ((0, rows - w_t.shape[0]), (0, 0)))


def _even_layer(x, ln_g, w_in, w_out, a_qn_g, a_kn_g, a_lq1, a_lk1, a_lq2, a_lk2, a_subln_g,
                b_qn_g, b_kn_g, b_f_bias, layer_idx):
    bsz, seq, _ = x.shape
    colv = lambda v: v.astype(f32).reshape(-1, 1)
    n_wide = EVEN["bf"][0]
    wt = w_in[:, :n_wide].T.astype(bf16)
    wf = _pad_rows(w_in[:, n_wide:].T.astype(bf16), BF16_ROWS)
    aq, ak, av, asg, bq, bk, bv, bsg, am, bm = _proj_even(
        x, ln_g.astype(f32).reshape(1, -1), wt, wf, colv(a_qn_g), colv(a_kn_g), colv(b_qn_g),
        colv(b_kn_g), colv(b_f_bias))
    lam_init = 0.8 - 0.6 * math.exp(-0.3 * layer_idx)
    qaug_a, dtab = _a_tables()
    lamv = jnp.stack([a_lq1, a_lk1, a_lq2, a_lk2]).astype(f32)
    mix_a = _attn_a(_fixed_max_ok(a_qn_g, a_kn_g), aq, ak, av, asg, am, dtab, qaug_a,
                    colv(a_subln_g), lamv, lam_init)
    qaug_b = np.zeros((KPAD - HEAD_DIM, 1), np.float32)
    qaug_b[:B_BIAS_ROWS] = 1.0
    mix_b = _attn_b(_fixed_max_ok(b_qn_g, b_kn_g), bq, bk, bv, bsg, bm, jnp.asarray(qaug_b))
    return mix_a.reshape(bsz, -1, seq), mix_b.reshape(bsz, -1, seq), w_out.T.astype(bf16), x


def _odd_layer(pending, ln_g, w_in, w_out, c_qn_g, c_kn_g, c_sinks, d_qn_g, d_kn_g, d_rel_bias):
    bsz, seq, _ = pending[3].shape
    colv = lambda v: v.astype(f32).reshape(-1, 1)
    x, cq, ck, cv, csg, dq, dk, dv, dsg, cm, dm = _proj_odd(
        *pending, ln_g.astype(f32).reshape(1, -1), w_in.T.astype(bf16), colv(c_qn_g), colv(c_kn_g),
        colv(d_qn_g), colv(d_kn_g))
    tab_c, sink_c, back_c = _c_tables(c_sinks)
    fixed_c = _fixed_max_ok(c_qn_g, c_kn_g, LOG2E * jnp.maximum(jnp.max(c_sinks.astype(f32)), 0.0))
    mix_c = _band_attn(fixed_c, cq, ck, cv, csg, cm, tab_c, sink_c, C_GROUP, back_c, C_BAND_TQ,
                       C_KV_HEADS, C_TILES_PER_STEP, "attn_c")
    tab_d, sink_d, back_d = _d_tables(d_rel_bias)
    fixed_d = _fixed_max_ok(d_qn_g, d_kn_g, LOG2E * jnp.max(jnp.abs(d_rel_bias.astype(f32))))
    mix_d = _band_attn(fixed_d, dq, dk, dv, dsg, dm, tab_d, sink_d, 1, back_d, D_BAND_TQ,
                       D_HEADS_PER_STEP, D_TILES_PER_STEP, "attn_d")
    return mix_c.reshape(bsz, -1, seq), mix_d.reshape(bsz, -1, seq), w_out.T.astype(bf16), x


def kernel(x, even_ln_g, even_w_in, even_w_out, a_q_norm_g, a_k_norm_g, a_lambda_q1, a_lambda_k1, a_lambda_q2, a_lambda_k2, a_subln_g, b_q_norm_g, b_k_norm_g, b_forget_bias, odd_ln_g, odd_w_in, odd_w_out, c_q_norm_g, c_k_norm_g, c_sinks, d_q_norm_g, d_k_norm_g, d_rel_bias):
    depth = even_ln_g.shape[0] + odd_ln_g.shape[0]
    seq = x.shape[1]
    assert x.shape[2] == D_MODEL and even_w_in.shape[2] == P_EVEN and odd_w_in.shape[2] == P_ODD
    for tile in (PROJ_TOKENS, OUT_TOKENS, ATT_TQ, C_BAND_TQ * C_TILES_PER_STEP,
                 D_BAND_TQ * D_TILES_PER_STEP):
        assert seq % tile == 0, (seq, tile)
    pending = None
    for i in range(depth):
        j = i // 2
        if i % 2 == 0:
            if pending is not None:
                x = _out_proj(*pending)
            pending = _even_layer(x, even_ln_g[j], even_w_in[j], even_w_out[j], a_q_norm_g[j],
                                  a_k_norm_g[j], a_lambda_q1[j], a_lambda_k1[j], a_lambda_q2[j],
                                  a_lambda_k2[j], a_subln_g[j], b_q_norm_g[j], b_k_norm_g[j],
                                  b_forget_bias[j], i)
        else:
            pending = _odd_layer(pending, odd_ln_g[j], odd_w_in[j], odd_w_out[j], c_q_norm_g[j],
                                 c_k_norm_g[j], c_sinks[j], d_q_norm_g[j], d_k_norm_g[j],
                                 d_rel_bias[j])
    return _out_proj(*pending)
```

```python
import functools
import math

import numpy as np
import jax
import jax.numpy as jnp
from jax import lax
from jax.experimental import pallas as pl
from jax.experimental.pallas import tpu as pltpu

D_MODEL = 1024
CHUNK = 64
HEAD_DIM = 64
NORM_EPS = 1e-6

A_HEADS = 4
A_STREAMS = 2 * A_HEADS
A_VDIM = 2 * HEAD_DIM
B_HEADS = 8
C_HEADS = 8
C_KV_HEADS = 2
C_GROUP = C_HEADS // C_KV_HEADS
WIN_CHUNKS = 2
D_HEADS = 8
D_LEFT_CHUNKS = 8
REL_MAX = 256


def _row_ranges(names, sizes):
    stops = np.cumsum(sizes)
    return {n: (int(b - w), int(b)) for n, w, b in zip(names, sizes, stops)}


EVEN = _row_ranges(("aq", "ak", "av", "ag", "bq", "bk", "bv", "bg", "bf"),
                   (A_STREAMS * HEAD_DIM,) * 2 + (A_HEADS * A_VDIM,) * 2 + (B_HEADS * HEAD_DIM,) * 4
                   + (B_HEADS,))
ODD = _row_ranges(("cq", "ck", "cv", "cg", "dq", "dk", "dv", "dg"),
                  (C_HEADS * HEAD_DIM, C_KV_HEADS * HEAD_DIM, C_KV_HEADS * HEAD_DIM, C_HEADS * HEAD_DIM)
                  + (D_HEADS * HEAD_DIM,) * 4)
P_EVEN = EVEN["bf"][1]
P_ODD = ODD["dg"][1]

LOG2E = 1.4426950408889634
QK_SCALE = HEAD_DIM ** -0.5 * LOG2E
NEG = -1e30
A_RATES = (2.0 ** (-8.0 * np.arange(1, A_HEADS + 1) / A_HEADS) * LOG2E).astype(np.float32)

LANES = 128
SUBLANES = 8
KPAD = 128
BF16_ROWS = 16

PROJ_TOKENS = 512
OUT_TOKENS = 1024
ATT_TQ = 512
ATT_TK = 512
A_HEADS_PER_STEP = 4
B_HEADS_PER_STEP = 8
A_BIAS_ROWS = 6
B_BIAS_ROWS = 3
MAX_ROWS = 3
FIXED_MAX_LIMIT = 96.0
C_BAND_TQ = 128
D_BAND_TQ = 256
D_HEADS_PER_STEP = 4
C_TILES_PER_STEP = 8
D_TILES_PER_STEP = 4
VMEM_LIMIT = 56 * 1024 * 1024

f32 = jnp.float32
bf16 = jnp.bfloat16


def _split3(v):
    hi = v.astype(bf16).astype(f32)
    r = v - hi
    mid = r.astype(bf16).astype(f32)
    lo = (r - mid).astype(bf16).astype(f32)
    return hi, mid, lo


def _silu(z):
    return z * (1.0 / (1.0 + jnp.exp(-z)))


def _rms_rows(x, g_ref):
    ms = jnp.mean(x * x, axis=-1, keepdims=True)
    return (x * lax.rsqrt(ms + NORM_EPS) * g_ref[...]).astype(bf16)


def _residual_add(m1_ref, m2_ref, wt_ref, x_ref):
    half = m1_ref.shape[1]
    y_t = (jnp.dot(wt_ref[:, :half], m1_ref[0], preferred_element_type=f32)
           + jnp.dot(wt_ref[:, half:], m2_ref[0], preferred_element_type=f32))
    return x_ref[0] + y_t.T


def _proj_t(wt_ref, rows, xn):
    return lax.dot_general(wt_ref[rows[0]:rows[1], :], xn, (((1,), (1,)), ((), ())),
                           preferred_element_type=f32)


def _head_norm(z_t, gain_col, mult):
    n = z_t.shape[0] // HEAD_DIM
    z3 = z_t.reshape(n, HEAD_DIM, z_t.shape[1])
    ms = jnp.mean(z3 * z3, axis=1, keepdims=True)
    return z3 * lax.rsqrt(ms + NORM_EPS) * (gain_col[...] * mult)[None]


def _ones_rows(row, first):
    return jnp.where((row >= first) & (row < first + MAX_ROWS), 1.0, 0.0)


def _store_heads(o_ref, z_t):
    o_ref[0] = z_t.reshape(o_ref.shape[1], o_ref.shape[2], z_t.shape[1]).astype(bf16)


def _store_keys(k_ref, kn, aug_fn):
    n, _, t = kn.shape
    zeros = jnp.zeros((KPAD - HEAD_DIM - BF16_ROWS, t), f32)
    for s in range(n):
        blk = jnp.concatenate([kn[s], aug_fn(s), zeros], axis=0)
        k_ref[0, s] = blk.T.astype(bf16)


def _proj_even_kernel(x_ref, lng_ref, wt_ref, wf_ref, aqg_ref, akg_ref, bqg_ref, bkg_ref, bfb_ref,
                      tri_ref, aq_ref, ak_ref, av_ref, asg_ref, bq_ref, bk_ref, bv_ref, bsg_ref,
                      am_ref, bm_ref, cum_ref):
    t = pl.program_id(1)
    tt = x_ref.shape[1]
    xn = _rms_rows(x_ref[0], lng_ref)
    row = lax.broadcasted_iota(jnp.int32, (BF16_ROWS, tt), 0)

    z = _proj_t(wf_ref, (0, BF16_ROWS), xn)[:B_HEADS] + bfb_ref[...]
    aqn = _head_norm(_proj_t(wt_ref, EVEN["aq"], xn), aqg_ref, QK_SCALE)
    aq_ref[0] = aqn.astype(bf16)

    pos = t * tt + lax.broadcasted_iota(jnp.int32, (BF16_ROWS, tt), 1)
    pos_a = lax.shift_right_logical(pos, int(math.log2(CHUNK))).astype(f32)
    pos_b = lax.bitwise_and(pos, CHUNK - 1).astype(f32)
    aug_a = jnp.where(row < A_BIAS_ROWS // 2, pos_a,
                      jnp.where(row < A_BIAS_ROWS, pos_b, _ones_rows(row, A_BIAS_ROWS)))
    akn = _head_norm(_proj_t(wt_ref, EVEN["ak"], xn), akg_ref, 1.0)
    _store_keys(ak_ref, akn, lambda s: aug_a)
    self_a = jnp.sum(aqn * akn, axis=1, keepdims=True)
    for s in range(A_STREAMS):
        am_ref[0, s] = self_a[s] + float(A_RATES[s // 2]) * pos[:1].astype(f32)

    _store_heads(asg_ref, _silu(_proj_t(wt_ref, EVEN["ag"], xn)))
    bqn = _head_norm(_proj_t(wt_ref, EVEN["bq"], xn), bqg_ref, QK_SCALE)
    bq_ref[0] = bqn.astype(bf16)

    log_f = jnp.minimum(z, 0.0) - jnp.log(1.0 + jnp.exp(-jnp.abs(z)))
    pieces = jnp.concatenate(_split3(log_f) + (jnp.zeros_like(log_f),), axis=0).astype(bf16)
    part = jnp.dot(pieces, tri_ref[...], preferred_element_type=f32)
    local = part[:B_HEADS] + part[B_HEADS:2 * B_HEADS] + part[2 * B_HEADS:3 * B_HEADS]

    @pl.when(t == 0)
    def _():
        cum_ref[...] = jnp.zeros_like(cum_ref)

    cum = cum_ref[...] + local
    cum_ref[...] = cum[:, tt - 1:tt]
    gate = -LOG2E * cum
    g_hi, g_mid, g_lo = _split3(gate)

    def aug_b(s):
        pick = lambda a: jnp.broadcast_to(a[s:s + 1], (BF16_ROWS, tt))
        return jnp.where(row == 0, pick(g_hi),
                         jnp.where(row == 1, pick(g_mid),
                                   jnp.where(row == 2, pick(g_lo), _ones_rows(row, B_BIAS_ROWS))))

    bkn = _head_norm(_proj_t(wt_ref, EVEN["bk"], xn), bkg_ref, 1.0)
    _store_keys(bk_ref, bkn, aug_b)
    self_b = jnp.sum(bqn * bkn, axis=1, keepdims=True)
    for s in range(B_HEADS):
        bm_ref[0, s] = self_b[s] + gate[s:s + 1]
    _store_heads(bsg_ref, _silu(_proj_t(wt_ref, EVEN["bg"], xn)))
    _store_heads(av_ref, _proj_t(wt_ref, EVEN["av"], xn))
    _store_heads(bv_ref, _proj_t(wt_ref, EVEN["bv"], xn))


def _proj_even(x, ln_g, wt, wf, aqg, akg, bqg, bkg, bfb):
    bsz, seq, _ = x.shape
    tt = PROJ_TOKENS
    col = lambda n: pl.BlockSpec((n, 1), lambda b, t: (0, 0))
    fm = lambda n, d: pl.BlockSpec((1, n, d, tt), lambda b, t: (b, 0, 0, t))
    km = lambda n: pl.BlockSpec((1, n, tt, KPAD), lambda b, t: (b, 0, t, 0))
    fm_shape = lambda n, d: jax.ShapeDtypeStruct((bsz, n, d, seq), bf16)
    km_shape = lambda n: jax.ShapeDtypeStruct((bsz, n, seq, KPAD), bf16)
    return pl.pallas_call(
        _proj_even_kernel,
        grid=(bsz, seq // tt),
        in_specs=[
            pl.BlockSpec((1, tt, D_MODEL), lambda b, t: (b, t, 0)),
            pl.BlockSpec((1, D_MODEL), lambda b, t: (0, 0)),
            pl.BlockSpec(wt.shape, lambda b, t: (0, 0)),
            pl.BlockSpec(wf.shape, lambda b, t: (0, 0)),
            col(HEAD_DIM), col(HEAD_DIM), col(HEAD_DIM), col(HEAD_DIM), col(B_HEADS),
            pl.BlockSpec((tt, tt), lambda b, t: (0, 0)),
        ],
        out_specs=[fm(A_STREAMS, HEAD_DIM), km(A_STREAMS), fm(A_HEADS, A_VDIM), fm(A_HEADS, A_VDIM),
                   fm(B_HEADS, HEAD_DIM), km(B_HEADS), fm(B_HEADS, HEAD_DIM), fm(B_HEADS, HEAD_DIM),
                   fm(A_STREAMS, 1), fm(B_HEADS, 1)],
        out_shape=[fm_shape(A_STREAMS, HEAD_DIM), km_shape(A_STREAMS), fm_shape(A_HEADS, A_VDIM),
                   fm_shape(A_HEADS, A_VDIM), fm_shape(B_HEADS, HEAD_DIM), km_shape(B_HEADS),
                   fm_shape(B_HEADS, HEAD_DIM), fm_shape(B_HEADS, HEAD_DIM),
                   jax.ShapeDtypeStruct((bsz, A_STREAMS, 1, seq), f32),
                   jax.ShapeDtypeStruct((bsz, B_HEADS, 1, seq), f32)],
        scratch_shapes=[pltpu.VMEM((B_HEADS, 1), f32)],
        compiler_params=pltpu.CompilerParams(
            dimension_semantics=("arbitrary", "arbitrary"), vmem_limit_bytes=VMEM_LIMIT),
        name="proj_even",
    )(x, ln_g, wt, wf, aqg, akg, bqg, bkg, bfb,
      jnp.asarray(np.triu(np.ones((tt, tt), np.float32)), bf16))


def _proj_odd_kernel(m1_ref, m2_ref, wo_ref, x_ref, lng_ref, wt_ref, cqg_ref, ckg_ref, dqg_ref,
                     dkg_ref, x1_ref, cq_ref, ck_ref, cv_ref, csg_ref, dq_ref, dk_ref, dv_ref,
                     dsg_ref, cm_ref, dm_ref):
    tt = x_ref.shape[1]
    x1 = _residual_add(m1_ref, m2_ref, wo_ref, x_ref)
    x1_ref[0] = x1
    xn = _rms_rows(x1, lng_ref)
    ones = _ones_rows(lax.broadcasted_iota(jnp.int32, (BF16_ROWS, tt), 0), 0)
    aug = lambda s: ones

    cqn = _head_norm(_proj_t(wt_ref, ODD["cq"], xn), cqg_ref, QK_SCALE)
    cq_ref[0] = cqn.astype(bf16)
    ckn = _head_norm(_proj_t(wt_ref, ODD["ck"], xn), ckg_ref, 1.0)
    _store_keys(ck_ref, ckn, aug)
    cm_ref[0] = jnp.sum(cqn.reshape(C_KV_HEADS, C_GROUP, HEAD_DIM, tt) * ckn[:, None], axis=2,
                        keepdims=True).reshape(C_HEADS, 1, tt)
    _store_heads(cv_ref, _proj_t(wt_ref, ODD["cv"], xn))
    _store_heads(csg_ref, _silu(_proj_t(wt_ref, ODD["cg"], xn)))
    dqn = _head_norm(_proj_t(wt_ref, ODD["dq"], xn), dqg_ref, QK_SCALE)
    dq_ref[0] = dqn.astype(bf16)
    dkn = _head_norm(_proj_t(wt_ref, ODD["dk"], xn), dkg_ref, 1.0)
    _store_keys(dk_ref, dkn, aug)
    dm_ref[0] = jnp.sum(dqn * dkn, axis=1, keepdims=True)
    _store_heads(dsg_ref, _silu(_proj_t(wt_ref, ODD["dg"], xn)))
    _store_heads(dv_ref, _proj_t(wt_ref, ODD["dv"], xn))


def _proj_odd(m1, m2, wo_t, x, ln_g, wt, cqg, ckg, dqg, dkg):
    bsz, seq, _ = x.shape
    tt = PROJ_TOKENS
    half = m1.shape[1]
    col = lambda n: pl.BlockSpec((n, 1), lambda b, t: (0, 0))
    rows = pl.BlockSpec((1, tt, D_MODEL), lambda b, t: (b, t, 0))
    fm = lambda n: pl.BlockSpec((1, n, HEAD_DIM, tt), lambda b, t: (b, 0, 0, t))
    km = lambda n: pl.BlockSpec((1, n, tt, KPAD), lambda b, t: (b, 0, t, 0))
    fm_shape = lambda n: jax.ShapeDtypeStruct((bsz, n, HEAD_DIM, seq), bf16)
    km_shape = lambda n: jax.ShapeDtypeStruct((bsz, n, seq, KPAD), bf16)
    return pl.pallas_call(
        _proj_odd_kernel,
        grid=(bsz, seq // tt),
        in_specs=[
            pl.BlockSpec((1, half, tt), lambda b, t: (b, 0, t)),
            pl.BlockSpec((1, half, tt), lambda b, t: (b, 0, t)),
            pl.BlockSpec(wo_t.shape, lambda b, t: (0, 0)),
            rows,
            pl.BlockSpec((1, D_MODEL), lambda b, t: (0, 0)),
            pl.BlockSpec(wt.shape, lambda b, t: (0, 0)),
            col(HEAD_DIM), col(HEAD_DIM), col(HEAD_DIM), col(HEAD_DIM),
        ],
        out_specs=[rows, fm(C_HEADS), km(C_KV_HEADS), fm(C_KV_HEADS), fm(C_HEADS),
                   fm(D_HEADS), km(D_HEADS), fm(D_HEADS), fm(D_HEADS),
                   pl.BlockSpec((1, C_HEADS, 1, tt), lambda b, t: (b, 0, 0, t)),
                   pl.BlockSpec((1, D_HEADS, 1, tt), lambda b, t: (b, 0, 0, t))],
        out_shape=[jax.ShapeDtypeStruct(x.shape, f32),
                   fm_shape(C_HEADS), km_shape(C_KV_HEADS), fm_shape(C_KV_HEADS), fm_shape(C_HEADS),
                   fm_shape(D_HEADS), km_shape(D_HEADS), fm_shape(D_HEADS), fm_shape(D_HEADS),
                   jax.ShapeDtypeStruct((bsz, C_HEADS, 1, seq), f32),
                   jax.ShapeDtypeStruct((bsz, D_HEADS, 1, seq), f32)],
        compiler_params=pltpu.CompilerParams(
            dimension_semantics=("arbitrary", "arbitrary"), vmem_limit_bytes=VMEM_LIMIT),
        name="proj_odd",
    )(m1, m2, wo_t, x, ln_g, wt, cqg, ckg, dqg, dkg)


def _online_step(carry, s, v):
    m, l, acc = carry
    m_new = jnp.maximum(m, jnp.max(s, axis=0, keepdims=True))
    p = jnp.exp2(s - m_new)
    alpha = jnp.exp2(m - m_new)
    l = alpha * l + jnp.sum(p, axis=0, keepdims=True)
    acc = alpha * acc + jnp.dot(v, p.astype(bf16), preferred_element_type=f32)
    return m_new, l, acc


def _colsum8(p):
    return p.reshape(p.shape[0] // SUBLANES, SUBLANES, p.shape[1]).sum(axis=0)


def _augment_q(q, aug_col, n_bias, m=None):
    tq = q.shape[1]
    aug = jnp.broadcast_to(aug_col, (KPAD - HEAD_DIM, tq))
    if m is not None:
        row = lax.broadcasted_iota(jnp.int32, aug.shape, 0)
        pieces = _split3(-m)
        for r in range(MAX_ROWS):
            aug = jnp.where(row == n_bias + r, pieces[r], aug)
    return jnp.concatenate([q, aug.astype(bf16)], axis=0)


def _staggered(n, scores, finish):
    out, pending = [], scores(0)
    for c in range(1, n):
        nxt = scores(c)
        out.append(finish(c - 1, pending))
        pending = nxt
    out.append(finish(n - 1, pending))
    return out


def _diag_full(chains, d0, tq):
    def scores(c):
        k_at, _, q, aug_col, n_bias, diag_bias, _ = chains[c]
        return jnp.dot(k_at(d0, tq), _augment_q(q, aug_col, n_bias),
                       preferred_element_type=f32) + diag_bias

    def finish(c, s):
        m = jnp.max(s, axis=0, keepdims=True)
        p = jnp.exp2(s - m)
        return m, _colsum8(p), jnp.dot(chains[c][1](d0, tq), p.astype(bf16), preferred_element_type=f32)

    return _staggered(len(chains), scores, finish)


def _diag_halves(chains, q_aug, d0, tq):
    h = tq // 2
    d1 = pl.multiple_of(d0 + h, h)

    def scores(c):
        k_at = chains[c][0]
        return (jnp.dot(k_at(d0, h), q_aug[c], preferred_element_type=f32),
                jnp.dot(k_at(d1, h), q_aug[c][:, h:], preferred_element_type=f32))

    def finish(c, s):
        v_at, bias = chains[c][1], chains[c][5][:h, :h]
        p0 = jnp.concatenate([jnp.exp2(s[0][:, :h] + bias), jnp.exp2(s[0][:, h:])], axis=1)
        p1 = jnp.exp2(s[1] + bias)
        l0 = _colsum8(p0)
        a0 = jnp.dot(v_at(d0, h), p0.astype(bf16), preferred_element_type=f32)
        a1 = jnp.dot(v_at(d1, h), p1.astype(bf16), preferred_element_type=f32)
        return (jnp.concatenate([l0[:, :h], l0[:, h:] + _colsum8(p1)], axis=1),
                jnp.concatenate([a0[:, :h], a0[:, h:] + a1], axis=1))

    return _staggered(len(chains), scores, finish)


def _causal_sweep(chains, i, fixed_max, l_s, acc_s):
    n = len(chains)
    tq = chains[0][2].shape[1]
    d0 = pl.multiple_of(i * tq, tq)
    n_tiles = i * (tq // ATT_TK)
    tile = lambda j: (pl.multiple_of(j * ATT_TK, ATT_TK), ATT_TK)

    if fixed_max:
        q_aug = [_augment_q(q, aug_col, n_bias, m) for _, _, q, aug_col, n_bias, _, m in chains]

        for c, (l, acc) in enumerate(_diag_halves(chains, q_aug, d0, tq)):
            l_s[c] = l
            acc_s[c] = acc

        def body(j, carry):
            def finish(c, s):
                p = jnp.exp2(s)
                l_s[c] += _colsum8(p)
                acc_s[c] += jnp.dot(chains[c][1](*tile(j)), p.astype(bf16), preferred_element_type=f32)

            scores = lambda c: jnp.dot(chains[c][0](*tile(j)), q_aug[c], preferred_element_type=f32)
            _staggered(n, scores, finish)
            return carry

        lax.fori_loop(0, n_tiles, body, 0)
        return [(acc_s[c], jnp.sum(l_s[c], axis=0, keepdims=True)) for c in range(n)]

    q_aug = [_augment_q(q, aug_col, n_bias) for _, _, q, aug_col, n_bias, _, _ in chains]

    def body(j, carries):
        scores = lambda c: jnp.dot(chains[c][0](*tile(j)), q_aug[c], preferred_element_type=f32)
        finish = lambda c, s: _online_step(carries[c], s, chains[c][1](*tile(j)))
        return tuple(_staggered(n, scores, finish))

    init = tuple((m, jnp.sum(l, axis=0, keepdims=True), acc) for m, l, acc in _diag_full(chains, d0, tq))
    return [(acc, l) for _, l, acc in lax.fori_loop(0, n_tiles, body, init)]


def _either_sweep(fixed_ref, run):
    @pl.when(fixed_ref[0] != 0)
    def _():
        run(True)

    @pl.when(fixed_ref[0] == 0)
    def _():
        run(False)


def _attn_a_kernel(fixed_ref, q_ref, k_ref, v_ref, sg_ref, m_ref, dtab_ref, qaug_ref, subg_ref,
                   lamv_ref, o_ref, l_s, acc_s, *, lam_init):
    i = pl.program_id(2)
    chains = []
    for h in range(A_HEADS_PER_STEP):
        v_at = lambda start, size, h=h: v_ref[0, h, :, pl.ds(start, size)]
        for c in range(2):
            s = 2 * h + c
            k_at = lambda start, size, s=s: k_ref[0, s, pl.ds(start, size), :]
            chains.append((k_at, v_at, q_ref[0, s], qaug_ref[h], A_BIAS_ROWS, dtab_ref[h],
                           m_ref[0, s]))

    def run(fixed_max):
        lv = lamv_ref[...]
        lam = (jnp.exp(jnp.sum(lv[0:1] * lv[1:2], axis=1, keepdims=True))
               - jnp.exp(jnp.sum(lv[2:3] * lv[3:4], axis=1, keepdims=True)) + lam_init)
        outs = [acc * (1.0 / l) for acc, l in _causal_sweep(chains, i, fixed_max, l_s, acc_s)]
        for h in range(A_HEADS_PER_STEP):
            o = outs[2 * h] - lam * outs[2 * h + 1]
            ms = jnp.mean(o * o, axis=0, keepdims=True)
            y = o * lax.rsqrt(ms + NORM_EPS) * (subg_ref[...] * (1.0 - lam_init))
            o_ref[0, h] = (y * sg_ref[0, h].astype(f32)).astype(bf16)

    _either_sweep(fixed_ref, run)


def _attn_a(fixed, aq, ak, av, asg, am, dtab, qaug, subg, lamv, lam_init):
    bsz, _, _, seq = aq.shape
    nq = seq // ATT_TQ
    hs = A_HEADS_PER_STEP
    return pl.pallas_call(
        functools.partial(_attn_a_kernel, lam_init=lam_init),
        grid=(bsz, A_HEADS // hs, nq),
        in_specs=[
            pl.BlockSpec(memory_space=pltpu.SMEM),
            pl.BlockSpec((1, 2 * hs, HEAD_DIM, ATT_TQ), lambda b, h, i: (b, h, 0, i)),
            pl.BlockSpec((1, 2 * hs, seq, KPAD), lambda b, h, i: (b, h, 0, 0)),
            pl.BlockSpec((1, hs, A_VDIM, seq), lambda b, h, i: (b, h, 0, 0)),
            pl.BlockSpec((1, hs, A_VDIM, ATT_TQ), lambda b, h, i: (b, h, 0, i)),
            pl.BlockSpec((1, 2 * hs, 1, ATT_TQ), lambda b, h, i: (b, h, 0, i)),
            pl.BlockSpec((hs, ATT_TQ, ATT_TQ), lambda b, h, i: (h, 0, 0)),
            pl.BlockSpec((hs, KPAD - HEAD_DIM, 1), lambda b, h, i: (h, 0, 0)),
            pl.BlockSpec((A_VDIM, 1), lambda b, h, i: (0, 0)),
            pl.BlockSpec(lamv.shape, lambda b, h, i: (0, 0)),
        ],
        out_specs=pl.BlockSpec((1, hs, A_VDIM, ATT_TQ), lambda b, h, i: (b, h, 0, i)),
        out_shape=jax.ShapeDtypeStruct((bsz, A_HEADS, A_VDIM, seq), bf16),
        scratch_shapes=[pltpu.VMEM((2 * hs, SUBLANES, ATT_TQ), f32),
                        pltpu.VMEM((2 * hs, A_VDIM, ATT_TQ), f32)],
        compiler_params=pltpu.CompilerParams(
            dimension_semantics=("arbitrary", "arbitrary", "arbitrary"),
            vmem_limit_bytes=VMEM_LIMIT),
        name="attn_a",
    )(fixed, aq, ak, av, asg, am, dtab, qaug, subg, lamv)


def _attn_b_kernel(fixed_ref, q_ref, k_ref, v_ref, sg_ref, m_ref, qaug_ref, o_ref, l_s, acc_s):
    i = pl.program_id(2)

    def run(fixed_max):
        causal = jnp.where(lax.broadcasted_iota(jnp.int32, (ATT_TQ, ATT_TQ), 0)
                           <= lax.broadcasted_iota(jnp.int32, (ATT_TQ, ATT_TQ), 1), 0.0, NEG)
        chains = []
        for h in range(B_HEADS_PER_STEP):
            k_at = lambda start, size, h=h: k_ref[0, h, pl.ds(start, size), :]
            v_at = lambda start, size, h=h: v_ref[0, h, :, pl.ds(start, size)]
            chains.append((k_at, v_at, q_ref[0, h], qaug_ref[...], B_BIAS_ROWS, causal, m_ref[0, h]))
        for h, (acc, l) in enumerate(_causal_sweep(chains, i, fixed_max, l_s, acc_s)):
            o_ref[0, h] = (acc * (1.0 / l) * sg_ref[0, h].astype(f32)).astype(bf16)

    _either_sweep(fixed_ref, run)


def _attn_b(fixed, bq, bk, bv, bsg, bm, qaug):
    bsz, _, _, seq = bq.shape
    nq = seq // ATT_TQ
    hs = B_HEADS_PER_STEP
    return pl.pallas_call(
        _attn_b_kernel,
        grid=(bsz, B_HEADS // hs, nq),
        in_specs=[
            pl.BlockSpec(memory_space=pltpu.SMEM),
            pl.BlockSpec((1, hs, HEAD_DIM, ATT_TQ), lambda b, h, i: (b, h, 0, i)),
            pl.BlockSpec((1, hs, seq, KPAD), lambda b, h, i: (b, h, 0, 0)),
            pl.BlockSpec((1, hs, HEAD_DIM, seq), lambda b, h, i: (b, h, 0, 0)),
            pl.BlockSpec((1, hs, HEAD_DIM, ATT_TQ), lambda b, h, i: (b, h, 0, i)),
            pl.BlockSpec((1, hs, 1, ATT_TQ), lambda b, h, i: (b, h, 0, i)),
            pl.BlockSpec((KPAD - HEAD_DIM, 1), lambda b, h, i: (0, 0)),
        ],
        out_specs=pl.BlockSpec((1, hs, HEAD_DIM, ATT_TQ), lambda b, h, i: (b, h, 0, i)),
        out_shape=jax.ShapeDtypeStruct((bsz, B_HEADS, HEAD_DIM, seq), bf16),
        scratch_shapes=[pltpu.VMEM((hs, SUBLANES, ATT_TQ), f32), pltpu.VMEM((hs, HEAD_DIM, ATT_TQ), f32)],
        compiler_params=pltpu.CompilerParams(
            dimension_semantics=("arbitrary", "arbitrary", "arbitrary"),
            vmem_limit_bytes=VMEM_LIMIT),
        name="attn_b",
    )(fixed, bq, bk, bv, bsg, bm, qaug)


def _band_kernel(fixed_ref, q_ref, k_ref, v_ref, sg_ref, m_ref, tab_ref, sink_ref, o_ref,
                 *, group, back, tq):
    blocks = tq // LANES
    band = (back + blocks) * LANES
    tiles = q_ref.shape[3] // tq
    work = [(u, h) for u in range(tiles) for h in range(k_ref.shape[1])]
    grouped = lambda ref, u, h: jnp.concatenate(
        [ref[0, h * group + g, :, u * tq:(u + 1) * tq] for g in range(group)], axis=1)

    def window(u):
        first = (pl.program_id(2) * tiles + u) * blocks
        return (pl.multiple_of(jnp.maximum(first - back, 0) * LANES, LANES),
                pl.multiple_of(jnp.maximum(back - first, 0) * LANES, LANES))

    def run(fixed_max):
        def scores(w):
            u, h = work[w]
            k_start, tab_start = window(u)
            q = _augment_q(grouped(q_ref, u, h), 0.0, 0, grouped(m_ref, u, h) if fixed_max else None)
            s = jnp.dot(k_ref[0, h, pl.ds(k_start, band), :], q, preferred_element_type=f32)
            return s + tab_ref[h, pl.ds(tab_start, band), :]

        def finish(w, s):
            u, h = work[w]
            k_start, _ = window(u)
            if fixed_max:
                m = grouped(m_ref, u, h)
                p = jnp.exp2(s)
            else:
                m = jnp.maximum(jnp.max(s, axis=0, keepdims=True), sink_ref[h])
                p = jnp.exp2(s - m)
            l = jnp.sum(p, axis=0, keepdims=True) + jnp.exp2(sink_ref[h] - m)
            o = jnp.dot(v_ref[0, h, :, pl.ds(k_start, band)], p.astype(bf16),
                        preferred_element_type=f32) * (1.0 / l)
            for g in range(group):
                hq = h * group + g
                gate = sg_ref[0, hq, :, u * tq:(u + 1) * tq].astype(f32)
                o_ref[0, hq, :, u * tq:(u + 1) * tq] = (o[:, g * tq:(g + 1) * tq] * gate).astype(bf16)

        _staggered(len(work), scores, finish)

    _either_sweep(fixed_ref, run)


def _band_attn(fixed, q, k, v, sg, m, tab, sink, group, back, tq, kv_per_step, tiles_per_step, name):
    bsz, nheads, _, seq = q.shape
    hs = kv_per_step
    tile = tq * tiles_per_step
    return pl.pallas_call(
        functools.partial(_band_kernel, group=group, back=back, tq=tq),
        grid=(bsz, nheads // (group * hs), seq // tile),
        in_specs=[
            pl.BlockSpec(memory_space=pltpu.SMEM),
            pl.BlockSpec((1, hs * group, HEAD_DIM, tile), lambda b, h, i: (b, h, 0, i)),
            pl.BlockSpec((1, hs, seq, KPAD), lambda b, h, i: (b, h, 0, 0)),
            pl.BlockSpec((1, hs, HEAD_DIM, seq), lambda b, h, i: (b, h, 0, 0)),
            pl.BlockSpec((1, hs * group, HEAD_DIM, tile), lambda b, h, i: (b, h, 0, i)),
            pl.BlockSpec((1, hs * group, 1, tile), lambda b, h, i: (b, h, 0, i)),
            pl.BlockSpec((hs,) + tab.shape[1:], lambda b, h, i: (h, 0, 0)),
            pl.BlockSpec((hs,) + sink.shape[1:], lambda b, h, i: (h, 0, 0)),
        ],
        out_specs=pl.BlockSpec((1, hs * group, HEAD_DIM, tile), lambda b, h, i: (b, h, 0, i)),
        out_shape=jax.ShapeDtypeStruct((bsz, nheads, HEAD_DIM, seq), bf16),
        compiler_params=pltpu.CompilerParams(
            dimension_semantics=("arbitrary", "arbitrary", "arbitrary"),
            vmem_limit_bytes=VMEM_LIMIT),
        name=name,
    )(fixed, q, k, v, sg, m, tab, sink)


def _out_proj_kernel(m1_ref, m2_ref, wt_ref, x_ref, o_ref):
    o_ref[0] = _residual_add(m1_ref, m2_ref, wt_ref, x_ref)


def _out_proj(m1, m2, wt, x):
    bsz, seq, _ = x.shape
    tt = OUT_TOKENS
    half = m1.shape[1]
    return pl.pallas_call(
        _out_proj_kernel,
        grid=(bsz, seq // tt),
        in_specs=[
            pl.BlockSpec((1, half, tt), lambda b, t: (b, 0, t)),
            pl.BlockSpec((1, half, tt), lambda b, t: (b, 0, t)),
            pl.BlockSpec(wt.shape, lambda b, t: (0, 0)),
            pl.BlockSpec((1, tt, D_MODEL), lambda b, t: (b, t, 0)),
        ],
        out_specs=pl.BlockSpec((1, tt, D_MODEL), lambda b, t: (b, t, 0)),
        out_shape=jax.ShapeDtypeStruct(x.shape, f32),
        compiler_params=pltpu.CompilerParams(
            dimension_semantics=("arbitrary", "arbitrary"), vmem_limit_bytes=VMEM_LIMIT),
        name="out_proj",
    )(m1, m2, wt, x)


def _alibi_slopes(n):
    return 2.0 ** (-8.0 * np.arange(1, n + 1, dtype=np.float64) / n)


def _np_split3(v):
    v = np.asarray(v, np.float32)
    to_bf = lambda a: a.astype(bf16).astype(np.float32)
    hi = to_bf(v)
    mid = to_bf(v - hi)
    lo = to_bf(v - hi - mid)
    return hi, mid, lo


def _a_tables():
    rate = A_RATES
    qaug = np.zeros((A_HEADS, KPAD - HEAD_DIM, 1), np.float32)
    for idx, piece in enumerate(_np_split3(rate * CHUNK) + _np_split3(rate)):
        qaug[:, idx, 0] = piece
    kk = np.arange(ATT_TQ)[:, None]
    qq = np.arange(ATT_TQ)[None, :]
    future = np.maximum(kk - qq, 0).astype(np.float32)
    corr = -2.0 * rate[:, None, None] * future[None]
    allowed = (kk // CHUNK) <= (qq // CHUNK)
    dtab = np.where(allowed[None], corr, NEG).astype(np.float32)
    return jnp.asarray(qaug), jnp.asarray(dtab)


def _band_frames(back, tq):
    k_pos = np.arange(back * LANES + tq)[:, None]
    q_pos = back * LANES + np.arange(tq)[None, :]
    return q_pos - k_pos, q_pos // CHUNK - k_pos // CHUNK


def _c_tables(sinks):
    back, tq = WIN_CHUNKS * CHUNK // LANES, C_BAND_TQ
    rel, chunk_diff = _band_frames(back, tq)
    allowed = (chunk_diff >= 0) & (chunk_diff <= WIN_CHUNKS)
    slopes = _alibi_slopes(C_HEADS)
    per_head = np.where(allowed[None], -slopes[:, None, None] * np.abs(rel)[None] * LOG2E, NEG)
    tab = per_head.reshape(C_KV_HEADS, C_GROUP, *rel.shape).transpose(0, 2, 1, 3)
    tab = tab.reshape(C_KV_HEADS, rel.shape[0], C_GROUP * tq).astype(np.float32)
    tab = np.concatenate([tab, np.full((C_KV_HEADS, back * LANES, tab.shape[2]), NEG, np.float32)], 1)
    sink = jnp.repeat(sinks.astype(f32) * LOG2E, tq).reshape(C_KV_HEADS, 1, C_GROUP * tq)
    return jnp.asarray(tab), sink, back


def _d_tables(rel_table):
    back, t = D_LEFT_CHUNKS * CHUNK // LANES, D_BAND_TQ
    band = back * LANES + t
    rel, chunk_diff = _band_frames(back, t)
    allowed = (chunk_diff >= 0) & (chunk_diff <= D_LEFT_CHUNKS)
    tbl = rel_table.astype(f32) * LOG2E
    n_lo = (t - 1) - (CHUNK - 1)
    n_hi = (band - 1) - REL_MAX
    diag = jnp.concatenate([jnp.broadcast_to(tbl[:, :1], (D_HEADS, n_lo)), tbl,
                            jnp.broadcast_to(tbl[:, -1:], (D_HEADS, n_hi))], axis=1)
    m = t + LANES - 1
    blocks = []
    for kb in range(band // LANES):
        lo = rel[kb * LANES:(kb + 1) * LANES].min()
        if lo >= REL_MAX:
            blocks.append(jnp.broadcast_to(tbl[:, -1:, None], (D_HEADS, LANES, t)))
            continue
        window = diag[:, band - (kb + 1) * LANES:band - (kb + 1) * LANES + m]
        skew = jnp.broadcast_to(window[:, None, :], (D_HEADS, LANES + 1, m)).reshape(D_HEADS, -1)
        skew = skew[:, :LANES * (m + 1)].reshape(D_HEADS, LANES, m + 1)[:, :, :t]
        blocks.append(jnp.flip(skew, axis=1))
    blocks += [jnp.zeros((D_HEADS, LANES, t), f32)] * back
    allowed = np.concatenate([allowed, np.zeros((back * LANES, t), bool)], axis=0)
    tab = jnp.where(jnp.asarray(allowed)[None], jnp.concatenate(blocks, axis=1), NEG)
    sink = jnp.full((D_HEADS, 1, t), NEG, f32)
    return tab, sink, back


def _fixed_max_ok(q_gain, k_gain, bias_range=0.0):
    spread = (2.0 * 1.02 * QK_SCALE * HEAD_DIM
              * jnp.max(jnp.abs(q_gain.astype(f32))) * jnp.max(jnp.abs(k_gain.astype(f32))))
    return (spread + bias_range <= FIXED_MAX_LIMIT).astype(jnp.int32).reshape(1)


def _pad_rows(w_t, rows):
    return jnp.pad(w_t, ((0, rows - w_t.shape[0]), (0, 0)))


def _even_layer(x, ln_g, w_in, w_out, a_qn_g, a_kn_g, a_lq1, a_lk1, a_lq2, a_lk2, a_subln_g,
                b_qn_g, b_kn_g, b_f_bias, layer_idx):
    bsz, seq, _ = x.shape
    colv = lambda v: v.astype(f32).reshape(-1, 1)
    n_wide = EVEN["bf"][0]
    wt = w_in[:, :n_wide].T.astype(bf16)
    wf = _pad_rows(w_in[:, n_wide:].T.astype(bf16), BF16_ROWS)
    aq, ak, av, asg, bq, bk, bv, bsg, am, bm = _proj_even(
        x, ln_g.astype(f32).reshape(1, -1), wt, wf, colv(a_qn_g), colv(a_kn_g), colv(b_qn_g),
        colv(b_kn_g), colv(b_f_bias))
    lam_init = 0.8 - 0.6 * math.exp(-0.3 * layer_idx)
    qaug_a, dtab = _a_tables()
    lamv = jnp.stack([a_lq1, a_lk1, a_lq2, a_lk2]).astype(f32)
    mix_a = _attn_a(_fixed_max_ok(a_qn_g, a_kn_g), aq, ak, av, asg, am, dtab, qaug_a,
                    colv(a_subln_g), lamv, lam_init)
    qaug_b = np.zeros((KPAD - HEAD_DIM, 1), np.float32)
    qaug_b[:B_BIAS_ROWS] = 1.0
    mix_b = _attn_b(_fixed_max_ok(b_qn_g, b_kn_g), bq, bk, bv, bsg, bm, jnp.asarray(qaug_b))
    return mix_a.reshape(bsz, -1, seq), mix_b.reshape(bsz, -1, seq), w_out.T.astype(bf16), x


def _odd_layer(pending, ln_g, w_in, w_out, c_qn_g, c_kn_g, c_sinks, d_qn_g, d_kn_g, d_rel_bias):
    bsz, seq, _ = pending[3].shape
    colv = lambda v: v.astype(f32).reshape(-1, 1)
    x, cq, ck, cv, csg, dq, dk, dv, dsg, cm, dm = _proj_odd(
        *pending, ln_g.astype(f32).reshape(1, -1), w_in.T.astype(bf16), colv(c_qn_g), colv(c_kn_g),
        colv(d_qn_g), colv(d_kn_g))
    tab_c, sink_c, back_c = _c_tables(c_sinks)
    fixed_c = _fixed_max_ok(c_qn_g, c_kn_g, LOG2E * jnp.maximum(jnp.max(c_sinks.astype(f32)), 0.0))
    mix_c = _band_attn(fixed_c, cq, ck, cv, csg, cm, tab_c, sink_c, C_GROUP, back_c, C_BAND_TQ,
                       C_KV_HEADS, C_TILES_PER_STEP, "attn_c")
    tab_d, sink_d, back_d = _d_tables(d_rel_bias)
    fixed_d = _fixed_max_ok(d_qn_g, d_kn_g, LOG2E * jnp.max(jnp.abs(d_rel_bias.astype(f32))))
    mix_d = _band_attn(fixed_d, dq, dk, dv, dsg, dm, tab_d, sink_d, 1, back_d, D_BAND_TQ,
                       D_HEADS_PER_STEP, D_TILES_PER_STEP, "attn_d")
    return mix_c.reshape(bsz, -1, seq), mix_d.reshape(bsz, -1, seq), w_out.T.astype(bf16), x


def kernel(x, even_ln_g, even_w_in, even_w_out, a_q_norm_g, a_k_norm_g, a_lambda_q1, a_lambda_k1, a_lambda_q2, a_lambda_k2, a_subln_g, b_q_norm_g, b_k_norm_g, b_forget_bias, odd_ln_g, odd_w_in, odd_w_out, c_q_norm_g, c_k_norm_g, c_sinks, d_q_norm_g, d_k_norm_g, d_rel_bias):
    depth = even_ln_g.shape[0] + odd_ln_g.shape[0]
    seq = x.shape[1]
    assert x.shape[2] == D_MODEL and even_w_in.shape[2] == P_EVEN and odd_w_in.shape[2] == P_ODD
    for tile in (PROJ_TOKENS, OUT_TOKENS, ATT_TQ, C_BAND_TQ * C_TILES_PER_STEP,
                 D_BAND_TQ * D_TILES_PER_STEP):
        assert seq % tile == 0, (seq, tile)
    pending = None
    for i in range(depth):
        j = i // 2
        if i % 2 == 0:
            if pending is not None:
                x = _out_proj(*pending)
            pending = _even_layer(x, even_ln_g[j], even_w_in[j], even_w_out[j], a_q_norm_g[j],
                                  a_k_norm_g[j], a_lambda_q1[j], a_lambda_k1[j], a_lambda_q2[j],
                                  a_lambda_k2[j], a_subln_g[j], b_q_norm_g[j], b_k_norm_g[j],
                                  b_forget_bias[j], i)
        else:
            pending = _odd_layer(pending, odd_ln_g[j], odd_w_in[j], odd_w_out[j], c_q_norm_g[j],
                                 c_k_norm_g[j], c_sinks[j], d_q_norm_g[j], d_k_norm_g[j],
                                 d_rel_bias[j])
    return _out_proj(*pending)
```

```python
import functools
import math

import numpy as np
import jax
import jax.numpy as jnp
from jax import lax
from jax.experimental import pallas as pl
from jax.experimental.pallas import tpu as pltpu

D_MODEL = 1024
CHUNK = 64
HEAD_DIM = 64
NORM_EPS = 1e-6

A_HEADS = 4
A_STREAMS = 2 * A_HEADS
A_VDIM = 2 * HEAD_DIM
B_HEADS = 8
C_HEADS = 8
C_KV_HEADS = 2
C_GROUP = C_HEADS // C_KV_HEADS
WIN_CHUNKS = 2
D_HEADS = 8
D_LEFT_CHUNKS = 8
REL_MAX = 256


def _row_ranges(names, sizes):
    stops = np.cumsum(sizes)
    return {n: (int(b - w), int(b)) for n, w, b in zip(names, sizes, stops)}


EVEN = _row_ranges(("aq", "ak", "av", "ag", "bq", "bk", "bv", "bg", "bf"),
                   (A_STREAMS * HEAD_DIM,) * 2 + (A_HEADS * A_VDIM,) * 2 + (B_HEADS * HEAD_DIM,) * 4
                   + (B_HEADS,))
ODD = _row_ranges(("cq", "ck", "cv", "cg", "dq", "dk", "dv", "dg"),
                  (C_HEADS * HEAD_DIM, C_KV_HEADS * HEAD_DIM, C_KV_HEADS * HEAD_DIM, C_HEADS * HEAD_DIM)
                  + (D_HEADS * HEAD_DIM,) * 4)
P_EVEN = EVEN["bf"][1]
P_ODD = ODD["dg"][1]

LOG2E = 1.4426950408889634
QK_SCALE = HEAD_DIM ** -0.5 * LOG2E
NEG = -1e30
A_RATES = (2.0 ** (-8.0 * np.arange(1, A_HEADS + 1) / A_HEADS) * LOG2E).astype(np.float32)

LANES = 128
SUBLANES = 8
KPAD = 128
BF16_ROWS = 16

PROJ_TOKENS = 512
OUT_TOKENS = 1024
ATT_TQ = 512
ATT_TK = 512
A_HEADS_PER_STEP = 4
B_HEADS_PER_STEP = 8
A_BIAS_ROWS = 6
B_BIAS_ROWS = 3
MAX_ROWS = 3
FIXED_MAX_LIMIT = 96.0
C_BAND_TQ = 128
D_BAND_TQ = 256
D_HEADS_PER_STEP = 4
C_TILES_PER_STEP = 8
D_TILES_PER_STEP = 4
VMEM_LIMIT = 56 * 1024 * 1024

f32 = jnp.float32
bf16 = jnp.bfloat16


def _split3(v):
    hi = v.astype(bf16).astype(f32)
    r = v - hi
    mid = r.astype(bf16).astype(f32)
    lo = (r - mid).astype(bf16).astype(f32)
    return hi, mid, lo


def _silu(z):
    return z * (1.0 / (1.0 + jnp.exp(-z)))


def _rms_rows(x, g_ref):
    ms = jnp.mean(x * x, axis=-1, keepdims=True)
    return (x * lax.rsqrt(ms + NORM_EPS) * g_ref[...]).astype(bf16)


def _residual_add(m1_ref, m2_ref, wt_ref, x_ref):
    half = m1_ref.shape[1]
    y_t = (jnp.dot(wt_ref[:, :half], m1_ref[0], preferred_element_type=f32)
           + jnp.dot(wt_ref[:, half:], m2_ref[0], preferred_element_type=f32))
    return x_ref[0] + y_t.T


def _proj_t(wt_ref, rows, xn):
    return lax.dot_general(wt_ref[rows[0]:rows[1], :], xn, (((1,), (1,)), ((), ())),
                           preferred_element_type=f32)


def _head_norm(z_t, gain_col, mult):
    n = z_t.shape[0] // HEAD_DIM
    z3 = z_t.reshape(n, HEAD_DIM, z_t.shape[1])
    ms = jnp.mean(z3 * z3, axis=1, keepdims=True)
    return z3 * lax.rsqrt(ms + NORM_EPS) * (gain_col[...] * mult)[None]


def _ones_rows(row, first):
    return jnp.where((row >= first) & (row < first + MAX_ROWS), 1.0, 0.0)


def _store_heads(o_ref, z_t):
    o_ref[0] = z_t.reshape(o_ref.shape[1], o_ref.shape[2], z_t.shape[1]).astype(bf16)


def _store_keys(k_ref, kn, aug_fn):
    n, _, t = kn.shape
    zeros = jnp.zeros((KPAD - HEAD_DIM - BF16_ROWS, t), f32)
    for s in range(n):
        blk = jnp.concatenate([kn[s], aug_fn(s), zeros], axis=0)
        k_ref[0, s] = blk.T.astype(bf16)


def _proj_even_kernel(x_ref, lng_ref, wt_ref, wf_ref, aqg_ref, akg_ref, bqg_ref, bkg_ref, bfb_ref,
                      tri_ref, aq_ref, ak_ref, av_ref, asg_ref, bq_ref, bk_ref, bv_ref, bsg_ref,
                      am_ref, bm_ref, cum_ref):
    t = pl.program_id(1)
    tt = x_ref.shape[1]
    xn = _rms_rows(x_ref[0], lng_ref)
    row = lax.broadcasted_iota(jnp.int32, (BF16_ROWS, tt), 0)

    z = _proj_t(wf_ref, (0, BF16_ROWS), xn)[:B_HEADS] + bfb_ref[...]
    aqn = _head_norm(_proj_t(wt_ref, EVEN["aq"], xn), aqg_ref, QK_SCALE)
    aq_ref[0] = aqn.astype(bf16)

    pos = t * tt + lax.broadcasted_iota(jnp.int32, (BF16_ROWS, tt), 1)
    pos_a = lax.shift_right_logical(pos, int(math.log2(CHUNK))).astype(f32)
    pos_b = lax.bitwise_and(pos, CHUNK - 1).astype(f32)
    aug_a = jnp.where(row < A_BIAS_ROWS // 2, pos_a,
                      jnp.where(row < A_BIAS_ROWS, pos_b, _ones_rows(row, A_BIAS_ROWS)))
    akn = _head_norm(_proj_t(wt_ref, EVEN["ak"], xn), akg_ref, 1.0)
    _store_keys(ak_ref, akn, lambda s: aug_a)
    self_a = jnp.sum(aqn * akn, axis=1, keepdims=True)
    for s in range(A_STREAMS):
        am_ref[0, s] = self_a[s] + float(A_RATES[s // 2]) * pos[:1].astype(f32)

    _store_heads(asg_ref, _silu(_proj_t(wt_ref, EVEN["ag"], xn)))
    bqn = _head_norm(_proj_t(wt_ref, EVEN["bq"], xn), bqg_ref, QK_SCALE)
    bq_ref[0] = bqn.astype(bf16)

    log_f = jnp.minimum(z, 0.0) - jnp.log(1.0 + jnp.exp(-jnp.abs(z)))
    pieces = jnp.concatenate(_split3(log_f) + (jnp.zeros_like(log_f),), axis=0).astype(bf16)
    part = jnp.dot(pieces, tri_ref[...], preferred_element_type=f32)
    local = part[:B_HEADS] + part[B_HEADS:2 * B_HEADS] + part[2 * B_HEADS:3 * B_HEADS]

    @pl.when(t == 0)
    def _():
        cum_ref[...] = jnp.zeros_like(cum_ref)

    cum = cum_ref[...] + local
    cum_ref[...] = cum[:, tt - 1:tt]
    gate = -LOG2E * cum
    g_hi, g_mid, g_lo = _split3(gate)

    def aug_b(s):
        pick = lambda a: jnp.broadcast_to(a[s:s + 1], (BF16_ROWS, tt))
        return jnp.where(row == 0, pick(g_hi),
                         jnp.where(row == 1, pick(g_mid),
                                   jnp.where(row == 2, pick(g_lo), _ones_rows(row, B_BIAS_ROWS))))

    bkn = _head_norm(_proj_t(wt_ref, EVEN["bk"], xn), bkg_ref, 1.0)
    _store_keys(bk_ref, bkn, aug_b)
    self_b = jnp.sum(bqn * bkn, axis=1, keepdims=True)
    for s in range(B_HEADS):
        bm_ref[0, s] = self_b[s] + gate[s:s + 1]
    _store_heads(bsg_ref, _silu(_proj_t(wt_ref, EVEN["bg"], xn)))
    _store_heads(av_ref, _proj_t(wt_ref, EVEN["av"], xn))
    _store_heads(bv_ref, _proj_t(wt_ref, EVEN["bv"], xn))


def _proj_even(x, ln_g, wt, wf, aqg, akg, bqg, bkg, bfb):
    bsz, seq, _ = x.shape
    tt = PROJ_TOKENS
    col = lambda n: pl.BlockSpec((n, 1), lambda b, t: (0, 0))
    fm = lambda n, d: pl.BlockSpec((1, n, d, tt), lambda b, t: (b, 0, 0, t))
    km = lambda n: pl.BlockSpec((1, n, tt, KPAD), lambda b, t: (b, 0, t, 0))
    fm_shape = lambda n, d: jax.ShapeDtypeStruct((bsz, n, d, seq), bf16)
    km_shape = lambda n: jax.ShapeDtypeStruct((bsz, n, seq, KPAD), bf16)
    return pl.pallas_call(
        _proj_even_kernel,
        grid=(bsz, seq // tt),
        in_specs=[
            pl.BlockSpec((1, tt, D_MODEL), lambda b, t: (b, t, 0)),
            pl.BlockSpec((1, D_MODEL), lambda b, t: (0, 0)),
            pl.BlockSpec(wt.shape, lambda b, t: (0, 0)),
            pl.BlockSpec(wf.shape, lambda b, t: (0, 0)),
            col(HEAD_DIM), col(HEAD_DIM), col(HEAD_DIM), col(HEAD_DIM), col(B_HEADS),
            pl.BlockSpec((tt, tt), lambda b, t: (0, 0)),
        ],
        out_specs=[fm(A_STREAMS, HEAD_DIM), km(A_STREAMS), fm(A_HEADS, A_VDIM), fm(A_HEADS, A_VDIM),
                   fm(B_HEADS, HEAD_DIM), km(B_HEADS), fm(B_HEADS, HEAD_DIM), fm(B_HEADS, HEAD_DIM),
                   fm(A_STREAMS, 1), fm(B_HEADS, 1)],
        out_shape=[fm_shape(A_STREAMS, HEAD_DIM), km_shape(A_STREAMS), fm_shape(A_HEADS, A_VDIM),
                   fm_shape(A_HEADS, A_VDIM), fm_shape(B_HEADS, HEAD_DIM), km_shape(B_HEADS),
                   fm_shape(B_HEADS, HEAD_DIM), fm_shape(B_HEADS, HEAD_DIM),
                   jax.ShapeDtypeStruct((bsz, A_STREAMS, 1, seq), f32),
                   jax.ShapeDtypeStruct((bsz, B_HEADS, 1, seq), f32)],
        scratch_shapes=[pltpu.VMEM((B_HEADS, 1), f32)],
        compiler_params=pltpu.CompilerParams(
            dimension_semantics=("arbitrary", "arbitrary"), vmem_limit_bytes=VMEM_LIMIT),
        name="proj_even",
    )(x, ln_g, wt, wf, aqg, akg, bqg, bkg, bfb,
      jnp.asarray(np.triu(np.ones((tt, tt), np.float32)), bf16))


def _proj_odd_kernel(m1_ref, m2_ref, wo_ref, x_ref, lng_ref, wt_ref, cqg_ref, ckg_ref, dqg_ref,
                     dkg_ref, x1_ref, cq_ref, ck_ref, cv_ref, csg_ref, dq_ref, dk_ref, dv_ref,
                     dsg_ref, cm_ref, dm_ref):
    tt = x_ref.shape[1]
    x1 = _residual_add(m1_ref, m2_ref, wo_ref, x_ref)
    x1_ref[0] = x1
    xn = _rms_rows(x1, lng_ref)
    ones = _ones_rows(lax.broadcasted_iota(jnp.int32, (BF16_ROWS, tt), 0), 0)
    aug = lambda s: ones

    cqn = _head_norm(_proj_t(wt_ref, ODD["cq"], xn), cqg_ref, QK_SCALE)
    cq_ref[0] = cqn.astype(bf16)
    ckn = _head_norm(_proj_t(wt_ref, ODD["ck"], xn), ckg_ref, 1.0)
    _store_keys(ck_ref, ckn, aug)
    cm_ref[0] = jnp.sum(cqn.reshape(C_KV_HEADS, C_GROUP, HEAD_DIM, tt) * ckn[:, None], axis=2,
                        keepdims=True).reshape(C_HEADS, 1, tt)
    _store_heads(cv_ref, _proj_t(wt_ref, ODD["cv"], xn))
    _store_heads(csg_ref, _silu(_proj_t(wt_ref, ODD["cg"], xn)))
    dqn = _head_norm(_proj_t(wt_ref, ODD["dq"], xn), dqg_ref, QK_SCALE)
    dq_ref[0] = dqn.astype(bf16)
    dkn = _head_norm(_proj_t(wt_ref, ODD["dk"], xn), dkg_ref, 1.0)
    _store_keys(dk_ref, dkn, aug)
    dm_ref[0] = jnp.sum(dqn * dkn, axis=1, keepdims=True)
    _store_heads(dsg_ref, _silu(_proj_t(wt_ref, ODD["dg"], xn)))
    _store_heads(dv_ref, _proj_t(wt_ref, ODD["dv"], xn))


def _proj_odd(m1, m2, wo_t, x, ln_g, wt, cqg, ckg, dqg, dkg):
    bsz, seq, _ = x.shape
    tt = PROJ_TOKENS
    half = m1.shape[1]
    col = lambda n: pl.BlockSpec((n, 1), lambda b, t: (0, 0))
    rows = pl.BlockSpec((1, tt, D_MODEL), lambda b, t: (b, t, 0))
    fm = lambda n: pl.BlockSpec((1, n, HEAD_DIM, tt), lambda b, t: (b, 0, 0, t))
    km = lambda n: pl.BlockSpec((1, n, tt, KPAD), lambda b, t: (b, 0, t, 0))
    fm_shape = lambda n: jax.ShapeDtypeStruct((bsz, n, HEAD_DIM, seq), bf16)
    km_shape = lambda n: jax.ShapeDtypeStruct((bsz, n, seq, KPAD), bf16)
    return pl.pallas_call(
        _proj_odd_kernel,
        grid=(bsz, seq // tt),
        in_specs=[
            pl.BlockSpec((1, half, tt), lambda b, t: (b, 0, t)),
            pl.BlockSpec((1, half, tt), lambda b, t: (b, 0, t)),
            pl.BlockSpec(wo_t.shape, lambda b, t: (0, 0)),
            rows,
            pl.BlockSpec((1, D_MODEL), lambda b, t: (0, 0)),
            pl.BlockSpec(wt.shape, lambda b, t: (0, 0)),
            col(HEAD_DIM), col(HEAD_DIM), col(HEAD_DIM), col(HEAD_DIM),
        ],
        out_specs=[rows, fm(C_HEADS), km(C_KV_HEADS), fm(C_KV_HEADS), fm(C_HEADS),
                   fm(D_HEADS), km(D_HEADS), fm(D_HEADS), fm(D_HEADS),
                   pl.BlockSpec((1, C_HEADS, 1, tt), lambda b, t: (b, 0, 0, t)),
                   pl.BlockSpec((1, D_HEADS, 1, tt), lambda b, t: (b, 0, 0, t))],
        out_shape=[jax.ShapeDtypeStruct(x.shape, f32),
                   fm_shape(C_HEADS), km_shape(C_KV_HEADS), fm_shape(C_KV_HEADS), fm_shape(C_HEADS),
                   fm_shape(D_HEADS), km_shape(D_HEADS), fm_shape(D_HEADS), fm_shape(D_HEADS),
                   jax.ShapeDtypeStruct((bsz, C_HEADS, 1, seq), f32),
                   jax.ShapeDtypeStruct((bsz, D_HEADS, 1, seq), f32)],
        compiler_params=pltpu.CompilerParams(
            dimension_semantics=("arbitrary", "arbitrary"), vmem_limit_bytes=VMEM_LIMIT),
        name="proj_odd",
    )(m1, m2, wo_t, x, ln_g, wt, cqg, ckg, dqg, dkg)


def _online_step(carry, s, v):
    m, l, acc = carry
    m_new = jnp.maximum(m, jnp.max(s, axis=0, keepdims=True))
    p = jnp.exp2(s - m_new)
    alpha = jnp.exp2(m - m_new)
    l = alpha * l + jnp.sum(p, axis=0, keepdims=True)
    acc = alpha * acc + jnp.dot(v, p.astype(bf16), preferred_element_type=f32)
    return m_new, l, acc


def _colsum8(p):
    return p.reshape(p.shape[0] // SUBLANES, SUBLANES, p.shape[1]).sum(axis=0)


def _augment_q(q, aug_col, n_bias, m=None):
    tq = q.shape[1]
    aug = jnp.broadcast_to(aug_col, (KPAD - HEAD_DIM, tq))
    if m is not None:
        row = lax.broadcasted_iota(jnp.int32, aug.shape, 0)
        pieces = _split3(-m)
        for r in range(MAX_ROWS):
            aug = jnp.where(row == n_bias + r, pieces[r], aug)
    return jnp.concatenate([q, aug.astype(bf16)], axis=0)


def _staggered(n, scores, finish):
    out, pending = [], scores(0)
    for c in range(1, n):
        nxt = scores(c)
        out.append(finish(c - 1, pending))
        pending = nxt
    out.append(finish(n - 1, pending))
    return out


def _diag_full(chains, d0, tq):
    def scores(c):
        k_at, _, q, aug_col, n_bias, diag_bias, _ = chains[c]
        return jnp.dot(k_at(d0, tq), _augment_q(q, aug_col, n_bias),
                       preferred_element_type=f32) + diag_bias

    def finish(c, s):
        m = jnp.max(s, axis=0, keepdims=True)
        p = jnp.exp2(s - m)
        return m, _colsum8(p), jnp.dot(chains[c][1](d0, tq), p.astype(bf16), preferred_element_type=f32)

    return _staggered(len(chains), scores, finish)


def _diag_halves(chains, q_aug, d0, tq):
    h = tq // 2
    d1 = pl.multiple_of(d0 + h, h)

    def scores(c):
        k_at = chains[c][0]
        return (jnp.dot(k_at(d0, h), q_aug[c], preferred_element_type=f32),
                jnp.dot(k_at(d1, h), q_aug[c][:, h:], preferred_element_type=f32))

    def finish(c, s):
        v_at, bias = chains[c][1], chains[c][5][:h, :h]
        p0 = jnp.concatenate([jnp.exp2(s[0][:, :h] + bias), jnp.exp2(s[0][:, h:])], axis=1)
        p1 = jnp.exp2(s[1] + bias)
        l0 = _colsum8(p0)
        a0 = jnp.dot(v_at(d0, h), p0.astype(bf16), preferred_element_type=f32)
        a1 = jnp.dot(v_at(d1, h), p1.astype(bf16), preferred_element_type=f32)
        return (jnp.concatenate([l0[:, :h], l0[:, h:] + _colsum8(p1)], axis=1),
                jnp.concatenate([a0[:, :h], a0[:, h:] + a1], axis=1))

    return _staggered(len(chains), scores, finish)


def _causal_sweep(chains, i, fixed_max, l_s, acc_s):
    n = len(chains)
    tq = chains[0][2].shape[1]
    d0 = pl.multiple_of(i * tq, tq)
    n_tiles = i * (tq // ATT_TK)
    tile = lambda j: (pl.multiple_of(j * ATT_TK, ATT_TK), ATT_TK)

    if fixed_max:
        q_aug = [_augment_q(q, aug_col, n_bias, m) for _, _, q, aug_col, n_bias, _, m in chains]

        for c, (l, acc) in enumerate(_diag_halves(chains, q_aug, d0, tq)):
            l_s[c] = l
            acc_s[c] = acc

        def body(j, carry):
            def finish(c, s):
                p = jnp.exp2(s)
                l_s[c] += _colsum8(p)
                acc_s[c] += jnp.dot(chains[c][1](*tile(j)), p.astype(bf16), preferred_element_type=f32)

            scores = lambda c: jnp.dot(chains[c][0](*tile(j)), q_aug[c], preferred_element_type=f32)
            _staggered(n, scores, finish)
            return carry

        lax.fori_loop(0, n_tiles, body, 0)
        return [(acc_s[c], jnp.sum(l_s[c], axis=0, keepdims=True)) for c in range(n)]

    q_aug = [_augment_q(q, aug_col, n_bias) for _, _, q, aug_col, n_bias, _, _ in chains]

    def body(j, carries):
        scores = lambda c: jnp.dot(chains[c][0](*tile(j)), q_aug[c], preferred_element_type=f32)
        finish = lambda c, s: _online_step(carries[c], s, chains[c][1](*tile(j)))
        return tuple(_staggered(n, scores, finish))

    init = tuple((m, jnp.sum(l, axis=0, keepdims=True), acc) for m, l, acc in _diag_full(chains, d0, tq))
    return [(acc, l) for _, l, acc in lax.fori_loop(0, n_tiles, body, init)]


def _either_sweep(fixed_ref, run):
    @pl.when(fixed_ref[0] != 0)
    def _():
        run(True)

    @pl.when(fixed_ref[0] == 0)
    def _():
        run(False)


def _attn_a_kernel(fixed_ref, q_ref, k_ref, v_ref, sg_ref, m_ref, dtab_ref, qaug_ref, subg_ref,
                   lamv_ref, o_ref, l_s, acc_s, *, lam_init):
    i = pl.program_id(2)
    chains = []
    for h in range(A_HEADS_PER_STEP):
        v_at = lambda start, size, h=h: v_ref[0, h, :, pl.ds(start, size)]
        for c in range(2):
            s = 2 * h + c
            k_at = lambda start, size, s=s: k_ref[0, s, pl.ds(start, size), :]
            chains.append((k_at, v_at, q_ref[0, s], qaug_ref[h], A_BIAS_ROWS, dtab_ref[h],
                           m_ref[0, s]))

    def run(fixed_max):
        lv = lamv_ref[...]
        lam = (jnp.exp(jnp.sum(lv[0:1] * lv[1:2], axis=1, keepdims=True))
               - jnp.exp(jnp.sum(lv[2:3] * lv[3:4], axis=1, keepdims=True)) + lam_init)
        outs = [acc * (1.0 / l) for acc, l in _causal_sweep(chains, i, fixed_max, l_s, acc_s)]
        for h in range(A_HEADS_PER_STEP):
            o = outs[2 * h] - lam * outs[2 * h + 1]
            ms = jnp.mean(o * o, axis=0, keepdims=True)
            y = o * lax.rsqrt(ms + NORM_EPS) * (subg_ref[...] * (1.0 - lam_init))
            o_ref[0, h] = (y * sg_ref[0, h].astype(f32)).astype(bf16)

    _either_sweep(fixed_ref, run)


def _attn_a(fixed, aq, ak, av, asg, am, dtab, qaug, subg, lamv, lam_init):
    bsz, _, _, seq = aq.shape
    nq = seq // ATT_TQ
    hs = A_HEADS_PER_STEP
    return pl.pallas_call(
        functools.partial(_attn_a_kernel, lam_init=lam_init),
        grid=(bsz, A_HEADS // hs, nq),
        in_specs=[
            pl.BlockSpec(memory_space=pltpu.SMEM),
            pl.BlockSpec((1, 2 * hs, HEAD_DIM, ATT_TQ), lambda b, h, i: (b, h, 0, i)),
            pl.BlockSpec((1, 2 * hs, seq, KPAD), lambda b, h, i: (b, h, 0, 0)),
            pl.BlockSpec((1, hs, A_VDIM, seq), lambda b, h, i: (b, h, 0, 0)),
            pl.BlockSpec((1, hs, A_VDIM, ATT_TQ), lambda b, h, i: (b, h, 0, i)),
            pl.BlockSpec((1, 2 * hs, 1, ATT_TQ), lambda b, h, i: (b, h, 0, i)),
            pl.BlockSpec((hs, ATT_TQ, ATT_TQ), lambda b, h, i: (h, 0, 0)),
            pl.BlockSpec((hs, KPAD - HEAD_DIM, 1), lambda b, h, i: (h, 0, 0)),
            pl.BlockSpec((A_VDIM, 1), lambda b, h, i: (0, 0)),
            pl.BlockSpec(lamv.shape, lambda b, h, i: (0, 0)),
        ],
        out_specs=pl.BlockSpec((1, hs, A_VDIM, ATT_TQ), lambda b, h, i: (b, h, 0, i)),
        out_shape=jax.ShapeDtypeStruct((bsz, A_HEADS, A_VDIM, seq), bf16),
        scratch_shapes=[pltpu.VMEM((2 * hs, SUBLANES, ATT_TQ), f32),
                        pltpu.VMEM((2 * hs, A_VDIM, ATT_TQ), f32)],
        compiler_params=pltpu.CompilerParams(
            dimension_semantics=("arbitrary", "arbitrary", "arbitrary"),
            vmem_limit_bytes=VMEM_LIMIT),
        name="attn_a",
    )(fixed, aq, ak, av, asg, am, dtab, qaug, subg, lamv)


def _attn_b_kernel(fixed_ref, q_ref, k_ref, v_ref, sg_ref, m_ref, qaug_ref, o_ref, l_s, acc_s):
    i = pl.program_id(2)

    def run(fixed_max):
        causal = jnp.where(lax.broadcasted_iota(jnp.int32, (ATT_TQ, ATT_TQ), 0)
                           <= lax.broadcasted_iota(jnp.int32, (ATT_TQ, ATT_TQ), 1), 0.0, NEG)
        chains = []
        for h in range(B_HEADS_PER_STEP):
            k_at = lambda start, size, h=h: k_ref[0, h, pl.ds(start, size), :]
            v_at = lambda start, size, h=h: v_ref[0, h, :, pl.ds(start, size)]
            chains.append((k_at, v_at, q_ref[0, h], qaug_ref[...], B_BIAS_ROWS, causal, m_ref[0, h]))
        for h, (acc, l) in enumerate(_causal_sweep(chains, i, fixed_max, l_s, acc_s)):
            o_ref[0, h] = (acc * (1.0 / l) * sg_ref[0, h].astype(f32)).astype(bf16)

    _either_sweep(fixed_ref, run)


def _attn_b(fixed, bq, bk, bv, bsg, bm, qaug):
    bsz, _, _, seq = bq.shape
    nq = seq // ATT_TQ
    hs = B_HEADS_PER_STEP
    return pl.pallas_call(
        _attn_b_kernel,
        grid=(bsz, B_HEADS // hs, nq),
        in_specs=[
            pl.BlockSpec(memory_space=pltpu.SMEM),
            pl.BlockSpec((1, hs, HEAD_DIM, ATT_TQ), lambda b, h, i: (b, h, 0, i)),
            pl.BlockSpec((1, hs, seq, KPAD), lambda b, h, i: (b, h, 0, 0)),
            pl.BlockSpec((1, hs, HEAD_DIM, seq), lambda b, h, i: (b, h, 0, 0)),
            pl.BlockSpec((1, hs, HEAD_DIM, ATT_TQ), lambda b, h, i: (b, h, 0, i)),
            pl.BlockSpec((1, hs, 1, ATT_TQ), lambda b, h, i: (b, h, 0, i)),
            pl.BlockSpec((KPAD - HEAD_DIM, 1), lambda b, h, i: (0, 0)),
        ],
        out_specs=pl.BlockSpec((1, hs, HEAD_DIM, ATT_TQ), lambda b, h, i: (b, h, 0, i)),
        out_shape=jax.ShapeDtypeStruct((bsz, B_HEADS, HEAD_DIM, seq), bf16),
        scratch_shapes=[pltpu.VMEM((hs, SUBLANES, ATT_TQ), f32), pltpu.VMEM((hs, HEAD_DIM, ATT_TQ), f32)],
        compiler_params=pltpu.CompilerParams(
            dimension_semantics=("arbitrary", "arbitrary", "arbitrary"),
            vmem_limit_bytes=VMEM_LIMIT),
        name="attn_b",
    )(fixed, bq, bk, bv, bsg, bm, qaug)


def _band_kernel(fixed_ref, q_ref, k_ref, v_ref, sg_ref, m_ref, tab_ref, sink_ref, o_ref,
                 *, group, back, tq):
    blocks = tq // LANES
    band = (back + blocks) * LANES
    tiles = q_ref.shape[3] // tq
    work = [(u, h) for u in range(tiles) for h in range(k_ref.shape[1])]
    grouped = lambda ref, u, h: jnp.concatenate(
        [ref[0, h * group + g, :, u * tq:(u + 1) * tq] for g in range(group)], axis=1)

    def window(u):
        first = (pl.program_id(2) * tiles + u) * blocks
        return (pl.multiple_of(jnp.maximum(first - back, 0) * LANES, LANES),
                pl.multiple_of(jnp.maximum(back - first, 0) * LANES, LANES))

    def run(fixed_max):
        def scores(w):
            u, h = work[w]
            k_start, tab_start = window(u)
            q = _augment_q(grouped(q_ref, u, h), 0.0, 0, grouped(m_ref, u, h) if fixed_max else None)
            s = jnp.dot(k_ref[0, h, pl.ds(k_start, band), :], q, preferred_element_type=f32)
            return s + tab_ref[h, pl.ds(tab_start, band), :]

        def finish(w, s):
            u, h = work[w]
            k_start, _ = window(u)
            if fixed_max:
                m = grouped(m_ref, u, h)
                p = jnp.exp2(s)
            else:
                m = jnp.maximum(jnp.max(s, axis=0, keepdims=True), sink_ref[h])
                p = jnp.exp2(s - m)
            l = jnp.sum(p, axis=0, keepdims=True) + jnp.exp2(sink_ref[h] - m)
            o = jnp.dot(v_ref[0, h, :, pl.ds(k_start, band)], p.astype(bf16),
                        preferred_element_type=f32) * (1.0 / l)
            for g in range(group):
                hq = h * group + g
                gate = sg_ref[0, hq, :, u * tq:(u + 1) * tq].astype(f32)
                o_ref[0, hq, :, u * tq:(u + 1) * tq] = (o[:, g * tq:(g + 1) * tq] * gate).astype(bf16)

        _staggered(len(work), scores, finish)

    _either_sweep(fixed_ref, run)


def _band_attn(fixed, q, k, v, sg, m, tab, sink, group, back, tq, kv_per_step, tiles_per_step, name):
    bsz, nheads, _, seq = q.shape
    hs = kv_per_step
    tile = tq * tiles_per_step
    return pl.pallas_call(
        functools.partial(_band_kernel, group=group, back=back, tq=tq),
        grid=(nheads // (group * hs), bsz, seq // tile),
        in_specs=[
            pl.BlockSpec(memory_space=pltpu.SMEM),
            pl.BlockSpec((1, hs * group, HEAD_DIM, tile), lambda h, b, i: (b, h, 0, i)),
            pl.BlockSpec((1, hs, seq, KPAD), lambda h, b, i: (b, h, 0, 0)),
            pl.BlockSpec((1, hs, HEAD_DIM, seq), lambda h, b, i: (b, h, 0, 0)),
            pl.BlockSpec((1, hs * group, HEAD_DIM, tile), lambda h, b, i: (b, h, 0, i)),
            pl.BlockSpec((1, hs * group, 1, tile), lambda h, b, i: (b, h, 0, i)),
            pl.BlockSpec((hs,) + tab.shape[1:], lambda h, b, i: (h, 0, 0)),
            pl.BlockSpec((hs,) + sink.shape[1:], lambda h, b, i: (h, 0, 0)),
        ],
        out_specs=pl.BlockSpec((1, hs * group, HEAD_DIM, tile), lambda h, b, i: (b, h, 0, i)),
        out_shape=jax.ShapeDtypeStruct((bsz, nheads, HEAD_DIM, seq), bf16),
        compiler_params=pltpu.CompilerParams(
            dimension_semantics=("arbitrary", "arbitrary", "arbitrary"),
            vmem_limit_bytes=VMEM_LIMIT),
        name=name,
    )(fixed, q, k, v, sg, m, tab, sink)


def _out_proj_kernel(m1_ref, m2_ref, wt_ref, x_ref, o_ref):
    o_ref[0] = _residual_add(m1_ref, m2_ref, wt_ref, x_ref)


def _out_proj(m1, m2, wt, x):
    bsz, seq, _ = x.shape
    tt = OUT_TOKENS
    half = m1.shape[1]
    return pl.pallas_call(
        _out_proj_kernel,
        grid=(bsz, seq // tt),
        in_specs=[
            pl.BlockSpec((1, half, tt), lambda b, t: (b, 0, t)),
            pl.BlockSpec((1, half, tt), lambda b, t: (b, 0, t)),
            pl.BlockSpec(wt.shape, lambda b, t: (0, 0)),
            pl.BlockSpec((1, tt, D_MODEL), lambda b, t: (b, t, 0)),
        ],
        out_specs=pl.BlockSpec((1, tt, D_MODEL), lambda b, t: (b, t, 0)),
        out_shape=jax.ShapeDtypeStruct(x.shape, f32),
        compiler_params=pltpu.CompilerParams(
            dimension_semantics=("arbitrary", "arbitrary"), vmem_limit_bytes=VMEM_LIMIT),
        name="out_proj",
    )(m1, m2, wt, x)


def _alibi_slopes(n):
    return 2.0 ** (-8.0 * np.arange(1, n + 1, dtype=np.float64) / n)


def _np_split3(v):
    v = np.asarray(v, np.float32)
    to_bf = lambda a: a.astype(bf16).astype(np.float32)
    hi = to_bf(v)
    mid = to_bf(v - hi)
    lo = to_bf(v - hi - mid)
    return hi, mid, lo


def _a_tables():
    rate = A_RATES
    qaug = np.zeros((A_HEADS, KPAD - HEAD_DIM, 1), np.float32)
    for idx, piece in enumerate(_np_split3(rate * CHUNK) + _np_split3(rate)):
        qaug[:, idx, 0] = piece
    kk = np.arange(ATT_TQ)[:, None]
    qq = np.arange(ATT_TQ)[None, :]
    future = np.maximum(kk - qq, 0).astype(np.float32)
    corr = -2.0 * rate[:, None, None] * future[None]
    allowed = (kk // CHUNK) <= (qq // CHUNK)
    dtab = np.where(allowed[None], corr, NEG).astype(np.float32)
    return jnp.asarray(qaug), jnp.asarray(dtab)


def _band_frames(back, tq):
    k_pos = np.arange(back * LANES + tq)[:, None]
    q_pos = back * LANES + np.arange(tq)[None, :]
    return q_pos - k_pos, q_pos // CHUNK - k_pos // CHUNK


def _c_tables(sinks):
    back, tq = WIN_CHUNKS * CHUNK // LANES, C_BAND_TQ
    rel, chunk_diff = _band_frames(back, tq)
    allowed = (chunk_diff >= 0) & (chunk_diff <= WIN_CHUNKS)
    slopes = _alibi_slopes(C_HEADS)
    per_head = np.where(allowed[None], -slopes[:, None, None] * np.abs(rel)[None] * LOG2E, NEG)
    tab = per_head.reshape(C_KV_HEADS, C_GROUP, *rel.shape).transpose(0, 2, 1, 3)
    tab = tab.reshape(C_KV_HEADS, rel.shape[0], C_GROUP * tq).astype(np.float32)
    tab = np.concatenate([tab, np.full((C_KV_HEADS, back * LANES, tab.shape[2]), NEG, np.float32)], 1)
    sink = jnp.repeat(sinks.astype(f32) * LOG2E, tq).reshape(C_KV_HEADS, 1, C_GROUP * tq)
    return jnp.asarray(tab), sink, back


def _d_tables(rel_table):
    back, t = D_LEFT_CHUNKS * CHUNK // LANES, D_BAND_TQ
    band = back * LANES + t
    rel, chunk_diff = _band_frames(back, t)
    allowed = (chunk_diff >= 0) & (chunk_diff <= D_LEFT_CHUNKS)
    tbl = rel_table.astype(f32) * LOG2E
    n_lo = (t - 1) - (CHUNK - 1)
    n_hi = (band - 1) - REL_MAX
    diag = jnp.concatenate([jnp.broadcast_to(tbl[:, :1], (D_HEADS, n_lo)), tbl,
                            jnp.broadcast_to(tbl[:, -1:], (D_HEADS, n_hi))], axis=1)
    m = t + LANES - 1
    blocks = []
    for kb in range(band // LANES):
        lo = rel[kb * LANES:(kb + 1) * LANES].min()
        if lo >= REL_MAX:
            blocks.append(jnp.broadcast_to(tbl[:, -1:, None], (D_HEADS, LANES, t)))
            continue
        window = diag[:, band - (kb + 1) * LANES:band - (kb + 1) * LANES + m]
        skew = jnp.broadcast_to(window[:, None, :], (D_HEADS, LANES + 1, m)).reshape(D_HEADS, -1)
        skew = skew[:, :LANES * (m + 1)].reshape(D_HEADS, LANES, m + 1)[:, :, :t]
        blocks.append(jnp.flip(skew, axis=1))
    blocks += [jnp.zeros((D_HEADS, LANES, t), f32)] * back
    allowed = np.concatenate([allowed, np.zeros((back * LANES, t), bool)], axis=0)
    tab = jnp.where(jnp.asarray(allowed)[None], jnp.concatenate(blocks, axis=1), NEG)
    sink = jnp.full((D_HEADS, 1, t), NEG, f32)
    return tab, sink, back


def _fixed_max_ok(q_gain, k_gain, bias_range=0.0):
    spread = (2.0 * 1.02 * QK_SCALE * HEAD_DIM
              * jnp.max(jnp.abs(q_gain.astype(f32))) * jnp.max(jnp.abs(k_gain.astype(f32))))
    return (spread + bias_range <= FIXED_MAX_LIMIT).astype(jnp.int32).reshape(1)


def _pad_rows(w_t, rows):
    return jnp.pad(w_t, ((0, rows - w_t.shape[0]), (0, 0)))


def _even_layer(x, ln_g, w_in, w_out, a_qn_g, a_kn_g, a_lq1, a_lk1, a_lq2, a_lk2, a_subln_g,
                b_qn_g, b_kn_g, b_f_bias, layer_idx):
    bsz, seq, _ = x.shape
    colv = lambda v: v.astype(f32).reshape(-1, 1)
    n_wide = EVEN["bf"][0]
    wt = w_in[:, :n_wide].T.astype(bf16)
    wf = _pad_rows(w_in[:, n_wide:].T.astype(bf16), BF16_ROWS)
    aq, ak, av, asg, bq, bk, bv, bsg, am, bm = _proj_even(
        x, ln_g.astype(f32).reshape(1, -1), wt, wf, colv(a_qn_g), colv(a_kn_g), colv(b_qn_g),
        colv(b_kn_g), colv(b_f_bias))
    lam_init = 0.8 - 0.6 * math.exp(-0.3 * layer_idx)
    qaug_a, dtab = _a_tables()
    lamv = jnp.stack([a_lq1, a_lk1, a_lq2, a_lk2]).astype(f32)
    mix_a = _attn_a(_fixed_max_ok(a_qn_g, a_kn_g), aq, ak, av, asg, am, dtab, qaug_a,
                    colv(a_subln_g), lamv, lam_init)
    qaug_b = np.zeros((KPAD - HEAD_DIM, 1), np.float32)
    qaug_b[:B_BIAS_ROWS] = 1.0
    mix_b = _attn_b(_fixed_max_ok(b_qn_g, b_kn_g), bq, bk, bv, bsg, bm, jnp.asarray(qaug_b))
    return mix_a.reshape(bsz, -1, seq), mix_b.reshape(bsz, -1, seq), w_out.T.astype(bf16), x


def _odd_layer(pending, ln_g, w_in, w_out, c_qn_g, c_kn_g, c_sinks, d_qn_g, d_kn_g, d_rel_bias):
    bsz, seq, _ = pending[3].shape
    colv = lambda v: v.astype(f32).reshape(-1, 1)
    x, cq, ck, cv, csg, dq, dk, dv, dsg, cm, dm = _proj_odd(
        *pending, ln_g.astype(f32).reshape(1, -1), w_in.T.astype(bf16), colv(c_qn_g), colv(c_kn_g),
        colv(d_qn_g), colv(d_kn_g))
    tab_c, sink_c, back_c = _c_tables(c_sinks)
    fixed_c = _fixed_max_ok(c_qn_g, c_kn_g, LOG2E * jnp.maximum(jnp.max(c_sinks.astype(f32)), 0.0))
    mix_c = _band_attn(fixed_c, cq, ck, cv, csg, cm, tab_c, sink_c, C_GROUP, back_c, C_BAND_TQ,
                       C_KV_HEADS, C_TILES_PER_STEP, "attn_c")
    tab_d, sink_d, back_d = _d_tables(d_rel_bias)
    fixed_d = _fixed_max_ok(d_qn_g, d_kn_g, LOG2E * jnp.max(jnp.abs(d_rel_bias.astype(f32))))
    mix_d = _band_attn(fixed_d, dq, dk, dv, dsg, dm, tab_d, sink_d, 1, back_d, D_BAND_TQ,
                       D_HEADS_PER_STEP, D_TILES_PER_STEP, "attn_d")
    return mix_c.reshape(bsz, -1, seq), mix_d.reshape(bsz, -1, seq), w_out.T.astype(bf16), x


def kernel(x, even_ln_g, even_w_in, even_w_out, a_q_norm_g, a_k_norm_g, a_lambda_q1, a_lambda_k1, a_lambda_q2, a_lambda_k2, a_subln_g, b_q_norm_g, b_k_norm_g, b_forget_bias, odd_ln_g, odd_w_in, odd_w_out, c_q_norm_g, c_k_norm_g, c_sinks, d_q_norm_g, d_k_norm_g, d_rel_bias):
    depth = even_ln_g.shape[0] + odd_ln_g.shape[0]
    seq = x.shape[1]
    assert x.shape[2] == D_MODEL and even_w_in.shape[2] == P_EVEN and odd_w_in.shape[2] == P_ODD
    for tile in (PROJ_TOKENS, OUT_TOKENS, ATT_TQ, C_BAND_TQ * C_TILES_PER_STEP,
                 D_BAND_TQ * D_TILES_PER_STEP):
        assert seq % tile == 0, (seq, tile)
    pending = None
    for i in range(depth):
        j = i // 2
        if i % 2 == 0:
            if pending is not None:
                x = _out_proj(*pending)
            pending = _even_layer(x, even_ln_g[j], even_w_in[j], even_w_out[j], a_q_norm_g[j],
                                  a_k_norm_g[j], a_lambda_q1[j], a_lambda_k1[j], a_lambda_q2[j],
                                  a_lambda_k2[j], a_subln_g[j], b_q_norm_g[j], b_k_norm_g[j],
                                  b_forget_bias[j], i)
        else:
            pending = _odd_layer(pending, odd_ln_g[j], odd_w_in[j], odd_w_out[j], c_q_norm_g[j],
                                 c_k_norm_g[j], c_sinks[j], d_q_norm_g[j], d_k_norm_g[j],
                                 d_rel_bias[j])
    return _out_proj(*pending)
```

```python
import functools
import math

import numpy as np
import jax
import jax.numpy as jnp
from jax import lax
from jax.experimental import pallas as pl
from jax.experimental.pallas import tpu as pltpu

D_MODEL = 1024
CHUNK = 64
HEAD_DIM = 64
NORM_EPS = 1e-6

A_HEADS = 4
A_STREAMS = 2 * A_HEADS
A_VDIM = 2 * HEAD_DIM
B_HEADS = 8
C_HEADS = 8
C_KV_HEADS = 2
C_GROUP = C_HEADS // C_KV_HEADS
WIN_CHUNKS = 2
D_HEADS = 8
D_LEFT_CHUNKS = 8
REL_MAX = 256


def _row_ranges(names, sizes):
    stops = np.cumsum(sizes)
    return {n: (int(b - w), int(b)) for n, w, b in zip(names, sizes, stops)}


EVEN = _row_ranges(("aq", "ak", "av", "ag", "bq", "bk", "bv", "bg", "bf"),
                   (A_STREAMS * HEAD_DIM,) * 2 + (A_HEADS * A_VDIM,) * 2 + (B_HEADS * HEAD_DIM,) * 4
                   + (B_HEADS,))
ODD = _row_ranges(("cq", "ck", "cv", "cg", "dq", "dk", "dv", "dg"),
                  (C_HEADS * HEAD_DIM, C_KV_HEADS * HEAD_DIM, C_KV_HEADS * HEAD_DIM, C_HEADS * HEAD_DIM)
                  + (D_HEADS * HEAD_DIM,) * 4)
P_EVEN = EVEN["bf"][1]
P_ODD = ODD["dg"][1]

LOG2E = 1.4426950408889634
QK_SCALE = HEAD_DIM ** -0.5 * LOG2E
NEG = -1e30
A_RATES = (2.0 ** (-8.0 * np.arange(1, A_HEADS + 1) / A_HEADS) * LOG2E).astype(np.float32)

LANES = 128
SUBLANES = 8
KPAD = 128
BF16_ROWS = 16

PROJ_TOKENS = 512
OUT_TOKENS = 1024
ATT_TQ = 512
ATT_TK = 512
A_HEADS_PER_STEP = 4
B_HEADS_PER_STEP = 8
A_BIAS_ROWS = 6
B_BIAS_ROWS = 3
MAX_ROWS = 3
FIXED_MAX_LIMIT = 96.0
C_BAND_TQ = 128
D_BAND_TQ = 256
D_HEADS_PER_STEP = 8
C_TILES_PER_STEP = 8
D_TILES_PER_STEP = 4
VMEM_LIMIT = 56 * 1024 * 1024

f32 = jnp.float32
bf16 = jnp.bfloat16


def _split3(v):
    hi = v.astype(bf16).astype(f32)
    r = v - hi
    mid = r.astype(bf16).astype(f32)
    lo = (r - mid).astype(bf16).astype(f32)
    return hi, mid, lo


def _silu(z):
    return z * (1.0 / (1.0 + jnp.exp(-z)))


def _rms_rows(x, g_ref):
    ms = jnp.mean(x * x, axis=-1, keepdims=True)
    return (x * lax.rsqrt(ms + NORM_EPS) * g_ref[...]).astype(bf16)


def _residual_add(m1_ref, m2_ref, wt_ref, x_ref):
    half = m1_ref.shape[1]
    y_t = (jnp.dot(wt_ref[:, :half], m1_ref[0], preferred_element_type=f32)
           + jnp.dot(wt_ref[:, half:], m2_ref[0], preferred_element_type=f32))
    return x_ref[0] + y_t.T


def _proj_t(wt_ref, rows, xn):
    return lax.dot_general(wt_ref[rows[0]:rows[1], :], xn, (((1,), (1,)), ((), ())),
                           preferred_element_type=f32)


def _head_norm(z_t, gain_col, mult):
    n = z_t.shape[0] // HEAD_DIM
    z3 = z_t.reshape(n, HEAD_DIM, z_t.shape[1])
    ms = jnp.mean(z3 * z3, axis=1, keepdims=True)
    return z3 * lax.rsqrt(ms + NORM_EPS) * (gain_col[...] * mult)[None]


def _ones_rows(row, first):
    return jnp.where((row >= first) & (row < first + MAX_ROWS), 1.0, 0.0)


def _store_heads(o_ref, z_t):
    o_ref[0] = z_t.reshape(o_ref.shape[1], o_ref.shape[2], z_t.shape[1]).astype(bf16)


def _store_keys(k_ref, kn, aug_fn):
    n, _, t = kn.shape
    zeros = jnp.zeros((KPAD - HEAD_DIM - BF16_ROWS, t), f32)
    for s in range(n):
        blk = jnp.concatenate([kn[s], aug_fn(s), zeros], axis=0)
        k_ref[0, s] = blk.T.astype(bf16)


def _proj_even_kernel(x_ref, lng_ref, wt_ref, wf_ref, aqg_ref, akg_ref, bqg_ref, bkg_ref, bfb_ref,
                      tri_ref, aq_ref, ak_ref, av_ref, asg_ref, bq_ref, bk_ref, bv_ref, bsg_ref,
                      am_ref, bm_ref, cum_ref):
    t = pl.program_id(1)
    tt = x_ref.shape[1]
    xn = _rms_rows(x_ref[0], lng_ref)
    row = lax.broadcasted_iota(jnp.int32, (BF16_ROWS, tt), 0)

    z = _proj_t(wf_ref, (0, BF16_ROWS), xn)[:B_HEADS] + bfb_ref[...]
    aqn = _head_norm(_proj_t(wt_ref, EVEN["aq"], xn), aqg_ref, QK_SCALE)
    aq_ref[0] = aqn.astype(bf16)

    pos = t * tt + lax.broadcasted_iota(jnp.int32, (BF16_ROWS, tt), 1)
    pos_a = lax.shift_right_logical(pos, int(math.log2(CHUNK))).astype(f32)
    pos_b = lax.bitwise_and(pos, CHUNK - 1).astype(f32)
    aug_a = jnp.where(row < A_BIAS_ROWS // 2, pos_a,
                      jnp.where(row < A_BIAS_ROWS, pos_b, _ones_rows(row, A_BIAS_ROWS)))
    akn = _head_norm(_proj_t(wt_ref, EVEN["ak"], xn), akg_ref, 1.0)
    _store_keys(ak_ref, akn, lambda s: aug_a)
    self_a = jnp.sum(aqn * akn, axis=1, keepdims=True)
    for s in range(A_STREAMS):
        am_ref[0, s] = self_a[s] + float(A_RATES[s // 2]) * pos[:1].astype(f32)

    _store_heads(asg_ref, _silu(_proj_t(wt_ref, EVEN["ag"], xn)))
    bqn = _head_norm(_proj_t(wt_ref, EVEN["bq"], xn), bqg_ref, QK_SCALE)
    bq_ref[0] = bqn.astype(bf16)

    log_f = jnp.minimum(z, 0.0) - jnp.log(1.0 + jnp.exp(-jnp.abs(z)))
    pieces = jnp.concatenate(_split3(log_f) + (jnp.zeros_like(log_f),), axis=0).astype(bf16)
    part = jnp.dot(pieces, tri_ref[...], preferred_element_type=f32)
    local = part[:B_HEADS] + part[B_HEADS:2 * B_HEADS] + part[2 * B_HEADS:3 * B_HEADS]

    @pl.when(t == 0)
    def _():
        cum_ref[...] = jnp.zeros_like(cum_ref)

    cum = cum_ref[...] + local
    cum_ref[...] = cum[:, tt - 1:tt]
    gate = -LOG2E * cum
    g_hi, g_mid, g_lo = _split3(gate)

    def aug_b(s):
        pick = lambda a: jnp.broadcast_to(a[s:s + 1], (BF16_ROWS, tt))
        return jnp.where(row == 0, pick(g_hi),
                         jnp.where(row == 1, pick(g_mid),
                                   jnp.where(row == 2, pick(g_lo), _ones_rows(row, B_BIAS_ROWS))))

    bkn = _head_norm(_proj_t(wt_ref, EVEN["bk"], xn), bkg_ref, 1.0)
    _store_keys(bk_ref, bkn, aug_b)
    self_b = jnp.sum(bqn * bkn, axis=1, keepdims=True)
    for s in range(B_HEADS):
        bm_ref[0, s] = self_b[s] + gate[s:s + 1]
    _store_heads(bsg_ref, _silu(_proj_t(wt_ref, EVEN["bg"], xn)))
    _store_heads(av_ref, _proj_t(wt_ref, EVEN["av"], xn))
    _store_heads(bv_ref, _proj_t(wt_ref, EVEN["bv"], xn))


def _proj_even(x, ln_g, wt, wf, aqg, akg, bqg, bkg, bfb):
    bsz, seq, _ = x.shape
    tt = PROJ_TOKENS
    col = lambda n: pl.BlockSpec((n, 1), lambda b, t: (0, 0))
    fm = lambda n, d: pl.BlockSpec((1, n, d, tt), lambda b, t: (b, 0, 0, t))
    km = lambda n: pl.BlockSpec((1, n, tt, KPAD), lambda b, t: (b, 0, t, 0))
    fm_shape = lambda n, d: jax.ShapeDtypeStruct((bsz, n, d, seq), bf16)
    km_shape = lambda n: jax.ShapeDtypeStruct((bsz, n, seq, KPAD), bf16)
    return pl.pallas_call(
        _proj_even_kernel,
        grid=(bsz, seq // tt),
        in_specs=[
            pl.BlockSpec((1, tt, D_MODEL), lambda b, t: (b, t, 0)),
            pl.BlockSpec((1, D_MODEL), lambda b, t: (0, 0)),
            pl.BlockSpec(wt.shape, lambda b, t: (0, 0)),
            pl.BlockSpec(wf.shape, lambda b, t: (0, 0)),
            col(HEAD_DIM), col(HEAD_DIM), col(HEAD_DIM), col(HEAD_DIM), col(B_HEADS),
            pl.BlockSpec((tt, tt), lambda b, t: (0, 0)),
        ],
        out_specs=[fm(A_STREAMS, HEAD_DIM), km(A_STREAMS), fm(A_HEADS, A_VDIM), fm(A_HEADS, A_VDIM),
                   fm(B_HEADS, HEAD_DIM), km(B_HEADS), fm(B_HEADS, HEAD_DIM), fm(B_HEADS, HEAD_DIM),
                   fm(A_STREAMS, 1), fm(B_HEADS, 1)],
        out_shape=[fm_shape(A_STREAMS, HEAD_DIM), km_shape(A_STREAMS), fm_shape(A_HEADS, A_VDIM),
                   fm_shape(A_HEADS, A_VDIM), fm_shape(B_HEADS, HEAD_DIM), km_shape(B_HEADS),
                   fm_shape(B_HEADS, HEAD_DIM), fm_shape(B_HEADS, HEAD_DIM),
                   jax.ShapeDtypeStruct((bsz, A_STREAMS, 1, seq), f32),
                   jax.ShapeDtypeStruct((bsz, B_HEADS, 1, seq), f32)],
        scratch_shapes=[pltpu.VMEM((B_HEADS, 1), f32)],
        compiler_params=pltpu.CompilerParams(
            dimension_semantics=("arbitrary", "arbitrary"), vmem_limit_bytes=VMEM_LIMIT),
        name="proj_even",
    )(x, ln_g, wt, wf, aqg, akg, bqg, bkg, bfb,
      jnp.asarray(np.triu(np.ones((tt, tt), np.float32)), bf16))


def _proj_odd_kernel(m1_ref, m2_ref, wo_ref, x_ref, lng_ref, wt_ref, cqg_ref, ckg_ref, dqg_ref,
                     dkg_ref, x1_ref, cq_ref, ck_ref, cv_ref, csg_ref, dq_ref, dk_ref, dv_ref,
                     dsg_ref, cm_ref, dm_ref):
    tt = x_ref.shape[1]
    x1 = _residual_add(m1_ref, m2_ref, wo_ref, x_ref)
    x1_ref[0] = x1
    xn = _rms_rows(x1, lng_ref)
    ones = _ones_rows(lax.broadcasted_iota(jnp.int32, (BF16_ROWS, tt), 0), 0)
    aug = lambda s: ones

    cqn = _head_norm(_proj_t(wt_ref, ODD["cq"], xn), cqg_ref, QK_SCALE)
    cq_ref[0] = cqn.astype(bf16)
    ckn = _head_norm(_proj_t(wt_ref, ODD["ck"], xn), ckg_ref, 1.0)
    _store_keys(ck_ref, ckn, aug)
    cm_ref[0] = jnp.sum(cqn.reshape(C_KV_HEADS, C_GROUP, HEAD_DIM, tt) * ckn[:, None], axis=2,
                        keepdims=True).reshape(C_HEADS, 1, tt)
    _store_heads(cv_ref, _proj_t(wt_ref, ODD["cv"], xn))
    _store_heads(csg_ref, _silu(_proj_t(wt_ref, ODD["cg"], xn)))
    dqn = _head_norm(_proj_t(wt_ref, ODD["dq"], xn), dqg_ref, QK_SCALE)
    dq_ref[0] = dqn.astype(bf16)
    dkn = _head_norm(_proj_t(wt_ref, ODD["dk"], xn), dkg_ref, 1.0)
    _store_keys(dk_ref, dkn, aug)
    dm_ref[0] = jnp.sum(dqn * dkn, axis=1, keepdims=True)
    _store_heads(dsg_ref, _silu(_proj_t(wt_ref, ODD["dg"], xn)))
    _store_heads(dv_ref, _proj_t(wt_ref, ODD["dv"], xn))


def _proj_odd(m1, m2, wo_t, x, ln_g, wt, cqg, ckg, dqg, dkg):
    bsz, seq, _ = x.shape
    tt = PROJ_TOKENS
    half = m1.shape[1]
    col = lambda n: pl.BlockSpec((n, 1), lambda b, t: (0, 0))
    rows = pl.BlockSpec((1, tt, D_MODEL), lambda b, t: (b, t, 0))
    fm = lambda n: pl.BlockSpec((1, n, HEAD_DIM, tt), lambda b, t: (b, 0, 0, t))
    km = lambda n: pl.BlockSpec((1, n, tt, KPAD), lambda b, t: (b, 0, t, 0))
    fm_shape = lambda n: jax.ShapeDtypeStruct((bsz, n, HEAD_DIM, seq), bf16)
    km_shape = lambda n: jax.ShapeDtypeStruct((bsz, n, seq, KPAD), bf16)
    return pl.pallas_call(
        _proj_odd_kernel,
        grid=(bsz, seq // tt),
        in_specs=[
            pl.BlockSpec((1, half, tt), lambda b, t: (b, 0, t)),
            pl.BlockSpec((1, half, tt), lambda b, t: (b, 0, t)),
            pl.BlockSpec(wo_t.shape, lambda b, t: (0, 0)),
            rows,
            pl.BlockSpec((1, D_MODEL), lambda b, t: (0, 0)),
            pl.BlockSpec(wt.shape, lambda b, t: (0, 0)),
            col(HEAD_DIM), col(HEAD_DIM), col(HEAD_DIM), col(HEAD_DIM),
        ],
        out_specs=[rows, fm(C_HEADS), km(C_KV_HEADS), fm(C_KV_HEADS), fm(C_HEADS),
                   fm(D_HEADS), km(D_HEADS), fm(D_HEADS), fm(D_HEADS),
                   pl.BlockSpec((1, C_HEADS, 1, tt), lambda b, t: (b, 0, 0, t)),
                   pl.BlockSpec((1, D_HEADS, 1, tt), lambda b, t: (b, 0, 0, t))],
        out_shape=[jax.ShapeDtypeStruct(x.shape, f32),
                   fm_shape(C_HEADS), km_shape(C_KV_HEADS), fm_shape(C_KV_HEADS), fm_shape(C_HEADS),
                   fm_shape(D_HEADS), km_shape(D_HEADS), fm_shape(D_HEADS), fm_shape(D_HEADS),
                   jax.ShapeDtypeStruct((bsz, C_HEADS, 1, seq), f32),
                   jax.ShapeDtypeStruct((bsz, D_HEADS, 1, seq), f32)],
        compiler_params=pltpu.CompilerParams(
            dimension_semantics=("arbitrary", "arbitrary"), vmem_limit_bytes=VMEM_LIMIT),
        name="proj_odd",
    )(m1, m2, wo_t, x, ln_g, wt, cqg, ckg, dqg, dkg)


def _online_step(carry, s, v):
    m, l, acc = carry
    m_new = jnp.maximum(m, jnp.max(s, axis=0, keepdims=True))
    p = jnp.exp2(s - m_new)
    alpha = jnp.exp2(m - m_new)
    l = alpha * l + jnp.sum(p, axis=0, keepdims=True)
    acc = alpha * acc + jnp.dot(v, p.astype(bf16), preferred_element_type=f32)
    return m_new, l, acc


def _colsum8(p):
    return p.reshape(p.shape[0] // SUBLANES, SUBLANES, p.shape[1]).sum(axis=0)


def _augment_q(q, aug_col, n_bias, m=None):
    tq = q.shape[1]
    aug = jnp.broadcast_to(aug_col, (KPAD - HEAD_DIM, tq))
    if m is not None:
        row = lax.broadcasted_iota(jnp.int32, aug.shape, 0)
        pieces = _split3(-m)
        for r in range(MAX_ROWS):
            aug = jnp.where(row == n_bias + r, pieces[r], aug)
    return jnp.concatenate([q, aug.astype(bf16)], axis=0)


def _staggered(n, scores, finish):
    out, pending = [], scores(0)
    for c in range(1, n):
        nxt = scores(c)
        out.append(finish(c - 1, pending))
        pending = nxt
    out.append(finish(n - 1, pending))
    return out


def _diag_full(chains, d0, tq):
    def scores(c):
        k_at, _, q, aug_col, n_bias, diag_bias, _ = chains[c]
        return jnp.dot(k_at(d0, tq), _augment_q(q, aug_col, n_bias),
                       preferred_element_type=f32) + diag_bias

    def finish(c, s):
        m = jnp.max(s, axis=0, keepdims=True)
        p = jnp.exp2(s - m)
        return m, _colsum8(p), jnp.dot(chains[c][1](d0, tq), p.astype(bf16), preferred_element_type=f32)

    return _staggered(len(chains), scores, finish)


def _diag_halves(chains, q_aug, d0, tq):
    h = tq // 2
    d1 = pl.multiple_of(d0 + h, h)

    def scores(c):
        k_at = chains[c][0]
        return (jnp.dot(k_at(d0, h), q_aug[c], preferred_element_type=f32),
                jnp.dot(k_at(d1, h), q_aug[c][:, h:], preferred_element_type=f32))

    def finish(c, s):
        v_at, bias = chains[c][1], chains[c][5][:h, :h]
        p0 = jnp.concatenate([jnp.exp2(s[0][:, :h] + bias), jnp.exp2(s[0][:, h:])], axis=1)
        p1 = jnp.exp2(s[1] + bias)
        l0 = _colsum8(p0)
        a0 = jnp.dot(v_at(d0, h), p0.astype(bf16), preferred_element_type=f32)
        a1 = jnp.dot(v_at(d1, h), p1.astype(bf16), preferred_element_type=f32)
        return (jnp.concatenate([l0[:, :h], l0[:, h:] + _colsum8(p1)], axis=1),
                jnp.concatenate([a0[:, :h], a0[:, h:] + a1], axis=1))

    return _staggered(len(chains), scores, finish)


def _causal_sweep(chains, i, fixed_max, l_s, acc_s):
    n = len(chains)
    tq = chains[0][2].shape[1]
    d0 = pl.multiple_of(i * tq, tq)
    n_tiles = i * (tq // ATT_TK)
    tile = lambda j: (pl.multiple_of(j * ATT_TK, ATT_TK), ATT_TK)

    if fixed_max:
        q_aug = [_augment_q(q, aug_col, n_bias, m) for _, _, q, aug_col, n_bias, _, m in chains]

        for c, (l, acc) in enumerate(_diag_halves(chains, q_aug, d0, tq)):
            l_s[c] = l
            acc_s[c] = acc

        def sweep(tiles):
            work = [(j, c) for j in tiles for c in range(n)]

            def finish(w, s):
                j, c = work[w]
                p = jnp.exp2(s)
                l_s[c] += _colsum8(p)
                acc_s[c] += jnp.dot(chains[c][1](*tile(j)), p.astype(bf16), preferred_element_type=f32)

            def scores(w):
                j, c = work[w]
                return jnp.dot(chains[c][0](*tile(j)), q_aug[c], preferred_element_type=f32)

            _staggered(len(work), scores, finish)

        def pair(j, carry):
            sweep((2 * j, 2 * j + 1))
            return carry

        lax.fori_loop(0, n_tiles // 2, pair, 0)

        @pl.when(n_tiles % 2 == 1)
        def _():
            sweep((n_tiles - 1,))

        return [(acc_s[c], jnp.sum(l_s[c], axis=0, keepdims=True)) for c in range(n)]

    q_aug = [_augment_q(q, aug_col, n_bias) for _, _, q, aug_col, n_bias, _, _ in chains]

    def body(j, carries):
        scores = lambda c: jnp.dot(chains[c][0](*tile(j)), q_aug[c], preferred_element_type=f32)
        finish = lambda c, s: _online_step(carries[c], s, chains[c][1](*tile(j)))
        return tuple(_staggered(n, scores, finish))

    init = tuple((m, jnp.sum(l, axis=0, keepdims=True), acc) for m, l, acc in _diag_full(chains, d0, tq))
    return [(acc, l) for _, l, acc in lax.fori_loop(0, n_tiles, body, init)]


def _either_sweep(fixed_ref, run):
    @pl.when(fixed_ref[0] != 0)
    def _():
        run(True)

    @pl.when(fixed_ref[0] == 0)
    def _():
        run(False)


def _attn_a_kernel(fixed_ref, q_ref, k_ref, v_ref, sg_ref, m_ref, dtab_ref, qaug_ref, subg_ref,
                   lamv_ref, o_ref, l_s, acc_s, *, lam_init):
    i = pl.program_id(2)
    chains = []
    for h in range(A_HEADS_PER_STEP):
        v_at = lambda start, size, h=h: v_ref[0, h, :, pl.ds(start, size)]
        for c in range(2):
            s = 2 * h + c
            k_at = lambda start, size, s=s: k_ref[0, s, pl.ds(start, size), :]
            chains.append((k_at, v_at, q_ref[0, s], qaug_ref[h], A_BIAS_ROWS, dtab_ref[h],
                           m_ref[0, s]))

    def run(fixed_max):
        lv = lamv_ref[...]
        lam = (jnp.exp(jnp.sum(lv[0:1] * lv[1:2], axis=1, keepdims=True))
               - jnp.exp(jnp.sum(lv[2:3] * lv[3:4], axis=1, keepdims=True)) + lam_init)
        outs = [acc * (1.0 / l) for acc, l in _causal_sweep(chains, i, fixed_max, l_s, acc_s)]
        for h in range(A_HEADS_PER_STEP):
            o = outs[2 * h] - lam * outs[2 * h + 1]
            ms = jnp.mean(o * o, axis=0, keepdims=True)
            y = o * lax.rsqrt(ms + NORM_EPS) * (subg_ref[...] * (1.0 - lam_init))
            o_ref[0, h] = (y * sg_ref[0, h].astype(f32)).astype(bf16)

    _either_sweep(fixed_ref, run)


def _attn_a(fixed, aq, ak, av, asg, am, dtab, qaug, subg, lamv, lam_init):
    bsz, _, _, seq = aq.shape
    nq = seq // ATT_TQ
    hs = A_HEADS_PER_STEP
    return pl.pallas_call(
        functools.partial(_attn_a_kernel, lam_init=lam_init),
        grid=(bsz, A_HEADS // hs, nq),
        in_specs=[
            pl.BlockSpec(memory_space=pltpu.SMEM),
            pl.BlockSpec((1, 2 * hs, HEAD_DIM, ATT_TQ), lambda b, h, i: (b, h, 0, i)),
            pl.BlockSpec((1, 2 * hs, seq, KPAD), lambda b, h, i: (b, h, 0, 0)),
            pl.BlockSpec((1, hs, A_VDIM, seq), lambda b, h, i: (b, h, 0, 0)),
            pl.BlockSpec((1, hs, A_VDIM, ATT_TQ), lambda b, h, i: (b, h, 0, i)),
            pl.BlockSpec((1, 2 * hs, 1, ATT_TQ), lambda b, h, i: (b, h, 0, i)),
            pl.BlockSpec((hs, ATT_TQ, ATT_TQ), lambda b, h, i: (h, 0, 0)),
            pl.BlockSpec((hs, KPAD - HEAD_DIM, 1), lambda b, h, i: (h, 0, 0)),
            pl.BlockSpec((A_VDIM, 1), lambda b, h, i: (0, 0)),
            pl.BlockSpec(lamv.shape, lambda b, h, i: (0, 0)),
        ],
        out_specs=pl.BlockSpec((1, hs, A_VDIM, ATT_TQ), lambda b, h, i: (b, h, 0, i)),
        out_shape=jax.ShapeDtypeStruct((bsz, A_HEADS, A_VDIM, seq), bf16),
        scratch_shapes=[pltpu.VMEM((2 * hs, SUBLANES, ATT_TQ), f32),
                        pltpu.VMEM((2 * hs, A_VDIM, ATT_TQ), f32)],
        compiler_params=pltpu.CompilerParams(
            dimension_semantics=("arbitrary", "arbitrary", "arbitrary"),
            vmem_limit_bytes=VMEM_LIMIT),
        name="attn_a",
    )(fixed, aq, ak, av, asg, am, dtab, qaug, subg, lamv)


def _attn_b_kernel(fixed_ref, q_ref, k_ref, v_ref, sg_ref, m_ref, qaug_ref, o_ref, l_s, acc_s):
    i = pl.program_id(2)

    def run(fixed_max):
        causal = jnp.where(lax.broadcasted_iota(jnp.int32, (ATT_TQ, ATT_TQ), 0)
                           <= lax.broadcasted_iota(jnp.int32, (ATT_TQ, ATT_TQ), 1), 0.0, NEG)
        chains = []
        for h in range(B_HEADS_PER_STEP):
            k_at = lambda start, size, h=h: k_ref[0, h, pl.ds(start, size), :]
            v_at = lambda start, size, h=h: v_ref[0, h, :, pl.ds(start, size)]
            chains.append((k_at, v_at, q_ref[0, h], qaug_ref[...], B_BIAS_ROWS, causal, m_ref[0, h]))
        for h, (acc, l) in enumerate(_causal_sweep(chains, i, fixed_max, l_s, acc_s)):
            o_ref[0, h] = (acc * (1.0 / l) * sg_ref[0, h].astype(f32)).astype(bf16)

    _either_sweep(fixed_ref, run)


def _attn_b(fixed, bq, bk, bv, bsg, bm, qaug):
    bsz, _, _, seq = bq.shape
    nq = seq // ATT_TQ
    hs = B_HEADS_PER_STEP
    return pl.pallas_call(
        _attn_b_kernel,
        grid=(bsz, B_HEADS // hs, nq),
        in_specs=[
            pl.BlockSpec(memory_space=pltpu.SMEM),
            pl.BlockSpec((1, hs, HEAD_DIM, ATT_TQ), lambda b, h, i: (b, h, 0, i)),
            pl.BlockSpec((1, hs, seq, KPAD), lambda b, h, i: (b, h, 0, 0)),
            pl.BlockSpec((1, hs, HEAD_DIM, seq), lambda b, h, i: (b, h, 0, 0)),
            pl.BlockSpec((1, hs, HEAD_DIM, ATT_TQ), lambda b, h, i: (b, h, 0, i)),
            pl.BlockSpec((1, hs, 1, ATT_TQ), lambda b, h, i: (b, h, 0, i)),
            pl.BlockSpec((KPAD - HEAD_DIM, 1), lambda b, h, i: (0, 0)),
        ],
        out_specs=pl.BlockSpec((1, hs, HEAD_DIM, ATT_TQ), lambda b, h, i: (b, h, 0, i)),
        out_shape=jax.ShapeDtypeStruct((bsz, B_HEADS, HEAD_DIM, seq), bf16),
        scratch_shapes=[pltpu.VMEM((hs, SUBLANES, ATT_TQ), f32), pltpu.VMEM((hs, HEAD_DIM, ATT_TQ), f32)],
        compiler_params=pltpu.CompilerParams(
            dimension_semantics=("arbitrary", "arbitrary", "arbitrary"),
            vmem_limit_bytes=VMEM_LIMIT),
        name="attn_b",
    )(fixed, bq, bk, bv, bsg, bm, qaug)


def _band_kernel(fixed_ref, q_ref, k_ref, v_ref, sg_ref, m_ref, tab_ref, sink_ref, o_ref,
                 *, group, back, tq):
    blocks = tq // LANES
    band = (back + blocks) * LANES
    tiles = q_ref.shape[3] // tq
    work = [(u, h) for u in range(tiles) for h in range(k_ref.shape[1])]
    grouped = lambda ref, u, h: jnp.concatenate(
        [ref[0, h * group + g, :, u * tq:(u + 1) * tq] for g in range(group)], axis=1)

    def window(u):
        first = (pl.program_id(2) * tiles + u) * blocks
        return (pl.multiple_of(jnp.maximum(first - back, 0) * LANES, LANES),
                pl.multiple_of(jnp.maximum(back - first, 0) * LANES, LANES))

    def run(fixed_max):
        def scores(w):
            u, h = work[w]
            k_start, tab_start = window(u)
            q = _augment_q(grouped(q_ref, u, h), 0.0, 0, grouped(m_ref, u, h) if fixed_max else None)
            s = jnp.dot(k_ref[0, h, pl.ds(k_start, band), :], q, preferred_element_type=f32)
            return s + tab_ref[h, pl.ds(tab_start, band), :]

        def finish(w, s):
            u, h = work[w]
            k_start, _ = window(u)
            if fixed_max:
                m = grouped(m_ref, u, h)
                p = jnp.exp2(s)
            else:
                m = jnp.maximum(jnp.max(s, axis=0, keepdims=True), sink_ref[h])
                p = jnp.exp2(s - m)
            l = jnp.sum(p, axis=0, keepdims=True) + jnp.exp2(sink_ref[h] - m)
            o = jnp.dot(v_ref[0, h, :, pl.ds(k_start, band)], p.astype(bf16),
                        preferred_element_type=f32) * (1.0 / l)
            for g in range(group):
                hq = h * group + g
                gate = sg_ref[0, hq, :, u * tq:(u + 1) * tq].astype(f32)
                o_ref[0, hq, :, u * tq:(u + 1) * tq] = (o[:, g * tq:(g + 1) * tq] * gate).astype(bf16)

        _staggered(len(work), scores, finish)

    _either_sweep(fixed_ref, run)


def _band_attn(fixed, q, k, v, sg, m, tab, sink, group, back, tq, kv_per_step, tiles_per_step, name):
    bsz, nheads, _, seq = q.shape
    hs = kv_per_step
    tile = tq * tiles_per_step
    return pl.pallas_call(
        functools.partial(_band_kernel, group=group, back=back, tq=tq),
        grid=(nheads // (group * hs), bsz, seq // tile),
        in_specs=[
            pl.BlockSpec(memory_space=pltpu.SMEM),
            pl.BlockSpec((1, hs * group, HEAD_DIM, tile), lambda h, b, i: (b, h, 0, i)),
            pl.BlockSpec((1, hs, seq, KPAD), lambda h, b, i: (b, h, 0, 0)),
            pl.BlockSpec((1, hs, HEAD_DIM, seq), lambda h, b, i: (b, h, 0, 0)),
            pl.BlockSpec((1, hs * group, HEAD_DIM, tile), lambda h, b, i: (b, h, 0, i)),
            pl.BlockSpec((1, hs * group, 1, tile), lambda h, b, i: (b, h, 0, i)),
            pl.BlockSpec((hs,) + tab.shape[1:], lambda h, b, i: (h, 0, 0),
                         pipeline_mode=pl.Buffered(1)),
            pl.BlockSpec((hs,) + sink.shape[1:], lambda h, b, i: (h, 0, 0)),
        ],
        out_specs=pl.BlockSpec((1, hs * group, HEAD_DIM, tile), lambda h, b, i: (b, h, 0, i)),
        out_shape=jax.ShapeDtypeStruct((bsz, nheads, HEAD_DIM, seq), bf16),
        compiler_params=pltpu.CompilerParams(
            dimension_semantics=("arbitrary", "arbitrary", "arbitrary"),
            vmem_limit_bytes=VMEM_LIMIT),
        name=name,
    )(fixed, q, k, v, sg, m, tab, sink)


def _out_proj_kernel(m1_ref, m2_ref, wt_ref, x_ref, o_ref):
    o_ref[0] = _residual_add(m1_ref, m2_ref, wt_ref, x_ref)


def _out_proj(m1, m2, wt, x):
    bsz, seq, _ = x.shape
    tt = OUT_TOKENS
    half = m1.shape[1]
    return pl.pallas_call(
        _out_proj_kernel,
        grid=(bsz, seq // tt),
        in_specs=[
            pl.BlockSpec((1, half, tt), lambda b, t: (b, 0, t)),
            pl.BlockSpec((1, half, tt), lambda b, t: (b, 0, t)),
            pl.BlockSpec(wt.shape, lambda b, t: (0, 0)),
            pl.BlockSpec((1, tt, D_MODEL), lambda b, t: (b, t, 0)),
        ],
        out_specs=pl.BlockSpec((1, tt, D_MODEL), lambda b, t: (b, t, 0)),
        out_shape=jax.ShapeDtypeStruct(x.shape, f32),
        compiler_params=pltpu.CompilerParams(
            dimension_semantics=("arbitrary", "arbitrary"), vmem_limit_bytes=VMEM_LIMIT),
        name="out_proj",
    )(m1, m2, wt, x)


def _alibi_slopes(n):
    return 2.0 ** (-8.0 * np.arange(1, n + 1, dtype=np.float64) / n)


def _np_split3(v):
    v = np.asarray(v, np.float32)
    to_bf = lambda a: a.astype(bf16).astype(np.float32)
    hi = to_bf(v)
    mid = to_bf(v - hi)
    lo = to_bf(v - hi - mid)
    return hi, mid, lo


def _a_tables():
    rate = A_RATES
    qaug = np.zeros((A_HEADS, KPAD - HEAD_DIM, 1), np.float32)
    for idx, piece in enumerate(_np_split3(rate * CHUNK) + _np_split3(rate)):
        qaug[:, idx, 0] = piece
    kk = np.arange(ATT_TQ)[:, None]
    qq = np.arange(ATT_TQ)[None, :]
    future = np.maximum(kk - qq, 0).astype(np.float32)
    corr = -2.0 * rate[:, None, None] * future[None]
    allowed = (kk // CHUNK) <= (qq // CHUNK)
    dtab = np.where(allowed[None], corr, NEG).astype(np.float32)
    return jnp.asarray(qaug), jnp.asarray(dtab)


def _band_frames(back, tq):
    k_pos = np.arange(back * LANES + tq)[:, None]
    q_pos = back * LANES + np.arange(tq)[None, :]
    return q_pos - k_pos, q_pos // CHUNK - k_pos // CHUNK


def _c_tables(sinks):
    back, tq = WIN_CHUNKS * CHUNK // LANES, C_BAND_TQ
    rel, chunk_diff = _band_frames(back, tq)
    allowed = (chunk_diff >= 0) & (chunk_diff <= WIN_CHUNKS)
    slopes = _alibi_slopes(C_HEADS)
    per_head = np.where(allowed[None], -slopes[:, None, None] * np.abs(rel)[None] * LOG2E, NEG)
    tab = per_head.reshape(C_KV_HEADS, C_GROUP, *rel.shape).transpose(0, 2, 1, 3)
    tab = tab.reshape(C_KV_HEADS, rel.shape[0], C_GROUP * tq).astype(np.float32)
    tab = np.concatenate([tab, np.full((C_KV_HEADS, back * LANES, tab.shape[2]), NEG, np.float32)], 1)
    sink = jnp.repeat(sinks.astype(f32) * LOG2E, tq).reshape(C_KV_HEADS, 1, C_GROUP * tq)
    return jnp.asarray(tab), sink, back


def _d_tables(rel_table):
    back, t = D_LEFT_CHUNKS * CHUNK // LANES, D_BAND_TQ
    band = back * LANES + t
    rel, chunk_diff = _band_frames(back, t)
    allowed = (chunk_diff >= 0) & (chunk_diff <= D_LEFT_CHUNKS)
    tbl = rel_table.astype(f32) * LOG2E
    n_lo = (t - 1) - (CHUNK - 1)
    n_hi = (band - 1) - REL_MAX
    diag = jnp.concatenate([jnp.broadcast_to(tbl[:, :1], (D_HEADS, n_lo)), tbl,
                            jnp.broadcast_to(tbl[:, -1:], (D_HEADS, n_hi))], axis=1)
    m = t + LANES - 1
    blocks = []
    for kb in range(band // LANES):
        lo = rel[kb * LANES:(kb + 1) * LANES].min()
        if lo >= REL_MAX:
            blocks.append(jnp.broadcast_to(tbl[:, -1:, None], (D_HEADS, LANES, t)))
            continue
        window = diag[:, band - (kb + 1) * LANES:band - (kb + 1) * LANES + m]
        skew = jnp.broadcast_to(window[:, None, :], (D_HEADS, LANES + 1, m)).reshape(D_HEADS, -1)
        skew = skew[:, :LANES * (m + 1)].reshape(D_HEADS, LANES, m + 1)[:, :, :t]
        blocks.append(jnp.flip(skew, axis=1))
    blocks += [jnp.zeros((D_HEADS, LANES, t), f32)] * back
    allowed = np.concatenate([allowed, np.zeros((back * LANES, t), bool)], axis=0)
    tab = jnp.where(jnp.asarray(allowed)[None], jnp.concatenate(blocks, axis=1), NEG)
    sink = jnp.full((D_HEADS, 1, t), NEG, f32)
    return tab, sink, back


def _fixed_max_ok(q_gain, k_gain, bias_range=0.0):
    spread = (2.0 * 1.02 * QK_SCALE * HEAD_DIM
              * jnp.max(jnp.abs(q_gain.astype(f32))) * jnp.max(jnp.abs(k_gain.astype(f32))))
    return (spread + bias_range <= FIXED_MAX_LIMIT).astype(jnp.int32).reshape(1)


def _pad_rows(w_t, rows):
    return jnp.pad(w_t, ((0, rows - w_t.shape[0]), (0, 0)))


def _even_layer(x, ln_g, w_in, w_out, a_qn_g, a_kn_g, a_lq1, a_lk1, a_lq2, a_lk2, a_subln_g,
                b_qn_g, b_kn_g, b_f_bias, layer_idx):
    bsz, seq, _ = x.shape
    colv = lambda v: v.astype(f32).reshape(-1, 1)
    n_wide = EVEN["bf"][0]
    wt = w_in[:, :n_wide].T.astype(bf16)
    wf = _pad_rows(w_in[:, n_wide:].T.astype(bf16), BF16_ROWS)
    aq, ak, av, asg, bq, bk, bv, bsg, am, bm = _proj_even(
        x, ln_g.astype(f32).reshape(1, -1), wt, wf, colv(a_qn_g), colv(a_kn_g), colv(b_qn_g),
        colv(b_kn_g), colv(b_f_bias))
    lam_init = 0.8 - 0.6 * math.exp(-0.3 * layer_idx)
    qaug_a, dtab = _a_tables()
    lamv = jnp.stack([a_lq1, a_lk1, a_lq2, a_lk2]).astype(f32)
    mix_a = _attn_a(_fixed_max_ok(a_qn_g, a_kn_g), aq, ak, av, asg, am, dtab, qaug_a,
                    colv(a_subln_g), lamv, lam_init)
    qaug_b = np.zeros((KPAD - HEAD_DIM, 1), np.float32)
    qaug_b[:B_BIAS_ROWS] = 1.0
    mix_b = _attn_b(_fixed_max_ok(b_qn_g, b_kn_g), bq, bk, bv, bsg, bm, jnp.asarray(qaug_b))
    return mix_a.reshape(bsz, -1, seq), mix_b.reshape(bsz, -1, seq), w_out.T.astype(bf16), x


def _odd_layer(pending, ln_g, w_in, w_out, c_qn_g, c_kn_g, c_sinks, d_qn_g, d_kn_g, d_rel_bias):
    bsz, seq, _ = pending[3].shape
    colv = lambda v: v.astype(f32).reshape(-1, 1)
    x, cq, ck, cv, csg, dq, dk, dv, dsg, cm, dm = _proj_odd(
        *pending, ln_g.astype(f32).reshape(1, -1), w_in.T.astype(bf16), colv(c_qn_g), colv(c_kn_g),
        colv(d_qn_g), colv(d_kn_g))
    tab_c, sink_c, back_c = _c_tables(c_sinks)
    fixed_c = _fixed_max_ok(c_qn_g, c_kn_g, LOG2E * jnp.maximum(jnp.max(c_sinks.astype(f32)), 0.0))
    mix_c = _band_attn(fixed_c, cq, ck, cv, csg, cm, tab_c, sink_c, C_GROUP, back_c, C_BAND_TQ,
                       C_KV_HEADS, C_TILES_PER_STEP, "attn_c")
    tab_d, sink_d, back_d = _d_tables(d_rel_bias)
    fixed_d = _fixed_max_ok(d_qn_g, d_kn_g, LOG2E * jnp.max(jnp.abs(d_rel_bias.astype(f32))))
    mix_d = _band_attn(fixed_d, dq, dk, dv, dsg, dm, tab_d, sink_d, 1, back_d, D_BAND_TQ,
                       D_HEADS_PER_STEP, D_TILES_PER_STEP, "attn_d")
    return mix_c.reshape(bsz, -1, seq), mix_d.reshape(bsz, -1, seq), w_out.T.astype(bf16), x


def kernel(x, even_ln_g, even_w_in, even_w_out, a_q_norm_g, a_k_norm_g, a_lambda_q1, a_lambda_k1, a_lambda_q2, a_lambda_k2, a_subln_g, b_q_norm_g, b_k_norm_g, b_forget_bias, odd_ln_g, odd_w_in, odd_w_out, c_q_norm_g, c_k_norm_g, c_sinks, d_q_norm_g, d_k_norm_g, d_rel_bias):
    depth = even_ln_g.shape[0] + odd_ln_g.shape[0]
    seq = x.shape[1]
    assert x.shape[2] == D_MODEL and even_w_in.shape[2] == P_EVEN and odd_w_in.shape[2] == P_ODD
    for tile in (PROJ_TOKENS, OUT_TOKENS, ATT_TQ, C_BAND_TQ * C_TILES_PER_STEP,
                 D_BAND_TQ * D_TILES_PER_STEP):
        assert seq % tile == 0, (seq, tile)
    pending = None
    for i in range(depth):
        j = i // 2
        if i % 2 == 0:
            if pending is not None:
                x = _out_proj(*pending)
            pending = _even_layer(x, even_ln_g[j], even_w_in[j], even_w_out[j], a_q_norm_g[j],
                                  a_k_norm_g[j], a_lambda_q1[j], a_lambda_k1[j], a_lambda_q2[j],
                                  a_lambda_k2[j], a_subln_g[j], b_q_norm_g[j], b_k_norm_g[j],
                                  b_forget_bias[j], i)
        else:
            pending = _odd_layer(pending, odd_ln_g[j], odd_w_in[j], odd_w_out[j], c_q_norm_g[j],
                                 c_k_norm_g[j], c_sinks[j], d_q_norm_g[j], d_k_norm_g[j],
                                 d_rel_bias[j])
    return _out_proj(*pending)
```

```python
import functools
import math

import numpy as np
import jax
import jax.numpy as jnp
from jax import lax
from jax.experimental import pallas as pl
from jax.experimental.pallas import tpu as pltpu

D_MODEL = 1024
CHUNK = 64
HEAD_DIM = 64
NORM_EPS = 1e-6

A_HEADS = 4
A_STREAMS = 2 * A_HEADS
A_VDIM = 2 * HEAD_DIM
B_HEADS = 8
C_HEADS = 8
C_KV_HEADS = 2
C_GROUP = C_HEADS // C_KV_HEADS
WIN_CHUNKS = 2
D_HEADS = 8
D_LEFT_CHUNKS = 8
REL_MAX = 256


def _row_ranges(names, sizes):
    stops = np.cumsum(sizes)
    return {n: (int(b - w), int(b)) for n, w, b in zip(names, sizes, stops)}


EVEN = _row_ranges(("aq", "ak", "av", "ag", "bq", "bk", "bv", "bg", "bf"),
                   (A_STREAMS * HEAD_DIM,) * 2 + (A_HEADS * A_VDIM,) * 2 + (B_HEADS * HEAD_DIM,) * 4
                   + (B_HEADS,))
ODD = _row_ranges(("cq", "ck", "cv", "cg", "dq", "dk", "dv", "dg"),
                  (C_HEADS * HEAD_DIM, C_KV_HEADS * HEAD_DIM, C_KV_HEADS * HEAD_DIM, C_HEADS * HEAD_DIM)
                  + (D_HEADS * HEAD_DIM,) * 4)
P_EVEN = EVEN["bf"][1]
P_ODD = ODD["dg"][1]

LOG2E = 1.4426950408889634
QK_SCALE = HEAD_DIM ** -0.5 * LOG2E
NEG = -1e30
A_RATES = (2.0 ** (-8.0 * np.arange(1, A_HEADS + 1) / A_HEADS) * LOG2E).astype(np.float32)

LANES = 128
SUBLANES = 8
KPAD = 128
BF16_ROWS = 16

PROJ_TOKENS = 512
OUT_TOKENS = 1024
ATT_TQ = 512
ATT_TK = 512
A_HEADS_PER_STEP = 4
B_HEADS_PER_STEP = 8
A_BIAS_ROWS = 6
B_BIAS_ROWS = 3
MAX_ROWS = 3
FIXED_MAX_LIMIT = 96.0
C_BAND_TQ = 128
D_BAND_TQ = 256
D_HEADS_PER_STEP = 4
C_TILES_PER_STEP = 8
D_TILES_PER_STEP = 4
VMEM_MIB = {"proj_even": 32, "proj_odd": 36, "attn_a": 52, "attn_b": 40, "attn_c": 20, "attn_d": 32,
            "out_proj": 32}


def _vmem(name):
    return VMEM_MIB[name] * 1024 * 1024

f32 = jnp.float32
bf16 = jnp.bfloat16


def _split3(v):
    hi = v.astype(bf16).astype(f32)
    r = v - hi
    mid = r.astype(bf16).astype(f32)
    lo = (r - mid).astype(bf16).astype(f32)
    return hi, mid, lo


def _silu(z):
    return z * (1.0 / (1.0 + jnp.exp(-z)))


def _rms_rows(x, g_ref):
    ms = jnp.mean(x * x, axis=-1, keepdims=True)
    return (x * lax.rsqrt(ms + NORM_EPS) * g_ref[...]).astype(bf16)


def _residual_add(m1_ref, m2_ref, wt_ref, x_ref):
    half = m1_ref.shape[1]
    y_t = (jnp.dot(wt_ref[:, :half], m1_ref[0], preferred_element_type=f32)
           + jnp.dot(wt_ref[:, half:], m2_ref[0], preferred_element_type=f32))
    return x_ref[0] + y_t.T


def _proj_t(wt_ref, rows, xn):
    return lax.dot_general(wt_ref[rows[0]:rows[1], :], xn, (((1,), (1,)), ((), ())),
                           preferred_element_type=f32)


def _head_norm(z_t, gain_col, mult):
    n = z_t.shape[0] // HEAD_DIM
    z3 = z_t.reshape(n, HEAD_DIM, z_t.shape[1])
    ms = jnp.mean(z3 * z3, axis=1, keepdims=True)
    return z3 * lax.rsqrt(ms + NORM_EPS) * (gain_col[...] * mult)[None]


def _ones_rows(row, first):
    return jnp.where((row >= first) & (row < first + MAX_ROWS), 1.0, 0.0)


def _store_heads(o_ref, z_t):
    o_ref[0] = z_t.reshape(o_ref.shape[1], o_ref.shape[2], z_t.shape[1]).astype(bf16)


def _store_keys(k_ref, kn, aug_fn):
    n, _, t = kn.shape
    zeros = jnp.zeros((KPAD - HEAD_DIM - BF16_ROWS, t), f32)
    for s in range(n):
        blk = jnp.concatenate([kn[s], aug_fn(s), zeros], axis=0)
        k_ref[0, s] = blk.T.astype(bf16)


def _proj_even_kernel(x_ref, lng_ref, wt_ref, wf_ref, aqg_ref, akg_ref, bqg_ref, bkg_ref, bfb_ref,
                      tri_ref, aq_ref, ak_ref, av_ref, asg_ref, bq_ref, bk_ref, bv_ref, bsg_ref,
                      am_ref, bm_ref, cum_ref):
    t = pl.program_id(1)
    tt = x_ref.shape[1]
    xn = _rms_rows(x_ref[0], lng_ref)
    row = lax.broadcasted_iota(jnp.int32, (BF16_ROWS, tt), 0)

    z = _proj_t(wf_ref, (0, BF16_ROWS), xn)[:B_HEADS] + bfb_ref[...]
    aqn = _head_norm(_proj_t(wt_ref, EVEN["aq"], xn), aqg_ref, QK_SCALE)
    aq_ref[0] = aqn.astype(bf16)

    pos = t * tt + lax.broadcasted_iota(jnp.int32, (BF16_ROWS, tt), 1)
    pos_a = lax.shift_right_logical(pos, int(math.log2(CHUNK))).astype(f32)
    pos_b = lax.bitwise_and(pos, CHUNK - 1).astype(f32)
    aug_a = jnp.where(row < A_BIAS_ROWS // 2, pos_a,
                      jnp.where(row < A_BIAS_ROWS, pos_b, _ones_rows(row, A_BIAS_ROWS)))
    akn = _head_norm(_proj_t(wt_ref, EVEN["ak"], xn), akg_ref, 1.0)
    _store_keys(ak_ref, akn, lambda s: aug_a)
    self_a = jnp.sum(aqn * akn, axis=1, keepdims=True)
    for s in range(A_STREAMS):
        am_ref[0, s] = self_a[s] + float(A_RATES[s // 2]) * pos[:1].astype(f32)

    _store_heads(asg_ref, _silu(_proj_t(wt_ref, EVEN["ag"], xn)))
    bqn = _head_norm(_proj_t(wt_ref, EVEN["bq"], xn), bqg_ref, QK_SCALE)
    bq_ref[0] = bqn.astype(bf16)

    log_f = jnp.minimum(z, 0.0) - jnp.log(1.0 + jnp.exp(-jnp.abs(z)))
    pieces = jnp.concatenate(_split3(log_f) + (jnp.zeros_like(log_f),), axis=0).astype(bf16)
    part = jnp.dot(pieces, tri_ref[...], preferred_element_type=f32)
    local = part[:B_HEADS] + part[B_HEADS:2 * B_HEADS] + part[2 * B_HEADS:3 * B_HEADS]

    @pl.when(t == 0)
    def _():
        cum_ref[...] = jnp.zeros_like(cum_ref)

    cum = cum_ref[...] + local
    cum_ref[...] = cum[:, tt - 1:tt]
    gate = -LOG2E * cum
    g_hi, g_mid, g_lo = _split3(gate)

    def aug_b(s):
        pick = lambda a: jnp.broadcast_to(a[s:s + 1], (BF16_ROWS, tt))
        return jnp.where(row == 0, pick(g_hi),
                         jnp.where(row == 1, pick(g_mid),
                                   jnp.where(row == 2, pick(g_lo), _ones_rows(row, B_BIAS_ROWS))))

    bkn = _head_norm(_proj_t(wt_ref, EVEN["bk"], xn), bkg_ref, 1.0)
    _store_keys(bk_ref, bkn, aug_b)
    self_b = jnp.sum(bqn * bkn, axis=1, keepdims=True)
    for s in range(B_HEADS):
        bm_ref[0, s] = self_b[s] + gate[s:s + 1]
    _store_heads(bsg_ref, _silu(_proj_t(wt_ref, EVEN["bg"], xn)))
    _store_heads(av_ref, _proj_t(wt_ref, EVEN["av"], xn))
    _store_heads(bv_ref, _proj_t(wt_ref, EVEN["bv"], xn))


def _proj_even(x, ln_g, wt, wf, aqg, akg, bqg, bkg, bfb):
    bsz, seq, _ = x.shape
    tt = PROJ_TOKENS
    col = lambda n: pl.BlockSpec((n, 1), lambda b, t: (0, 0))
    fm = lambda n, d: pl.BlockSpec((1, n, d, tt), lambda b, t: (b, 0, 0, t))
    km = lambda n: pl.BlockSpec((1, n, tt, KPAD), lambda b, t: (b, 0, t, 0))
    fm_shape = lambda n, d: jax.ShapeDtypeStruct((bsz, n, d, seq), bf16)
    km_shape = lambda n: jax.ShapeDtypeStruct((bsz, n, seq, KPAD), bf16)
    return pl.pallas_call(
        _proj_even_kernel,
        grid=(bsz, seq // tt),
        in_specs=[
            pl.BlockSpec((1, tt, D_MODEL), lambda b, t: (b, t, 0)),
            pl.BlockSpec((1, D_MODEL), lambda b, t: (0, 0)),
            pl.BlockSpec(wt.shape, lambda b, t: (0, 0)),
            pl.BlockSpec(wf.shape, lambda b, t: (0, 0)),
            col(HEAD_DIM), col(HEAD_DIM), col(HEAD_DIM), col(HEAD_DIM), col(B_HEADS),
            pl.BlockSpec((tt, tt), lambda b, t: (0, 0)),
        ],
        out_specs=[fm(A_STREAMS, HEAD_DIM), km(A_STREAMS), fm(A_HEADS, A_VDIM), fm(A_HEADS, A_VDIM),
                   fm(B_HEADS, HEAD_DIM), km(B_HEADS), fm(B_HEADS, HEAD_DIM), fm(B_HEADS, HEAD_DIM),
                   fm(A_STREAMS, 1), fm(B_HEADS, 1)],
        out_shape=[fm_shape(A_STREAMS, HEAD_DIM), km_shape(A_STREAMS), fm_shape(A_HEADS, A_VDIM),
                   fm_shape(A_HEADS, A_VDIM), fm_shape(B_HEADS, HEAD_DIM), km_shape(B_HEADS),
                   fm_shape(B_HEADS, HEAD_DIM), fm_shape(B_HEADS, HEAD_DIM),
                   jax.ShapeDtypeStruct((bsz, A_STREAMS, 1, seq), f32),
                   jax.ShapeDtypeStruct((bsz, B_HEADS, 1, seq), f32)],
        scratch_shapes=[pltpu.VMEM((B_HEADS, 1), f32)],
        compiler_params=pltpu.CompilerParams(
            dimension_semantics=("arbitrary", "arbitrary"), vmem_limit_bytes=_vmem("proj_even")),
        name="proj_even",
    )(x, ln_g, wt, wf, aqg, akg, bqg, bkg, bfb,
      jnp.asarray(np.triu(np.ones((tt, tt), np.float32)), bf16))


def _proj_odd_kernel(m1_ref, m2_ref, wo_ref, x_ref, lng_ref, wt_ref, cqg_ref, ckg_ref, dqg_ref,
                     dkg_ref, x1_ref, cq_ref, ck_ref, cv_ref, csg_ref, dq_ref, dk_ref, dv_ref,
                     dsg_ref, cm_ref, dm_ref):
    tt = x_ref.shape[1]
    x1 = _residual_add(m1_ref, m2_ref, wo_ref, x_ref)
    x1_ref[0] = x1
    xn = _rms_rows(x1, lng_ref)
    ones = _ones_rows(lax.broadcasted_iota(jnp.int32, (BF16_ROWS, tt), 0), 0)
    aug = lambda s: ones

    cqn = _head_norm(_proj_t(wt_ref, ODD["cq"], xn), cqg_ref, QK_SCALE)
    cq_ref[0] = cqn.astype(bf16)
    ckn = _head_norm(_proj_t(wt_ref, ODD["ck"], xn), ckg_ref, 1.0)
    _store_keys(ck_ref, ckn, aug)
    cm_ref[0] = jnp.sum(cqn.reshape(C_KV_HEADS, C_GROUP, HEAD_DIM, tt) * ckn[:, None], axis=2,
                        keepdims=True).reshape(C_HEADS, 1, tt)
    _store_heads(cv_ref, _proj_t(wt_ref, ODD["cv"], xn))
    _store_heads(csg_ref, _silu(_proj_t(wt_ref, ODD["cg"], xn)))
    dqn = _head_norm(_proj_t(wt_ref, ODD["dq"], xn), dqg_ref, QK_SCALE)
    dq_ref[0] = dqn.astype(bf16)
    dkn = _head_norm(_proj_t(wt_ref, ODD["dk"], xn), dkg_ref, 1.0)
    _store_keys(dk_ref, dkn, aug)
    dm_ref[0] = jnp.sum(dqn * dkn, axis=1, keepdims=True)
    _store_heads(dsg_ref, _silu(_proj_t(wt_ref, ODD["dg"], xn)))
    _store_heads(dv_ref, _proj_t(wt_ref, ODD["dv"], xn))


def _proj_odd(m1, m2, wo_t, x, ln_g, wt, cqg, ckg, dqg, dkg):
    bsz, seq, _ = x.shape
    tt = PROJ_TOKENS
    half = m1.shape[1]
    col = lambda n: pl.BlockSpec((n, 1), lambda b, t: (0, 0))
    rows = pl.BlockSpec((1, tt, D_MODEL), lambda b, t: (b, t, 0))
    fm = lambda n: pl.BlockSpec((1, n, HEAD_DIM, tt), lambda b, t: (b, 0, 0, t))
    km = lambda n: pl.BlockSpec((1, n, tt, KPAD), lambda b, t: (b, 0, t, 0))
    fm_shape = lambda n: jax.ShapeDtypeStruct((bsz, n, HEAD_DIM, seq), bf16)
    km_shape = lambda n: jax.ShapeDtypeStruct((bsz, n, seq, KPAD), bf16)
    return pl.pallas_call(
        _proj_odd_kernel,
        grid=(bsz, seq // tt),
        in_specs=[
            pl.BlockSpec((1, half, tt), lambda b, t: (b, 0, t)),
            pl.BlockSpec((1, half, tt), lambda b, t: (b, 0, t)),
            pl.BlockSpec(wo_t.shape, lambda b, t: (0, 0)),
            rows,
            pl.BlockSpec((1, D_MODEL), lambda b, t: (0, 0)),
            pl.BlockSpec(wt.shape, lambda b, t: (0, 0)),
            col(HEAD_DIM), col(HEAD_DIM), col(HEAD_DIM), col(HEAD_DIM),
        ],
        out_specs=[rows, fm(C_HEADS), km(C_KV_HEADS), fm(C_KV_HEADS), fm(C_HEADS),
                   fm(D_HEADS), km(D_HEADS), fm(D_HEADS), fm(D_HEADS),
                   pl.BlockSpec((1, C_HEADS, 1, tt), lambda b, t: (b, 0, 0, t)),
                   pl.BlockSpec((1, D_HEADS, 1, tt), lambda b, t: (b, 0, 0, t))],
        out_shape=[jax.ShapeDtypeStruct(x.shape, f32),
                   fm_shape(C_HEADS), km_shape(C_KV_HEADS), fm_shape(C_KV_HEADS), fm_shape(C_HEADS),
                   fm_shape(D_HEADS), km_shape(D_HEADS), fm_shape(D_HEADS), fm_shape(D_HEADS),
                   jax.ShapeDtypeStruct((bsz, C_HEADS, 1, seq), f32),
                   jax.ShapeDtypeStruct((bsz, D_HEADS, 1, seq), f32)],
        compiler_params=pltpu.CompilerParams(
            dimension_semantics=("arbitrary", "arbitrary"), vmem_limit_bytes=_vmem("proj_odd")),
        name="proj_odd",
    )(m1, m2, wo_t, x, ln_g, wt, cqg, ckg, dqg, dkg)


def _online_step(carry, s, v):
    m, l, acc = carry
    m_new = jnp.maximum(m, jnp.max(s, axis=0, keepdims=True))
    p = jnp.exp2(s - m_new)
    alpha = jnp.exp2(m - m_new)
    l = alpha * l + jnp.sum(p, axis=0, keepdims=True)
    acc = alpha * acc + jnp.dot(v, p.astype(bf16), preferred_element_type=f32)
    return m_new, l, acc


def _colsum8(p):
    return p.reshape(p.shape[0] // SUBLANES, SUBLANES, p.shape[1]).sum(axis=0)


def _augment_q(q, aug_col, n_bias, m=None):
    tq = q.shape[1]
    aug = jnp.broadcast_to(aug_col, (KPAD - HEAD_DIM, tq))
    if m is not None:
        row = lax.broadcasted_iota(jnp.int32, aug.shape, 0)
        pieces = _split3(-m)
        for r in range(MAX_ROWS):
            aug = jnp.where(row == n_bias + r, pieces[r], aug)
    return jnp.concatenate([q, aug.astype(bf16)], axis=0)


def _staggered(n, scores, finish):
    out, pending = [], scores(0)
    for c in range(1, n):
        nxt = scores(c)
        out.append(finish(c - 1, pending))
        pending = nxt
    out.append(finish(n - 1, pending))
    return out


def _diag_full(chains, d0, tq):
    def scores(c):
        k_at, _, q, aug_col, n_bias, diag_bias, _ = chains[c]
        return jnp.dot(k_at(d0, tq), _augment_q(q, aug_col, n_bias),
                       preferred_element_type=f32) + diag_bias

    def finish(c, s):
        m = jnp.max(s, axis=0, keepdims=True)
        p = jnp.exp2(s - m)
        return m, _colsum8(p), jnp.dot(chains[c][1](d0, tq), p.astype(bf16), preferred_element_type=f32)

    return _staggered(len(chains), scores, finish)


def _diag_halves(chains, q_aug, d0, tq):
    h = tq // 2
    d1 = pl.multiple_of(d0 + h, h)

    def scores(c):
        k_at = chains[c][0]
        return (jnp.dot(k_at(d0, h), q_aug[c], preferred_element_type=f32),
                jnp.dot(k_at(d1, h), q_aug[c][:, h:], preferred_element_type=f32))

    def finish(c, s):
        v_at, bias = chains[c][1], chains[c][5][:h, :h]
        p0 = jnp.concatenate([jnp.exp2(s[0][:, :h] + bias), jnp.exp2(s[0][:, h:])], axis=1)
        p1 = jnp.exp2(s[1] + bias)
        l0 = _colsum8(p0)
        a0 = jnp.dot(v_at(d0, h), p0.astype(bf16), preferred_element_type=f32)
        a1 = jnp.dot(v_at(d1, h), p1.astype(bf16), preferred_element_type=f32)
        return (jnp.concatenate([l0[:, :h], l0[:, h:] + _colsum8(p1)], axis=1),
                jnp.concatenate([a0[:, :h], a0[:, h:] + a1], axis=1))

    return _staggered(len(chains), scores, finish)


def _causal_sweep(chains, i, fixed_max, l_s, acc_s):
    n = len(chains)
    tq = chains[0][2].shape[1]
    d0 = pl.multiple_of(i * tq, tq)
    n_tiles = i * (tq // ATT_TK)
    tile = lambda j: (pl.multiple_of(j * ATT_TK, ATT_TK), ATT_TK)

    if fixed_max:
        q_aug = [_augment_q(q, aug_col, n_bias, m) for _, _, q, aug_col, n_bias, _, m in chains]

        for c, (l, acc) in enumerate(_diag_halves(chains, q_aug, d0, tq)):
            l_s[c] = l
            acc_s[c] = acc

        def sweep(tiles):
            work = [(j, c) for j in tiles for c in range(n)]

            def finish(w, s):
                j, c = work[w]
                p = jnp.exp2(s)
                l_s[c] += _colsum8(p)
                acc_s[c] += jnp.dot(chains[c][1](*tile(j)), p.astype(bf16), preferred_element_type=f32)

            def scores(w):
                j, c = work[w]
                return jnp.dot(chains[c][0](*tile(j)), q_aug[c], preferred_element_type=f32)

            _staggered(len(work), scores, finish)

        def pair(j, carry):
            sweep((2 * j, 2 * j + 1))
            return carry

        lax.fori_loop(0, n_tiles // 2, pair, 0)

        @pl.when(n_tiles % 2 == 1)
        def _():
            sweep((n_tiles - 1,))

        return [(acc_s[c], jnp.sum(l_s[c], axis=0, keepdims=True)) for c in range(n)]

    q_aug = [_augment_q(q, aug_col, n_bias) for _, _, q, aug_col, n_bias, _, _ in chains]

    def body(j, carries):
        scores = lambda c: jnp.dot(chains[c][0](*tile(j)), q_aug[c], preferred_element_type=f32)
        finish = lambda c, s: _online_step(carries[c], s, chains[c][1](*tile(j)))
        return tuple(_staggered(n, scores, finish))

    init = tuple((m, jnp.sum(l, axis=0, keepdims=True), acc) for m, l, acc in _diag_full(chains, d0, tq))
    return [(acc, l) for _, l, acc in lax.fori_loop(0, n_tiles, body, init)]


def _either_sweep(fixed_ref, run):
    @pl.when(fixed_ref[0] != 0)
    def _():
        run(True)

    @pl.when(fixed_ref[0] == 0)
    def _():
        run(False)


def _attn_a_kernel(fixed_ref, q_ref, k_ref, v_ref, sg_ref, m_ref, dtab_ref, qaug_ref, subg_ref,
                   lamv_ref, o_ref, l_s, acc_s, *, lam_init):
    i = pl.program_id(2)
    chains = []
    for h in range(A_HEADS_PER_STEP):
        v_at = lambda start, size, h=h: v_ref[0, h, :, pl.ds(start, size)]
        for c in range(2):
            s = 2 * h + c
            k_at = lambda start, size, s=s: k_ref[0, s, pl.ds(start, size), :]
            chains.append((k_at, v_at, q_ref[0, s], qaug_ref[h], A_BIAS_ROWS, dtab_ref[h],
                           m_ref[0, s]))

    def run(fixed_max):
        lv = lamv_ref[...]
        lam = (jnp.exp(jnp.sum(lv[0:1] * lv[1:2], axis=1, keepdims=True))
               - jnp.exp(jnp.sum(lv[2:3] * lv[3:4], axis=1, keepdims=True)) + lam_init)
        outs = [acc * (1.0 / l) for acc, l in _causal_sweep(chains, i, fixed_max, l_s, acc_s)]
        for h in range(A_HEADS_PER_STEP):
            o = outs[2 * h] - lam * outs[2 * h + 1]
            ms = jnp.mean(o * o, axis=0, keepdims=True)
            y = o * lax.rsqrt(ms + NORM_EPS) * (subg_ref[...] * (1.0 - lam_init))
            o_ref[0, h] = (y * sg_ref[0, h].astype(f32)).astype(bf16)

    _either_sweep(fixed_ref, run)


def _attn_a(fixed, aq, ak, av, asg, am, dtab, qaug, subg, lamv, lam_init):
    bsz, _, _, seq = aq.shape
    nq = seq // ATT_TQ
    hs = A_HEADS_PER_STEP
    return pl.pallas_call(
        functools.partial(_attn_a_kernel, lam_init=lam_init),
        grid=(bsz, A_HEADS // hs, nq),
        in_specs=[
            pl.BlockSpec(memory_space=pltpu.SMEM),
            pl.BlockSpec((1, 2 * hs, HEAD_DIM, ATT_TQ), lambda b, h, i: (b, h, 0, i)),
            pl.BlockSpec((1, 2 * hs, seq, KPAD), lambda b, h, i: (b, h, 0, 0)),
            pl.BlockSpec((1, hs, A_VDIM, seq), lambda b, h, i: (b, h, 0, 0)),
            pl.BlockSpec((1, hs, A_VDIM, ATT_TQ), lambda b, h, i: (b, h, 0, i)),
            pl.BlockSpec((1, 2 * hs, 1, ATT_TQ), lambda b, h, i: (b, h, 0, i)),
            pl.BlockSpec((hs, ATT_TQ, ATT_TQ), lambda b, h, i: (h, 0, 0)),
            pl.BlockSpec((hs, KPAD - HEAD_DIM, 1), lambda b, h, i: (h, 0, 0)),
            pl.BlockSpec((A_VDIM, 1), lambda b, h, i: (0, 0)),
            pl.BlockSpec(lamv.shape, lambda b, h, i: (0, 0)),
        ],
        out_specs=pl.BlockSpec((1, hs, A_VDIM, ATT_TQ), lambda b, h, i: (b, h, 0, i)),
        out_shape=jax.ShapeDtypeStruct((bsz, A_HEADS, A_VDIM, seq), bf16),
        scratch_shapes=[pltpu.VMEM((2 * hs, SUBLANES, ATT_TQ), f32),
                        pltpu.VMEM((2 * hs, A_VDIM, ATT_TQ), f32)],
        compiler_params=pltpu.CompilerParams(
            dimension_semantics=("arbitrary", "arbitrary", "arbitrary"),
            vmem_limit_bytes=_vmem("attn_a")),
        name="attn_a",
    )(fixed, aq, ak, av, asg, am, dtab, qaug, subg, lamv)


def _attn_b_kernel(fixed_ref, q_ref, k_ref, v_ref, sg_ref, m_ref, qaug_ref, o_ref, l_s, acc_s):
    i = pl.program_id(2)

    def run(fixed_max):
        causal = jnp.where(lax.broadcasted_iota(jnp.int32, (ATT_TQ, ATT_TQ), 0)
                           <= lax.broadcasted_iota(jnp.int32, (ATT_TQ, ATT_TQ), 1), 0.0, NEG)
        chains = []
        for h in range(B_HEADS_PER_STEP):
            k_at = lambda start, size, h=h: k_ref[0, h, pl.ds(start, size), :]
            v_at = lambda start, size, h=h: v_ref[0, h, :, pl.ds(start, size)]
            chains.append((k_at, v_at, q_ref[0, h], qaug_ref[...], B_BIAS_ROWS, causal, m_ref[0, h]))
        for h, (acc, l) in enumerate(_causal_sweep(chains, i, fixed_max, l_s, acc_s)):
            o_ref[0, h] = (acc * (1.0 / l) * sg_ref[0, h].astype(f32)).astype(bf16)

    _either_sweep(fixed_ref, run)


def _attn_b(fixed, bq, bk, bv, bsg, bm, qaug):
    bsz, _, _, seq = bq.shape
    nq = seq // ATT_TQ
    hs = B_HEADS_PER_STEP
    return pl.pallas_call(
        _attn_b_kernel,
        grid=(bsz, B_HEADS // hs, nq),
        in_specs=[
            pl.BlockSpec(memory_space=pltpu.SMEM),
            pl.BlockSpec((1, hs, HEAD_DIM, ATT_TQ), lambda b, h, i: (b, h, 0, i)),
            pl.BlockSpec((1, hs, seq, KPAD), lambda b, h, i: (b, h, 0, 0)),
            pl.BlockSpec((1, hs, HEAD_DIM, seq), lambda b, h, i: (b, h, 0, 0)),
            pl.BlockSpec((1, hs, HEAD_DIM, ATT_TQ), lambda b, h, i: (b, h, 0, i)),
            pl.BlockSpec((1, hs, 1, ATT_TQ), lambda b, h, i: (b, h, 0, i)),
            pl.BlockSpec((KPAD - HEAD_DIM, 1), lambda b, h, i: (0, 0)),
        ],
        out_specs=pl.BlockSpec((1, hs, HEAD_DIM, ATT_TQ), lambda b, h, i: (b, h, 0, i)),
        out_shape=jax.ShapeDtypeStruct((bsz, B_HEADS, HEAD_DIM, seq), bf16),
        scratch_shapes=[pltpu.VMEM((hs, SUBLANES, ATT_TQ), f32), pltpu.VMEM((hs, HEAD_DIM, ATT_TQ), f32)],
        compiler_params=pltpu.CompilerParams(
            dimension_semantics=("arbitrary", "arbitrary", "arbitrary"),
            vmem_limit_bytes=_vmem("attn_b")),
        name="attn_b",
    )(fixed, bq, bk, bv, bsg, bm, qaug)


def _band_kernel(fixed_ref, q_ref, k_ref, v_ref, sg_ref, m_ref, tab_ref, sink_ref, o_ref,
                 *, group, back, tq):
    blocks = tq // LANES
    band = (back + blocks) * LANES
    tiles = q_ref.shape[3] // tq
    work = [(u, h) for u in range(tiles) for h in range(k_ref.shape[1])]
    grouped = lambda ref, u, h: jnp.concatenate(
        [ref[0, h * group + g, :, u * tq:(u + 1) * tq] for g in range(group)], axis=1)

    def window(u):
        first = (pl.program_id(2) * tiles + u) * blocks
        return (pl.multiple_of(jnp.maximum(first - back, 0) * LANES, LANES),
                pl.multiple_of(jnp.maximum(back - first, 0) * LANES, LANES))

    def run(fixed_max):
        def scores(w):
            u, h = work[w]
            k_start, tab_start = window(u)
            q = _augment_q(grouped(q_ref, u, h), 0.0, 0, grouped(m_ref, u, h) if fixed_max else None)
            s = jnp.dot(k_ref[0, h, pl.ds(k_start, band), :], q, preferred_element_type=f32)
            return s + tab_ref[h, pl.ds(tab_start, band), :]

        def finish(w, s):
            u, h = work[w]
            k_start, _ = window(u)
            if fixed_max:
                m = grouped(m_ref, u, h)
                p = jnp.exp2(s)
            else:
                m = jnp.maximum(jnp.max(s, axis=0, keepdims=True), sink_ref[h])
                p = jnp.exp2(s - m)
            l = jnp.sum(p, axis=0, keepdims=True) + jnp.exp2(sink_ref[h] - m)
            o = jnp.dot(v_ref[0, h, :, pl.ds(k_start, band)], p.astype(bf16),
                        preferred_element_type=f32) * (1.0 / l)
            for g in range(group):
                hq = h * group + g
                gate = sg_ref[0, hq, :, u * tq:(u + 1) * tq].astype(f32)
                o_ref[0, hq, :, u * tq:(u + 1) * tq] = (o[:, g * tq:(g + 1) * tq] * gate).astype(bf16)

        _staggered(len(work), scores, finish)

    _either_sweep(fixed_ref, run)


def _band_attn(fixed, q, k, v, sg, m, tab, sink, group, back, tq, kv_per_step, tiles_per_step, name):
    bsz, nheads, _, seq = q.shape
    hs = kv_per_step
    tile = tq * tiles_per_step
    return pl.pallas_call(
        functools.partial(_band_kernel, group=group, back=back, tq=tq),
        grid=(nheads // (group * hs), bsz, seq // tile),
        in_specs=[
            pl.BlockSpec(memory_space=pltpu.SMEM),
            pl.BlockSpec((1, hs * group, HEAD_DIM, tile), lambda h, b, i: (b, h, 0, i)),
            pl.BlockSpec((1, hs, seq, KPAD), lambda h, b, i: (b, h, 0, 0)),
            pl.BlockSpec((1, hs, HEAD_DIM, seq), lambda h, b, i: (b, h, 0, 0)),
            pl.BlockSpec((1, hs * group, HEAD_DIM, tile), lambda h, b, i: (b, h, 0, i)),
            pl.BlockSpec((1, hs * group, 1, tile), lambda h, b, i: (b, h, 0, i)),
            pl.BlockSpec((hs,) + tab.shape[1:], lambda h, b, i: (h, 0, 0)),
            pl.BlockSpec((hs,) + sink.shape[1:], lambda h, b, i: (h, 0, 0)),
        ],
        out_specs=pl.BlockSpec((1, hs * group, HEAD_DIM, tile), lambda h, b, i: (b, h, 0, i)),
        out_shape=jax.ShapeDtypeStruct((bsz, nheads, HEAD_DIM, seq), bf16),
        compiler_params=pltpu.CompilerParams(
            dimension_semantics=("arbitrary", "arbitrary", "arbitrary"),
            vmem_limit_bytes=_vmem(name)),
        name=name,
    )(fixed, q, k, v, sg, m, tab, sink)


def _out_proj_kernel(m1_ref, m2_ref, wt_ref, x_ref, o_ref):
    o_ref[0] = _residual_add(m1_ref, m2_ref, wt_ref, x_ref)


def _out_proj(m1, m2, wt, x):
    bsz, seq, _ = x.shape
    tt = OUT_TOKENS
    half = m1.shape[1]
    return pl.pallas_call(
        _out_proj_kernel,
        grid=(bsz, seq // tt),
        in_specs=[
            pl.BlockSpec((1, half, tt), lambda b, t: (b, 0, t)),
            pl.BlockSpec((1, half, tt), lambda b, t: (b, 0, t)),
            pl.BlockSpec(wt.shape, lambda b, t: (0, 0)),
            pl.BlockSpec((1, tt, D_MODEL), lambda b, t: (b, t, 0)),
        ],
        out_specs=pl.BlockSpec((1, tt, D_MODEL), lambda b, t: (b, t, 0)),
        out_shape=jax.ShapeDtypeStruct(x.shape, f32),
        compiler_params=pltpu.CompilerParams(
            dimension_semantics=("arbitrary", "arbitrary"), vmem_limit_bytes=_vmem("out_proj")),
        name="out_proj",
    )(m1, m2, wt, x)


def _alibi_slopes(n):
    return 2.0 ** (-8.0 * np.arange(1, n + 1, dtype=np.float64) / n)


def _np_split3(v):
    v = np.asarray(v, np.float32)
    to_bf = lambda a: a.astype(bf16).astype(np.float32)
    hi = to_bf(v)
    mid = to_bf(v - hi)
    lo = to_bf(v - hi - mid)
    return hi, mid, lo


def _a_tables():
    rate = A_RATES
    qaug = np.zeros((A_HEADS, KPAD - HEAD_DIM, 1), np.float32)
    for idx, piece in enumerate(_np_split3(rate * CHUNK) + _np_split3(rate)):
        qaug[:, idx, 0] = piece
    kk = np.arange(ATT_TQ)[:, None]
    qq = np.arange(ATT_TQ)[None, :]
    future = np.maximum(kk - qq, 0).astype(np.float32)
    corr = -2.0 * rate[:, None, None] * future[None]
    allowed = (kk // CHUNK) <= (qq // CHUNK)
    dtab = np.where(allowed[None], corr, NEG).astype(np.float32)
    return jnp.asarray(qaug), jnp.asarray(dtab)


def _band_frames(back, tq):
    k_pos = np.arange(back * LANES + tq)[:, None]
    q_pos = back * LANES + np.arange(tq)[None, :]
    return q_pos - k_pos, q_pos // CHUNK - k_pos // CHUNK


def _c_tables(sinks):
    back, tq = WIN_CHUNKS * CHUNK // LANES, C_BAND_TQ
    rel, chunk_diff = _band_frames(back, tq)
    allowed = (chunk_diff >= 0) & (chunk_diff <= WIN_CHUNKS)
    slopes = _alibi_slopes(C_HEADS)
    per_head = np.where(allowed[None], -slopes[:, None, None] * np.abs(rel)[None] * LOG2E, NEG)
    tab = per_head.reshape(C_KV_HEADS, C_GROUP, *rel.shape).transpose(0, 2, 1, 3)
    tab = tab.reshape(C_KV_HEADS, rel.shape[0], C_GROUP * tq).astype(np.float32)
    tab = np.concatenate([tab, np.full((C_KV_HEADS, back * LANES, tab.shape[2]), NEG, np.float32)], 1)
    sink = jnp.repeat(sinks.astype(f32) * LOG2E, tq).reshape(C_KV_HEADS, 1, C_GROUP * tq)
    return jnp.asarray(tab), sink, back


def _d_tables(rel_table):
    back, t = D_LEFT_CHUNKS * CHUNK // LANES, D_BAND_TQ
    band = back * LANES + t
    rel, chunk_diff = _band_frames(back, t)
    allowed = (chunk_diff >= 0) & (chunk_diff <= D_LEFT_CHUNKS)
    tbl = rel_table.astype(f32) * LOG2E
    n_lo = (t - 1) - (CHUNK - 1)
    n_hi = (band - 1) - REL_MAX
    diag = jnp.concatenate([jnp.broadcast_to(tbl[:, :1], (D_HEADS, n_lo)), tbl,
                            jnp.broadcast_to(tbl[:, -1:], (D_HEADS, n_hi))], axis=1)
    m = t + LANES - 1
    blocks = []
    for kb in range(band // LANES):
        lo = rel[kb * LANES:(kb + 1) * LANES].min()
        if lo >= REL_MAX:
            blocks.append(jnp.broadcast_to(tbl[:, -1:, None], (D_HEADS, LANES, t)))
            continue
        window = diag[:, band - (kb + 1) * LANES:band - (kb + 1) * LANES + m]
        skew = jnp.broadcast_to(window[:, None, :], (D_HEADS, LANES + 1, m)).reshape(D_HEADS, -1)
        skew = skew[:, :LANES * (m + 1)].reshape(D_HEADS, LANES, m + 1)[:, :, :t]
        blocks.append(jnp.flip(skew, axis=1))
    blocks += [jnp.zeros((D_HEADS, LANES, t), f32)] * back
    allowed = np.concatenate([allowed, np.zeros((back * LANES, t), bool)], axis=0)
    tab = jnp.where(jnp.asarray(allowed)[None], jnp.concatenate(blocks, axis=1), NEG)
    sink = jnp.full((D_HEADS, 1, t), NEG, f32)
    return tab, sink, back


def _fixed_max_ok(q_gain, k_gain, bias_range=0.0):
    spread = (2.0 * 1.02 * QK_SCALE * HEAD_DIM
              * jnp.max(jnp.abs(q_gain.astype(f32))) * jnp.max(jnp.abs(k_gain.astype(f32))))
    return (spread + bias_range <= FIXED_MAX_LIMIT).astype(jnp.int32).reshape(1)


def _pad_rows(w_t, rows):
    return jnp.pad(w_t, ((0, rows - w_t.shape[0]), (0, 0)))


def _even_layer(x, ln_g, w_in, w_out, a_qn_g, a_kn_g, a_lq1, a_lk1, a_lq2, a_lk2, a_subln_g,
                b_qn_g, b_kn_g, b_f_bias, layer_idx):
    bsz, seq, _ = x.shape
    colv = lambda v: v.astype(f32).reshape(-1, 1)
    n_wide = EVEN["bf"][0]
    wt = w_in[:, :n_wide].T.astype(bf16)
    wf = _pad_rows(w_in[:, n_wide:].T.astype(bf16), BF16_ROWS)
    aq, ak, av, asg, bq, bk, bv, bsg, am, bm = _proj_even(
        x, ln_g.astype(f32).reshape(1, -1), wt, wf, colv(a_qn_g), colv(a_kn_g), colv(b_qn_g),
        colv(b_kn_g), colv(b_f_bias))
    lam_init = 0.8 - 0.6 * math.exp(-0.3 * layer_idx)
    qaug_a, dtab = _a_tables()
    lamv = jnp.stack([a_lq1, a_lk1, a_lq2, a_lk2]).astype(f32)
    mix_a = _attn_a(_fixed_max_ok(a_qn_g, a_kn_g), aq, ak, av, asg, am, dtab, qaug_a,
                    colv(a_subln_g), lamv, lam_init)
    qaug_b = np.zeros((KPAD - HEAD_DIM, 1), np.float32)
    qaug_b[:B_BIAS_ROWS] = 1.0
    mix_b = _attn_b(_fixed_max_ok(b_qn_g, b_kn_g), bq, bk, bv, bsg, bm, jnp.asarray(qaug_b))
    return mix_a.reshape(bsz, -1, seq), mix_b.reshape(bsz, -1, seq), w_out.T.astype(bf16), x


def _odd_layer(pending, ln_g, w_in, w_out, c_qn_g, c_kn_g, c_sinks, d_qn_g, d_kn_g, d_rel_bias):
    bsz, seq, _ = pending[3].shape
    colv = lambda v: v.astype(f32).reshape(-1, 1)
    x, cq, ck, cv, csg, dq, dk, dv, dsg, cm, dm = _proj_odd(
        *pending, ln_g.astype(f32).reshape(1, -1), w_in.T.astype(bf16), colv(c_qn_g), colv(c_kn_g),
        colv(d_qn_g), colv(d_kn_g))
    tab_c, sink_c, back_c = _c_tables(c_sinks)
    fixed_c = _fixed_max_ok(c_qn_g, c_kn_g, LOG2E * jnp.maximum(jnp.max(c_sinks.astype(f32)), 0.0))
    mix_c = _band_attn(fixed_c, cq, ck, cv, csg, cm, tab_c, sink_c, C_GROUP, back_c, C_BAND_TQ,
                       C_KV_HEADS, C_TILES_PER_STEP, "attn_c")
    tab_d, sink_d, back_d = _d_tables(d_rel_bias)
    fixed_d = _fixed_max_ok(d_qn_g, d_kn_g, LOG2E * jnp.max(jnp.abs(d_rel_bias.astype(f32))))
    mix_d = _band_attn(fixed_d, dq, dk, dv, dsg, dm, tab_d, sink_d, 1, back_d, D_BAND_TQ,
                       D_HEADS_PER_STEP, D_TILES_PER_STEP, "attn_d")
    return mix_c.reshape(bsz, -1, seq), mix_d.reshape(bsz, -1, seq), w_out.T.astype(bf16), x


def kernel(x, even_ln_g, even_w_in, even_w_out, a_q_norm_g, a_k_norm_g, a_lambda_q1, a_lambda_k1, a_lambda_q2, a_lambda_k2, a_subln_g, b_q_norm_g, b_k_norm_g, b_forget_bias, odd_ln_g, odd_w_in, odd_w_out, c_q_norm_g, c_k_norm_g, c_sinks, d_q_norm_g, d_k_norm_g, d_rel_bias):
    depth = even_ln_g.shape[0] + odd_ln_g.shape[0]
    seq = x.shape[1]
    assert x.shape[2] == D_MODEL and even_w_in.shape[2] == P_EVEN and odd_w_in.shape[2] == P_ODD
    for tile in (PROJ_TOKENS, OUT_TOKENS, ATT_TQ, C_BAND_TQ * C_TILES_PER_STEP,
                 D_BAND_TQ * D_TILES_PER_STEP):
        assert seq % tile == 0, (seq, tile)
    pending = None
    for i in range(depth):
        j = i // 2
        if i % 2 == 0:
            if pending is not None:
                x = _out_proj(*pending)
            pending = _even_layer(x, even_ln_g[j], even_w_in[j], even_w_out[j], a_q_norm_g[j],
                                  a_k_norm_g[j], a_lambda_q1[j], a_lambda_k1[j], a_lambda_q2[j],
                                  a_lambda_k2[j], a_subln_g[j], b_q_norm_g[j], b_k_norm_g[j],
                                  b_forget_bias[j], i)
        else:
            pending = _odd_layer(pending, odd_ln_g[j], odd_w_in[j], odd_w_out[j], c_q_norm_g[j],
                                 c_k_norm_g[j], c_sinks[j], d_q_norm_g[j], d_k_norm_g[j],
                                 d_rel_bias[j])
    return _out_proj(*pending)
```

```python
import functools
import math

import numpy as np
import jax
import jax.numpy as jnp
from jax import lax
from jax.experimental import pallas as pl
from jax.experimental.pallas import tpu as pltpu

D_MODEL = 1024
CHUNK = 64
HEAD_DIM = 64
NORM_EPS = 1e-6

A_HEADS = 4
A_STREAMS = 2 * A_HEADS
A_VDIM = 2 * HEAD_DIM
B_HEADS = 8
C_HEADS = 8
C_KV_HEADS = 2
C_GROUP = C_HEADS // C_KV_HEADS
WIN_CHUNKS = 2
D_HEADS = 8
D_LEFT_CHUNKS = 8
REL_MAX = 256


def _row_ranges(names, sizes):
    stops = np.cumsum(sizes)
    return {n: (int(b - w), int(b)) for n, w, b in zip(names, sizes, stops)}


EVEN = _row_ranges(("aq", "ak", "av", "ag", "bq", "bk", "bv", "bg", "bf"),
                   (A_STREAMS * HEAD_DIM,) * 2 + (A_HEADS * A_VDIM,) * 2 + (B_HEADS * HEAD_DIM,) * 4
                   + (B_HEADS,))
ODD = _row_ranges(("cq", "ck", "cv", "cg", "dq", "dk", "dv", "dg"),
                  (C_HEADS * HEAD_DIM, C_KV_HEADS * HEAD_DIM, C_KV_HEADS * HEAD_DIM, C_HEADS * HEAD_DIM)
                  + (D_HEADS * HEAD_DIM,) * 4)
P_EVEN = EVEN["bf"][1]
P_ODD = ODD["dg"][1]

LOG2E = 1.4426950408889634
QK_SCALE = HEAD_DIM ** -0.5 * LOG2E
NEG = -1e30
A_RATES = (2.0 ** (-8.0 * np.arange(1, A_HEADS + 1) / A_HEADS) * LOG2E).astype(np.float32)

LANES = 128
SUBLANES = 8
KPAD = 128
BF16_ROWS = 16

PROJ_TOKENS = 512
OUT_TOKENS = 1024
ATT_TQ = 512
ATT_TK = 512
A_HEADS_PER_STEP = 4
B_HEADS_PER_STEP = 8
A_BIAS_ROWS = 6
B_BIAS_ROWS = 3
MAX_ROWS = 3
FIXED_MAX_LIMIT = 96.0
C_BAND_TQ = 128
D_BAND_TQ = 256
D_HEADS_PER_STEP = 4
C_TILES_PER_STEP = 8
C_KV_PER_STEP = 1
D_TILES_PER_STEP = 4
VMEM_LIMIT = 56 * 1024 * 1024

f32 = jnp.float32
bf16 = jnp.bfloat16


def _split3(v):
    hi = v.astype(bf16).astype(f32)
    r = v - hi
    mid = r.astype(bf16).astype(f32)
    lo = (r - mid).astype(bf16).astype(f32)
    return hi, mid, lo


def _silu(z):
    return z * (1.0 / (1.0 + jnp.exp(-z)))


def _rms_rows(x, g_ref):
    ms = jnp.mean(x * x, axis=-1, keepdims=True)
    return (x * lax.rsqrt(ms + NORM_EPS) * g_ref[...]).astype(bf16)


def _residual_add(m1_ref, m2_ref, wt_ref, x_ref):
    half = m1_ref.shape[1]
    y_t = (jnp.dot(wt_ref[:, :half], m1_ref[0], preferred_element_type=f32)
           + jnp.dot(wt_ref[:, half:], m2_ref[0], preferred_element_type=f32))
    return x_ref[0] + y_t.T


def _proj_t(wt_ref, rows, xn):
    return lax.dot_general(wt_ref[rows[0]:rows[1], :], xn, (((1,), (1,)), ((), ())),
                           preferred_element_type=f32)


def _head_norm(z_t, gain_col, mult):
    n = z_t.shape[0] // HEAD_DIM
    z3 = z_t.reshape(n, HEAD_DIM, z_t.shape[1])
    ms = jnp.mean(z3 * z3, axis=1, keepdims=True)
    return z3 * lax.rsqrt(ms + NORM_EPS) * (gain_col[...] * mult)[None]


def _ones_rows(row, first):
    return jnp.where((row >= first) & (row < first + MAX_ROWS), 1.0, 0.0)


def _store_heads(o_ref, z_t):
    o_ref[0] = z_t.reshape(o_ref.shape[1], o_ref.shape[2], z_t.shape[1]).astype(bf16)


def _store_keys(k_ref, kn, aug_fn):
    n, _, t = kn.shape
    zeros = jnp.zeros((KPAD - HEAD_DIM - BF16_ROWS, t), f32)
    for s in range(n):
        blk = jnp.concatenate([kn[s], aug_fn(s), zeros], axis=0)
        k_ref[0, s] = blk.T.astype(bf16)


def _proj_even_kernel(x_ref, lng_ref, wt_ref, wf_ref, aqg_ref, akg_ref, bqg_ref, bkg_ref, bfb_ref,
                      tri_ref, aq_ref, ak_ref, av_ref, asg_ref, bq_ref, bk_ref, bv_ref, bsg_ref,
                      am_ref, bm_ref, cum_ref):
    t = pl.program_id(1)
    tt = x_ref.shape[1]
    xn = _rms_rows(x_ref[0], lng_ref)
    row = lax.broadcasted_iota(jnp.int32, (BF16_ROWS, tt), 0)

    z = _proj_t(wf_ref, (0, BF16_ROWS), xn)[:B_HEADS] + bfb_ref[...]
    aqn = _head_norm(_proj_t(wt_ref, EVEN["aq"], xn), aqg_ref, QK_SCALE)
    aq_ref[0] = aqn.astype(bf16)

    pos = t * tt + lax.broadcasted_iota(jnp.int32, (BF16_ROWS, tt), 1)
    pos_a = lax.shift_right_logical(pos, int(math.log2(CHUNK))).astype(f32)
    pos_b = lax.bitwise_and(pos, CHUNK - 1).astype(f32)
    aug_a = jnp.where(row < A_BIAS_ROWS // 2, pos_a,
                      jnp.where(row < A_BIAS_ROWS, pos_b, _ones_rows(row, A_BIAS_ROWS)))
    akn = _head_norm(_proj_t(wt_ref, EVEN["ak"], xn), akg_ref, 1.0)
    _store_keys(ak_ref, akn, lambda s: aug_a)
    self_a = jnp.sum(aqn * akn, axis=1, keepdims=True)
    for s in range(A_STREAMS):
        am_ref[0, s] = self_a[s] + float(A_RATES[s // 2]) * pos[:1].astype(f32)

    _store_heads(asg_ref, _silu(_proj_t(wt_ref, EVEN["ag"], xn)))
    bqn = _head_norm(_proj_t(wt_ref, EVEN["bq"], xn), bqg_ref, QK_SCALE)
    bq_ref[0] = bqn.astype(bf16)

    log_f = jnp.minimum(z, 0.0) - jnp.log(1.0 + jnp.exp(-jnp.abs(z)))
    pieces = jnp.concatenate(_split3(log_f) + (jnp.zeros_like(log_f),), axis=0).astype(bf16)
    part = jnp.dot(pieces, tri_ref[...], preferred_element_type=f32)
    local = part[:B_HEADS] + part[B_HEADS:2 * B_HEADS] + part[2 * B_HEADS:3 * B_HEADS]

    @pl.when(t == 0)
    def _():
        cum_ref[...] = jnp.zeros_like(cum_ref)

    cum = cum_ref[...] + local
    cum_ref[...] = cum[:, tt - 1:tt]
    gate = -LOG2E * cum
    g_hi, g_mid, g_lo = _split3(gate)

    def aug_b(s):
        pick = lambda a: jnp.broadcast_to(a[s:s + 1], (BF16_ROWS, tt))
        return jnp.where(row == 0, pick(g_hi),
                         jnp.where(row == 1, pick(g_mid),
                                   jnp.where(row == 2, pick(g_lo), _ones_rows(row, B_BIAS_ROWS))))

    bkn = _head_norm(_proj_t(wt_ref, EVEN["bk"], xn), bkg_ref, 1.0)
    _store_keys(bk_ref, bkn, aug_b)
    self_b = jnp.sum(bqn * bkn, axis=1, keepdims=True)
    for s in range(B_HEADS):
        bm_ref[0, s] = self_b[s] + gate[s:s + 1]
    _store_heads(bsg_ref, _silu(_proj_t(wt_ref, EVEN["bg"], xn)))
    _store_heads(av_ref, _proj_t(wt_ref, EVEN["av"], xn))
    _store_heads(bv_ref, _proj_t(wt_ref, EVEN["bv"], xn))


def _proj_even(x, ln_g, wt, wf, aqg, akg, bqg, bkg, bfb):
    bsz, seq, _ = x.shape
    tt = PROJ_TOKENS
    col = lambda n: pl.BlockSpec((n, 1), lambda b, t: (0, 0))
    fm = lambda n, d: pl.BlockSpec((1, n, d, tt), lambda b, t: (b, 0, 0, t))
    km = lambda n: pl.BlockSpec((1, n, tt, KPAD), lambda b, t: (b, 0, t, 0))
    fm_shape = lambda n, d: jax.ShapeDtypeStruct((bsz, n, d, seq), bf16)
    km_shape = lambda n: jax.ShapeDtypeStruct((bsz, n, seq, KPAD), bf16)
    return pl.pallas_call(
        _proj_even_kernel,
        grid=(bsz, seq // tt),
        in_specs=[
            pl.BlockSpec((1, tt, D_MODEL), lambda b, t: (b, t, 0)),
            pl.BlockSpec((1, D_MODEL), lambda b, t: (0, 0)),
            pl.BlockSpec(wt.shape, lambda b, t: (0, 0)),
            pl.BlockSpec(wf.shape, lambda b, t: (0, 0)),
            col(HEAD_DIM), col(HEAD_DIM), col(HEAD_DIM), col(HEAD_DIM), col(B_HEADS),
            pl.BlockSpec((tt, tt), lambda b, t: (0, 0)),
        ],
        out_specs=[fm(A_STREAMS, HEAD_DIM), km(A_STREAMS), fm(A_HEADS, A_VDIM), fm(A_HEADS, A_VDIM),
                   fm(B_HEADS, HEAD_DIM), km(B_HEADS), fm(B_HEADS, HEAD_DIM), fm(B_HEADS, HEAD_DIM),
                   fm(A_STREAMS, 1), fm(B_HEADS, 1)],
        out_shape=[fm_shape(A_STREAMS, HEAD_DIM), km_shape(A_STREAMS), fm_shape(A_HEADS, A_VDIM),
                   fm_shape(A_HEADS, A_VDIM), fm_shape(B_HEADS, HEAD_DIM), km_shape(B_HEADS),
                   fm_shape(B_HEADS, HEAD_DIM), fm_shape(B_HEADS, HEAD_DIM),
                   jax.ShapeDtypeStruct((bsz, A_STREAMS, 1, seq), f32),
                   jax.ShapeDtypeStruct((bsz, B_HEADS, 1, seq), f32)],
        scratch_shapes=[pltpu.VMEM((B_HEADS, 1), f32)],
        compiler_params=pltpu.CompilerParams(
            dimension_semantics=("arbitrary", "arbitrary"), vmem_limit_bytes=VMEM_LIMIT),
        name="proj_even",
    )(x, ln_g, wt, wf, aqg, akg, bqg, bkg, bfb,
      jnp.asarray(np.triu(np.ones((tt, tt), np.float32)), bf16))


def _proj_odd_kernel(m1_ref, m2_ref, wo_ref, x_ref, lng_ref, wt_ref, cqg_ref, ckg_ref, dqg_ref,
                     dkg_ref, x1_ref, cq_ref, ck_ref, cv_ref, csg_ref, dq_ref, dk_ref, dv_ref,
                     dsg_ref, cm_ref, dm_ref):
    tt = x_ref.shape[1]
    x1 = _residual_add(m1_ref, m2_ref, wo_ref, x_ref)
    x1_ref[0] = x1
    xn = _rms_rows(x1, lng_ref)
    ones = _ones_rows(lax.broadcasted_iota(jnp.int32, (BF16_ROWS, tt), 0), 0)
    aug = lambda s: ones

    cqn = _head_norm(_proj_t(wt_ref, ODD["cq"], xn), cqg_ref, QK_SCALE)
    cq_ref[0] = cqn.astype(bf16)
    ckn = _head_norm(_proj_t(wt_ref, ODD["ck"], xn), ckg_ref, 1.0)
    _store_keys(ck_ref, ckn, aug)
    cm_ref[0] = jnp.sum(cqn.reshape(C_KV_HEADS, C_GROUP, HEAD_DIM, tt) * ckn[:, None], axis=2,
                        keepdims=True).reshape(C_HEADS, 1, tt)
    _store_heads(cv_ref, _proj_t(wt_ref, ODD["cv"], xn))
    _store_heads(csg_ref, _silu(_proj_t(wt_ref, ODD["cg"], xn)))
    dqn = _head_norm(_proj_t(wt_ref, ODD["dq"], xn), dqg_ref, QK_SCALE)
    dq_ref[0] = dqn.astype(bf16)
    dkn = _head_norm(_proj_t(wt_ref, ODD["dk"], xn), dkg_ref, 1.0)
    _store_keys(dk_ref, dkn, aug)
    dm_ref[0] = jnp.sum(dqn * dkn, axis=1, keepdims=True)
    _store_heads(dsg_ref, _silu(_proj_t(wt_ref, ODD["dg"], xn)))
    _store_heads(dv_ref, _proj_t(wt_ref, ODD["dv"], xn))


def _proj_odd(m1, m2, wo_t, x, ln_g, wt, cqg, ckg, dqg, dkg):
    bsz, seq, _ = x.shape
    tt = PROJ_TOKENS
    half = m1.shape[1]
    col = lambda n: pl.BlockSpec((n, 1), lambda b, t: (0, 0))
    rows = pl.BlockSpec((1, tt, D_MODEL), lambda b, t: (b, t, 0))
    fm = lambda n: pl.BlockSpec((1, n, HEAD_DIM, tt), lambda b, t: (b, 0, 0, t))
    km = lambda n: pl.BlockSpec((1, n, tt, KPAD), lambda b, t: (b, 0, t, 0))
    fm_shape = lambda n: jax.ShapeDtypeStruct((bsz, n, HEAD_DIM, seq), bf16)
    km_shape = lambda n: jax.ShapeDtypeStruct((bsz, n, seq, KPAD), bf16)
    return pl.pallas_call(
        _proj_odd_kernel,
        grid=(bsz, seq // tt),
        in_specs=[
            pl.BlockSpec((1, half, tt), lambda b, t: (b, 0, t)),
            pl.BlockSpec((1, half, tt), lambda b, t: (b, 0, t)),
            pl.BlockSpec(wo_t.shape, lambda b, t: (0, 0)),
            rows,
            pl.BlockSpec((1, D_MODEL), lambda b, t: (0, 0)),
            pl.BlockSpec(wt.shape, lambda b, t: (0, 0)),
            col(HEAD_DIM), col(HEAD_DIM), col(HEAD_DIM), col(HEAD_DIM),
        ],
        out_specs=[rows, fm(C_HEADS), km(C_KV_HEADS), fm(C_KV_HEADS), fm(C_HEADS),
                   fm(D_HEADS), km(D_HEADS), fm(D_HEADS), fm(D_HEADS),
                   pl.BlockSpec((1, C_HEADS, 1, tt), lambda b, t: (b, 0, 0, t)),
                   pl.BlockSpec((1, D_HEADS, 1, tt), lambda b, t: (b, 0, 0, t))],
        out_shape=[jax.ShapeDtypeStruct(x.shape, f32),
                   fm_shape(C_HEADS), km_shape(C_KV_HEADS), fm_shape(C_KV_HEADS), fm_shape(C_HEADS),
                   fm_shape(D_HEADS), km_shape(D_HEADS), fm_shape(D_HEADS), fm_shape(D_HEADS),
                   jax.ShapeDtypeStruct((bsz, C_HEADS, 1, seq), f32),
                   jax.ShapeDtypeStruct((bsz, D_HEADS, 1, seq), f32)],
        compiler_params=pltpu.CompilerParams(
            dimension_semantics=("arbitrary", "arbitrary"), vmem_limit_bytes=VMEM_LIMIT),
        name="proj_odd",
    )(m1, m2, wo_t, x, ln_g, wt, cqg, ckg, dqg, dkg)


def _online_step(carry, s, v):
    m, l, acc = carry
    m_new = jnp.maximum(m, jnp.max(s, axis=0, keepdims=True))
    p = jnp.exp2(s - m_new)
    alpha = jnp.exp2(m - m_new)
    l = alpha * l + jnp.sum(p, axis=0, keepdims=True)
    acc = alpha * acc + jnp.dot(v, p.astype(bf16), preferred_element_type=f32)
    return m_new, l, acc


def _colsum8(p):
    return p.reshape(p.shape[0] // SUBLANES, SUBLANES, p.shape[1]).sum(axis=0)


def _augment_q(q, aug_col, n_bias, m=None):
    tq = q.shape[1]
    aug = jnp.broadcast_to(aug_col, (KPAD - HEAD_DIM, tq))
    if m is not None:
        row = lax.broadcasted_iota(jnp.int32, aug.shape, 0)
        pieces = _split3(-m)
        for r in range(MAX_ROWS):
            aug = jnp.where(row == n_bias + r, pieces[r], aug)
    return jnp.concatenate([q, aug.astype(bf16)], axis=0)


def _staggered(n, scores, finish):
    out, pending = [], scores(0)
    for c in range(1, n):
        nxt = scores(c)
        out.append(finish(c - 1, pending))
        pending = nxt
    out.append(finish(n - 1, pending))
    return out


def _diag_full(chains, d0, tq):
    def scores(c):
        k_at, _, q, aug_col, n_bias, diag_bias, _ = chains[c]
        return jnp.dot(k_at(d0, tq), _augment_q(q, aug_col, n_bias),
                       preferred_element_type=f32) + diag_bias

    def finish(c, s):
        m = jnp.max(s, axis=0, keepdims=True)
        p = jnp.exp2(s - m)
        return m, _colsum8(p), jnp.dot(chains[c][1](d0, tq), p.astype(bf16), preferred_element_type=f32)

    return _staggered(len(chains), scores, finish)


def _diag_halves(chains, q_aug, d0, tq):
    h = tq // 2
    d1 = pl.multiple_of(d0 + h, h)

    def scores(c):
        k_at = chains[c][0]
        return (jnp.dot(k_at(d0, h), q_aug[c], preferred_element_type=f32),
                jnp.dot(k_at(d1, h), q_aug[c][:, h:], preferred_element_type=f32))

    def finish(c, s):
        v_at, bias = chains[c][1], chains[c][5][:h, :h]
        p0 = jnp.concatenate([jnp.exp2(s[0][:, :h] + bias), jnp.exp2(s[0][:, h:])], axis=1)
        p1 = jnp.exp2(s[1] + bias)
        l0 = _colsum8(p0)
        a0 = jnp.dot(v_at(d0, h), p0.astype(bf16), preferred_element_type=f32)
        a1 = jnp.dot(v_at(d1, h), p1.astype(bf16), preferred_element_type=f32)
        return (jnp.concatenate([l0[:, :h], l0[:, h:] + _colsum8(p1)], axis=1),
                jnp.concatenate([a0[:, :h], a0[:, h:] + a1], axis=1))

    return _staggered(len(chains), scores, finish)


def _causal_sweep(chains, i, fixed_max, l_s, acc_s):
    n = len(chains)
    tq = chains[0][2].shape[1]
    d0 = pl.multiple_of(i * tq, tq)
    n_tiles = i * (tq // ATT_TK)
    tile = lambda j: (pl.multiple_of(j * ATT_TK, ATT_TK), ATT_TK)

    if fixed_max:
        q_aug = [_augment_q(q, aug_col, n_bias, m) for _, _, q, aug_col, n_bias, _, m in chains]

        for c, (l, acc) in enumerate(_diag_halves(chains, q_aug, d0, tq)):
            l_s[c] = l
            acc_s[c] = acc

        def sweep(tiles):
            work = [(j, c) for j in tiles for c in range(n)]

            def finish(w, s):
                j, c = work[w]
                p = jnp.exp2(s)
                l_s[c] += _colsum8(p)
                acc_s[c] += jnp.dot(chains[c][1](*tile(j)), p.astype(bf16), preferred_element_type=f32)

            def scores(w):
                j, c = work[w]
                return jnp.dot(chains[c][0](*tile(j)), q_aug[c], preferred_element_type=f32)

            _staggered(len(work), scores, finish)

        def pair(j, carry):
            sweep((2 * j, 2 * j + 1))
            return carry

        lax.fori_loop(0, n_tiles // 2, pair, 0)

        @pl.when(n_tiles % 2 == 1)
        def _():
            sweep((n_tiles - 1,))

        return [(acc_s[c], jnp.sum(l_s[c], axis=0, keepdims=True)) for c in range(n)]

    q_aug = [_augment_q(q, aug_col, n_bias) for _, _, q, aug_col, n_bias, _, _ in chains]

    def body(j, carries):
        scores = lambda c: jnp.dot(chains[c][0](*tile(j)), q_aug[c], preferred_element_type=f32)
        finish = lambda c, s: _online_step(carries[c], s, chains[c][1](*tile(j)))
        return tuple(_staggered(n, scores, finish))

    init = tuple((m, jnp.sum(l, axis=0, keepdims=True), acc) for m, l, acc in _diag_full(chains, d0, tq))
    return [(acc, l) for _, l, acc in lax.fori_loop(0, n_tiles, body, init)]


def _either(flag, run):
    @pl.when(flag != 0)
    def _():
        run(True)

    @pl.when(flag == 0)
    def _():
        run(False)


def _either_sweep(fixed_ref, run):
    _either(fixed_ref[0], run)


def _attn_a_kernel(fixed_ref, q_ref, k_ref, v_ref, sg_ref, m_ref, dtab_ref, qaug_ref, subg_ref,
                   lamv_ref, o_ref, l_s, acc_s, *, lam_init):
    i = pl.program_id(2)
    chains = []
    for h in range(A_HEADS_PER_STEP):
        v_at = lambda start, size, h=h: v_ref[0, h, :, pl.ds(start, size)]
        for c in range(2):
            s = 2 * h + c
            k_at = lambda start, size, s=s: k_ref[0, s, pl.ds(start, size), :]
            chains.append((k_at, v_at, q_ref[0, s], qaug_ref[h], A_BIAS_ROWS, dtab_ref[h],
                           m_ref[0, s]))

    def run(fixed_max):
        lv = lamv_ref[...]
        lam = (jnp.exp(jnp.sum(lv[0:1] * lv[1:2], axis=1, keepdims=True))
               - jnp.exp(jnp.sum(lv[2:3] * lv[3:4], axis=1, keepdims=True)) + lam_init)
        outs = [acc * (1.0 / l) for acc, l in _causal_sweep(chains, i, fixed_max, l_s, acc_s)]
        for h in range(A_HEADS_PER_STEP):
            o = outs[2 * h] - lam * outs[2 * h + 1]
            ms = jnp.mean(o * o, axis=0, keepdims=True)
            y = o * lax.rsqrt(ms + NORM_EPS) * (subg_ref[...] * (1.0 - lam_init))
            o_ref[0, h] = (y * sg_ref[0, h].astype(f32)).astype(bf16)

    _either_sweep(fixed_ref, run)


def _attn_a(fixed, aq, ak, av, asg, am, dtab, qaug, subg, lamv, lam_init):
    bsz, _, _, seq = aq.shape
    nq = seq // ATT_TQ
    hs = A_HEADS_PER_STEP
    return pl.pallas_call(
        functools.partial(_attn_a_kernel, lam_init=lam_init),
        grid=(bsz, A_HEADS // hs, nq),
        in_specs=[
            pl.BlockSpec(memory_space=pltpu.SMEM),
            pl.BlockSpec((1, 2 * hs, HEAD_DIM, ATT_TQ), lambda b, h, i: (b, h, 0, i)),
            pl.BlockSpec((1, 2 * hs, seq, KPAD), lambda b, h, i: (b, h, 0, 0)),
            pl.BlockSpec((1, hs, A_VDIM, seq), lambda b, h, i: (b, h, 0, 0)),
            pl.BlockSpec((1, hs, A_VDIM, ATT_TQ), lambda b, h, i: (b, h, 0, i)),
            pl.BlockSpec((1, 2 * hs, 1, ATT_TQ), lambda b, h, i: (b, h, 0, i)),
            pl.BlockSpec((hs, ATT_TQ, ATT_TQ), lambda b, h, i: (h, 0, 0)),
            pl.BlockSpec((hs, KPAD - HEAD_DIM, 1), lambda b, h, i: (h, 0, 0)),
            pl.BlockSpec((A_VDIM, 1), lambda b, h, i: (0, 0)),
            pl.BlockSpec(lamv.shape, lambda b, h, i: (0, 0)),
        ],
        out_specs=pl.BlockSpec((1, hs, A_VDIM, ATT_TQ), lambda b, h, i: (b, h, 0, i)),
        out_shape=jax.ShapeDtypeStruct((bsz, A_HEADS, A_VDIM, seq), bf16),
        scratch_shapes=[pltpu.VMEM((2 * hs, SUBLANES, ATT_TQ), f32),
                        pltpu.VMEM((2 * hs, A_VDIM, ATT_TQ), f32)],
        compiler_params=pltpu.CompilerParams(
            dimension_semantics=("arbitrary", "arbitrary", "arbitrary"),
            vmem_limit_bytes=VMEM_LIMIT),
        name="attn_a",
    )(fixed, aq, ak, av, asg, am, dtab, qaug, subg, lamv)


def _attn_b_kernel(fixed_ref, q_ref, k_ref, v_ref, sg_ref, m_ref, qaug_ref, o_ref, l_s, acc_s):
    i = pl.program_id(2)

    def run(fixed_max):
        causal = jnp.where(lax.broadcasted_iota(jnp.int32, (ATT_TQ, ATT_TQ), 0)
                           <= lax.broadcasted_iota(jnp.int32, (ATT_TQ, ATT_TQ), 1), 0.0, NEG)
        chains = []
        for h in range(B_HEADS_PER_STEP):
            k_at = lambda start, size, h=h: k_ref[0, h, pl.ds(start, size), :]
            v_at = lambda start, size, h=h: v_ref[0, h, :, pl.ds(start, size)]
            chains.append((k_at, v_at, q_ref[0, h], qaug_ref[...], B_BIAS_ROWS, causal, m_ref[0, h]))
        for h, (acc, l) in enumerate(_causal_sweep(chains, i, fixed_max, l_s, acc_s)):
            o_ref[0, h] = (acc * (1.0 / l) * sg_ref[0, h].astype(f32)).astype(bf16)

    _either_sweep(fixed_ref, run)


def _attn_b(fixed, bq, bk, bv, bsg, bm, qaug):
    bsz, _, _, seq = bq.shape
    nq = seq // ATT_TQ
    hs = B_HEADS_PER_STEP
    return pl.pallas_call(
        _attn_b_kernel,
        grid=(bsz, B_HEADS // hs, nq),
        in_specs=[
            pl.BlockSpec(memory_space=pltpu.SMEM),
            pl.BlockSpec((1, hs, HEAD_DIM, ATT_TQ), lambda b, h, i: (b, h, 0, i)),
            pl.BlockSpec((1, hs, seq, KPAD), lambda b, h, i: (b, h, 0, 0)),
            pl.BlockSpec((1, hs, HEAD_DIM, seq), lambda b, h, i: (b, h, 0, 0)),
            pl.BlockSpec((1, hs, HEAD_DIM, ATT_TQ), lambda b, h, i: (b, h, 0, i)),
            pl.BlockSpec((1, hs, 1, ATT_TQ), lambda b, h, i: (b, h, 0, i)),
            pl.BlockSpec((KPAD - HEAD_DIM, 1), lambda b, h, i: (0, 0)),
        ],
        out_specs=pl.BlockSpec((1, hs, HEAD_DIM, ATT_TQ), lambda b, h, i: (b, h, 0, i)),
        out_shape=jax.ShapeDtypeStruct((bsz, B_HEADS, HEAD_DIM, seq), bf16),
        scratch_shapes=[pltpu.VMEM((hs, SUBLANES, ATT_TQ), f32), pltpu.VMEM((hs, HEAD_DIM, ATT_TQ), f32)],
        compiler_params=pltpu.CompilerParams(
            dimension_semantics=("arbitrary", "arbitrary", "arbitrary"),
            vmem_limit_bytes=VMEM_LIMIT),
        name="attn_b",
    )(fixed, bq, bk, bv, bsg, bm, qaug)


def _band_kernel(fixed_ref, q_ref, k_ref, v_ref, sg_ref, m_ref, tab_ref, sink_ref, o_ref,
                 *, group, back, tq, defer=False):
    blocks = tq // LANES
    band = (back + blocks) * LANES
    tiles = q_ref.shape[3] // tq
    work = [(u, h) for u in range(tiles) for h in range(k_ref.shape[1])]
    grouped = lambda ref, u, h: jnp.concatenate(
        [ref[0, h * group + g, :, u * tq:(u + 1) * tq] for g in range(group)], axis=1)

    def window(u):
        first = (pl.program_id(2) * tiles + u) * blocks
        return (pl.multiple_of(jnp.maximum(first - back, 0) * LANES, LANES),
                pl.multiple_of(jnp.maximum(back - first, 0) * LANES, LANES))

    def parts(fixed_max):
        def scores(w):
            u, h = work[w]
            k_start, tab_start = window(u)
            q = _augment_q(grouped(q_ref, u, h), 0.0, 0, grouped(m_ref, u, h) if fixed_max else None)
            s = jnp.dot(k_ref[0, h, pl.ds(k_start, band), :], q, preferred_element_type=f32)
            return s + tab_ref[h, pl.ds(tab_start, band), :]

        def finish(w, s):
            u, h = work[w]
            k_start, _ = window(u)
            if fixed_max:
                m = grouped(m_ref, u, h)
                p = jnp.exp2(s)
            else:
                m = jnp.maximum(jnp.max(s, axis=0, keepdims=True), sink_ref[h])
                p = jnp.exp2(s - m)
            l = jnp.sum(p, axis=0, keepdims=True) + jnp.exp2(sink_ref[h] - m)
            o = jnp.dot(v_ref[0, h, :, pl.ds(k_start, band)], p.astype(bf16),
                        preferred_element_type=f32) * (1.0 / l)
            for g in range(group):
                hq = h * group + g
                gate = sg_ref[0, hq, :, u * tq:(u + 1) * tq].astype(f32)
                o_ref[0, hq, :, u * tq:(u + 1) * tq] = (o[:, g * tq:(g + 1) * tq] * gate).astype(bf16)

        return len(work), scores, finish

    if defer:
        return parts
    _either_sweep(fixed_ref, lambda fixed_max: _staggered(*parts(fixed_max)))


def _band_specs(q, tab, sink, group, hs, tile):
    _, nheads, _, seq = q.shape
    heads = pl.BlockSpec((1, hs * group, HEAD_DIM, tile), lambda h, b, i: (b, h, 0, i))
    ins = [
        pl.BlockSpec(memory_space=pltpu.SMEM),
        heads,
        pl.BlockSpec((1, hs, seq, KPAD), lambda h, b, i: (b, h, 0, 0)),
        pl.BlockSpec((1, hs, HEAD_DIM, seq), lambda h, b, i: (b, h, 0, 0)),
        heads,
        pl.BlockSpec((1, hs * group, 1, tile), lambda h, b, i: (b, h, 0, i)),
        pl.BlockSpec((hs,) + tab.shape[1:], lambda h, b, i: (h, 0, 0)),
        pl.BlockSpec((hs,) + sink.shape[1:], lambda h, b, i: (h, 0, 0)),
    ]
    return ins, heads, jax.ShapeDtypeStruct(q.shape, bf16)


def _band_pair_kernel(*refs, first, second):
    n = len(refs) // 2 - 1
    a = _band_kernel(*refs[:n], refs[-2], defer=True, **first)
    b = _band_kernel(*refs[n:2 * n], refs[-1], defer=True, **second)

    def run(fixed_max):
        (na, *fa), (nb, *fb) = a(fixed_max), b(fixed_max)
        order = sorted([((i + 0.5) / na, 0, i) for i in range(na)]
                       + [((i + 0.5) / nb, 1, i) for i in range(nb)])
        pick = lambda w: ((fa, fb)[order[w][1]], order[w][2])
        _staggered(len(order), lambda w: pick(w)[0][0](pick(w)[1]),
                   lambda w, s: pick(w)[0][1](pick(w)[1], s))

    _either(jnp.minimum(refs[0][0], refs[n][0]), run)


def _band_pair(args_c, cfg_c, args_d, cfg_d):
    bsz, _, _, seq = args_c[1].shape
    specs, kernels, steps = [], [], []
    for args, (group, back, tq, hs, tiles) in ((args_c, cfg_c), (args_d, cfg_d)):
        q, tab, sink = args[1], args[6], args[7]
        specs.append(_band_specs(q, tab, sink, group, hs, tq * tiles))
        kernels.append(dict(group=group, back=back, tq=tq))
        steps.append((q.shape[1] // (group * hs), seq // (tq * tiles)))
    assert steps[0] == steps[1], steps
    return pl.pallas_call(
        functools.partial(_band_pair_kernel, first=kernels[0], second=kernels[1]),
        grid=(steps[0][0], bsz, steps[0][1]),
        in_specs=specs[0][0] + specs[1][0],
        out_specs=[specs[0][1], specs[1][1]],
        out_shape=[specs[0][2], specs[1][2]],
        compiler_params=pltpu.CompilerParams(
            dimension_semantics=("arbitrary", "arbitrary", "arbitrary"),
            vmem_limit_bytes=VMEM_LIMIT),
        name="attn_cd",
    )(*args_c, *args_d)


def _out_proj_kernel(m1_ref, m2_ref, wt_ref, x_ref, o_ref):
    o_ref[0] = _residual_add(m1_ref, m2_ref, wt_ref, x_ref)


def _out_proj(m1, m2, wt, x):
    bsz, seq, _ = x.shape
    tt = OUT_TOKENS
    half = m1.shape[1]
    return pl.pallas_call(
        _out_proj_kernel,
        grid=(bsz, seq // tt),
        in_specs=[
            pl.BlockSpec((1, half, tt), lambda b, t: (b, 0, t)),
            pl.BlockSpec((1, half, tt), lambda b, t: (b, 0, t)),
            pl.BlockSpec(wt.shape, lambda b, t: (0, 0)),
            pl.BlockSpec((1, tt, D_MODEL), lambda b, t: (b, t, 0)),
        ],
        out_specs=pl.BlockSpec((1, tt, D_MODEL), lambda b, t: (b, t, 0)),
        out_shape=jax.ShapeDtypeStruct(x.shape, f32),
        compiler_params=pltpu.CompilerParams(
            dimension_semantics=("arbitrary", "arbitrary"), vmem_limit_bytes=VMEM_LIMIT),
        name="out_proj",
    )(m1, m2, wt, x)


def _alibi_slopes(n):
    return 2.0 ** (-8.0 * np.arange(1, n + 1, dtype=np.float64) / n)


def _np_split3(v):
    v = np.asarray(v, np.float32)
    to_bf = lambda a: a.astype(bf16).astype(np.float32)
    hi = to_bf(v)
    mid = to_bf(v - hi)
    lo = to_bf(v - hi - mid)
    return hi, mid, lo


def _a_tables():
    rate = A_RATES
    qaug = np.zeros((A_HEADS, KPAD - HEAD_DIM, 1), np.float32)
    for idx, piece in enumerate(_np_split3(rate * CHUNK) + _np_split3(rate)):
        qaug[:, idx, 0] = piece
    kk = np.arange(ATT_TQ)[:, None]
    qq = np.arange(ATT_TQ)[None, :]
    future = np.maximum(kk - qq, 0).astype(np.float32)
    corr = -2.0 * rate[:, None, None] * future[None]
    allowed = (kk // CHUNK) <= (qq // CHUNK)
    dtab = np.where(allowed[None], corr, NEG).astype(np.float32)
    return jnp.asarray(qaug), jnp.asarray(dtab)


def _band_frames(back, tq):
    k_pos = np.arange(back * LANES + tq)[:, None]
    q_pos = back * LANES + np.arange(tq)[None, :]
    return q_pos - k_pos, q_pos // CHUNK - k_pos // CHUNK


def _c_tables(sinks):
    back, tq = WIN_CHUNKS * CHUNK // LANES, C_BAND_TQ
    rel, chunk_diff = _band_frames(back, tq)
    allowed = (chunk_diff >= 0) & (chunk_diff <= WIN_CHUNKS)
    slopes = _alibi_slopes(C_HEADS)
    per_head = np.where(allowed[None], -slopes[:, None, None] * np.abs(rel)[None] * LOG2E, NEG)
    tab = per_head.reshape(C_KV_HEADS, C_GROUP, *rel.shape).transpose(0, 2, 1, 3)
    tab = tab.reshape(C_KV_HEADS, rel.shape[0], C_GROUP * tq).astype(np.float32)
    tab = np.concatenate([tab, np.full((C_KV_HEADS, back * LANES, tab.shape[2]), NEG, np.float32)], 1)
    sink = jnp.repeat(sinks.astype(f32) * LOG2E, tq).reshape(C_KV_HEADS, 1, C_GROUP * tq)
    return jnp.asarray(tab), sink, back


def _d_tables(rel_table):
    back, t = D_LEFT_CHUNKS * CHUNK // LANES, D_BAND_TQ
    band = back * LANES + t
    rel, chunk_diff = _band_frames(back, t)
    allowed = (chunk_diff >= 0) & (chunk_diff <= D_LEFT_CHUNKS)
    tbl = rel_table.astype(f32) * LOG2E
    n_lo = (t - 1) - (CHUNK - 1)
    n_hi = (band - 1) - REL_MAX
    diag = jnp.concatenate([jnp.broadcast_to(tbl[:, :1], (D_HEADS, n_lo)), tbl,
                            jnp.broadcast_to(tbl[:, -1:], (D_HEADS, n_hi))], axis=1)
    m = t + LANES - 1
    blocks = []
    for kb in range(band // LANES):
        lo = rel[kb * LANES:(kb + 1) * LANES].min()
        if lo >= REL_MAX:
            blocks.append(jnp.broadcast_to(tbl[:, -1:, None], (D_HEADS, LANES, t)))
            continue
        window = diag[:, band - (kb + 1) * LANES:band - (kb + 1) * LANES + m]
        skew = jnp.broadcast_to(window[:, None, :], (D_HEADS, LANES + 1, m)).reshape(D_HEADS, -1)
        skew = skew[:, :LANES * (m + 1)].reshape(D_HEADS, LANES, m + 1)[:, :, :t]
        blocks.append(jnp.flip(skew, axis=1))
    blocks += [jnp.zeros((D_HEADS, LANES, t), f32)] * back
    allowed = np.concatenate([allowed, np.zeros((back * LANES, t), bool)], axis=0)
    tab = jnp.where(jnp.asarray(allowed)[None], jnp.concatenate(blocks, axis=1), NEG)
    sink = jnp.full((D_HEADS, 1, t), NEG, f32)
    return tab, sink, back


def _fixed_max_ok(q_gain, k_gain, bias_range=0.0):
    spread = (2.0 * 1.02 * QK_SCALE * HEAD_DIM
              * jnp.max(jnp.abs(q_gain.astype(f32))) * jnp.max(jnp.abs(k_gain.astype(f32))))
    return (spread + bias_range <= FIXED_MAX_LIMIT).astype(jnp.int32).reshape(1)


def _pad_rows(w_t, rows):
    return jnp.pad(w_t, ((0, rows - w_t.shape[0]), (0, 0)))


def _even_layer(x, ln_g, w_in, w_out, a_qn_g, a_kn_g, a_lq1, a_lk1, a_lq2, a_lk2, a_subln_g,
                b_qn_g, b_kn_g, b_f_bias, layer_idx):
    bsz, seq, _ = x.shape
    colv = lambda v: v.astype(f32).reshape(-1, 1)
    n_wide = EVEN["bf"][0]
    wt = w_in[:, :n_wide].T.astype(bf16)
    wf = _pad_rows(w_in[:, n_wide:].T.astype(bf16), BF16_ROWS)
    aq, ak, av, asg, bq, bk, bv, bsg, am, bm = _proj_even(
        x, ln_g.astype(f32).reshape(1, -1), wt, wf, colv(a_qn_g), colv(a_kn_g), colv(b_qn_g),
        colv(b_kn_g), colv(b_f_bias))
    lam_init = 0.8 - 0.6 * math.exp(-0.3 * layer_idx)
    qaug_a, dtab = _a_tables()
    lamv = jnp.stack([a_lq1, a_lk1, a_lq2, a_lk2]).astype(f32)
    mix_a = _attn_a(_fixed_max_ok(a_qn_g, a_kn_g), aq, ak, av, asg, am, dtab, qaug_a,
                    colv(a_subln_g), lamv, lam_init)
    qaug_b = np.zeros((KPAD - HEAD_DIM, 1), np.float32)
    qaug_b[:B_BIAS_ROWS] = 1.0
    mix_b = _attn_b(_fixed_max_ok(b_qn_g, b_kn_g), bq, bk, bv, bsg, bm, jnp.asarray(qaug_b))
    return mix_a.reshape(bsz, -1, seq), mix_b.reshape(bsz, -1, seq), w_out.T.astype(bf16), x


def _odd_layer(pending, ln_g, w_in, w_out, c_qn_g, c_kn_g, c_sinks, d_qn_g, d_kn_g, d_rel_bias):
    bsz, seq, _ = pending[3].shape
    colv = lambda v: v.astype(f32).reshape(-1, 1)
    x, cq, ck, cv, csg, dq, dk, dv, dsg, cm, dm = _proj_odd(
        *pending, ln_g.astype(f32).reshape(1, -1), w_in.T.astype(bf16), colv(c_qn_g), colv(c_kn_g),
        colv(d_qn_g), colv(d_kn_g))
    tab_c, sink_c, back_c = _c_tables(c_sinks)
    fixed_c = _fixed_max_ok(c_qn_g, c_kn_g, LOG2E * jnp.maximum(jnp.max(c_sinks.astype(f32)), 0.0))
    tab_d, sink_d, back_d = _d_tables(d_rel_bias)
    fixed_d = _fixed_max_ok(d_qn_g, d_kn_g, LOG2E * jnp.max(jnp.abs(d_rel_bias.astype(f32))))
    mix_c, mix_d = _band_pair(
        (fixed_c, cq, ck, cv, csg, cm, tab_c, sink_c),
        (C_GROUP, back_c, C_BAND_TQ, C_KV_PER_STEP, C_TILES_PER_STEP),
        (fixed_d, dq, dk, dv, dsg, dm, tab_d, sink_d),
        (1, back_d, D_BAND_TQ, D_HEADS_PER_STEP, D_TILES_PER_STEP))
    return mix_c.reshape(bsz, -1, seq), mix_d.reshape(bsz, -1, seq), w_out.T.astype(bf16), x


def kernel(x, even_ln_g, even_w_in, even_w_out, a_q_norm_g, a_k_norm_g, a_lambda_q1, a_lambda_k1, a_lambda_q2, a_lambda_k2, a_subln_g, b_q_norm_g, b_k_norm_g, b_forget_bias, odd_ln_g, odd_w_in, odd_w_out, c_q_norm_g, c_k_norm_g, c_sinks, d_q_norm_g, d_k_norm_g, d_rel_bias):
    depth = even_ln_g.shape[0] + odd_ln_g.shape[0]
    seq = x.shape[1]
    assert x.shape[2] == D_MODEL and even_w_in.shape[2] == P_EVEN and odd_w_in.shape[2] == P_ODD
    for tile in (PROJ_TOKENS, OUT_TOKENS, ATT_TQ, C_BAND_TQ * C_TILES_PER_STEP,
                 D_BAND_TQ * D_TILES_PER_STEP):
        assert seq % tile == 0, (seq, tile)
    pending = None
    for i in range(depth):
        j = i // 2
        if i % 2 == 0:
            if pending is not None:
                x = _out_proj(*pending)
            pending = _even_layer(x, even_ln_g[j], even_w_in[j], even_w_out[j], a_q_norm_g[j],
                                  a_k_norm_g[j], a_lambda_q1[j], a_lambda_k1[j], a_lambda_q2[j],
                                  a_lambda_k2[j], a_subln_g[j], b_q_norm_g[j], b_k_norm_g[j],
                                  b_forget_bias[j], i)
        else:
            pending = _odd_layer(pending, odd_ln_g[j], odd_w_in[j], odd_w_out[j], c_q_norm_g[j],
                                 c_k_norm_g[j], c_sinks[j], d_q_norm_g[j], d_k_norm_g[j],
                                 d_rel_bias[j])
    return _out_proj(*pending)
```
